```python
import math
import jax, jax.numpy as jnp
from jax import lax
import numpy as np

D_MODEL = 2048
BATCH = 2
SEQ = 8192
DEPTH = 4

HEAD_DIM = 128
MIX_WIDTH = 2048
Q_BLOCK = 128
ROPE_THETA = 10000.0
EPS = 1e-6
MASKED = -1e30
FORCED = 1e9
NSA_HEADS = 8
NSA_KV_GROUPS = 2
NSA_CMP_LEN = 32
NSA_CMP_STRIDE = 16
NSA_CMP_HIDDEN = 256
NSA_SEL_LEN = 64
NSA_SEL_TOPN = 16
NSA_WINDOW = 512
DIFF_HEADS = 4
DIFF_QK_DIM = 64
DIFF_V_DIM = 128
DSA_HEADS = 12
DSA_KV_HEADS = 4
IDX_HEADS = 16
IDX_DIM = 64
DSA_TOPK_MAX = 256
MEM_HEADS = 4
MEM_LEN = 256

EVEN_SPLITS = (
    NSA_HEADS * HEAD_DIM,
    6 * NSA_KV_GROUPS * HEAD_DIM,
    NSA_HEADS * 3,
    NSA_HEADS * HEAD_DIM,
    DIFF_HEADS * 2 * DIFF_QK_DIM,
    DIFF_HEADS * 2 * DIFF_QK_DIM,
    DIFF_HEADS * DIFF_V_DIM,
    DIFF_HEADS * DIFF_V_DIM,
    MEM_HEADS * HEAD_DIM,
    MEM_HEADS * HEAD_DIM,
)
ODD_SPLITS = (
    DSA_HEADS * HEAD_DIM,
    DSA_KV_HEADS * HEAD_DIM,
    DSA_KV_HEADS * HEAD_DIM,
    IDX_HEADS * IDX_DIM,
    IDX_DIM,
    IDX_HEADS,
    DSA_HEADS * HEAD_DIM,
    MEM_HEADS * HEAD_DIM,
    MEM_HEADS * HEAD_DIM,
)
EVEN_IN = sum(EVEN_SPLITS)
ODD_IN = sum(ODD_SPLITS)

kernel_name = "hybrid_nsa_diff_dsa_memory_trunk"


def rms_norm(x, gain):
    xf = x.astype(jnp.float32)
    y = xf * lax.rsqrt(jnp.mean(xf * xf, axis=-1, keepdims=True) + EPS)
    return (y * gain.astype(jnp.float32)).astype(x.dtype)


def rope(x, pos):
    half = x.shape[-1] // 2
    inv = ROPE_THETA ** (-jnp.arange(half, dtype=jnp.float32) / half)
    ang = pos.astype(jnp.float32)[:, None] * inv[None, :]
    cos = jnp.cos(ang)[:, None, :]
    sin = jnp.sin(ang)[:, None, :]
    xf = x.astype(jnp.float32)
    x1, x2 = xf[..., :half], xf[..., half:]
    return jnp.concatenate([x1 * cos - x2 * sin, x2 * cos + x1 * sin], axis=-1).astype(x.dtype)


def masked_softmax(scores, mask):
    s = jnp.where(mask, scores.astype(jnp.float32), MASKED)
    m = jnp.max(s, axis=-1, keepdims=True)
    e = jnp.exp(s - m) * mask
    return e / jnp.maximum(jnp.sum(e, axis=-1, keepdims=True), 1e-30)


def split_cols(h, sizes):
    return jnp.split(h, np.cumsum(sizes)[:-1].tolist(), axis=-1)


def unblock(y):
    y = jnp.moveaxis(y, 0, 1)
    return y.reshape((y.shape[0], y.shape[1] * y.shape[2]) + y.shape[3:])


def nsa_attention(q, kv, gate_logits, pos, cmp_pos, cmp_w1, cmp_w2, qk_gain):
    B, S, H, Dh = q.shape
    G = NSA_KV_GROUPS
    R = H // G
    scale = Dh ** -0.5
    n_cmp = (S - NSA_CMP_LEN) // NSA_CMP_STRIDE + 1
    n_sel = S // NSA_SEL_LEN
    top_n = min(NSA_SEL_TOPN, n_sel)
    cmp_start = np.arange(n_cmp) * NSA_CMP_STRIDE
    cmp_last = jnp.asarray(cmp_start + NSA_CMP_LEN - 1, dtype=jnp.int32)
    cmp_gather = cmp_start[:, None] + np.arange(NSA_CMP_LEN)[None, :]
    sel_start = np.arange(n_sel) * NSA_SEL_LEN
    ov = np.clip(np.minimum(cmp_start[:, None] + NSA_CMP_LEN, sel_start[None, :] + NSA_SEL_LEN)
                 - np.maximum(cmp_start[:, None], sel_start[None, :]), 0, None) / NSA_CMP_LEN
    overlap = jnp.asarray(ov, dtype=jnp.float32)

    qg = rope(rms_norm(q, qk_gain[0]), pos).reshape(B, S, G, R, Dh)
    k_c_raw, v_c_raw, k_s, v_s, k_w, v_w = [kv[:, :, j] for j in range(6)]

    def compress(t, pe, w1, w2):
        blk = t[:, cmp_gather] + pe[:, None, :]
        blk = blk.transpose(0, 1, 3, 2, 4).reshape(B, n_cmp, G, NSA_CMP_LEN * Dh)
        hid = jax.nn.silu(jnp.einsum('bcgf,fh->bcgh', blk, w1))
        return jnp.einsum('bcgh,hd->bcgd', hid, w2)

    k_c = rope(rms_norm(compress(k_c_raw, cmp_pos[0], cmp_w1[0], cmp_w2[0]), qk_gain[1]), cmp_last)
    v_c = compress(v_c_raw, cmp_pos[1], cmp_w1[1], cmp_w2[1])
    k_s = rope(rms_norm(k_s, qk_gain[2]), pos)
    k_w = rope(rms_norm(k_w, qk_gain[3]), pos)

    def to_blocks(t):
        return t.reshape(B, n_sel, NSA_SEL_LEN, G, Dh).transpose(0, 3, 1, 2, 4)

    k_s_blk, v_s_blk = to_blocks(k_s), to_blocks(v_s)
    pad = ((0, 0), (NSA_WINDOW, 0), (0, 0), (0, 0))
    k_w_pad, v_w_pad = jnp.pad(k_w, pad), jnp.pad(v_w, pad)
    gather_blocks = jax.vmap(jax.vmap(lambda tb, ii: tb[ii]))
    sel_ids = jnp.arange(n_sel)
    sel_off = jnp.arange(NSA_SEL_LEN)
    win_off = jnp.arange(NSA_WINDOW + Q_BLOCK)
    vdt = v_c.dtype

    def block(bi):
        q0 = bi * Q_BLOCK
        t = q0 + jnp.arange(Q_BLOCK)
        qb = lax.dynamic_slice_in_dim(qg, q0, Q_BLOCK, 1)
        p_c = masked_softmax(jnp.einsum('btgrd,bcgd->bgrtc', qb, k_c) * scale,
                             cmp_last[None, :] <= t[:, None])
        o_c = jnp.einsum('bgrtc,bcgd->btgrd', p_c.astype(vdt), v_c)
        imp = jnp.einsum('bgrtc,cj->bgtj', p_c, overlap)
        cur = t // NSA_SEL_LEN
        visible = sel_ids[None, :] <= cur[:, None]
        forced = (sel_ids[None, :] == 0) | (sel_ids[None, :] >= cur[:, None] - 1)
        imp = jnp.where(visible, jnp.where(forced, FORCED, imp), MASKED)
        _, idx = lax.top_k(imp, top_n)
        k_sel = gather_blocks(k_s_blk, idx).reshape(B, G, Q_BLOCK, top_n * NSA_SEL_LEN, Dh)
        v_sel = gather_blocks(v_s_blk, idx).reshape(B, G, Q_BLOCK, top_n * NSA_SEL_LEN, Dh)
        tok = (idx[..., None] * NSA_SEL_LEN + sel_off).reshape(B, G, Q_BLOCK, top_n * NSA_SEL_LEN)
        mask_s = (tok <= t[None, None, :, None])[:, :, None]
        p_s = masked_softmax(jnp.einsum('btgrd,bgtkd->bgrtk', qb, k_sel) * scale, mask_s)
        o_s = jnp.einsum('bgrtk,bgtkd->btgrd', p_s.astype(vdt), v_sel)
        kw = lax.dynamic_slice_in_dim(k_w_pad, q0, NSA_WINDOW + Q_BLOCK, 1)
        vw = lax.dynamic_slice_in_dim(v_w_pad, q0, NSA_WINDOW + Q_BLOCK, 1)
        s_pos = q0 - NSA_WINDOW + win_off
        mask_w = ((s_pos[None, :] <= t[:, None]) & (s_pos[None, :] > t[:, None] - NSA_WINDOW)
                  & (s_pos[None, :] >= 0))
        p_w = masked_softmax(jnp.einsum('btgrd,bkgd->bgrtk', qb, kw) * scale, mask_w)
        o_w = jnp.einsum('bgrtk,bkgd->btgrd', p_w.astype(vdt), vw)
        g = jax.nn.sigmoid(lax.dynamic_slice_in_dim(gate_logits, q0, Q_BLOCK, 1).astype(jnp.float32))
        g = g.astype(vdt).reshape(B, Q_BLOCK, G, R, 3)
        return g[..., 0:1] * o_c + g[..., 1:2] * o_s + g[..., 2:3] * o_w

    out = unblock(lax.map(block, jnp.arange(S // Q_BLOCK)))
    return out.reshape(B, S, H * Dh)


def diff_attention(q, k, v, pos, qk_gain, lam_vecs, subln_gain, lambda_init):
    B, S, H, _, dq = q.shape
    q = rope(rms_norm(q, qk_gain[0]).reshape(B, S, 2 * H, dq), pos).reshape(B, S, H, 2, dq)
    k = rope(rms_norm(k, qk_gain[1]).reshape(B, S, 2 * H, dq), pos).reshape(B, S, H, 2, dq)
    lv = lam_vecs.astype(jnp.float32)
    lam = jnp.exp(jnp.sum(lv[0] * lv[1])) - jnp.exp(jnp.sum(lv[2] * lv[3])) + lambda_init
    key_pos = jnp.arange(S)

    def block(bi):
        q0 = bi * Q_BLOCK
        t = q0 + jnp.arange(Q_BLOCK)
        qb = lax.dynamic_slice_in_dim(q, q0, Q_BLOCK, 1)
        p = masked_softmax(jnp.einsum('bthcd,bshcd->bhcts', qb, k) * dq ** -0.5,
                           key_pos[None, :] <= t[:, None])
        a = p[:, :, 0] - lam * p[:, :, 1]
        return jnp.einsum('bhts,bshd->bthd', a.astype(v.dtype), v)

    o = unblock(lax.map(block, jnp.arange(S // Q_BLOCK)))
    o = rms_norm(o, subln_gain) * (1.0 - lambda_init)
    return o.reshape(B, S, H * v.shape[-1])


def dsa_attention(q, k, v, q_idx, k_idx, w_idx, pos, qk_gain):
    B, S, H, Dh = q.shape
    G = DSA_KV_HEADS
    R = H // G
    top_k = min(DSA_TOPK_MAX, S // 4)
    scale = Dh ** -0.5
    qg = rope(rms_norm(q, qk_gain[0]), pos).reshape(B, S, G, R, Dh)
    k = rope(rms_norm(k, qk_gain[1]), pos)
    q_idx = rope(q_idx, pos)
    k_idx = rope(k_idx[:, :, None, :], pos)[:, :, 0]
    w_idx = w_idx * (IDX_HEADS ** -0.5 * IDX_DIM ** -0.5)
    key_pos = jnp.arange(S)
    gather = jax.vmap(lambda tt, ii: tt[ii])

    def block(bi):
        q0 = bi * Q_BLOCK
        t = q0 + jnp.arange(Q_BLOCK)
        qb = lax.dynamic_slice_in_dim(qg, q0, Q_BLOCK, 1)
        qi = lax.dynamic_slice_in_dim(q_idx, q0, Q_BLOCK, 1)
        wi = lax.dynamic_slice_in_dim(w_idx, q0, Q_BLOCK, 1)
        rel = jax.nn.relu(jnp.einsum('bthd,bsd->bths', qi, k_idx)).astype(jnp.float32)
        score = jnp.einsum('bths,bth->bts', rel, wi.astype(jnp.float32))
        causal = key_pos[None, :] <= t[:, None]
        _, idx = lax.top_k(jnp.where(causal, score, MASKED), top_k)
        kg = gather(k, idx)
        vg = gather(v, idx)
        ok = (idx <= t[None, :, None])[:, None, None]
        p = masked_softmax(jnp.einsum('btgrd,btkgd->bgrtk', qb, kg) * scale, ok)
        return jnp.einsum('bgrtk,btkgd->btgrd', p.astype(v.dtype), vg)

    out = unblock(lax.map(block, jnp.arange(S // Q_BLOCK)))
    return out.reshape(B, S, H * Dh)


def memory_attention(q, mem_n, w_kv, qk_gain):
    B, S, H, Dh = q.shape
    kv = jnp.einsum('bmd,dc->bmc', mem_n, w_kv).reshape(B, mem_n.shape[1], 2, H, Dh)
    k = rms_norm(kv[:, :, 0], qk_gain[1])
    v = kv[:, :, 1]
    q = rms_norm(q, qk_gain[0])
    p = jax.nn.softmax((jnp.einsum('bshd,bmhd->bhsm', q, k) * Dh ** -0.5).astype(jnp.float32), axis=-1)
    return jnp.einsum('bhsm,bmhd->bshd', p.astype(v.dtype), v).reshape(B, S, H * Dh)


def setup_inputs(seed: int = 0) -> dict:
    key = jax.random.key(seed)
    ks = jax.random.split(key, 17)
    n_even = (DEPTH + 1) // 2
    n_odd = DEPTH // 2

    def nrm(k, shape, scale):
        return jax.random.normal(k, shape, jnp.float32) * scale

    def gain(k, shape):
        return 1.0 + 0.02 * jax.random.normal(k, shape, jnp.float32)

    return {
        "x": nrm(ks[0], (BATCH, SEQ, D_MODEL), 1.0),
        "mem": nrm(ks[1], (BATCH, MEM_LEN, D_MODEL), 1.0),
        "norm_gain": gain(ks[2], (DEPTH, D_MODEL)),
        "mem_norm_gain": gain(ks[3], (D_MODEL,)),
        "mem_w_kv": nrm(ks[4], (DEPTH, D_MODEL, 2 * MEM_HEADS * HEAD_DIM), D_MODEL ** -0.5),
        "mem_qk_gain": gain(ks[5], (DEPTH, 2, HEAD_DIM)),
        "w_out": nrm(ks[6], (DEPTH, MIX_WIDTH, D_MODEL), MIX_WIDTH ** -0.5),
        "even_w_in": nrm(ks[7], (n_even, D_MODEL, EVEN_IN), D_MODEL ** -0.5),
        "nsa_qk_gain": gain(ks[8], (n_even, 4, HEAD_DIM)),
        "nsa_cmp_pos": nrm(ks[9], (n_even, 2, NSA_CMP_LEN, HEAD_DIM), 0.3),
        "nsa_cmp_w1": nrm(ks[10], (n_even, 2, NSA_CMP_LEN * HEAD_DIM, NSA_CMP_HIDDEN), (NSA_CMP_LEN * HEAD_DIM) ** -0.5),
        "nsa_cmp_w2": nrm(ks[11], (n_even, 2, NSA_CMP_HIDDEN, HEAD_DIM), NSA_CMP_HIDDEN ** -0.5),
        "diff_qk_gain": gain(ks[12], (n_even, 2, DIFF_QK_DIM)),
        "diff_lambda": nrm(ks[13], (n_even, 4, DIFF_QK_DIM), 0.1),
        "diff_subln_gain": gain(ks[14], (n_even, DIFF_V_DIM)),
        "odd_w_in": nrm(ks[15], (n_odd, D_MODEL, ODD_IN), D_MODEL ** -0.5),
        "dsa_qk_gain": gain(ks[16], (n_odd, 2, HEAD_DIM)),
    }


def reference(x, mem, norm_gain, mem_norm_gain, mem_w_kv, mem_qk_gain, w_out, even_w_in,
              nsa_qk_gain, nsa_cmp_pos, nsa_cmp_w1, nsa_cmp_w2, diff_qk_gain, diff_lambda,
              diff_subln_gain, odd_w_in, dsa_qk_gain):
    B, S, _ = x.shape
    pos = jnp.arange(S)
    mem_n = rms_norm(mem, mem_norm_gain)
    for i in range(DEPTH):
        h = rms_norm(x, norm_gain[i])
        if i % 2 == 0:
            e = i // 2
            proj = jnp.einsum('bsd,dc->bsc', h, even_w_in[e])
            a_q, a_kv, a_g, a_z, b_q, b_k, b_v, b_z, m_q, m_z = split_cols(proj, EVEN_SPLITS)
            y_a = nsa_attention(a_q.reshape(B, S, NSA_HEADS, HEAD_DIM),
                                a_kv.reshape(B, S, 6, NSA_KV_GROUPS, HEAD_DIM),
                                a_g.reshape(B, S, NSA_HEADS, 3), pos,
                                nsa_cmp_pos[e], nsa_cmp_w1[e], nsa_cmp_w2[e], nsa_qk_gain[e])
            lambda_init = 0.8 - 0.6 * math.exp(-0.3 * i)
            y_b = diff_attention(b_q.reshape(B, S, DIFF_HEADS, 2, DIFF_QK_DIM),
                                 b_k.reshape(B, S, DIFF_HEADS, 2, DIFF_QK_DIM),
                                 b_v.reshape(B, S, DIFF_HEADS, DIFF_V_DIM), pos,
                                 diff_qk_gain[e], diff_lambda[e], diff_subln_gain[e], lambda_init)
            y_m = memory_attention(m_q.reshape(B, S, MEM_HEADS, HEAD_DIM), mem_n, mem_w_kv[i], mem_qk_gain[i])
            y = jnp.concatenate([y_a * jax.nn.silu(a_z), y_b * jax.nn.silu(b_z),
                                 y_m * jax.nn.silu(m_z)], axis=-1)
        else:
            o = i // 2
            proj = jnp.einsum('bsd,dc->bsc', h, odd_w_in[o])
            c_q, c_k, c_v, i_q, i_k, i_w, c_z, m_q, m_z = split_cols(proj, ODD_SPLITS)
            y_c = dsa_attention(c_q.reshape(B, S, DSA_HEADS, HEAD_DIM),
                                c_k.reshape(B, S, DSA_KV_HEADS, HEAD_DIM),
                                c_v.reshape(B, S, DSA_KV_HEADS, HEAD_DIM),
                                i_q.reshape(B, S, IDX_HEADS, IDX_DIM), i_k, i_w, pos, dsa_qk_gain[o])
            y_m = memory_attention(m_q.reshape(B, S, MEM_HEADS, HEAD_DIM), mem_n, mem_w_kv[i], mem_qk_gain[i])
            y = jnp.concatenate([y_c * jax.nn.silu(c_z), y_m * jax.nn.silu(m_z)], axis=-1)
        x = x + jnp.einsum('bsc,cd->bsd', y, w_out[i])
    return x
```

```python
import functools
import math

import jax
import jax.numpy as jnp
import numpy as np
from jax import lax
from jax.experimental import pallas as pl
from jax.experimental.pallas import tpu as pltpu

F32 = jnp.float32
BF16 = jnp.bfloat16

D_MODEL = 2048
DEPTH = 4
HEAD_DIM = 128
ROPE_THETA = 10000.0
EPS = 1e-6
MASKED = -1e30
FORCED = 1e9
NSA_HEADS = 8
NSA_KV_GROUPS = 2
NSA_CMP_LEN = 32
NSA_CMP_STRIDE = 16
NSA_CMP_HIDDEN = 256
NSA_SEL_LEN = 64
NSA_SEL_TOPN = 16
NSA_WINDOW = 512
DIFF_HEADS = 4
DIFF_QK_DIM = 64
DSA_HEADS = 12
DSA_KV_HEADS = 4
IDX_HEADS = 16
IDX_DIM = 64
DSA_TOPK_MAX = 256
MEM_HEADS = 4

LANES = 128
VMEM_LIMIT_BYTES = 56 * 1024 * 1024

EVEN_UNITS = 54
E_AQ, E_AKV, E_BQ, E_BK, E_BV, E_MQ, E_AZ, E_BZ, E_MZ, E_AG = 0, 8, 20, 24, 28, 32, 36, 44, 48, 52
ODD_UNITS = 50
O_CZ, O_CQ, O_CK, O_CV, O_IQ, O_MQ, O_MZ, O_IKW = 0, 12, 24, 28, 32, 40, 44, 48


def _cparams(sem):
    return pltpu.CompilerParams(dimension_semantics=sem, vmem_limit_bytes=VMEM_LIMIT_BYTES)


def _dot(a, b):
    return jnp.dot(a, b, preferred_element_type=F32)


def _dot_nt(a, b):
    return lax.dot_general(a, b, (((1,), (1,)), ((), ())), preferred_element_type=F32)


def _silu(x):
    return x * jax.nn.sigmoid(x)


def _rms128(x, gain):
    return x * lax.rsqrt(jnp.mean(x * x, axis=-1, keepdims=True) + EPS) * gain


def _rms64(x, gain):
    lo = lax.broadcasted_iota(jnp.int32, x.shape, 1) < 64
    xx = x * x
    s_lo = jnp.sum(jnp.where(lo, xx, 0.0), axis=-1, keepdims=True)
    s_hi = jnp.sum(jnp.where(lo, 0.0, xx), axis=-1, keepdims=True)
    ms = jnp.where(lo, s_lo, s_hi) * (1.0 / 64.0)
    return x * lax.rsqrt(ms + EPS) * gain


def _partner(x, half):
    n = x.shape[-1]
    lane = lax.broadcasted_iota(jnp.int32, x.shape, 1)
    if 2 * half == n:
        return pltpu.roll(x, half, 1)
    a = pltpu.roll(x, half, 1)
    b = pltpu.roll(x, n - half, 1)
    src_a = pltpu.roll(lane, half, 1)
    want = jnp.where((lane & (2 * half - 1)) < half, lane + half, lane - half)
    return jnp.where(src_a == want, a, b)


def _rope(x, cos, sin_signed, half):
    return x * cos + _partner(x, half) * sin_signed


def _proj_kernel(x_ref, g_ref, w_ref, o_ref, hn_ref):
    @pl.when(pl.program_id(1) == 0)
    def _():
        x = x_ref[...]
        hn_ref[...] = _rms128(x, g_ref[...]).astype(BF16)

    o_ref[...] = _dot(hn_ref[...], w_ref[...])


def _proj(x2, gain, w, tn):
    n, d = x2.shape
    c = w.shape[1]
    tm = min(1024, n)
    return pl.pallas_call(
        _proj_kernel,
        grid=(n // tm, c // tn),
        in_specs=[
            pl.BlockSpec((tm, d), lambda i, j: (i, 0)),
            pl.BlockSpec((1, d), lambda i, j: (0, 0)),
            pl.BlockSpec((d, tn), lambda i, j: (0, j)),
        ],
        out_specs=pl.BlockSpec((tm, tn), lambda i, j: (i, j)),
        out_shape=jax.ShapeDtypeStruct((n, c), F32),
        scratch_shapes=[pltpu.VMEM((tm, d), BF16)],
        compiler_params=_cparams(("arbitrary", "arbitrary")),
        name="proj",
    )(x2, gain.reshape(1, d), w)


def _prep_even_kernel(p_ref, cos_ref, sin_ref, cos64_ref, sin64_ref, ng_ref, dg_ref, mg_ref,
                      qa_ref, kv4_ref, craw_ref, bq_ref, bk_ref, bv_ref, mq_ref):
    cos, sin = cos_ref[...], sin_ref[...]
    cos64, sin64 = cos64_ref[...], sin64_ref[...]

    def unit(u):
        return p_ref[:, u * LANES:(u + 1) * LANES]

    for h in range(NSA_HEADS):
        qa_ref[0, h] = _rope(_rms128(unit(E_AQ + h), ng_ref[0:1, :]), cos, sin, 64).astype(BF16)
    for g in range(NSA_KV_GROUPS):
        craw_ref[0, 0, g] = unit(E_AKV + 0 + g)
        craw_ref[0, 1, g] = unit(E_AKV + 2 + g)
        kv4_ref[0, 0, g] = _rope(_rms128(unit(E_AKV + 4 + g), ng_ref[2:3, :]), cos, sin, 64).astype(BF16)
        kv4_ref[0, 1, g] = unit(E_AKV + 6 + g).astype(BF16)
        kv4_ref[0, 2, g] = _rope(_rms128(unit(E_AKV + 8 + g), ng_ref[3:4, :]), cos, sin, 64).astype(BF16)
        kv4_ref[0, 3, g] = unit(E_AKV + 10 + g).astype(BF16)
    lo = lax.broadcasted_iota(jnp.int32, cos.shape, 1) < 64
    for h in range(DIFF_HEADS):
        q = _rope(_rms64(unit(E_BQ + h), dg_ref[0:1, :]), cos64, sin64, 32)
        bq_ref[0, h, 0] = jnp.where(lo, q, 0.0).astype(BF16)
        bq_ref[0, h, 1] = jnp.where(lo, 0.0, q).astype(BF16)
        bk_ref[0, h] = _rope(_rms64(unit(E_BK + h), dg_ref[1:2, :]), cos64, sin64, 32).astype(BF16)
        bv_ref[0, h] = unit(E_BV + h).astype(BF16)
    for h in range(MEM_HEADS):
        mq_ref[0, h] = _rms128(unit(E_MQ + h), mg_ref[...]).astype(BF16)


def _prep_even(proj, b, s, cos, sin, cos64, sin64, nsa_gain, diff_gain2, mem_gain):
    ts = min(256, s)
    nb = s // ts
    c = proj.shape[1]
    row = lambda bb, i: (i, 0)
    const = lambda bb, i: (0, 0)
    hd = lambda n: pl.BlockSpec((1, n, ts, LANES), lambda bb, i: (bb, 0, i, 0))
    return pl.pallas_call(
        _prep_even_kernel,
        grid=(b, nb),
        in_specs=[
            pl.BlockSpec((ts, c), lambda bb, i: (bb * nb + i, 0)),
            pl.BlockSpec((ts, LANES), row), pl.BlockSpec((ts, LANES), row),
            pl.BlockSpec((ts, LANES), row), pl.BlockSpec((ts, LANES), row),
            pl.BlockSpec((4, LANES), const), pl.BlockSpec((2, LANES), const), pl.BlockSpec((1, LANES), const),
        ],
        out_specs=[
            hd(NSA_HEADS),
            pl.BlockSpec((1, 4, NSA_KV_GROUPS, ts, LANES), lambda bb, i: (bb, 0, 0, i, 0)),
            pl.BlockSpec((1, 2, NSA_KV_GROUPS, ts, LANES), lambda bb, i: (bb, 0, 0, i, 0)),
            pl.BlockSpec((1, DIFF_HEADS, 2, ts, LANES), lambda bb, i: (bb, 0, 0, i, 0)),
            hd(DIFF_HEADS), hd(DIFF_HEADS), hd(MEM_HEADS),
        ],
        out_shape=[
            jax.ShapeDtypeStruct((b, NSA_HEADS, s, LANES), BF16),
            jax.ShapeDtypeStruct((b, 4, NSA_KV_GROUPS, s, LANES), BF16),
            jax.ShapeDtypeStruct((b, 2, NSA_KV_GROUPS, s, LANES), F32),
            jax.ShapeDtypeStruct((b, DIFF_HEADS, 2, s, LANES), BF16),
            jax.ShapeDtypeStruct((b, DIFF_HEADS, s, LANES), BF16),
            jax.ShapeDtypeStruct((b, DIFF_HEADS, s, LANES), BF16),
            jax.ShapeDtypeStruct((b, MEM_HEADS, s, LANES), BF16),
        ],
        compiler_params=_cparams(("arbitrary", "arbitrary")),
        name="prep_even",
    )(proj, cos, sin, cos64, sin64, nsa_gain, diff_gain2, mem_gain)


def _compress_kernel(x_ref, pe_ref, w1_ref, w2_ref, g_ref, cos_ref, sin_ref, o_ref, pad_ref, *, s, n_pad):
    kind = pl.program_id(1)
    pad_ref[0:s, :] = x_ref[0, 0, 0]
    pad_ref[s:s + NSA_CMP_LEN, :] = jnp.zeros((NSA_CMP_LEN, LANES), F32)
    acc = jnp.zeros((n_pad, NSA_CMP_HIDDEN), F32)
    for l in range(NSA_CMP_LEN):
        rows = pad_ref[pl.ds(l, n_pad, stride=NSA_CMP_STRIDE), :] + pe_ref[0, l:l + 1, :]
        acc = acc + _dot(rows.astype(BF16), w1_ref[0, l])
    out = _dot(_silu(acc).astype(BF16), w2_ref[0])

    @pl.when(kind == 0)
    def _():
        o_ref[0, 0, 0] = _rope(_rms128(out, g_ref[...]), cos_ref[...], sin_ref[...], 64).astype(BF16)

    @pl.when(kind != 0)
    def _():
        o_ref[0, 0, 0] = out.astype(BF16)


def _compress(craw, pe, w1, w2, gain, cos_c, sin_c):
    b, _, g, s, _ = craw.shape
    n_pad = s // NSA_CMP_STRIDE
    kern = functools.partial(_compress_kernel, s=s, n_pad=n_pad)
    return pl.pallas_call(
        kern,
        grid=(b, 2, g),
        in_specs=[
            pl.BlockSpec((1, 1, 1, s, LANES), lambda bb, k, gg: (bb, k, gg, 0, 0)),
            pl.BlockSpec((1, NSA_CMP_LEN, LANES), lambda bb, k, gg: (k, 0, 0)),
            pl.BlockSpec((1, NSA_CMP_LEN, LANES, NSA_CMP_HIDDEN), lambda bb, k, gg: (k, 0, 0, 0)),
            pl.BlockSpec((1, NSA_CMP_HIDDEN, LANES), lambda bb, k, gg: (k, 0, 0)),
            pl.BlockSpec((1, LANES), lambda bb, k, gg: (0, 0)),
            pl.BlockSpec((n_pad, LANES), lambda bb, k, gg: (0, 0)),
            pl.BlockSpec((n_pad, LANES), lambda bb, k, gg: (0, 0)),
        ],
        out_specs=pl.BlockSpec((1, 1, 1, n_pad, LANES), lambda bb, k, gg: (bb, k, gg, 0, 0)),
        out_shape=jax.ShapeDtypeStruct((b, 2, g, n_pad, LANES), BF16),
        scratch_shapes=[pltpu.VMEM((s + NSA_CMP_LEN, LANES), F32)],
        compiler_params=_cparams(("arbitrary", "arbitrary", "arbitrary")),
        name="nsa_compress",
    )(craw, pe, w1, w2, gain, cos_c, sin_c)


def _masked_softmax(s, mask):
    s = jnp.where(mask, s, MASKED)
    m = jnp.max(s, axis=-1, keepdims=True)
    e = jnp.exp(s - m) * mask.astype(F32)
    return e / jnp.maximum(jnp.sum(e, axis=-1, keepdims=True), 1e-30)


def _online_init(m_ref, l_ref, acc_ref):
    m_ref[...] = jnp.full(m_ref.shape, MASKED, F32)
    l_ref[...] = jnp.zeros(l_ref.shape, F32)
    acc_ref[...] = jnp.zeros(acc_ref.shape, F32)


def _online_step(idx, s, mask, v, m_ref, l_ref, acc_ref):
    s = jnp.where(mask, s, MASKED)
    m_old = m_ref[idx]
    m_new = jnp.maximum(m_old, jnp.max(s, axis=-1, keepdims=True))
    alpha = jnp.exp(m_old - m_new)
    p = jnp.where(mask, jnp.exp(s - m_new), 0.0)
    l_ref[idx] = alpha * l_ref[idx] + jnp.sum(p, axis=-1, keepdims=True)
    acc_ref[idx] = alpha * acc_ref[idx] + _dot(p.astype(v.dtype), v)
    m_ref[idx] = m_new


def _online_result(idx, l_ref, acc_ref):
    return acc_ref[idx] / jnp.maximum(l_ref[idx], 1e-30)


def _top_n_mask(scores, n):
    lane = lax.broadcasted_iota(jnp.int32, scores.shape, 1).astype(F32)
    width = float(scores.shape[1])
    work = scores
    sel = jnp.zeros(scores.shape, F32)
    for _ in range(n):
        m = jnp.max(work, axis=-1, keepdims=True)
        first = jnp.min(jnp.where(work == m, lane, width), axis=-1, keepdims=True)
        pick = lane == first
        sel = jnp.where(pick, 1.0, sel)
        work = jnp.where(pick, -jnp.inf, work)
    return sel


def _nsa_kernel(q_ref, kc_ref, vc_ref, ks_ref, vs_ref, kw_ref, vw_ref, ovl_ref, gl_ref, z_ref, o_ref,
                m_ref, l_ref, acc_ref, oc_ref, ow_ref, *, s_len, tq, tk, n_cmp_pad, top_n):
    rep = NSA_HEADS // NSA_KV_GROUPS
    scale = HEAD_DIM ** -0.5
    q0 = pl.program_id(2) * tq
    t = q0 + lax.broadcasted_iota(jnp.int32, (tq, 1), 0)

    cmp_last = lax.broadcasted_iota(jnp.int32, (tq, n_cmp_pad), 1) * NSA_CMP_STRIDE + (NSA_CMP_LEN - 1)
    mask_c = cmp_last <= t
    kc, vc, ovl = kc_ref[0, 0, 0], vc_ref[0, 0, 0], ovl_ref[...]
    imp = jnp.zeros((tq, ovl.shape[1]), F32)
    for r in range(rep):
        p = _masked_softmax(_dot_nt(q_ref[0, r], kc) * scale, mask_c).astype(BF16)
        oc_ref[r] = _dot(p, vc)
        imp = imp + _dot(p, ovl)

    n_sel = imp.shape[1]
    j = lax.broadcasted_iota(jnp.int32, (tq, n_sel), 1)
    sel_shift = NSA_SEL_LEN.bit_length() - 1
    cur = jnp.right_shift(t, sel_shift)
    visible = j <= cur
    forced = (j == 0) | (j >= cur - 1)
    imp = jnp.where(visible, jnp.where(forced, FORCED, imp), MASKED)
    sel = _top_n_mask(imp, top_n).astype(BF16)

    span = NSA_WINDOW + tq
    start = pl.multiple_of(jnp.maximum(q0 - NSA_WINDOW, 0), tq)
    s_pos = start + lax.broadcasted_iota(jnp.int32, (tq, span), 1)
    mask_w = (s_pos <= t) & (s_pos > t - NSA_WINDOW)
    kw = kw_ref[0, 0, 0, pl.ds(start, span), :]
    vw = vw_ref[0, 0, 0, pl.ds(start, span), :]
    for r in range(rep):
        p = _masked_softmax(_dot_nt(q_ref[0, r], kw) * scale, mask_w).astype(BF16)
        ow_ref[r] = _dot(p, vw)

    _online_init(m_ref, l_ref, acc_ref)
    n_chunks = (q0 + tq + tk - 1) // tk

    def chunk(c, carry):
        k0 = pl.multiple_of(c * tk, tk)
        key = k0 + lax.broadcasted_iota(jnp.int32, (tq, tk), 1)
        blk_of_key = jnp.right_shift(k0 + lax.broadcasted_iota(jnp.int32, (n_sel, tk), 1), sel_shift)
        expand = (lax.broadcasted_iota(jnp.int32, (n_sel, tk), 0) == blk_of_key).astype(BF16)
        mask = (_dot(sel, expand) > 0.5) & (key <= t)
        ks = ks_ref[0, 0, 0, pl.ds(k0, tk), :]
        vs = vs_ref[0, 0, 0, pl.ds(k0, tk), :]
        for r in range(rep):
            _online_step(r, _dot_nt(q_ref[0, r], ks) * scale, mask, vs, m_ref, l_ref, acc_ref)
        return carry

    lax.fori_loop(0, n_chunks, chunk, 0)

    gates = jax.nn.sigmoid(gl_ref[...])
    for r in range(rep):
        o_s = _online_result(r, l_ref, acc_ref)
        out = (gates[:, 3 * r:3 * r + 1] * oc_ref[r] + gates[:, 3 * r + 1:3 * r + 2] * o_s
               + gates[:, 3 * r + 2:3 * r + 3] * ow_ref[r])
        o_ref[:, r * LANES:(r + 1) * LANES] = (out * _silu(z_ref[:, r * LANES:(r + 1) * LANES])).astype(BF16)


def _nsa(qa, kv4, ckv, ovl, proj, b, s):
    tq = min(128, s)
    tk = min(512, s)
    nq = s // tq
    rep = NSA_HEADS // NSA_KV_GROUPS
    n_cmp_pad = ckv.shape[3]
    n_sel = s // NSA_SEL_LEN
    kern = functools.partial(_nsa_kernel, s_len=s, tq=tq, tk=tk, n_cmp_pad=n_cmp_pad,
                             top_n=min(NSA_SEL_TOPN, n_sel))
    full = lambda kind: pl.BlockSpec((1, 1, 1, s, LANES), lambda bb, g, i: (bb, kind, g, 0, 0))
    cmp = lambda kind: pl.BlockSpec((1, 1, 1, n_cmp_pad, LANES), lambda bb, g, i: (bb, kind, g, 0, 0))
    return pl.pallas_call(
        kern,
        grid=(b, NSA_KV_GROUPS, nq),
        in_specs=[
            pl.BlockSpec((1, rep, tq, LANES), lambda bb, g, i: (bb, g, i, 0)),
            cmp(0), cmp(1), full(0), full(1), full(2), full(3),
            pl.BlockSpec((n_cmp_pad, n_sel), lambda bb, g, i: (0, 0)),
            pl.BlockSpec((tq, LANES), lambda bb, g, i: (bb * nq + i, E_AG + g)),
            pl.BlockSpec((tq, rep * LANES), lambda bb, g, i: (bb * nq + i, E_AZ // rep + g)),
        ],
        out_specs=pl.BlockSpec((tq, rep * LANES), lambda bb, g, i: (bb * nq + i, g)),
        out_shape=jax.ShapeDtypeStruct((b * s, NSA_HEADS * LANES), BF16),
        scratch_shapes=[
            pltpu.VMEM((rep, tq, 1), F32), pltpu.VMEM((rep, tq, 1), F32), pltpu.VMEM((rep, tq, LANES), F32),
            pltpu.VMEM((rep, tq, LANES), F32), pltpu.VMEM((rep, tq, LANES), F32),
        ],
        compiler_params=_cparams(("arbitrary", "arbitrary", "arbitrary")),
        name="nsa_attention",
    )(qa, ckv, ckv, kv4, kv4, kv4, kv4, ovl, proj, proj)


def _diff_kernel(q_ref, k_ref, v_ref, lam_ref, sg_ref, z_ref, o_ref, m_ref, l_ref, acc_ref,
                 *, tq, tk, lambda_init):
    scale = DIFF_QK_DIM ** -0.5
    q0 = pl.program_id(2) * tq
    t = q0 + lax.broadcasted_iota(jnp.int32, (tq, 1), 0)
    _online_init(m_ref, l_ref, acc_ref)
    n_chunks = (q0 + tq + tk - 1) // tk

    def chunk(c, carry):
        k0 = pl.multiple_of(c * tk, tk)
        mask = (k0 + lax.broadcasted_iota(jnp.int32, (tq, tk), 1)) <= t
        kk = k_ref[0, 0, pl.ds(k0, tk), :]
        vv = v_ref[0, 0, pl.ds(k0, tk), :]
        for mp in range(2):
            _online_step(mp, _dot_nt(q_ref[0, 0, mp], kk) * scale, mask, vv, m_ref, l_ref, acc_ref)
        return carry

    lax.fori_loop(0, n_chunks, chunk, 0)

    lv = lam_ref[...]
    lam = (jnp.exp(jnp.sum(lv[0:1] * lv[1:2], axis=-1, keepdims=True))
           - jnp.exp(jnp.sum(lv[2:3] * lv[3:4], axis=-1, keepdims=True)) + lambda_init)
    o = _online_result(0, l_ref, acc_ref) - lam * _online_result(1, l_ref, acc_ref)
    o = _rms128(o, sg_ref[...]) * (1.0 - lambda_init)
    o_ref[...] = (o * _silu(z_ref[...])).astype(BF16)


def _diff(bq, bk, bv, lam_vecs, subln_gain, proj, b, s, lambda_init):
    tq = min(256, s)
    tk = min(512, s)
    nq = s // tq
    kern = functools.partial(_diff_kernel, tq=tq, tk=tk, lambda_init=lambda_init)
    full = pl.BlockSpec((1, 1, s, LANES), lambda bb, h, i: (bb, h, 0, 0))
    return pl.pallas_call(
        kern,
        grid=(b, DIFF_HEADS, nq),
        in_specs=[
            pl.BlockSpec((1, 1, 2, tq, LANES), lambda bb, h, i: (bb, h, 0, i, 0)),
            full, full,
            pl.BlockSpec((4, DIFF_QK_DIM), lambda bb, h, i: (0, 0)),
            pl.BlockSpec((1, LANES), lambda bb, h, i: (0, 0)),
            pl.BlockSpec((tq, LANES), lambda bb, h, i: (bb * nq + i, E_BZ + h)),
        ],
        out_specs=pl.BlockSpec((tq, LANES), lambda bb, h, i: (bb * nq + i, h)),
        out_shape=jax.ShapeDtypeStruct((b * s, DIFF_HEADS * LANES), BF16),
        scratch_shapes=[pltpu.VMEM((2, tq, 1), F32), pltpu.VMEM((2, tq, 1), F32), pltpu.VMEM((2, tq, LANES), F32)],
        compiler_params=_cparams(("arbitrary", "arbitrary", "arbitrary")),
        name="diff_attention",
    )(bq, bk, bv, lam_vecs, subln_gain, proj)


def _memkv_kernel(mem_ref, mg_ref, w_ref, kg_ref, k_ref, v_ref):
    mem_n = _rms128(mem_ref[0], mg_ref[...]).astype(BF16)
    kv = _dot(mem_n, w_ref[0])
    for h in range(MEM_HEADS):
        k_ref[0, 0, h] = _rms128(kv[:, h * LANES:(h + 1) * LANES], kg_ref[0]).astype(BF16)
        v_ref[0, 0, h] = kv[:, (MEM_HEADS + h) * LANES:(MEM_HEADS + h + 1) * LANES].astype(BF16)


def _memkv(mem, mem_gain, w_kv, k_gain):
    b, m, d = mem.shape
    depth = w_kv.shape[0]
    c = w_kv.shape[2]
    out = jax.ShapeDtypeStruct((depth, b, MEM_HEADS, m, LANES), BF16)
    ospec = pl.BlockSpec((1, 1, MEM_HEADS, m, LANES), lambda i, bb: (i, bb, 0, 0, 0))
    return pl.pallas_call(
        _memkv_kernel,
        grid=(depth, b),
        in_specs=[
            pl.BlockSpec((1, m, d), lambda i, bb: (bb, 0, 0)),
            pl.BlockSpec((1, d), lambda i, bb: (0, 0)),
            pl.BlockSpec((1, d, c), lambda i, bb: (i, 0, 0)),
            pl.BlockSpec((1, 1, LANES), lambda i, bb: (i, 0, 0)),
        ],
        out_specs=[ospec, ospec],
        out_shape=[out, out],
        compiler_params=_cparams(("arbitrary", "arbitrary")),
        name="mem_kv",
    )(mem, mem_gain.reshape(1, d), w_kv, k_gain)


def _memattn_kernel(q_ref, k_ref, v_ref, z_ref, o_ref):
    scale = HEAD_DIM ** -0.5
    for h in range(MEM_HEADS):
        s = _dot_nt(q_ref[0, h], k_ref[0, h]) * scale
        e = jnp.exp(s - jnp.max(s, axis=-1, keepdims=True))
        p = e / jnp.sum(e, axis=-1, keepdims=True)
        o = _dot(p.astype(BF16), v_ref[0, h])
        o_ref[:, h * LANES:(h + 1) * LANES] = (o * _silu(z_ref[:, h * LANES:(h + 1) * LANES])).astype(BF16)


def _memattn(mq, mk, mv, proj, b, s, z_unit):
    tq = min(512, s)
    nq = s // tq
    m = mk.shape[2]
    kvspec = pl.BlockSpec((1, MEM_HEADS, m, LANES), lambda bb, i: (bb, 0, 0, 0))
    return pl.pallas_call(
        _memattn_kernel,
        grid=(b, nq),
        in_specs=[
            pl.BlockSpec((1, MEM_HEADS, tq, LANES), lambda bb, i: (bb, 0, i, 0)),
            kvspec, kvspec,
            pl.BlockSpec((tq, MEM_HEADS * LANES), lambda bb, i: (bb * nq + i, z_unit // MEM_HEADS)),
        ],
        out_specs=pl.BlockSpec((tq, MEM_HEADS * LANES), lambda bb, i: (bb * nq + i, 0)),
        out_shape=jax.ShapeDtypeStruct((b * s, MEM_HEADS * LANES), BF16),
        compiler_params=_cparams(("arbitrary", "arbitrary")),
        name="mem_attention",
    )(mq, mk, mv, proj)


def _prep_odd_kernel(p_ref, cos_ref, sin_ref, cos64_ref, sin64_ref, cg_ref, mg_ref,
                     cq_ref, ck_ref, cv_ref, iq_ref, ik_ref, mq_ref):
    cos, sin = cos_ref[...], sin_ref[...]
    cos64, sin64 = cos64_ref[...], sin64_ref[...]

    def unit(u):
        return p_ref[:, u * LANES:(u + 1) * LANES]

    for h in range(DSA_HEADS):
        cq_ref[0, h] = _rope(_rms128(unit(O_CQ + h), cg_ref[0:1, :]), cos, sin, 64).astype(BF16)
    for h in range(DSA_KV_HEADS):
        ck_ref[0, h] = _rope(_rms128(unit(O_CK + h), cg_ref[1:2, :]), cos, sin, 64).astype(BF16)
        cv_ref[0, h] = unit(O_CV + h).astype(BF16)
    lo = lax.broadcasted_iota(jnp.int32, cos.shape, 1) < 64
    for u in range(IDX_HEADS // 2):
        x = _rope(unit(O_IQ + u), cos64, sin64, 32)
        iq_ref[0, 2 * u] = jnp.where(lo, x, 0.0).astype(BF16)
        iq_ref[0, 2 * u + 1] = jnp.where(lo, pltpu.roll(x, 64, 1), 0.0).astype(BF16)
    ik = _rope(unit(O_IKW), cos64, sin64, 32)
    ik_ref[0] = jnp.where(lo, ik, 0.0).astype(BF16)
    for h in range(MEM_HEADS):
        mq_ref[0, h] = _rms128(unit(O_MQ + h), mg_ref[...]).astype(BF16)


def _prep_odd(proj, b, s, cos, sin, cos64, sin64, dsa_gain, mem_gain):
    ts = min(256, s)
    nb = s // ts
    c = proj.shape[1]
    row = lambda bb, i: (i, 0)
    const = lambda bb, i: (0, 0)
    hd = lambda n: pl.BlockSpec((1, n, ts, LANES), lambda bb, i: (bb, 0, i, 0))
    return pl.pallas_call(
        _prep_odd_kernel,
        grid=(b, nb),
        in_specs=[
            pl.BlockSpec((ts, c), lambda bb, i: (bb * nb + i, 0)),
            pl.BlockSpec((ts, LANES), row), pl.BlockSpec((ts, LANES), row),
            pl.BlockSpec((ts, LANES), row), pl.BlockSpec((ts, LANES), row),
            pl.BlockSpec((2, LANES), const), pl.BlockSpec((1, LANES), const),
        ],
        out_specs=[
            hd(DSA_HEADS), hd(DSA_KV_HEADS), hd(DSA_KV_HEADS), hd(IDX_HEADS),
            pl.BlockSpec((1, ts, LANES), lambda bb, i: (bb, i, 0)),
            hd(MEM_HEADS),
        ],
        out_shape=[
            jax.ShapeDtypeStruct((b, DSA_HEADS, s, LANES), BF16),
            jax.ShapeDtypeStruct((b, DSA_KV_HEADS, s, LANES), BF16),
            jax.ShapeDtypeStruct((b, DSA_KV_HEADS, s, LANES), BF16),
            jax.ShapeDtypeStruct((b, IDX_HEADS, s, LANES), BF16),
            jax.ShapeDtypeStruct((b, s, LANES), BF16),
            jax.ShapeDtypeStruct((b, MEM_HEADS, s, LANES), BF16),
        ],
        compiler_params=_cparams(("arbitrary", "arbitrary")),
        name="prep_odd",
    )(proj, cos, sin, cos64, sin64, dsa_gain, mem_gain)


def _sortable_key(x):
    bits = pltpu.bitcast(x + 0.0, jnp.int32)
    return jnp.where(bits < 0, bits ^ jnp.int32(0x7FFFFFFF), bits)


def _lane_fold(x):
    acc = x[:, 0:LANES]
    for c in range(1, x.shape[1] // LANES):
        acc = acc + x[:, c * LANES:(c + 1) * LANES]
    return acc


def _dsa_kernel(q_ref, k_ref, v_ref, iq_ref, ik_ref, w_ref, tri_ref, z_ref, o_ref,
                key_ref, m_ref, l_ref, acc_ref, *, tq, tk, top_k):
    rep = DSA_HEADS // DSA_KV_HEADS
    scale = HEAD_DIM ** -0.5
    q0 = pl.program_id(1) * tq
    t = q0 + lax.broadcasted_iota(jnp.int32, (tq, 1), 0)
    n_chunks = (q0 + tq + tk - 1) // tk
    int_min = jnp.int32(-2147483648)

    w = w_ref[...] * (IDX_HEADS ** -0.5 * IDX_DIM ** -0.5)

    def score_chunk(c, carry):
        k0 = pl.multiple_of(c * tk, tk)
        ik = ik_ref[0, pl.ds(k0, tk), :]
        sc = jnp.zeros((tq, tk), F32)
        for h in range(IDX_HEADS):
            sc = sc + jnp.maximum(_dot_nt(iq_ref[0, h], ik), 0.0) * w[:, IDX_DIM + h:IDX_DIM + h + 1]
        causal = (k0 + lax.broadcasted_iota(jnp.int32, (tq, tk), 1)) <= t
        key_ref[c] = _sortable_key(jnp.where(causal, sc, MASKED))
        return carry

    lax.fori_loop(0, n_chunks, score_chunk, 0)

    def count(pred):
        def body(c, acc):
            return acc + _lane_fold(pred(key_ref[c]).astype(F32))
        acc = lax.fori_loop(0, n_chunks, body, jnp.zeros((tq, LANES), F32))
        return jnp.sum(acc, axis=-1, keepdims=True)

    def search(it, thr_u):
        cand_u = thr_u | jnp.left_shift(jnp.int32(1), 31 - it)
        cand = cand_u ^ int_min
        return jnp.where(count(lambda kk: kk >= cand) >= top_k, cand_u, thr_u)

    thr = lax.fori_loop(0, 32, search, jnp.zeros((tq, 1), jnp.int32)) ^ int_min
    budget = top_k - count(lambda kk: kk > thr)

    _online_init(m_ref, l_ref, acc_ref)

    def attend(c, ties_before):
        k0 = pl.multiple_of(c * tk, tk)
        keys = key_ref[c]
        tie = keys == thr
        tie_f = tie.astype(F32)
        rank = _dot(tie.astype(BF16), tri_ref[...]) - tie_f + ties_before
        causal = (k0 + lax.broadcasted_iota(jnp.int32, (tq, tk), 1)) <= t
        mask = ((keys > thr) | (tie & (rank < budget))) & causal
        for g in range(DSA_KV_HEADS):
            kk = k_ref[0, g, pl.ds(k0, tk), :]
            vv = v_ref[0, g, pl.ds(k0, tk), :]
            for r in range(rep):
                h = g * rep + r
                _online_step(h, _dot_nt(q_ref[0, h], kk) * scale, mask, vv, m_ref, l_ref, acc_ref)
        return ties_before + jnp.sum(tie_f, axis=-1, keepdims=True)

    lax.fori_loop(0, n_chunks, attend, jnp.zeros((tq, 1), F32))

    for h in range(DSA_HEADS):
        o = _online_result(h, l_ref, acc_ref)
        o_ref[:, h * LANES:(h + 1) * LANES] = (o * _silu(z_ref[:, h * LANES:(h + 1) * LANES])).astype(BF16)


def _dsa(cq, ck, cv, iq, ik, tri, proj, b, s):
    tq = min(128, s)
    tk = min(512, s)
    nq = s // tq
    kern = functools.partial(_dsa_kernel, tq=tq, tk=tk, top_k=min(DSA_TOPK_MAX, s // 4))
    kvspec = pl.BlockSpec((1, DSA_KV_HEADS, s, LANES), lambda bb, i: (bb, 0, 0, 0))
    return pl.pallas_call(
        kern,
        grid=(b, nq),
        in_specs=[
            pl.BlockSpec((1, DSA_HEADS, tq, LANES), lambda bb, i: (bb, 0, i, 0)),
            kvspec, kvspec,
            pl.BlockSpec((1, IDX_HEADS, tq, LANES), lambda bb, i: (bb, 0, i, 0)),
            pl.BlockSpec((1, s, LANES), lambda bb, i: (bb, 0, 0)),
            pl.BlockSpec((tq, LANES), lambda bb, i: (bb * nq + i, O_IKW)),
            pl.BlockSpec((tk, tk), lambda bb, i: (0, 0)),
            pl.BlockSpec((tq, DSA_HEADS * LANES), lambda bb, i: (bb * nq + i, O_CZ // DSA_HEADS)),
        ],
        out_specs=pl.BlockSpec((tq, DSA_HEADS * LANES), lambda bb, i: (bb * nq + i, 0)),
        out_shape=jax.ShapeDtypeStruct((b * s, DSA_HEADS * LANES), BF16),
        scratch_shapes=[
            pltpu.VMEM((s // tk, tq, tk), jnp.int32),
            pltpu.VMEM((DSA_HEADS, tq, 1), F32), pltpu.VMEM((DSA_HEADS, tq, 1), F32),
            pltpu.VMEM((DSA_HEADS, tq, LANES), F32),
        ],
        compiler_params=_cparams(("arbitrary", "arbitrary")),
        name="dsa_attention",
    )(cq, ck, cv, iq, ik, proj, tri, proj)


def _outproj_kernel(*refs, n_parts):
    x_ref = refs[0]
    y_refs = refs[1:1 + n_parts]
    w_refs = refs[1 + n_parts:1 + 2 * n_parts]
    o_ref = refs[1 + 2 * n_parts]
    acc = x_ref[...]
    for y_ref, w_ref in zip(y_refs, w_refs):
        acc = acc + _dot(y_ref[...], w_ref[...])
    o_ref[...] = acc


def _outproj(x2, ys, w_out):
    n, d = x2.shape
    tm = min(1024, n)
    tn = 1024
    widths = [y.shape[1] for y in ys]
    starts = np.cumsum([0] + widths[:-1]).tolist()
    ws = [w_out[st:st + wd] for st, wd in zip(starts, widths)]
    kern = functools.partial(_outproj_kernel, n_parts=len(ys))
    return pl.pallas_call(
        kern,
        grid=(d // tn, n // tm),
        in_specs=([pl.BlockSpec((tm, tn), lambda j, i: (i, j))]
                  + [pl.BlockSpec((tm, wd), lambda j, i: (i, 0)) for wd in widths]
                  + [pl.BlockSpec((wd, tn), lambda j, i: (0, j)) for wd in widths]),
        out_specs=pl.BlockSpec((tm, tn), lambda j, i: (i, j)),
        out_shape=jax.ShapeDtypeStruct((n, d), F32),
        compiler_params=_cparams(("arbitrary", "arbitrary")),
        name="out_proj",
    )(x2, *ys, *ws)


def _rope_tables(pos, half, reps):
    inv = ROPE_THETA ** (-jnp.arange(half, dtype=F32) / half)
    ang = pos.astype(F32)[:, None] * inv[None, :]
    cos, sin = jnp.cos(ang), jnp.sin(ang)
    return jnp.tile(jnp.concatenate([cos, cos], -1), (1, reps)), jnp.tile(jnp.concatenate([-sin, sin], -1), (1, reps))


def _split_cols(w, sizes):
    return jnp.split(w, np.cumsum(sizes)[:-1].tolist(), axis=-1)


def _even_weight(w):
    sizes = (1024, 1536, 24, 1024, 512, 512, 512, 512, 512, 512)
    a_q, a_kv, a_g, a_z, b_q, b_k, b_v, b_z, m_q, m_z = _split_cols(w, sizes)
    d = w.shape[0]
    per_group = 3 * NSA_HEADS // NSA_KV_GROUPS
    gates = [jnp.pad(a_g[:, g * per_group:(g + 1) * per_group], ((0, 0), (0, LANES - per_group)))
             for g in range(NSA_KV_GROUPS)]
    out = jnp.concatenate([a_q, a_kv, b_q, b_k, b_v, m_q, a_z, b_z, m_z] + gates, axis=-1)
    assert out.shape == (d, EVEN_UNITS * LANES)
    return out.astype(BF16)


def _odd_weight(w):
    sizes = (1536, 512, 512, 1024, 64, 16, 1536, 512, 512)
    c_q, c_k, c_v, i_q, i_k, i_w, c_z, m_q, m_z = _split_cols(w, sizes)
    d = w.shape[0]
    ikw = jnp.pad(jnp.concatenate([i_k, i_w], -1), ((0, 0), (0, LANES - IDX_DIM - IDX_HEADS)))
    out = jnp.concatenate([c_z, c_q, c_k, c_v, i_q, m_q, m_z, ikw, jnp.zeros((d, LANES), w.dtype)], axis=-1)
    assert out.shape == (d, ODD_UNITS * LANES)
    return out.astype(BF16)


def _overlap_matrix(s, n_cmp_pad):
    n_cmp = (s - NSA_CMP_LEN) // NSA_CMP_STRIDE + 1
    n_sel = s // NSA_SEL_LEN
    cmp_start = np.arange(n_cmp) * NSA_CMP_STRIDE
    sel_start = np.arange(n_sel) * NSA_SEL_LEN
    ov = np.clip(np.minimum(cmp_start[:, None] + NSA_CMP_LEN, sel_start[None, :] + NSA_SEL_LEN)
                 - np.maximum(cmp_start[:, None], sel_start[None, :]), 0, None) / NSA_CMP_LEN
    full = np.zeros((n_cmp_pad, n_sel), np.float32)
    full[:n_cmp] = ov
    return jnp.asarray(full, dtype=BF16)


def kernel(x, mem, norm_gain, mem_norm_gain, mem_w_kv, mem_qk_gain, w_out, even_w_in, nsa_qk_gain, nsa_cmp_pos,
           nsa_cmp_w1, nsa_cmp_w2, diff_qk_gain, diff_lambda, diff_subln_gain, odd_w_in, dsa_qk_gain):
    b, s, d = x.shape
    assert d == D_MODEL and s % 512 == 0 and s >= NSA_WINDOW + 128
    pos = jnp.arange(s)
    cos, sin = _rope_tables(pos, HEAD_DIM // 2, 1)
    cos64, sin64 = _rope_tables(pos, DIFF_QK_DIM // 2, 2)
    n_cmp_pad = s // NSA_CMP_STRIDE
    cmp_last = jnp.arange(n_cmp_pad) * NSA_CMP_STRIDE + NSA_CMP_LEN - 1
    cos_c, sin_c = _rope_tables(cmp_last, HEAD_DIM // 2, 1)
    ovl = _overlap_matrix(s, n_cmp_pad)
    tk = min(512, s)
    tri = jnp.asarray(np.triu(np.ones((tk, tk), np.float32)), dtype=BF16)

    mk_all, mv_all = _memkv(mem, mem_norm_gain, mem_w_kv.astype(BF16), mem_qk_gain[:, 1:2, :])
    w_out_b = w_out.astype(BF16)

    x2 = x.reshape(b * s, d)
    for i in range(DEPTH):
        mem_q_gain = mem_qk_gain[i, 0:1, :]
        if i % 2 == 0:
            e = i // 2
            proj = _proj(x2, norm_gain[i], _even_weight(even_w_in[e]), 768)
            dg2 = jnp.tile(diff_qk_gain[e], (1, 2))
            qa, kv4, craw, bq, bk, bv, mq = _prep_even(proj, b, s, cos, sin, cos64, sin64,
                                                        nsa_qk_gain[e], dg2, mem_q_gain)
            w1 = nsa_cmp_w1[e].reshape(2, NSA_CMP_LEN, HEAD_DIM, NSA_CMP_HIDDEN).astype(BF16)
            ckv = _compress(craw, nsa_cmp_pos[e], w1, nsa_cmp_w2[e].astype(BF16),
                            nsa_qk_gain[e, 1:2, :], cos_c, sin_c)
            y_a = _nsa(qa, kv4, ckv, ovl, proj, b, s)
            lambda_init = 0.8 - 0.6 * math.exp(-0.3 * i)
            y_b = _diff(bq, bk, bv, diff_lambda[e], diff_subln_gain[e].reshape(1, LANES), proj, b, s, lambda_init)
            y_m = _memattn(mq, mk_all[i], mv_all[i], proj, b, s, E_MZ)
            ys = [y_a, y_b, y_m]
        else:
            o = i // 2
            proj = _proj(x2, norm_gain[i], _odd_weight(odd_w_in[o]), 640)
            cq, ck, cv, iq, ik, mq = _prep_odd(proj, b, s, cos, sin, cos64, sin64, dsa_qk_gain[o], mem_q_gain)
            y_c = _dsa(cq, ck, cv, iq, ik, tri, proj, b, s)
            y_m = _memattn(mq, mk_all[i], mv_all[i], proj, b, s, O_MZ)
            ys = [y_c, y_m]
        x2 = _outproj(x2, ys, w_out_b[i])
    return x2.reshape(b, s, d)
```

```python
import functools
import math

import jax
import jax.numpy as jnp
import numpy as np
from jax import lax
from jax.experimental import pallas as pl
from jax.experimental.pallas import tpu as pltpu

F32 = jnp.float32
BF16 = jnp.bfloat16

D_MODEL = 2048
DEPTH = 4
HEAD_DIM = 128
ROPE_THETA = 10000.0
EPS = 1e-6
MASKED = -1e30
FORCED = 1e9
NSA_HEADS = 8
NSA_KV_GROUPS = 2
NSA_CMP_LEN = 32
NSA_CMP_STRIDE = 16
NSA_CMP_HIDDEN = 256
NSA_SEL_LEN = 64
NSA_SEL_TOPN = 16
NSA_WINDOW = 512
DIFF_HEADS = 4
DIFF_QK_DIM = 64
DSA_HEADS = 12
DSA_KV_HEADS = 4
IDX_HEADS = 16
IDX_DIM = 64
DSA_TOPK_MAX = 256
MEM_HEADS = 4

LANES = 128
SUBLANES = 8
VMEM_LIMIT_BYTES = 56 * 1024 * 1024
LOG2E = 1.4426950408889634

PREP_ROWS = 256
TQ_SPARSE = 128
TQ_DIFF = 256
TK = 512
VT_BLOCK = 256
NEG_INIT = -1e30
NEG_MASK = -2e30

EVEN_UNITS = 54
E_AQ, E_AKV, E_BQ, E_BK, E_BV, E_MQ, E_AZ, E_BZ, E_MZ, E_AG = 0, 8, 20, 24, 28, 32, 36, 44, 48, 52
ODD_UNITS = 50
O_CZ, O_CQ, O_CK, O_CV, O_IQ, O_MQ, O_MZ, O_IKW = 0, 12, 24, 28, 32, 40, 44, 48


def _cparams(sem):
    return pltpu.CompilerParams(dimension_semantics=sem, vmem_limit_bytes=VMEM_LIMIT_BYTES)


def _dot(a, b):
    return jnp.dot(a, b, preferred_element_type=F32)


def _dot_nt(a, b):
    return lax.dot_general(a, b, (((1,), (1,)), ((), ())), preferred_element_type=F32)


def _silu(x):
    return x * jax.nn.sigmoid(x)


def _rms128(x, gain):
    return x * lax.rsqrt(jnp.mean(x * x, axis=-1, keepdims=True) + EPS) * gain


def _rms64(x, gain):
    lo = lax.broadcasted_iota(jnp.int32, x.shape, 1) < 64
    xx = x * x
    s_lo = jnp.sum(jnp.where(lo, xx, 0.0), axis=-1, keepdims=True)
    s_hi = jnp.sum(jnp.where(lo, 0.0, xx), axis=-1, keepdims=True)
    ms = jnp.where(lo, s_lo, s_hi) * (1.0 / 64.0)
    return x * lax.rsqrt(ms + EPS) * gain


def _partner(x, half):
    n = x.shape[-1]
    lane = lax.broadcasted_iota(jnp.int32, x.shape, 1)
    if 2 * half == n:
        return pltpu.roll(x, half, 1)
    a = pltpu.roll(x, half, 1)
    b = pltpu.roll(x, n - half, 1)
    src_a = pltpu.roll(lane, half, 1)
    want = jnp.where((lane & (2 * half - 1)) < half, lane + half, lane - half)
    return jnp.where(src_a == want, a, b)


def _rope(x, cos, sin_signed, half):
    return x * cos + _partner(x, half) * sin_signed


def _fold8(x, op, short_chains=False):
    rows, w = x.shape
    if short_chains:
        x = op(x.reshape(SUBLANES, rows // SUBLANES, w), axis=0)
        rows = rows // SUBLANES
    return op(x.reshape(rows // SUBLANES, SUBLANES, w), axis=0)


def _col_max(x):
    return jnp.max(_fold8(x, jnp.max), axis=0, keepdims=True)


def _col_sum(x):
    return jnp.sum(_fold8(x, jnp.sum), axis=0, keepdims=True)


def _proj_kernel(x_ref, g_ref, w_ref, o_ref, hn_ref):
    @pl.when(pl.program_id(1) == 0)
    def _():
        x = x_ref[...]
        hn_ref[...] = _rms128(x, g_ref[...]).astype(BF16)

    o_ref[...] = _dot(hn_ref[...], w_ref[...])


def _proj(x2, gain, w, tn):
    n, d = x2.shape
    c = w.shape[1]
    tm = min(1024, n)
    return pl.pallas_call(
        _proj_kernel,
        grid=(n // tm, c // tn),
        in_specs=[
            pl.BlockSpec((tm, d), lambda i, j: (i, 0)),
            pl.BlockSpec((1, d), lambda i, j: (0, 0)),
            pl.BlockSpec((d, tn), lambda i, j: (0, j)),
        ],
        out_specs=pl.BlockSpec((tm, tn), lambda i, j: (i, j)),
        out_shape=jax.ShapeDtypeStruct((n, c), F32),
        scratch_shapes=[pltpu.VMEM((tm, d), BF16)],
        compiler_params=_cparams(("arbitrary", "arbitrary")),
        name="proj",
    )(x2, gain.reshape(1, d), w)


def _prep_even_kernel(p_ref, cos_ref, sin_ref, cos64_ref, sin64_ref, ng_ref, dg_ref, mg_ref,
                      qa_ref, ks_ref, kw_ref, vst_ref, vwt_ref, craw_ref, bq_ref, bk_ref, bvt_ref, mq_ref):
    cos, sin = cos_ref[...], sin_ref[...]
    cos64, sin64 = cos64_ref[...], sin64_ref[...]
    rep = NSA_HEADS // NSA_KV_GROUPS
    tq = TQ_SPARSE

    def unit(u):
        return p_ref[:, u * LANES:(u + 1) * LANES]

    for h in range(NSA_HEADS):
        q = _rope(_rms128(unit(E_AQ + h), ng_ref[0:1, :]), cos, sin, 64).astype(BF16)
        g, r = divmod(h, rep)
        for qb in range(PREP_ROWS // tq):
            qa_ref[0, g, qb, r * tq:(r + 1) * tq, :] = q[qb * tq:(qb + 1) * tq, :]
    for g in range(NSA_KV_GROUPS):
        craw_ref[0, 0, g] = unit(E_AKV + 0 + g)
        craw_ref[0, 1, g] = unit(E_AKV + 2 + g)
        ks_ref[0, g] = _rope(_rms128(unit(E_AKV + 4 + g), ng_ref[2:3, :]), cos, sin, 64).astype(BF16)
        vst_ref[0, g, 0] = unit(E_AKV + 6 + g).T.astype(BF16)
        kw_ref[0, g] = _rope(_rms128(unit(E_AKV + 8 + g), ng_ref[3:4, :]), cos, sin, 64).astype(BF16)
        vw = unit(E_AKV + 10 + g)
        for kb in range(PREP_ROWS // LANES):
            vwt_ref[0, g, kb] = vw[kb * LANES:(kb + 1) * LANES, :].T.astype(BF16)
    lo = lax.broadcasted_iota(jnp.int32, cos.shape, 1) < 64
    for h in range(DIFF_HEADS):
        q = _rope(_rms64(unit(E_BQ + h), dg_ref[0:1, :]), cos64, sin64, 32)
        bq_ref[0, h, 0, 0:PREP_ROWS, :] = jnp.where(lo, q, 0.0).astype(BF16)
        bq_ref[0, h, 0, PREP_ROWS:2 * PREP_ROWS, :] = jnp.where(lo, 0.0, q).astype(BF16)
        bk_ref[0, h] = _rope(_rms64(unit(E_BK + h), dg_ref[1:2, :]), cos64, sin64, 32).astype(BF16)
        bvt_ref[0, h, 0] = unit(E_BV + h).T.astype(BF16)
    for h in range(MEM_HEADS):
        mq_ref[0, h] = _rms128(unit(E_MQ + h), mg_ref[...]).astype(BF16)


def _prep_even(proj, b, s, cos, sin, cos64, sin64, nsa_gain, diff_gain2, mem_gain):
    ts = PREP_ROWS
    assert ts == TQ_DIFF == VT_BLOCK
    nb = s // ts
    c = proj.shape[1]
    g = NSA_KV_GROUPS
    rep = NSA_HEADS // g
    row = lambda bb, i: (i, 0)
    const = lambda bb, i: (0, 0)
    seq = lambda n: pl.BlockSpec((1, n, ts, LANES), lambda bb, i: (bb, 0, i, 0))
    blk5 = lambda n, k, r, cdim: pl.BlockSpec((1, n, k, r, cdim), lambda bb, i: (bb, 0, i, 0, 0))
    return pl.pallas_call(
        _prep_even_kernel,
        grid=(b, nb),
        in_specs=[
            pl.BlockSpec((ts, c), lambda bb, i: (bb * nb + i, 0)),
            pl.BlockSpec((ts, LANES), row), pl.BlockSpec((ts, LANES), row),
            pl.BlockSpec((ts, LANES), row), pl.BlockSpec((ts, LANES), row),
            pl.BlockSpec((4, LANES), const), pl.BlockSpec((2, LANES), const), pl.BlockSpec((1, LANES), const),
        ],
        out_specs=[
            blk5(g, ts // TQ_SPARSE, rep * TQ_SPARSE, LANES),
            seq(g), seq(g),
            blk5(g, 1, LANES, VT_BLOCK),
            blk5(g, ts // LANES, LANES, LANES),
            pl.BlockSpec((1, 2, g, ts, LANES), lambda bb, i: (bb, 0, 0, i, 0)),
            blk5(DIFF_HEADS, 1, 2 * TQ_DIFF, LANES),
            seq(DIFF_HEADS),
            blk5(DIFF_HEADS, 1, LANES, VT_BLOCK),
            seq(MEM_HEADS),
        ],
        out_shape=[
            jax.ShapeDtypeStruct((b, g, s // TQ_SPARSE, rep * TQ_SPARSE, LANES), BF16),
            jax.ShapeDtypeStruct((b, g, s, LANES), BF16),
            jax.ShapeDtypeStruct((b, g, s, LANES), BF16),
            jax.ShapeDtypeStruct((b, g, s // VT_BLOCK, LANES, VT_BLOCK), BF16),
            jax.ShapeDtypeStruct((b, g, s // LANES, LANES, LANES), BF16),
            jax.ShapeDtypeStruct((b, 2, g, s, LANES), F32),
            jax.ShapeDtypeStruct((b, DIFF_HEADS, s // TQ_DIFF, 2 * TQ_DIFF, LANES), BF16),
            jax.ShapeDtypeStruct((b, DIFF_HEADS, s, LANES), BF16),
            jax.ShapeDtypeStruct((b, DIFF_HEADS, s // VT_BLOCK, LANES, VT_BLOCK), BF16),
            jax.ShapeDtypeStruct((b, MEM_HEADS, s, LANES), BF16),
        ],
        compiler_params=_cparams(("arbitrary", "arbitrary")),
        name="prep_even",
    )(proj, cos, sin, cos64, sin64, nsa_gain, diff_gain2, mem_gain)


def _compress_kernel(x_ref, pe_ref, w1_ref, w2_ref, g_ref, cos_ref, sin_ref, kc_ref, vct_ref, pad_ref, *, s, n_pad):
    for kind in range(2):
        pad_ref[0:s, :] = x_ref[0, kind, 0]
        pad_ref[s:s + NSA_CMP_LEN, :] = jnp.zeros((NSA_CMP_LEN, LANES), F32)
        acc = jnp.zeros((n_pad, NSA_CMP_HIDDEN), F32)
        for l in range(NSA_CMP_LEN):
            rows = pad_ref[pl.ds(l, n_pad, stride=NSA_CMP_STRIDE), :] + pe_ref[kind, l:l + 1, :]
            acc = acc + _dot(rows.astype(BF16), w1_ref[kind, l])
        out = _dot(_silu(acc).astype(BF16), w2_ref[kind])
        if kind == 0:
            kc_ref[0, 0] = _rope(_rms128(out, g_ref[...]), cos_ref[...], sin_ref[...], 64).astype(BF16)
        else:
            vct_ref[0, 0] = out.T.astype(BF16)


def _compress(craw, pe, w1, w2, gain, cos_c, sin_c):
    b, _, g, s, _ = craw.shape
    n_pad = s // NSA_CMP_STRIDE
    kern = functools.partial(_compress_kernel, s=s, n_pad=n_pad)
    whole = lambda shape: pl.BlockSpec(shape, lambda bb, gg: (0,) * len(shape))
    return pl.pallas_call(
        kern,
        grid=(b, g),
        in_specs=[
            pl.BlockSpec((1, 2, 1, s, LANES), lambda bb, gg: (bb, 0, gg, 0, 0)),
            whole((2, NSA_CMP_LEN, LANES)),
            whole((2, NSA_CMP_LEN, LANES, NSA_CMP_HIDDEN)),
            whole((2, NSA_CMP_HIDDEN, LANES)),
            whole((1, LANES)), whole((n_pad, LANES)), whole((n_pad, LANES)),
        ],
        out_specs=[
            pl.BlockSpec((1, 1, n_pad, LANES), lambda bb, gg: (bb, gg, 0, 0)),
            pl.BlockSpec((1, 1, LANES, n_pad), lambda bb, gg: (bb, gg, 0, 0)),
        ],
        out_shape=[
            jax.ShapeDtypeStruct((b, g, n_pad, LANES), BF16),
            jax.ShapeDtypeStruct((b, g, LANES, n_pad), BF16),
        ],
        scratch_shapes=[pltpu.VMEM((s + NSA_CMP_LEN, LANES), F32)],
        compiler_params=_cparams(("arbitrary", "arbitrary")),
        name="nsa_compress",
    )(craw, pe, w1, w2, gain, cos_c, sin_c)


def _masked_softmax_t(s, mask):
    s = jnp.where(mask, s, MASKED)
    e = jnp.exp(s - _col_max(s)) * mask.astype(F32)
    return e / jnp.maximum(_col_sum(e), 1e-30)


def _online_init(m_ref, l_ref, acc_ref):
    m_ref[...] = jnp.full(m_ref.shape, NEG_INIT, F32)
    l_ref[...] = jnp.zeros(l_ref.shape, F32)
    acc_ref[...] = jnp.zeros(acc_ref.shape, F32)


def _online_step(idx, st, pv, m_ref, l_ref, acc_ref):
    m_old = m_ref[idx]
    m_new = jnp.maximum(m_old, _col_max(st))
    alpha = jnp.exp2(m_old - m_new)
    p = jnp.exp2(st - m_new)
    l_ref[idx] = alpha * l_ref[idx] + _fold8(p, jnp.sum)
    acc_ref[idx] = alpha * acc_ref[idx] + pv(p.astype(BF16))
    m_ref[idx] = m_new


def _online_result(idx, l_ref, acc_ref):
    return acc_ref[idx] / jnp.maximum(jnp.sum(l_ref[idx], axis=0, keepdims=True), 1e-30)


def _flash_pairs(n_chunks, qk, soft, sa_ref, sb_ref):
    qk(0, sa_ref)

    def pair(j, carry):
        a = 2 * j
        qk(a + 1, sb_ref)
        soft(a, sa_ref)
        qk(a + 2, sa_ref)
        soft(a + 1, sb_ref)
        return carry

    lax.fori_loop(0, (n_chunks + 1) // 2, pair, 0)


def _pv_blocks(vt_ref, lead, chunk, tk):
    per = tk // VT_BLOCK

    def pv(p):
        out = _dot(vt_ref[lead + (chunk * per,)], p[0:VT_BLOCK])
        for i in range(1, per):
            out = out + _dot(vt_ref[lead + (chunk * per + i,)], p[i * VT_BLOCK:(i + 1) * VT_BLOCK])
        return out

    return pv


def _top_n_mask_t(scores, n):
    row = lax.broadcasted_iota(jnp.int32, scores.shape, 0).astype(F32)
    height = float(scores.shape[0])
    work = scores
    sel = jnp.zeros(scores.shape, F32)
    for _ in range(n):
        m = jnp.max(work, axis=0, keepdims=True)
        first = jnp.min(jnp.where(work == m, row, height), axis=0, keepdims=True)
        pick = row == first
        sel = jnp.where(pick, 1.0, sel)
        work = jnp.where(pick, -jnp.inf, work)
    return sel


def _nsa_kernel(q_ref, kc_ref, vct_ref, ks_ref, vst_ref, kw_ref, vwt_ref, ovlt_ref, gl_ref, z_ref, o_ref,
                m_ref, l_ref, acc_ref, sa_ref, sb_ref, selt_ref, oct_ref, owt_ref, *, tq, tk, top_n):
    rep = NSA_HEADS // NSA_KV_GROUPS
    w = rep * tq
    scale = HEAD_DIM ** -0.5
    q0 = pl.program_id(2) * tq
    t_row = q0 + (lax.broadcasted_iota(jnp.int32, (1, w), 1) & (tq - 1))
    t_one = t_row[:, 0:tq]
    q = q_ref[0, 0, 0]

    n_cmp_pad = kc_ref.shape[2]
    cmp_last = lax.broadcasted_iota(jnp.int32, (n_cmp_pad, w), 0) * NSA_CMP_STRIDE + (NSA_CMP_LEN - 1)
    p = _masked_softmax_t(_dot_nt(kc_ref[0, 0], q) * scale, cmp_last <= t_row).astype(BF16)
    oct_ref[...] = _dot(vct_ref[0, 0], p)
    imp_heads = _dot(ovlt_ref[...], p)
    imp = imp_heads[:, 0:tq]
    for r in range(1, rep):
        imp = imp + imp_heads[:, r * tq:(r + 1) * tq]

    n_sel = imp.shape[0]
    sel_shift = NSA_SEL_LEN.bit_length() - 1
    j = lax.broadcasted_iota(jnp.int32, (n_sel, tq), 0)
    cur = jnp.right_shift(t_one, sel_shift)
    visible = j <= cur
    forced = (j == 0) | (j >= cur - 1)
    imp = jnp.where(visible, jnp.where(forced, FORCED, imp), MASKED)
    selt_ref[...] = _top_n_mask_t(imp, top_n)

    span = NSA_WINDOW + tq
    start = pl.multiple_of(jnp.maximum(q0 - NSA_WINDOW, 0), tq)
    s_pos = start + lax.broadcasted_iota(jnp.int32, (span, w), 0)
    mask_w = (s_pos <= t_row) & (s_pos > t_row - NSA_WINDOW)
    p = _masked_softmax_t(_dot_nt(kw_ref[0, 0, pl.ds(start, span), :], q) * scale, mask_w).astype(BF16)
    blk0 = start // LANES
    ow = _dot(vwt_ref[0, 0, blk0], p[0:LANES])
    for i in range(1, span // LANES):
        ow = ow + _dot(vwt_ref[0, 0, blk0 + i], p[i * LANES:(i + 1) * LANES])
    owt_ref[...] = ow

    _online_init(m_ref, l_ref, acc_ref)
    n_chunks = (q0 + tq + tk - 1) // tk
    blocks_per_chunk = tk // NSA_SEL_LEN

    def qk(c, dst):
        k0 = pl.multiple_of(jnp.minimum(c, n_chunks - 1) * tk, tk)
        dst[0] = _dot_nt(ks_ref[0, 0, pl.ds(k0, tk), :], q) * (scale * LOG2E)

    def soft(c, src):
        cc = jnp.minimum(c, n_chunks - 1)
        rows = selt_ref[pl.ds(pl.multiple_of(cc * blocks_per_chunk, blocks_per_chunk), blocks_per_chunk), :]
        picked = jnp.concatenate(
            [jnp.broadcast_to(rows[i:i + 1, :], (NSA_SEL_LEN, tq)) for i in range(blocks_per_chunk)], axis=0)
        key = c * tk + lax.broadcasted_iota(jnp.int32, (tk, tq), 0)
        bias = jnp.where((picked > 0.5) & (key <= t_one), 0.0, NEG_MASK)
        st = src[0] + jnp.concatenate([bias] * rep, axis=1)
        _online_step(0, st, _pv_blocks(vst_ref, (0, 0), cc, tk), m_ref, l_ref, acc_ref)

    _flash_pairs(n_chunks, qk, soft, sa_ref, sb_ref)

    gates_t = jax.nn.sigmoid(gl_ref[...]).T
    os_t = _online_result(0, l_ref, acc_ref)
    for r in range(rep):
        sl = slice(r * tq, (r + 1) * tq)
        out_t = (gates_t[3 * r:3 * r + 1, :] * oct_ref[:, sl] + gates_t[3 * r + 1:3 * r + 2, :] * os_t[:, sl]
                 + gates_t[3 * r + 2:3 * r + 3, :] * owt_ref[:, sl])
        o_ref[:, r * LANES:(r + 1) * LANES] = (out_t.T * _silu(z_ref[:, r * LANES:(r + 1) * LANES])).astype(BF16)


def _nsa(qa, ks, vst, kw, vwt, kc, vct, ovlt, proj, b, s):
    tq, tk = TQ_SPARSE, TK
    nq = s // tq
    g = NSA_KV_GROUPS
    rep = NSA_HEADS // g
    w = rep * tq
    n_cmp_pad = kc.shape[2]
    n_sel = s // NSA_SEL_LEN
    kern = functools.partial(_nsa_kernel, tq=tq, tk=tk, top_n=min(NSA_SEL_TOPN, n_sel))
    per_group = lambda shape: pl.BlockSpec((1, 1) + shape, lambda bb, gg, i: (bb, gg) + (0,) * len(shape))
    return pl.pallas_call(
        kern,
        grid=(b, g, nq),
        in_specs=[
            pl.BlockSpec((1, 1, 1, w, LANES), lambda bb, gg, i: (bb, gg, i, 0, 0)),
            per_group((n_cmp_pad, LANES)), per_group((LANES, n_cmp_pad)),
            per_group((s, LANES)), per_group((s // VT_BLOCK, LANES, VT_BLOCK)),
            per_group((s, LANES)), per_group((s // LANES, LANES, LANES)),
            pl.BlockSpec((n_sel, n_cmp_pad), lambda bb, gg, i: (0, 0)),
            pl.BlockSpec((tq, LANES), lambda bb, gg, i: (bb * nq + i, E_AG + gg)),
            pl.BlockSpec((tq, rep * LANES), lambda bb, gg, i: (bb * nq + i, E_AZ // rep + gg)),
        ],
        out_specs=pl.BlockSpec((tq, rep * LANES), lambda bb, gg, i: (bb * nq + i, gg)),
        out_shape=jax.ShapeDtypeStruct((b * s, NSA_HEADS * LANES), BF16),
        scratch_shapes=[
            pltpu.VMEM((1, 1, w), F32), pltpu.VMEM((1, SUBLANES, w), F32), pltpu.VMEM((1, LANES, w), F32),
            pltpu.VMEM((1, tk, w), F32), pltpu.VMEM((1, tk, w), F32),
            pltpu.VMEM((n_sel, tq), F32), pltpu.VMEM((LANES, w), F32), pltpu.VMEM((LANES, w), F32),
        ],
        compiler_params=_cparams(("arbitrary", "arbitrary", "arbitrary")),
        name="nsa_attention",
    )(qa, kc, vct, ks, vst, kw, vwt, ovlt, proj, proj)


def _diff_kernel(q_ref, k_ref, vt_ref, lam_ref, sg_ref, z_ref, o_ref, m_ref, l_ref, acc_ref, sa_ref, sb_ref,
                 *, tq, tk, lambda_init):
    w = 2 * tq
    q0 = pl.program_id(2) * tq
    t_row = q0 + (lax.broadcasted_iota(jnp.int32, (1, w), 1) & (tq - 1))
    q = q_ref[0, 0, 0]
    _online_init(m_ref, l_ref, acc_ref)
    n_chunks = (q0 + tq + tk - 1) // tk

    def qk(c, dst):
        k0 = pl.multiple_of(jnp.minimum(c, n_chunks - 1) * tk, tk)
        dst[0] = _dot_nt(k_ref[0, 0, pl.ds(k0, tk), :], q) * (DIFF_QK_DIM ** -0.5 * LOG2E)

    def soft(c, src):
        cc = jnp.minimum(c, n_chunks - 1)
        key = c * tk + lax.broadcasted_iota(jnp.int32, (tk, w), 0)
        st = jnp.where(key <= t_row, src[0], NEG_MASK)
        _online_step(0, st, _pv_blocks(vt_ref, (0, 0), cc, tk), m_ref, l_ref, acc_ref)

    _flash_pairs(n_chunks, qk, soft, sa_ref, sb_ref)

    lv = lam_ref[...]
    lam = (jnp.exp(jnp.sum(lv[0:1] * lv[1:2], axis=-1, keepdims=True))
           - jnp.exp(jnp.sum(lv[2:3] * lv[3:4], axis=-1, keepdims=True)) + lambda_init)
    o_t = _online_result(0, l_ref, acc_ref)
    o = (o_t[:, 0:tq] - lam * o_t[:, tq:w]).T
    o = _rms128(o, sg_ref[...]) * (1.0 - lambda_init)
    o_ref[...] = (o * _silu(z_ref[...])).astype(BF16)


def _diff(bq, bk, bvt, lam_vecs, subln_gain, proj, b, s, lambda_init):
    tq, tk = TQ_DIFF, TK
    nq = s // tq
    w = 2 * tq
    kern = functools.partial(_diff_kernel, tq=tq, tk=tk, lambda_init=lambda_init)
    return pl.pallas_call(
        kern,
        grid=(b, DIFF_HEADS, nq),
        in_specs=[
            pl.BlockSpec((1, 1, 1, w, LANES), lambda bb, h, i: (bb, h, i, 0, 0)),
            pl.BlockSpec((1, 1, s, LANES), lambda bb, h, i: (bb, h, 0, 0)),
            pl.BlockSpec((1, 1, s // VT_BLOCK, LANES, VT_BLOCK), lambda bb, h, i: (bb, h, 0, 0, 0)),
            pl.BlockSpec((4, DIFF_QK_DIM), lambda bb, h, i: (0, 0)),
            pl.BlockSpec((1, LANES), lambda bb, h, i: (0, 0)),
            pl.BlockSpec((tq, LANES), lambda bb, h, i: (bb * nq + i, E_BZ + h)),
        ],
        out_specs=pl.BlockSpec((tq, LANES), lambda bb, h, i: (bb * nq + i, h)),
        out_shape=jax.ShapeDtypeStruct((b * s, DIFF_HEADS * LANES), BF16),
        scratch_shapes=[
            pltpu.VMEM((1, 1, w), F32), pltpu.VMEM((1, SUBLANES, w), F32), pltpu.VMEM((1, LANES, w), F32),
            pltpu.VMEM((1, tk, w), F32), pltpu.VMEM((1, tk, w), F32),
        ],
        compiler_params=_cparams(("arbitrary", "arbitrary", "arbitrary")),
        name="diff_attention",
    )(bq, bk, bvt, lam_vecs, subln_gain, proj)


def _memkv_kernel(mem_ref, mg_ref, w_ref, kg_ref, k_ref, v_ref):
    mem_n = _rms128(mem_ref[0], mg_ref[...]).astype(BF16)
    kv = _dot(mem_n, w_ref[0])
    for h in range(MEM_HEADS):
        k_ref[0, 0, h] = _rms128(kv[:, h * LANES:(h + 1) * LANES], kg_ref[0]).astype(BF16)
        v_ref[0, 0, h] = kv[:, (MEM_HEADS + h) * LANES:(MEM_HEADS + h + 1) * LANES].astype(BF16)


def _memkv(mem, mem_gain, w_kv, k_gain):
    b, m, d = mem.shape
    depth = w_kv.shape[0]
    c = w_kv.shape[2]
    out = jax.ShapeDtypeStruct((depth, b, MEM_HEADS, m, LANES), BF16)
    ospec = pl.BlockSpec((1, 1, MEM_HEADS, m, LANES), lambda i, bb: (i, bb, 0, 0, 0))
    return pl.pallas_call(
        _memkv_kernel,
        grid=(depth, b),
        in_specs=[
            pl.BlockSpec((1, m, d), lambda i, bb: (bb, 0, 0)),
            pl.BlockSpec((1, d), lambda i, bb: (0, 0)),
            pl.BlockSpec((1, d, c), lambda i, bb: (i, 0, 0)),
            pl.BlockSpec((1, 1, LANES), lambda i, bb: (i, 0, 0)),
        ],
        out_specs=[ospec, ospec],
        out_shape=[out, out],
        compiler_params=_cparams(("arbitrary", "arbitrary")),
        name="mem_kv",
    )(mem, mem_gain.reshape(1, d), w_kv, k_gain)


def _memattn_kernel(q_ref, k_ref, v_ref, z_ref, o_ref):
    scale = HEAD_DIM ** -0.5
    for h in range(MEM_HEADS):
        s = _dot_nt(q_ref[0, h], k_ref[0, h]) * scale
        e = jnp.exp(s - jnp.max(s, axis=-1, keepdims=True))
        p = e / jnp.sum(e, axis=-1, keepdims=True)
        o = _dot(p.astype(BF16), v_ref[0, h])
        o_ref[:, h * LANES:(h + 1) * LANES] = (o * _silu(z_ref[:, h * LANES:(h + 1) * LANES])).astype(BF16)


def _memattn(mq, mk, mv, proj, b, s, z_unit):
    tq = min(512, s)
    nq = s // tq
    m = mk.shape[2]
    kvspec = pl.BlockSpec((1, MEM_HEADS, m, LANES), lambda bb, i: (bb, 0, 0, 0))
    return pl.pallas_call(
        _memattn_kernel,
        grid=(b, nq),
        in_specs=[
            pl.BlockSpec((1, MEM_HEADS, tq, LANES), lambda bb, i: (bb, 0, i, 0)),
            kvspec, kvspec,
            pl.BlockSpec((tq, MEM_HEADS * LANES), lambda bb, i: (bb * nq + i, z_unit // MEM_HEADS)),
        ],
        out_specs=pl.BlockSpec((tq, MEM_HEADS * LANES), lambda bb, i: (bb * nq + i, 0)),
        out_shape=jax.ShapeDtypeStruct((b * s, MEM_HEADS * LANES), BF16),
        compiler_params=_cparams(("arbitrary", "arbitrary")),
        name="mem_attention",
    )(mq, mk, mv, proj)


def _prep_odd_kernel(p_ref, cos_ref, sin_ref, cos64_ref, sin64_ref, cg_ref, mg_ref,
                     cq_ref, ck_ref, cvt_ref, iq_ref, ik_ref, mq_ref):
    cos, sin = cos_ref[...], sin_ref[...]
    cos64, sin64 = cos64_ref[...], sin64_ref[...]
    rep = DSA_HEADS // DSA_KV_HEADS
    tq = TQ_SPARSE
    n_qb = PREP_ROWS // tq

    def unit(u):
        return p_ref[:, u * LANES:(u + 1) * LANES]

    for h in range(DSA_HEADS):
        q = _rope(_rms128(unit(O_CQ + h), cg_ref[0:1, :]), cos, sin, 64).astype(BF16)
        g, r = divmod(h, rep)
        for qb in range(n_qb):
            cq_ref[0, g, qb, r * tq:(r + 1) * tq, :] = q[qb * tq:(qb + 1) * tq, :]
    for h in range(DSA_KV_HEADS):
        ck_ref[0, h] = _rope(_rms128(unit(O_CK + h), cg_ref[1:2, :]), cos, sin, 64).astype(BF16)
        cvt_ref[0, h, 0] = unit(O_CV + h).T.astype(BF16)
    lo = lax.broadcasted_iota(jnp.int32, cos.shape, 1) < 64
    for u in range(IDX_HEADS // 2):
        x = _rope(unit(O_IQ + u), cos64, sin64, 32)
        even = jnp.where(lo, x, 0.0).astype(BF16)
        odd = jnp.where(lo, pltpu.roll(x, 64, 1), 0.0).astype(BF16)
        for qb in range(n_qb):
            iq_ref[0, qb, (2 * u) * tq:(2 * u + 1) * tq, :] = even[qb * tq:(qb + 1) * tq, :]
            iq_ref[0, qb, (2 * u + 1) * tq:(2 * u + 2) * tq, :] = odd[qb * tq:(qb + 1) * tq, :]
    ik = _rope(unit(O_IKW), cos64, sin64, 32)
    ik_ref[0] = jnp.where(lo, ik, 0.0).astype(BF16)
    for h in range(MEM_HEADS):
        mq_ref[0, h] = _rms128(unit(O_MQ + h), mg_ref[...]).astype(BF16)


def _prep_odd(proj, b, s, cos, sin, cos64, sin64, dsa_gain, mem_gain):
    ts = PREP_ROWS
    nb = s // ts
    c = proj.shape[1]
    rep = DSA_HEADS // DSA_KV_HEADS
    row = lambda bb, i: (i, 0)
    const = lambda bb, i: (0, 0)
    seq = lambda n: pl.BlockSpec((1, n, ts, LANES), lambda bb, i: (bb, 0, i, 0))
    return pl.pallas_call(
        _prep_odd_kernel,
        grid=(b, nb),
        in_specs=[
            pl.BlockSpec((ts, c), lambda bb, i: (bb * nb + i, 0)),
            pl.BlockSpec((ts, LANES), row), pl.BlockSpec((ts, LANES), row),
            pl.BlockSpec((ts, LANES), row), pl.BlockSpec((ts, LANES), row),
            pl.BlockSpec((2, LANES), const), pl.BlockSpec((1, LANES), const),
        ],
        out_specs=[
            pl.BlockSpec((1, DSA_KV_HEADS, ts // TQ_SPARSE, rep * TQ_SPARSE, LANES), lambda bb, i: (bb, 0, i, 0, 0)),
            seq(DSA_KV_HEADS),
            pl.BlockSpec((1, DSA_KV_HEADS, 1, LANES, VT_BLOCK), lambda bb, i: (bb, 0, i, 0, 0)),
            pl.BlockSpec((1, ts // TQ_SPARSE, IDX_HEADS * TQ_SPARSE, LANES), lambda bb, i: (bb, i, 0, 0)),
            pl.BlockSpec((1, ts, LANES), lambda bb, i: (bb, i, 0)),
            seq(MEM_HEADS),
        ],
        out_shape=[
            jax.ShapeDtypeStruct((b, DSA_KV_HEADS, s // TQ_SPARSE, rep * TQ_SPARSE, LANES), BF16),
            jax.ShapeDtypeStruct((b, DSA_KV_HEADS, s, LANES), BF16),
            jax.ShapeDtypeStruct((b, DSA_KV_HEADS, s // VT_BLOCK, LANES, VT_BLOCK), BF16),
            jax.ShapeDtypeStruct((b, s // TQ_SPARSE, IDX_HEADS * TQ_SPARSE, LANES), BF16),
            jax.ShapeDtypeStruct((b, s, LANES), BF16),
            jax.ShapeDtypeStruct((b, MEM_HEADS, s, LANES), BF16),
        ],
        compiler_params=_cparams(("arbitrary", "arbitrary")),
        name="prep_odd",
    )(proj, cos, sin, cos64, sin64, dsa_gain, mem_gain)


def _sortable_key(x):
    bits = pltpu.bitcast(x + 0.0, jnp.int32)
    return jnp.where(bits < 0, bits ^ jnp.int32(0x7FFFFFFF), bits)


def _dsa_kernel(q_ref, k_ref, vt_ref, iq_ref, ik_ref, w_ref, ltri_ref, z_ref, o_ref,
                key_ref, m_ref, l_ref, acc_ref, sa_ref, sb_ref, ties_ref, *, tq, tk, top_k):
    rep = DSA_HEADS // DSA_KV_HEADS
    scale = HEAD_DIM ** -0.5
    q0 = pl.program_id(1) * tq
    t_row = q0 + lax.broadcasted_iota(jnp.int32, (1, tq), 1)
    n_chunks = (q0 + tq + tk - 1) // tk
    int_min = jnp.int32(-2147483648)
    heads_per_dot = 4

    w_t = (w_ref[...] * (IDX_HEADS ** -0.5 * IDX_DIM ** -0.5)).T

    def score_chunk(c, carry):
        k0 = pl.multiple_of(c * tk, tk)
        ik = ik_ref[0, pl.ds(k0, tk), :]
        sc = jnp.zeros((tk, tq), F32)
        for h0 in range(0, IDX_HEADS, heads_per_dot):
            x = _dot_nt(ik, iq_ref[0, 0, h0 * tq:(h0 + heads_per_dot) * tq, :])
            for hh in range(heads_per_dot):
                h = h0 + hh
                sc = sc + jnp.maximum(x[:, hh * tq:(hh + 1) * tq], 0.0) * w_t[IDX_DIM + h:IDX_DIM + h + 1, :]
        causal = (k0 + lax.broadcasted_iota(jnp.int32, (tk, tq), 0)) <= t_row
        key_ref[c] = _sortable_key(jnp.where(causal, sc, MASKED))
        return carry

    lax.fori_loop(0, n_chunks, score_chunk, 0)

    def count(pred):
        def body(c, acc):
            return acc + _fold8(pred(key_ref[c]).astype(F32), jnp.sum, short_chains=True)
        acc = lax.fori_loop(0, n_chunks, body, jnp.zeros((SUBLANES, tq), F32))
        return jnp.sum(acc, axis=0, keepdims=True)

    def search(it, thr_u):
        cand_u = thr_u | jnp.left_shift(jnp.int32(1), 31 - it)
        cand = cand_u ^ int_min
        return jnp.where(count(lambda kk: kk >= cand) >= top_k, cand_u, thr_u)

    thr = lax.fori_loop(0, 32, search, jnp.zeros((1, tq), jnp.int32)) ^ int_min
    budget = top_k - count(lambda kk: kk > thr)

    _online_init(m_ref, l_ref, acc_ref)
    ties_ref[...] = jnp.zeros(ties_ref.shape, F32)

    def qk(c, dst):
        k0 = pl.multiple_of(jnp.minimum(c, n_chunks - 1) * tk, tk)
        for g in range(DSA_KV_HEADS):
            dst[g] = _dot_nt(k_ref[0, g, pl.ds(k0, tk), :], q_ref[0, g, 0]) * (scale * LOG2E)

    def soft(c, src):
        cc = jnp.minimum(c, n_chunks - 1)
        keys = key_ref[cc]
        tie = keys == thr
        rank = _dot(ltri_ref[...], tie.astype(BF16)) + ties_ref[...]
        causal = (c * tk + lax.broadcasted_iota(jnp.int32, (tk, tq), 0)) <= t_row
        ok = ((keys > thr) | (tie & (rank < budget))) & causal
        ties_ref[...] = ties_ref[...] + _col_sum(tie.astype(F32))
        bias = jnp.where(ok, 0.0, NEG_MASK)
        bias = jnp.concatenate([bias] * rep, axis=1)
        for g in range(DSA_KV_HEADS):
            _online_step(g, src[g] + bias, _pv_blocks(vt_ref, (0, g), cc, tk), m_ref, l_ref, acc_ref)

    _flash_pairs(n_chunks, qk, soft, sa_ref, sb_ref)

    for g in range(DSA_KV_HEADS):
        o_t = _online_result(g, l_ref, acc_ref)
        for r in range(rep):
            h = g * rep + r
            o = o_t[:, r * tq:(r + 1) * tq].T
            o_ref[:, h * LANES:(h + 1) * LANES] = (o * _silu(z_ref[:, h * LANES:(h + 1) * LANES])).astype(BF16)


def _dsa(cq, ck, cvt, iq, ik, ltri, proj, b, s):
    tq, tk = TQ_SPARSE, TK
    nq = s // tq
    g = DSA_KV_HEADS
    rep = DSA_HEADS // g
    w = rep * tq
    kern = functools.partial(_dsa_kernel, tq=tq, tk=tk, top_k=min(DSA_TOPK_MAX, s // 4))
    return pl.pallas_call(
        kern,
        grid=(b, nq),
        in_specs=[
            pl.BlockSpec((1, g, 1, w, LANES), lambda bb, i: (bb, 0, i, 0, 0)),
            pl.BlockSpec((1, g, s, LANES), lambda bb, i: (bb, 0, 0, 0), pipeline_mode=pl.Buffered(1)),
            pl.BlockSpec((1, g, s // VT_BLOCK, LANES, VT_BLOCK), lambda bb, i: (bb, 0, 0, 0, 0),
                         pipeline_mode=pl.Buffered(1)),
            pl.BlockSpec((1, 1, IDX_HEADS * tq, LANES), lambda bb, i: (bb, i, 0, 0)),
            pl.BlockSpec((1, s, LANES), lambda bb, i: (bb, 0, 0), pipeline_mode=pl.Buffered(1)),
            pl.BlockSpec((tq, LANES), lambda bb, i: (bb * nq + i, O_IKW)),
            pl.BlockSpec((tk, tk), lambda bb, i: (0, 0)),
            pl.BlockSpec((tq, DSA_HEADS * LANES), lambda bb, i: (bb * nq + i, O_CZ // DSA_HEADS)),
        ],
        out_specs=pl.BlockSpec((tq, DSA_HEADS * LANES), lambda bb, i: (bb * nq + i, 0)),
        out_shape=jax.ShapeDtypeStruct((b * s, DSA_HEADS * LANES), BF16),
        scratch_shapes=[
            pltpu.VMEM((s // tk, tk, tq), jnp.int32),
            pltpu.VMEM((g, 1, w), F32), pltpu.VMEM((g, SUBLANES, w), F32), pltpu.VMEM((g, LANES, w), F32),
            pltpu.VMEM((g, tk, w), F32), pltpu.VMEM((g, tk, w), F32),
            pltpu.VMEM((1, tq), F32),
        ],
        compiler_params=_cparams(("arbitrary", "arbitrary")),
        name="dsa_attention",
    )(cq, ck, cvt, iq, ik, proj, ltri, proj)


def _outproj_kernel(*refs, n_parts):
    x_ref = refs[0]
    y_refs = refs[1:1 + n_parts]
    w_refs = refs[1 + n_parts:1 + 2 * n_parts]
    o_ref = refs[1 + 2 * n_parts]
    acc = x_ref[...]
    for y_ref, w_ref in zip(y_refs, w_refs):
        acc = acc + _dot(y_ref[...], w_ref[...])
    o_ref[...] = acc


def _outproj(x2, ys, w_out):
    n, d = x2.shape
    tm = min(1024, n)
    tn = 1024
    widths = [y.shape[1] for y in ys]
    starts = np.cumsum([0] + widths[:-1]).tolist()
    ws = [w_out[st:st + wd] for st, wd in zip(starts, widths)]
    kern = functools.partial(_outproj_kernel, n_parts=len(ys))
    return pl.pallas_call(
        kern,
        grid=(d // tn, n // tm),
        in_specs=([pl.BlockSpec((tm, tn), lambda j, i: (i, j))]
                  + [pl.BlockSpec((tm, wd), lambda j, i: (i, 0)) for wd in widths]
                  + [pl.BlockSpec((wd, tn), lambda j, i: (0, j)) for wd in widths]),
        out_specs=pl.BlockSpec((tm, tn), lambda j, i: (i, j)),
        out_shape=jax.ShapeDtypeStruct((n, d), F32),
        compiler_params=_cparams(("arbitrary", "arbitrary")),
        name="out_proj",
    )(x2, *ys, *ws)


def _rope_tables(pos, half, reps):
    inv = ROPE_THETA ** (-jnp.arange(half, dtype=F32) / half)
    ang = pos.astype(F32)[:, None] * inv[None, :]
    cos, sin = jnp.cos(ang), jnp.sin(ang)
    return jnp.tile(jnp.concatenate([cos, cos], -1), (1, reps)), jnp.tile(jnp.concatenate([-sin, sin], -1), (1, reps))


def _split_cols(w, sizes):
    return jnp.split(w, np.cumsum(sizes)[:-1].tolist(), axis=-1)


def _even_weight(w):
    sizes = (1024, 1536, 24, 1024, 512, 512, 512, 512, 512, 512)
    a_q, a_kv, a_g, a_z, b_q, b_k, b_v, b_z, m_q, m_z = _split_cols(w, sizes)
    d = w.shape[0]
    per_group = 3 * NSA_HEADS // NSA_KV_GROUPS
    gates = [jnp.pad(a_g[:, g * per_group:(g + 1) * per_group], ((0, 0), (0, LANES - per_group)))
             for g in range(NSA_KV_GROUPS)]
    out = jnp.concatenate([a_q, a_kv, b_q, b_k, b_v, m_q, a_z, b_z, m_z] + gates, axis=-1)
    assert out.shape == (d, EVEN_UNITS * LANES)
    return out.astype(BF16)


def _odd_weight(w):
    sizes = (1536, 512, 512, 1024, 64, 16, 1536, 512, 512)
    c_q, c_k, c_v, i_q, i_k, i_w, c_z, m_q, m_z = _split_cols(w, sizes)
    d = w.shape[0]
    ikw = jnp.pad(jnp.concatenate([i_k, i_w], -1), ((0, 0), (0, LANES - IDX_DIM - IDX_HEADS)))
    out = jnp.concatenate([c_z, c_q, c_k, c_v, i_q, m_q, m_z, ikw, jnp.zeros((d, LANES), w.dtype)], axis=-1)
    assert out.shape == (d, ODD_UNITS * LANES)
    return out.astype(BF16)


def _overlap_matrix_t(s, n_cmp_pad):
    n_cmp = (s - NSA_CMP_LEN) // NSA_CMP_STRIDE + 1
    n_sel = s // NSA_SEL_LEN
    cmp_start = np.arange(n_cmp) * NSA_CMP_STRIDE
    sel_start = np.arange(n_sel) * NSA_SEL_LEN
    ov = np.clip(np.minimum(cmp_start[:, None] + NSA_CMP_LEN, sel_start[None, :] + NSA_SEL_LEN)
                 - np.maximum(cmp_start[:, None], sel_start[None, :]), 0, None) / NSA_CMP_LEN
    full = np.zeros((n_sel, n_cmp_pad), np.float32)
    full[:, :n_cmp] = ov.T
    return jnp.asarray(full, dtype=BF16)


def kernel(x, mem, norm_gain, mem_norm_gain, mem_w_kv, mem_qk_gain, w_out, even_w_in, nsa_qk_gain, nsa_cmp_pos,
           nsa_cmp_w1, nsa_cmp_w2, diff_qk_gain, diff_lambda, diff_subln_gain, odd_w_in, dsa_qk_gain):
    b, s, d = x.shape
    assert d == D_MODEL and s % TK == 0 and s >= NSA_WINDOW + TQ_SPARSE
    pos = jnp.arange(s)
    cos, sin = _rope_tables(pos, HEAD_DIM // 2, 1)
    cos64, sin64 = _rope_tables(pos, DIFF_QK_DIM // 2, 2)
    n_cmp_pad = s // NSA_CMP_STRIDE
    cmp_last = jnp.arange(n_cmp_pad) * NSA_CMP_STRIDE + NSA_CMP_LEN - 1
    cos_c, sin_c = _rope_tables(cmp_last, HEAD_DIM // 2, 1)
    ovlt = _overlap_matrix_t(s, n_cmp_pad)
    ltri = jnp.asarray(np.tril(np.ones((TK, TK), np.float32), -1), dtype=BF16)

    mk_all, mv_all = _memkv(mem, mem_norm_gain, mem_w_kv.astype(BF16), mem_qk_gain[:, 1:2, :])
    w_out_b = w_out.astype(BF16)

    x2 = x.reshape(b * s, d)
    for i in range(DEPTH):
        mem_q_gain = mem_qk_gain[i, 0:1, :]
        if i % 2 == 0:
            e = i // 2
            proj = _proj(x2, norm_gain[i], _even_weight(even_w_in[e]), 768)
            dg2 = jnp.tile(diff_qk_gain[e], (1, 2))
            qa, ks, kw, vst, vwt, craw, bq, bk, bvt, mq = _prep_even(
                proj, b, s, cos, sin, cos64, sin64, nsa_qk_gain[e], dg2, mem_q_gain)
            w1 = nsa_cmp_w1[e].reshape(2, NSA_CMP_LEN, HEAD_DIM, NSA_CMP_HIDDEN).astype(BF16)
            kc, vct = _compress(craw, nsa_cmp_pos[e], w1, nsa_cmp_w2[e].astype(BF16),
                                nsa_qk_gain[e, 1:2, :], cos_c, sin_c)
            y_a = _nsa(qa, ks, vst, kw, vwt, kc, vct, ovlt, proj, b, s)
            lambda_init = 0.8 - 0.6 * math.exp(-0.3 * i)
            y_b = _diff(bq, bk, bvt, diff_lambda[e], diff_subln_gain[e].reshape(1, LANES), proj, b, s, lambda_init)
            y_m = _memattn(mq, mk_all[i], mv_all[i], proj, b, s, E_MZ)
            ys = [y_a, y_b, y_m]
        else:
            o = i // 2
            proj = _proj(x2, norm_gain[i], _odd_weight(odd_w_in[o]), 640)
            cq, ck, cvt, iq, ik, mq = _prep_odd(proj, b, s, cos, sin, cos64, sin64, dsa_qk_gain[o], mem_q_gain)
            y_c = _dsa(cq, ck, cvt, iq, ik, ltri, proj, b, s)
            y_m = _memattn(mq, mk_all[i], mv_all[i], proj, b, s, O_MZ)
            ys = [y_c, y_m]
        x2 = _outproj(x2, ys, w_out_b[i])
    return x2.reshape(b, s, d)
```

```python
import functools
import math

import jax
import jax.numpy as jnp
import numpy as np
from jax import lax
from jax.experimental import pallas as pl
from jax.experimental.pallas import tpu as pltpu

F32 = jnp.float32
BF16 = jnp.bfloat16

D_MODEL = 2048
DEPTH = 4
HEAD_DIM = 128
ROPE_THETA = 10000.0
EPS = 1e-6
MASKED = -1e30
FORCED = 1e9
NSA_HEADS = 8
NSA_KV_GROUPS = 2
NSA_CMP_LEN = 32
NSA_CMP_STRIDE = 16
NSA_CMP_HIDDEN = 256
NSA_SEL_LEN = 64
NSA_SEL_TOPN = 16
NSA_WINDOW = 512
DIFF_HEADS = 4
DIFF_QK_DIM = 64
DSA_HEADS = 12
DSA_KV_HEADS = 4
IDX_HEADS = 16
IDX_DIM = 64
DSA_TOPK_MAX = 256
MEM_HEADS = 4

LANES = 128
SUBLANES = 8
VMEM_LIMIT_BYTES = 56 * 1024 * 1024
LOG2E = 1.4426950408889634

PREP_ROWS = 256
TQ_SPARSE = 128
TQ_DIFF = 256
TK = 512
VT_BLOCK = 256
NEG_INIT = -1e30
NEG_MASK = -2e30
FAST_LOG2_BOUND = 40.0

EVEN_UNITS = 54
E_AQ, E_AKV, E_BQ, E_BK, E_BV, E_MQ, E_AZ, E_BZ, E_MZ, E_AG = 0, 8, 20, 24, 28, 32, 36, 44, 48, 52
ODD_UNITS = 50
O_CZ, O_CQ, O_CK, O_CV, O_IQ, O_MQ, O_MZ, O_IKW = 0, 12, 24, 28, 32, 40, 44, 48


def _cparams(sem):
    return pltpu.CompilerParams(dimension_semantics=sem, vmem_limit_bytes=VMEM_LIMIT_BYTES)


def _dot(a, b):
    return jnp.dot(a, b, preferred_element_type=F32)


def _dot_nt(a, b):
    return lax.dot_general(a, b, (((1,), (1,)), ((), ())), preferred_element_type=F32)


def _silu(x):
    return x * jax.nn.sigmoid(x)


def _rms128(x, gain):
    return x * lax.rsqrt(jnp.mean(x * x, axis=-1, keepdims=True) + EPS) * gain


def _rms64(x, gain):
    lo = lax.broadcasted_iota(jnp.int32, x.shape, 1) < 64
    xx = x * x
    s_lo = jnp.sum(jnp.where(lo, xx, 0.0), axis=-1, keepdims=True)
    s_hi = jnp.sum(jnp.where(lo, 0.0, xx), axis=-1, keepdims=True)
    ms = jnp.where(lo, s_lo, s_hi) * (1.0 / 64.0)
    return x * lax.rsqrt(ms + EPS) * gain


def _partner(x, half):
    n = x.shape[-1]
    lane = lax.broadcasted_iota(jnp.int32, x.shape, 1)
    if 2 * half == n:
        return pltpu.roll(x, half, 1)
    a = pltpu.roll(x, half, 1)
    b = pltpu.roll(x, n - half, 1)
    src_a = pltpu.roll(lane, half, 1)
    want = jnp.where((lane & (2 * half - 1)) < half, lane + half, lane - half)
    return jnp.where(src_a == want, a, b)


def _rope(x, cos, sin_signed, half):
    return x * cos + _partner(x, half) * sin_signed


def _fold8(x, op, short_chains=False):
    rows, w = x.shape
    if short_chains:
        x = op(x.reshape(SUBLANES, rows // SUBLANES, w), axis=0)
        rows = rows // SUBLANES
    return op(x.reshape(rows // SUBLANES, SUBLANES, w), axis=0)


def _col_max(x):
    return jnp.max(_fold8(x, jnp.max), axis=0, keepdims=True)


def _col_sum(x):
    return jnp.sum(_fold8(x, jnp.sum), axis=0, keepdims=True)


def _proj_kernel(x_ref, g_ref, w_ref, o_ref, hn_ref):
    @pl.when(pl.program_id(1) == 0)
    def _():
        x = x_ref[...]
        hn_ref[...] = _rms128(x, g_ref[...]).astype(BF16)

    o_ref[...] = _dot(hn_ref[...], w_ref[...])


def _proj(x2, gain, w, tn):
    n, d = x2.shape
    c = w.shape[1]
    tm = min(1024, n)
    return pl.pallas_call(
        _proj_kernel,
        grid=(n // tm, c // tn),
        in_specs=[
            pl.BlockSpec((tm, d), lambda i, j: (i, 0)),
            pl.BlockSpec((1, d), lambda i, j: (0, 0)),
            pl.BlockSpec((d, tn), lambda i, j: (0, j)),
        ],
        out_specs=pl.BlockSpec((tm, tn), lambda i, j: (i, j)),
        out_shape=jax.ShapeDtypeStruct((n, c), F32),
        scratch_shapes=[pltpu.VMEM((tm, d), BF16)],
        compiler_params=_cparams(("arbitrary", "arbitrary")),
        name="proj",
    )(x2, gain.reshape(1, d), w)


def _prep_even_kernel(p_ref, cos_ref, sin_ref, cos64_ref, sin64_ref, ng_ref, dg_ref, mg_ref,
                      qa_ref, ks_ref, kw_ref, vst_ref, vwt_ref, craw_ref, bq_ref, bk_ref, bvt_ref, mq_ref):
    cos, sin = cos_ref[...], sin_ref[...]
    cos64, sin64 = cos64_ref[...], sin64_ref[...]
    rep = NSA_HEADS // NSA_KV_GROUPS
    tq = TQ_SPARSE

    def unit(u):
        return p_ref[:, u * LANES:(u + 1) * LANES]

    for h in range(NSA_HEADS):
        q = _rope(_rms128(unit(E_AQ + h), ng_ref[0:1, :]), cos, sin, 64).astype(BF16)
        g, r = divmod(h, rep)
        for qb in range(PREP_ROWS // tq):
            qa_ref[0, g, qb, r * tq:(r + 1) * tq, :] = q[qb * tq:(qb + 1) * tq, :]
    for g in range(NSA_KV_GROUPS):
        craw_ref[0, 0, g] = unit(E_AKV + 0 + g)
        craw_ref[0, 1, g] = unit(E_AKV + 2 + g)
        ks_ref[0, g] = _rope(_rms128(unit(E_AKV + 4 + g), ng_ref[2:3, :]), cos, sin, 64).astype(BF16)
        vst_ref[0, g, 0] = unit(E_AKV + 6 + g).T.astype(BF16)
        kw_ref[0, g] = _rope(_rms128(unit(E_AKV + 8 + g), ng_ref[3:4, :]), cos, sin, 64).astype(BF16)
        vw = unit(E_AKV + 10 + g)
        for kb in range(PREP_ROWS // LANES):
            vwt_ref[0, g, kb] = vw[kb * LANES:(kb + 1) * LANES, :].T.astype(BF16)
    lo = lax.broadcasted_iota(jnp.int32, cos.shape, 1) < 64
    for h in range(DIFF_HEADS):
        q = _rope(_rms64(unit(E_BQ + h), dg_ref[0:1, :]), cos64, sin64, 32)
        bq_ref[0, h, 0, 0:PREP_ROWS, :] = jnp.where(lo, q, 0.0).astype(BF16)
        bq_ref[0, h, 0, PREP_ROWS:2 * PREP_ROWS, :] = jnp.where(lo, 0.0, q).astype(BF16)
        bk_ref[0, h] = _rope(_rms64(unit(E_BK + h), dg_ref[1:2, :]), cos64, sin64, 32).astype(BF16)
        bvt_ref[0, h, 0] = unit(E_BV + h).T.astype(BF16)
    for h in range(MEM_HEADS):
        mq_ref[0, h] = _rms128(unit(E_MQ + h), mg_ref[...]).astype(BF16)


def _prep_even(proj, b, s, cos, sin, cos64, sin64, nsa_gain, diff_gain2, mem_gain):
    ts = PREP_ROWS
    assert ts == TQ_DIFF == VT_BLOCK
    nb = s // ts
    c = proj.shape[1]
    g = NSA_KV_GROUPS
    rep = NSA_HEADS // g
    row = lambda bb, i: (i, 0)
    const = lambda bb, i: (0, 0)
    seq = lambda n: pl.BlockSpec((1, n, ts, LANES), lambda bb, i: (bb, 0, i, 0))
    blk5 = lambda n, k, r, cdim: pl.BlockSpec((1, n, k, r, cdim), lambda bb, i: (bb, 0, i, 0, 0))
    return pl.pallas_call(
        _prep_even_kernel,
        grid=(b, nb),
        in_specs=[
            pl.BlockSpec((ts, c), lambda bb, i: (bb * nb + i, 0)),
            pl.BlockSpec((ts, LANES), row), pl.BlockSpec((ts, LANES), row),
            pl.BlockSpec((ts, LANES), row), pl.BlockSpec((ts, LANES), row),
            pl.BlockSpec((4, LANES), const), pl.BlockSpec((2, LANES), const), pl.BlockSpec((1, LANES), const),
        ],
        out_specs=[
            blk5(g, ts // TQ_SPARSE, rep * TQ_SPARSE, LANES),
            seq(g), seq(g),
            blk5(g, 1, LANES, VT_BLOCK),
            blk5(g, ts // LANES, LANES, LANES),
            pl.BlockSpec((1, 2, g, ts, LANES), lambda bb, i: (bb, 0, 0, i, 0)),
            blk5(DIFF_HEADS, 1, 2 * TQ_DIFF, LANES),
            seq(DIFF_HEADS),
            blk5(DIFF_HEADS, 1, LANES, VT_BLOCK),
            seq(MEM_HEADS),
        ],
        out_shape=[
            jax.ShapeDtypeStruct((b, g, s // TQ_SPARSE, rep * TQ_SPARSE, LANES), BF16),
            jax.ShapeDtypeStruct((b, g, s, LANES), BF16),
            jax.ShapeDtypeStruct((b, g, s, LANES), BF16),
            jax.ShapeDtypeStruct((b, g, s // VT_BLOCK, LANES, VT_BLOCK), BF16),
            jax.ShapeDtypeStruct((b, g, s // LANES, LANES, LANES), BF16),
            jax.ShapeDtypeStruct((b, 2, g, s, LANES), F32),
            jax.ShapeDtypeStruct((b, DIFF_HEADS, s // TQ_DIFF, 2 * TQ_DIFF, LANES), BF16),
            jax.ShapeDtypeStruct((b, DIFF_HEADS, s, LANES), BF16),
            jax.ShapeDtypeStruct((b, DIFF_HEADS, s // VT_BLOCK, LANES, VT_BLOCK), BF16),
            jax.ShapeDtypeStruct((b, MEM_HEADS, s, LANES), BF16),
        ],
        compiler_params=_cparams(("arbitrary", "arbitrary")),
        name="prep_even",
    )(proj, cos, sin, cos64, sin64, nsa_gain, diff_gain2, mem_gain)


def _compress_kernel(x_ref, pe_ref, w1_ref, w2_ref, g_ref, cos_ref, sin_ref, kc_ref, vct_ref, pad_ref, *, s, n_pad):
    for kind in range(2):
        pad_ref[0:s, :] = x_ref[0, kind, 0]
        pad_ref[s:s + NSA_CMP_LEN, :] = jnp.zeros((NSA_CMP_LEN, LANES), F32)
        acc = jnp.zeros((n_pad, NSA_CMP_HIDDEN), F32)
        for l in range(NSA_CMP_LEN):
            rows = pad_ref[pl.ds(l, n_pad, stride=NSA_CMP_STRIDE), :] + pe_ref[kind, l:l + 1, :]
            acc = acc + _dot(rows.astype(BF16), w1_ref[kind, l])
        out = _dot(_silu(acc).astype(BF16), w2_ref[kind])
        if kind == 0:
            kc_ref[0, 0] = _rope(_rms128(out, g_ref[...]), cos_ref[...], sin_ref[...], 64).astype(BF16)
        else:
            vct_ref[0, 0] = out.T.astype(BF16)


def _compress(craw, pe, w1, w2, gain, cos_c, sin_c):
    b, _, g, s, _ = craw.shape
    n_pad = s // NSA_CMP_STRIDE
    kern = functools.partial(_compress_kernel, s=s, n_pad=n_pad)
    whole = lambda shape: pl.BlockSpec(shape, lambda bb, gg: (0,) * len(shape))
    return pl.pallas_call(
        kern,
        grid=(b, g),
        in_specs=[
            pl.BlockSpec((1, 2, 1, s, LANES), lambda bb, gg: (bb, 0, gg, 0, 0)),
            whole((2, NSA_CMP_LEN, LANES)),
            whole((2, NSA_CMP_LEN, LANES, NSA_CMP_HIDDEN)),
            whole((2, NSA_CMP_HIDDEN, LANES)),
            whole((1, LANES)), whole((n_pad, LANES)), whole((n_pad, LANES)),
        ],
        out_specs=[
            pl.BlockSpec((1, 1, n_pad, LANES), lambda bb, gg: (bb, gg, 0, 0)),
            pl.BlockSpec((1, 1, LANES, n_pad), lambda bb, gg: (bb, gg, 0, 0)),
        ],
        out_shape=[
            jax.ShapeDtypeStruct((b, g, n_pad, LANES), BF16),
            jax.ShapeDtypeStruct((b, g, LANES, n_pad), BF16),
        ],
        scratch_shapes=[pltpu.VMEM((s + NSA_CMP_LEN, LANES), F32)],
        compiler_params=_cparams(("arbitrary", "arbitrary")),
        name="nsa_compress",
    )(craw, pe, w1, w2, gain, cos_c, sin_c)


def _masked_softmax_t(s, mask):
    s = jnp.where(mask, s, MASKED)
    e = jnp.exp(s - _col_max(s)) * mask.astype(F32)
    return e / jnp.maximum(_col_sum(e), 1e-30)


def _online_init(m_ref, l_ref, acc_ref):
    m_ref[...] = jnp.full(m_ref.shape, NEG_INIT, F32)
    l_ref[...] = jnp.zeros(l_ref.shape, F32)
    acc_ref[...] = jnp.zeros(acc_ref.shape, F32)


def _online_step(idx, st, pv, m_ref, l_ref, acc_ref, bounded):
    if bounded:
        p = jnp.exp2(st)
        l_ref[idx] = l_ref[idx] + _fold8(p, jnp.sum)
        acc_ref[idx] = acc_ref[idx] + pv(p.astype(BF16))
        return
    m_old = m_ref[idx]
    m_new = jnp.maximum(m_old, _col_max(st))
    alpha = jnp.exp2(m_old - m_new)
    p = jnp.exp2(st - m_new)
    l_ref[idx] = alpha * l_ref[idx] + _fold8(p, jnp.sum)
    acc_ref[idx] = alpha * acc_ref[idx] + pv(p.astype(BF16))
    m_ref[idx] = m_new


def _online_result(idx, l_ref, acc_ref):
    return acc_ref[idx] / jnp.maximum(jnp.sum(l_ref[idx], axis=0, keepdims=True), 1e-30)


def _flash_pairs(n_chunks, qk, soft, sa_ref, sb_ref):
    qk(0, sa_ref)
    n_pairs = (n_chunks - 1) // 2

    def pair(j, carry):
        a = 2 * j
        qk(a + 1, sb_ref)
        soft(a, sa_ref, False)
        qk(a + 2, sa_ref)
        soft(a + 1, sb_ref, False)
        return carry

    lax.fori_loop(0, n_pairs, pair, 0)
    e = 2 * n_pairs
    qk(e + 1, sb_ref)
    soft(e, sa_ref, True)
    soft(e + 1, sb_ref, True)


def _pv_blocks(vt_ref, lead, chunk, tk):
    per = tk // VT_BLOCK

    def pv(p):
        out = _dot(vt_ref[lead + (chunk * per,)], p[0:VT_BLOCK])
        for i in range(1, per):
            out = out + _dot(vt_ref[lead + (chunk * per + i,)], p[i * VT_BLOCK:(i + 1) * VT_BLOCK])
        return out

    return pv


def _top_n_mask_t(scores, n):
    row = lax.broadcasted_iota(jnp.int32, scores.shape, 0).astype(F32)
    height = float(scores.shape[0])
    work = scores
    sel = jnp.zeros(scores.shape, F32)
    for _ in range(n):
        m = jnp.max(work, axis=0, keepdims=True)
        first = jnp.min(jnp.where(work == m, row, height), axis=0, keepdims=True)
        pick = row == first
        sel = jnp.where(pick, 1.0, sel)
        work = jnp.where(pick, -jnp.inf, work)
    return sel


def _nsa_kernel(q_ref, kc_ref, vct_ref, ks_ref, vst_ref, kw_ref, vwt_ref, ovlt_ref, gl_ref, z_ref, o_ref,
                m_ref, l_ref, acc_ref, sa_ref, sb_ref, selt_ref, oct_ref, owt_ref, *, tq, tk, top_n, bounded):
    rep = NSA_HEADS // NSA_KV_GROUPS
    w = rep * tq
    scale = HEAD_DIM ** -0.5
    q0 = pl.program_id(2) * tq
    t_row = q0 + (lax.broadcasted_iota(jnp.int32, (1, w), 1) & (tq - 1))
    t_one = t_row[:, 0:tq]
    q = q_ref[0, 0, 0]

    n_cmp_pad = kc_ref.shape[2]
    cmp_last = lax.broadcasted_iota(jnp.int32, (n_cmp_pad, w), 0) * NSA_CMP_STRIDE + (NSA_CMP_LEN - 1)
    p = _masked_softmax_t(_dot_nt(kc_ref[0, 0], q) * scale, cmp_last <= t_row).astype(BF16)
    oct_ref[...] = _dot(vct_ref[0, 0], p)
    imp_heads = _dot(ovlt_ref[...], p)
    imp = imp_heads[:, 0:tq]
    for r in range(1, rep):
        imp = imp + imp_heads[:, r * tq:(r + 1) * tq]

    n_sel = imp.shape[0]
    sel_shift = NSA_SEL_LEN.bit_length() - 1
    j = lax.broadcasted_iota(jnp.int32, (n_sel, tq), 0)
    cur = jnp.right_shift(t_one, sel_shift)
    visible = j <= cur
    forced = (j == 0) | (j >= cur - 1)
    imp = jnp.where(visible, jnp.where(forced, FORCED, imp), MASKED)
    selt_ref[...] = _top_n_mask_t(imp, top_n)

    span = NSA_WINDOW + tq
    start = pl.multiple_of(jnp.maximum(q0 - NSA_WINDOW, 0), tq)
    s_pos = start + lax.broadcasted_iota(jnp.int32, (span, w), 0)
    mask_w = (s_pos <= t_row) & (s_pos > t_row - NSA_WINDOW)
    p = _masked_softmax_t(_dot_nt(kw_ref[0, 0, pl.ds(start, span), :], q) * scale, mask_w).astype(BF16)
    blk0 = start // LANES
    ow = _dot(vwt_ref[0, 0, blk0], p[0:LANES])
    for i in range(1, span // LANES):
        ow = ow + _dot(vwt_ref[0, 0, blk0 + i], p[i * LANES:(i + 1) * LANES])
    owt_ref[...] = ow

    _online_init(m_ref, l_ref, acc_ref)
    n_chunks = (q0 + tq + tk - 1) // tk
    blocks_per_chunk = tk // NSA_SEL_LEN

    def qk(c, dst):
        k0 = pl.multiple_of(jnp.minimum(c, n_chunks - 1) * tk, tk)
        dst[0] = _dot_nt(ks_ref[0, 0, pl.ds(k0, tk), :], q) * (scale * LOG2E)

    def soft(c, src, tail):
        cc = jnp.minimum(c, n_chunks - 1)
        rows = selt_ref[pl.ds(pl.multiple_of(cc * blocks_per_chunk, blocks_per_chunk), blocks_per_chunk), :]
        ok = jnp.concatenate(
            [jnp.broadcast_to(rows[i:i + 1, :], (NSA_SEL_LEN, tq)) for i in range(blocks_per_chunk)], axis=0) > 0.5
        if tail:
            ok = ok & ((c * tk + lax.broadcasted_iota(jnp.int32, (tk, tq), 0)) <= t_one)
        bias = jnp.where(ok, 0.0, NEG_MASK)
        st = src[0] + jnp.concatenate([bias] * rep, axis=1)
        _online_step(0, st, _pv_blocks(vst_ref, (0, 0), cc, tk), m_ref, l_ref, acc_ref, bounded)

    _flash_pairs(n_chunks, qk, soft, sa_ref, sb_ref)

    gates_t = jax.nn.sigmoid(gl_ref[...]).T
    os_t = _online_result(0, l_ref, acc_ref)
    for r in range(rep):
        sl = slice(r * tq, (r + 1) * tq)
        out_t = (gates_t[3 * r:3 * r + 1, :] * oct_ref[:, sl] + gates_t[3 * r + 1:3 * r + 2, :] * os_t[:, sl]
                 + gates_t[3 * r + 2:3 * r + 3, :] * owt_ref[:, sl])
        o_ref[:, r * LANES:(r + 1) * LANES] = (out_t.T * _silu(z_ref[:, r * LANES:(r + 1) * LANES])).astype(BF16)


def _nsa(qa, ks, vst, kw, vwt, kc, vct, ovlt, proj, b, s, bounded):
    tq, tk = TQ_SPARSE, TK
    nq = s // tq
    g = NSA_KV_GROUPS
    rep = NSA_HEADS // g
    w = rep * tq
    n_cmp_pad = kc.shape[2]
    n_sel = s // NSA_SEL_LEN
    kern = functools.partial(_nsa_kernel, tq=tq, tk=tk, top_n=min(NSA_SEL_TOPN, n_sel), bounded=bounded)
    per_group = lambda shape: pl.BlockSpec((1, 1) + shape, lambda bb, gg, i: (bb, gg) + (0,) * len(shape))
    return pl.pallas_call(
        kern,
        grid=(b, g, nq),
        in_specs=[
            pl.BlockSpec((1, 1, 1, w, LANES), lambda bb, gg, i: (bb, gg, i, 0, 0)),
            per_group((n_cmp_pad, LANES)), per_group((LANES, n_cmp_pad)),
            per_group((s, LANES)), per_group((s // VT_BLOCK, LANES, VT_BLOCK)),
            per_group((s, LANES)), per_group((s // LANES, LANES, LANES)),
            pl.BlockSpec((n_sel, n_cmp_pad), lambda bb, gg, i: (0, 0)),
            pl.BlockSpec((tq, LANES), lambda bb, gg, i: (bb * nq + i, E_AG + gg)),
            pl.BlockSpec((tq, rep * LANES), lambda bb, gg, i: (bb * nq + i, E_AZ // rep + gg)),
        ],
        out_specs=pl.BlockSpec((tq, rep * LANES), lambda bb, gg, i: (bb * nq + i, gg)),
        out_shape=jax.ShapeDtypeStruct((b * s, NSA_HEADS * LANES), BF16),
        scratch_shapes=[
            pltpu.VMEM((1, 1, w), F32), pltpu.VMEM((1, SUBLANES, w), F32), pltpu.VMEM((1, LANES, w), F32),
            pltpu.VMEM((1, tk, w), F32), pltpu.VMEM((1, tk, w), F32),
            pltpu.VMEM((n_sel, tq), F32), pltpu.VMEM((LANES, w), F32), pltpu.VMEM((LANES, w), F32),
        ],
        compiler_params=_cparams(("arbitrary", "arbitrary", "arbitrary")),
        name="nsa_attention",
    )(qa, kc, vct, ks, vst, kw, vwt, ovlt, proj, proj)


def _diff_kernel(q_ref, k_ref, vt_ref, lam_ref, sg_ref, z_ref, o_ref, m_ref, l_ref, acc_ref, sa_ref, sb_ref,
                 *, tq, tk, lambda_init, bounded):
    w = 2 * tq
    q0 = pl.program_id(2) * tq
    t_row = q0 + (lax.broadcasted_iota(jnp.int32, (1, w), 1) & (tq - 1))
    q = q_ref[0, 0, 0]
    _online_init(m_ref, l_ref, acc_ref)
    n_chunks = (q0 + tq + tk - 1) // tk

    def qk(c, dst):
        k0 = pl.multiple_of(jnp.minimum(c, n_chunks - 1) * tk, tk)
        dst[0] = _dot_nt(k_ref[0, 0, pl.ds(k0, tk), :], q) * (DIFF_QK_DIM ** -0.5 * LOG2E)

    def soft(c, src, tail):
        cc = jnp.minimum(c, n_chunks - 1)
        st = src[0]
        if tail:
            key = c * tk + lax.broadcasted_iota(jnp.int32, (tk, w), 0)
            st = jnp.where(key <= t_row, st, NEG_MASK)
        _online_step(0, st, _pv_blocks(vt_ref, (0, 0), cc, tk), m_ref, l_ref, acc_ref, bounded)

    _flash_pairs(n_chunks, qk, soft, sa_ref, sb_ref)

    lv = lam_ref[...]
    lam = (jnp.exp(jnp.sum(lv[0:1] * lv[1:2], axis=-1, keepdims=True))
           - jnp.exp(jnp.sum(lv[2:3] * lv[3:4], axis=-1, keepdims=True)) + lambda_init)
    o_t = _online_result(0, l_ref, acc_ref)
    o = (o_t[:, 0:tq] - lam * o_t[:, tq:w]).T
    o = _rms128(o, sg_ref[...]) * (1.0 - lambda_init)
    o_ref[...] = (o * _silu(z_ref[...])).astype(BF16)


def _diff(bq, bk, bvt, lam_vecs, subln_gain, proj, b, s, lambda_init, bounded):
    tq, tk = TQ_DIFF, TK
    nq = s // tq
    w = 2 * tq
    kern = functools.partial(_diff_kernel, tq=tq, tk=tk, lambda_init=lambda_init, bounded=bounded)
    return pl.pallas_call(
        kern,
        grid=(b, DIFF_HEADS, nq),
        in_specs=[
            pl.BlockSpec((1, 1, 1, w, LANES), lambda bb, h, i: (bb, h, i, 0, 0)),
            pl.BlockSpec((1, 1, s, LANES), lambda bb, h, i: (bb, h, 0, 0)),
            pl.BlockSpec((1, 1, s // VT_BLOCK, LANES, VT_BLOCK), lambda bb, h, i: (bb, h, 0, 0, 0)),
            pl.BlockSpec((4, DIFF_QK_DIM), lambda bb, h, i: (0, 0)),
            pl.BlockSpec((1, LANES), lambda bb, h, i: (0, 0)),
            pl.BlockSpec((tq, LANES), lambda bb, h, i: (bb * nq + i, E_BZ + h)),
        ],
        out_specs=pl.BlockSpec((tq, LANES), lambda bb, h, i: (bb * nq + i, h)),
        out_shape=jax.ShapeDtypeStruct((b * s, DIFF_HEADS * LANES), BF16),
        scratch_shapes=[
            pltpu.VMEM((1, 1, w), F32), pltpu.VMEM((1, SUBLANES, w), F32), pltpu.VMEM((1, LANES, w), F32),
            pltpu.VMEM((1, tk, w), F32), pltpu.VMEM((1, tk, w), F32),
        ],
        compiler_params=_cparams(("arbitrary", "arbitrary", "arbitrary")),
        name="diff_attention",
    )(bq, bk, bvt, lam_vecs, subln_gain, proj)


def _memkv_kernel(mem_ref, mg_ref, w_ref, kg_ref, k_ref, v_ref):
    mem_n = _rms128(mem_ref[0], mg_ref[...]).astype(BF16)
    kv = _dot(mem_n, w_ref[0])
    for h in range(MEM_HEADS):
        k_ref[0, 0, h] = _rms128(kv[:, h * LANES:(h + 1) * LANES], kg_ref[0]).astype(BF16)
        v_ref[0, 0, h] = kv[:, (MEM_HEADS + h) * LANES:(MEM_HEADS + h + 1) * LANES].astype(BF16)


def _memkv(mem, mem_gain, w_kv, k_gain):
    b, m, d = mem.shape
    depth = w_kv.shape[0]
    c = w_kv.shape[2]
    out = jax.ShapeDtypeStruct((depth, b, MEM_HEADS, m, LANES), BF16)
    ospec = pl.BlockSpec((1, 1, MEM_HEADS, m, LANES), lambda i, bb: (i, bb, 0, 0, 0))
    return pl.pallas_call(
        _memkv_kernel,
        grid=(depth, b),
        in_specs=[
            pl.BlockSpec((1, m, d), lambda i, bb: (bb, 0, 0)),
            pl.BlockSpec((1, d), lambda i, bb: (0, 0)),
            pl.BlockSpec((1, d, c), lambda i, bb: (i, 0, 0)),
            pl.BlockSpec((1, 1, LANES), lambda i, bb: (i, 0, 0)),
        ],
        out_specs=[ospec, ospec],
        out_shape=[out, out],
        compiler_params=_cparams(("arbitrary", "arbitrary")),
        name="mem_kv",
    )(mem, mem_gain.reshape(1, d), w_kv, k_gain)


def _memattn_kernel(q_ref, k_ref, v_ref, z_ref, o_ref):
    scale = HEAD_DIM ** -0.5
    for h in range(MEM_HEADS):
        s = _dot_nt(q_ref[0, h], k_ref[0, h]) * scale
        e = jnp.exp(s - jnp.max(s, axis=-1, keepdims=True))
        p = e / jnp.sum(e, axis=-1, keepdims=True)
        o = _dot(p.astype(BF16), v_ref[0, h])
        o_ref[:, h * LANES:(h + 1) * LANES] = (o * _silu(z_ref[:, h * LANES:(h + 1) * LANES])).astype(BF16)


def _memattn(mq, mk, mv, proj, b, s, z_unit):
    tq = min(512, s)
    nq = s // tq
    m = mk.shape[2]
    kvspec = pl.BlockSpec((1, MEM_HEADS, m, LANES), lambda bb, i: (bb, 0, 0, 0))
    return pl.pallas_call(
        _memattn_kernel,
        grid=(b, nq),
        in_specs=[
            pl.BlockSpec((1, MEM_HEADS, tq, LANES), lambda bb, i: (bb, 0, i, 0)),
            kvspec, kvspec,
            pl.BlockSpec((tq, MEM_HEADS * LANES), lambda bb, i: (bb * nq + i, z_unit // MEM_HEADS)),
        ],
        out_specs=pl.BlockSpec((tq, MEM_HEADS * LANES), lambda bb, i: (bb * nq + i, 0)),
        out_shape=jax.ShapeDtypeStruct((b * s, MEM_HEADS * LANES), BF16),
        compiler_params=_cparams(("arbitrary", "arbitrary")),
        name="mem_attention",
    )(mq, mk, mv, proj)


def _prep_odd_kernel(p_ref, cos_ref, sin_ref, cos64_ref, sin64_ref, cg_ref, mg_ref,
                     cq_ref, ck_ref, cvt_ref, iq_ref, ik_ref, mq_ref):
    cos, sin = cos_ref[...], sin_ref[...]
    cos64, sin64 = cos64_ref[...], sin64_ref[...]
    rep = DSA_HEADS // DSA_KV_HEADS
    tq = TQ_SPARSE
    n_qb = PREP_ROWS // tq

    def unit(u):
        return p_ref[:, u * LANES:(u + 1) * LANES]

    for h in range(DSA_HEADS):
        q = _rope(_rms128(unit(O_CQ + h), cg_ref[0:1, :]), cos, sin, 64).astype(BF16)
        g, r = divmod(h, rep)
        for qb in range(n_qb):
            cq_ref[0, g, qb, r * tq:(r + 1) * tq, :] = q[qb * tq:(qb + 1) * tq, :]
    for h in range(DSA_KV_HEADS):
        ck_ref[0, h] = _rope(_rms128(unit(O_CK + h), cg_ref[1:2, :]), cos, sin, 64).astype(BF16)
        cvt_ref[0, h, 0] = unit(O_CV + h).T.astype(BF16)
    lo = lax.broadcasted_iota(jnp.int32, cos.shape, 1) < 64
    for u in range(IDX_HEADS // 2):
        x = _rope(unit(O_IQ + u), cos64, sin64, 32)
        even = jnp.where(lo, x, 0.0).astype(BF16)
        odd = jnp.where(lo, pltpu.roll(x, 64, 1), 0.0).astype(BF16)
        for qb in range(n_qb):
            iq_ref[0, qb, (2 * u) * tq:(2 * u + 1) * tq, :] = even[qb * tq:(qb + 1) * tq, :]
            iq_ref[0, qb, (2 * u + 1) * tq:(2 * u + 2) * tq, :] = odd[qb * tq:(qb + 1) * tq, :]
    ik = _rope(unit(O_IKW), cos64, sin64, 32)
    ik_ref[0] = jnp.where(lo, ik, 0.0).astype(BF16)
    for h in range(MEM_HEADS):
        mq_ref[0, h] = _rms128(unit(O_MQ + h), mg_ref[...]).astype(BF16)


def _prep_odd(proj, b, s, cos, sin, cos64, sin64, dsa_gain, mem_gain):
    ts = PREP_ROWS
    nb = s // ts
    c = proj.shape[1]
    rep = DSA_HEADS // DSA_KV_HEADS
    row = lambda bb, i: (i, 0)
    const = lambda bb, i: (0, 0)
    seq = lambda n: pl.BlockSpec((1, n, ts, LANES), lambda bb, i: (bb, 0, i, 0))
    return pl.pallas_call(
        _prep_odd_kernel,
        grid=(b, nb),
        in_specs=[
            pl.BlockSpec((ts, c), lambda bb, i: (bb * nb + i, 0)),
            pl.BlockSpec((ts, LANES), row), pl.BlockSpec((ts, LANES), row),
            pl.BlockSpec((ts, LANES), row), pl.BlockSpec((ts, LANES), row),
            pl.BlockSpec((2, LANES), const), pl.BlockSpec((1, LANES), const),
        ],
        out_specs=[
            pl.BlockSpec((1, DSA_KV_HEADS, ts // TQ_SPARSE, rep * TQ_SPARSE, LANES), lambda bb, i: (bb, 0, i, 0, 0)),
            seq(DSA_KV_HEADS),
            pl.BlockSpec((1, DSA_KV_HEADS, 1, LANES, VT_BLOCK), lambda bb, i: (bb, 0, i, 0, 0)),
            pl.BlockSpec((1, ts // TQ_SPARSE, IDX_HEADS * TQ_SPARSE, LANES), lambda bb, i: (bb, i, 0, 0)),
            pl.BlockSpec((1, ts, LANES), lambda bb, i: (bb, i, 0)),
            seq(MEM_HEADS),
        ],
        out_shape=[
            jax.ShapeDtypeStruct((b, DSA_KV_HEADS, s // TQ_SPARSE, rep * TQ_SPARSE, LANES), BF16),
            jax.ShapeDtypeStruct((b, DSA_KV_HEADS, s, LANES), BF16),
            jax.ShapeDtypeStruct((b, DSA_KV_HEADS, s // VT_BLOCK, LANES, VT_BLOCK), BF16),
            jax.ShapeDtypeStruct((b, s // TQ_SPARSE, IDX_HEADS * TQ_SPARSE, LANES), BF16),
            jax.ShapeDtypeStruct((b, s, LANES), BF16),
            jax.ShapeDtypeStruct((b, MEM_HEADS, s, LANES), BF16),
        ],
        compiler_params=_cparams(("arbitrary", "arbitrary")),
        name="prep_odd",
    )(proj, cos, sin, cos64, sin64, dsa_gain, mem_gain)


def _sortable_key(x):
    bits = pltpu.bitcast(x + 0.0, jnp.int32)
    return jnp.where(bits < 0, bits ^ jnp.int32(0x7FFFFFFF), bits)


def _dsa_kernel(q_ref, k_ref, vt_ref, iq_ref, ik_ref, w_ref, ltri_ref, z_ref, o_ref,
                key_ref, m_ref, l_ref, acc_ref, sa_ref, sb_ref, ties_ref, *, tq, tk, top_k, bounded):
    rep = DSA_HEADS // DSA_KV_HEADS
    scale = HEAD_DIM ** -0.5
    q0 = pl.program_id(1) * tq
    t_row = q0 + lax.broadcasted_iota(jnp.int32, (1, tq), 1)
    n_chunks = (q0 + tq + tk - 1) // tk
    int_min = jnp.int32(-2147483648)
    heads_per_dot = 4

    w_t = (w_ref[...] * (IDX_HEADS ** -0.5 * IDX_DIM ** -0.5)).T

    def score_chunk(c, carry):
        k0 = pl.multiple_of(c * tk, tk)
        ik = ik_ref[0, pl.ds(k0, tk), :]
        sc = jnp.zeros((tk, tq), F32)
        for h0 in range(0, IDX_HEADS, heads_per_dot):
            x = _dot_nt(ik, iq_ref[0, 0, h0 * tq:(h0 + heads_per_dot) * tq, :])
            for hh in range(heads_per_dot):
                h = h0 + hh
                sc = sc + jnp.maximum(x[:, hh * tq:(hh + 1) * tq], 0.0) * w_t[IDX_DIM + h:IDX_DIM + h + 1, :]
        causal = (k0 + lax.broadcasted_iota(jnp.int32, (tk, tq), 0)) <= t_row
        key_ref[c] = _sortable_key(jnp.where(causal, sc, MASKED))
        return carry

    lax.fori_loop(0, n_chunks, score_chunk, 0)

    def count(pred):
        def body(c, acc):
            return acc + _fold8(pred(key_ref[c]).astype(F32), jnp.sum, short_chains=True)
        acc = lax.fori_loop(0, n_chunks, body, jnp.zeros((SUBLANES, tq), F32))
        return jnp.sum(acc, axis=0, keepdims=True)

    def search(it, thr_u):
        cand_u = thr_u | jnp.left_shift(jnp.int32(1), 31 - it)
        cand = cand_u ^ int_min
        return jnp.where(count(lambda kk: kk >= cand) >= top_k, cand_u, thr_u)

    thr = lax.fori_loop(0, 32, search, jnp.zeros((1, tq), jnp.int32)) ^ int_min
    budget = top_k - count(lambda kk: kk > thr)

    _online_init(m_ref, l_ref, acc_ref)
    ties_ref[...] = jnp.zeros(ties_ref.shape, F32)

    def qk(c, dst):
        k0 = pl.multiple_of(jnp.minimum(c, n_chunks - 1) * tk, tk)
        for g in range(DSA_KV_HEADS):
            dst[g] = _dot_nt(k_ref[0, g, pl.ds(k0, tk), :], q_ref[0, g, 0]) * (scale * LOG2E)

    def soft(c, src, tail):
        cc = jnp.minimum(c, n_chunks - 1)
        keys = key_ref[cc]
        tie = keys == thr
        rank = _dot(ltri_ref[...], tie.astype(BF16)) + ties_ref[...]
        ok = (keys > thr) | (tie & (rank < budget))
        if tail:
            ok = ok & ((c * tk + lax.broadcasted_iota(jnp.int32, (tk, tq), 0)) <= t_row)
        ties_ref[...] = ties_ref[...] + _col_sum(tie.astype(F32))
        bias = jnp.where(ok, 0.0, NEG_MASK)
        bias = jnp.concatenate([bias] * rep, axis=1)
        for g in range(DSA_KV_HEADS):
            _online_step(g, src[g] + bias, _pv_blocks(vt_ref, (0, g), cc, tk), m_ref, l_ref, acc_ref, bounded)

    _flash_pairs(n_chunks, qk, soft, sa_ref, sb_ref)

    for g in range(DSA_KV_HEADS):
        o_t = _online_result(g, l_ref, acc_ref)
        for r in range(rep):
            h = g * rep + r
            o = o_t[:, r * tq:(r + 1) * tq].T
            o_ref[:, h * LANES:(h + 1) * LANES] = (o * _silu(z_ref[:, h * LANES:(h + 1) * LANES])).astype(BF16)


def _dsa(cq, ck, cvt, iq, ik, ltri, proj, b, s, bounded):
    tq, tk = TQ_SPARSE, TK
    nq = s // tq
    g = DSA_KV_HEADS
    rep = DSA_HEADS // g
    w = rep * tq
    kern = functools.partial(_dsa_kernel, tq=tq, tk=tk, top_k=min(DSA_TOPK_MAX, s // 4), bounded=bounded)
    return pl.pallas_call(
        kern,
        grid=(b, nq),
        in_specs=[
            pl.BlockSpec((1, g, 1, w, LANES), lambda bb, i: (bb, 0, i, 0, 0)),
            pl.BlockSpec((1, g, s, LANES), lambda bb, i: (bb, 0, 0, 0), pipeline_mode=pl.Buffered(1)),
            pl.BlockSpec((1, g, s // VT_BLOCK, LANES, VT_BLOCK), lambda bb, i: (bb, 0, 0, 0, 0),
                         pipeline_mode=pl.Buffered(1)),
            pl.BlockSpec((1, 1, IDX_HEADS * tq, LANES), lambda bb, i: (bb, i, 0, 0)),
            pl.BlockSpec((1, s, LANES), lambda bb, i: (bb, 0, 0), pipeline_mode=pl.Buffered(1)),
            pl.BlockSpec((tq, LANES), lambda bb, i: (bb * nq + i, O_IKW)),
            pl.BlockSpec((tk, tk), lambda bb, i: (0, 0)),
            pl.BlockSpec((tq, DSA_HEADS * LANES), lambda bb, i: (bb * nq + i, O_CZ // DSA_HEADS)),
        ],
        out_specs=pl.BlockSpec((tq, DSA_HEADS * LANES), lambda bb, i: (bb * nq + i, 0)),
        out_shape=jax.ShapeDtypeStruct((b * s, DSA_HEADS * LANES), BF16),
        scratch_shapes=[
            pltpu.VMEM((s // tk, tk, tq), jnp.int32),
            pltpu.VMEM((g, 1, w), F32), pltpu.VMEM((g, SUBLANES, w), F32), pltpu.VMEM((g, LANES, w), F32),
            pltpu.VMEM((g, tk, w), F32), pltpu.VMEM((g, tk, w), F32),
            pltpu.VMEM((1, tq), F32),
        ],
        compiler_params=_cparams(("arbitrary", "arbitrary")),
        name="dsa_attention",
    )(cq, ck, cvt, iq, ik, proj, ltri, proj)


def _outproj_kernel(*refs, n_parts):
    x_ref = refs[0]
    y_refs = refs[1:1 + n_parts]
    w_refs = refs[1 + n_parts:1 + 2 * n_parts]
    o_ref = refs[1 + 2 * n_parts]
    acc = x_ref[...]
    for y_ref, w_ref in zip(y_refs, w_refs):
        acc = acc + _dot(y_ref[...], w_ref[...])
    o_ref[...] = acc


def _outproj(x2, ys, w_out):
    n, d = x2.shape
    tm = min(1024, n)
    tn = 1024
    widths = [y.shape[1] for y in ys]
    starts = np.cumsum([0] + widths[:-1]).tolist()
    ws = [w_out[st:st + wd] for st, wd in zip(starts, widths)]
    kern = functools.partial(_outproj_kernel, n_parts=len(ys))
    return pl.pallas_call(
        kern,
        grid=(d // tn, n // tm),
        in_specs=([pl.BlockSpec((tm, tn), lambda j, i: (i, j))]
                  + [pl.BlockSpec((tm, wd), lambda j, i: (i, 0)) for wd in widths]
                  + [pl.BlockSpec((wd, tn), lambda j, i: (0, j)) for wd in widths]),
        out_specs=pl.BlockSpec((tm, tn), lambda j, i: (i, j)),
        out_shape=jax.ShapeDtypeStruct((n, d), F32),
        compiler_params=_cparams(("arbitrary", "arbitrary")),
        name="out_proj",
    )(x2, *ys, *ws)


def _rope_tables(pos, half, reps):
    inv = ROPE_THETA ** (-jnp.arange(half, dtype=F32) / half)
    ang = pos.astype(F32)[:, None] * inv[None, :]
    cos, sin = jnp.cos(ang), jnp.sin(ang)
    return jnp.tile(jnp.concatenate([cos, cos], -1), (1, reps)), jnp.tile(jnp.concatenate([-sin, sin], -1), (1, reps))


def _split_cols(w, sizes):
    return jnp.split(w, np.cumsum(sizes)[:-1].tolist(), axis=-1)


def _even_weight(w):
    sizes = (1024, 1536, 24, 1024, 512, 512, 512, 512, 512, 512)
    a_q, a_kv, a_g, a_z, b_q, b_k, b_v, b_z, m_q, m_z = _split_cols(w, sizes)
    d = w.shape[0]
    per_group = 3 * NSA_HEADS // NSA_KV_GROUPS
    gates = [jnp.pad(a_g[:, g * per_group:(g + 1) * per_group], ((0, 0), (0, LANES - per_group)))
             for g in range(NSA_KV_GROUPS)]
    out = jnp.concatenate([a_q, a_kv, b_q, b_k, b_v, m_q, a_z, b_z, m_z] + gates, axis=-1)
    assert out.shape == (d, EVEN_UNITS * LANES)
    return out.astype(BF16)


def _odd_weight(w):
    sizes = (1536, 512, 512, 1024, 64, 16, 1536, 512, 512)
    c_q, c_k, c_v, i_q, i_k, i_w, c_z, m_q, m_z = _split_cols(w, sizes)
    d = w.shape[0]
    ikw = jnp.pad(jnp.concatenate([i_k, i_w], -1), ((0, 0), (0, LANES - IDX_DIM - IDX_HEADS)))
    out = jnp.concatenate([c_z, c_q, c_k, c_v, i_q, m_q, m_z, ikw, jnp.zeros((d, LANES), w.dtype)], axis=-1)
    assert out.shape == (d, ODD_UNITS * LANES)
    return out.astype(BF16)


def _overlap_matrix_t(s, n_cmp_pad):
    n_cmp = (s - NSA_CMP_LEN) // NSA_CMP_STRIDE + 1
    n_sel = s // NSA_SEL_LEN
    cmp_start = np.arange(n_cmp) * NSA_CMP_STRIDE
    sel_start = np.arange(n_sel) * NSA_SEL_LEN
    ov = np.clip(np.minimum(cmp_start[:, None] + NSA_CMP_LEN, sel_start[None, :] + NSA_SEL_LEN)
                 - np.maximum(cmp_start[:, None], sel_start[None, :]), 0, None) / NSA_CMP_LEN
    full = np.zeros((n_sel, n_cmp_pad), np.float32)
    full[:, :n_cmp] = ov.T
    return jnp.asarray(full, dtype=BF16)


def _score_bound(dim, gain_q, gain_k, scale):
    return dim * jnp.max(jnp.abs(gain_q)) * jnp.max(jnp.abs(gain_k)) * (scale * LOG2E * 1.02)


def _attend(bound, fn, *operands):
    return lax.cond(bound <= FAST_LOG2_BOUND,
                    lambda *a: fn(*a, bounded=True), lambda *a: fn(*a, bounded=False), *operands)


def kernel(x, mem, norm_gain, mem_norm_gain, mem_w_kv, mem_qk_gain, w_out, even_w_in, nsa_qk_gain, nsa_cmp_pos,
           nsa_cmp_w1, nsa_cmp_w2, diff_qk_gain, diff_lambda, diff_subln_gain, odd_w_in, dsa_qk_gain):
    b, s, d = x.shape
    assert d == D_MODEL and s % TK == 0 and s >= NSA_WINDOW + TQ_SPARSE
    pos = jnp.arange(s)
    cos, sin = _rope_tables(pos, HEAD_DIM // 2, 1)
    cos64, sin64 = _rope_tables(pos, DIFF_QK_DIM // 2, 2)
    n_cmp_pad = s // NSA_CMP_STRIDE
    cmp_last = jnp.arange(n_cmp_pad) * NSA_CMP_STRIDE + NSA_CMP_LEN - 1
    cos_c, sin_c = _rope_tables(cmp_last, HEAD_DIM // 2, 1)
    ovlt = _overlap_matrix_t(s, n_cmp_pad)
    ltri = jnp.asarray(np.tril(np.ones((TK, TK), np.float32), -1), dtype=BF16)

    mk_all, mv_all = _memkv(mem, mem_norm_gain, mem_w_kv.astype(BF16), mem_qk_gain[:, 1:2, :])
    w_out_b = w_out.astype(BF16)

    x2 = x.reshape(b * s, d)
    for i in range(DEPTH):
        mem_q_gain = mem_qk_gain[i, 0:1, :]
        if i % 2 == 0:
            e = i // 2
            proj = _proj(x2, norm_gain[i], _even_weight(even_w_in[e]), 768)
            dg2 = jnp.tile(diff_qk_gain[e], (1, 2))
            qa, ks, kw, vst, vwt, craw, bq, bk, bvt, mq = _prep_even(
                proj, b, s, cos, sin, cos64, sin64, nsa_qk_gain[e], dg2, mem_q_gain)
            w1 = nsa_cmp_w1[e].reshape(2, NSA_CMP_LEN, HEAD_DIM, NSA_CMP_HIDDEN).astype(BF16)
            kc, vct = _compress(craw, nsa_cmp_pos[e], w1, nsa_cmp_w2[e].astype(BF16),
                                nsa_qk_gain[e, 1:2, :], cos_c, sin_c)
            y_a = _attend(_score_bound(HEAD_DIM, nsa_qk_gain[e, 0], nsa_qk_gain[e, 2], HEAD_DIM ** -0.5),
                          functools.partial(_nsa, b=b, s=s), qa, ks, vst, kw, vwt, kc, vct, ovlt, proj)
            lambda_init = 0.8 - 0.6 * math.exp(-0.3 * i)
            y_b = _attend(_score_bound(DIFF_QK_DIM, diff_qk_gain[e, 0], diff_qk_gain[e, 1], DIFF_QK_DIM ** -0.5),
                          functools.partial(_diff, b=b, s=s, lambda_init=lambda_init),
                          bq, bk, bvt, diff_lambda[e], diff_subln_gain[e].reshape(1, LANES), proj)
            y_m = _memattn(mq, mk_all[i], mv_all[i], proj, b, s, E_MZ)
            ys = [y_a, y_b, y_m]
        else:
            o = i // 2
            proj = _proj(x2, norm_gain[i], _odd_weight(odd_w_in[o]), 640)
            cq, ck, cvt, iq, ik, mq = _prep_odd(proj, b, s, cos, sin, cos64, sin64, dsa_qk_gain[o], mem_q_gain)
            y_c = _attend(_score_bound(HEAD_DIM, dsa_qk_gain[o, 0], dsa_qk_gain[o, 1], HEAD_DIM ** -0.5),
                          functools.partial(_dsa, b=b, s=s), cq, ck, cvt, iq, ik, ltri, proj)
            y_m = _memattn(mq, mk_all[i], mv_all[i], proj, b, s, O_MZ)
            ys = [y_c, y_m]
        x2 = _outproj(x2, ys, w_out_b[i])
    return x2.reshape(b, s, d)
```

```python
import functools
import math

import jax
import jax.numpy as jnp
import numpy as np
from jax import lax
from jax.experimental import pallas as pl
from jax.experimental.pallas import tpu as pltpu

F32 = jnp.float32
BF16 = jnp.bfloat16

D_MODEL = 2048
DEPTH = 4
HEAD_DIM = 128
ROPE_THETA = 10000.0
EPS = 1e-6
MASKED = -1e30
FORCED = 1e9
NSA_HEADS = 8
NSA_KV_GROUPS = 2
NSA_CMP_LEN = 32
NSA_CMP_STRIDE = 16
NSA_CMP_HIDDEN = 256
NSA_SEL_LEN = 64
NSA_SEL_TOPN = 16
NSA_WINDOW = 512
DIFF_HEADS = 4
DIFF_QK_DIM = 64
DSA_HEADS = 12
DSA_KV_HEADS = 4
IDX_HEADS = 16
IDX_DIM = 64
DSA_TOPK_MAX = 256
MEM_HEADS = 4

LANES = 128
SUBLANES = 8
VMEM_LIMIT_BYTES = 56 * 1024 * 1024
LOG2E = 1.4426950408889634

PREP_ROWS = 256
TQ_SPARSE = 128
TQ_DIFF = 256
TK = 512
VT_BLOCK = 256
NEG_INIT = -1e30
NEG_MASK = -2e30
FAST_LOG2_BOUND = 40.0

EVEN_UNITS = 54
E_AQ, E_AKV, E_BQ, E_BK, E_BV, E_MQ, E_AZ, E_BZ, E_MZ, E_AG = 0, 8, 20, 24, 28, 32, 36, 44, 48, 52
ODD_UNITS = 50
O_CZ, O_CQ, O_CK, O_CV, O_IQ, O_MQ, O_MZ, O_IKW = 0, 12, 24, 28, 32, 40, 44, 48


def _cparams(sem):
    return pltpu.CompilerParams(dimension_semantics=sem, vmem_limit_bytes=VMEM_LIMIT_BYTES)


def _dot(a, b):
    return jnp.dot(a, b, preferred_element_type=F32)


def _dot_nt(a, b):
    return lax.dot_general(a, b, (((1,), (1,)), ((), ())), preferred_element_type=F32)


def _silu(x):
    return x * jax.nn.sigmoid(x)


def _rms128(x, gain):
    return x * lax.rsqrt(jnp.mean(x * x, axis=-1, keepdims=True) + EPS) * gain


def _rms64(x, gain):
    lo = lax.broadcasted_iota(jnp.int32, x.shape, 1) < 64
    xx = x * x
    s_lo = jnp.sum(jnp.where(lo, xx, 0.0), axis=-1, keepdims=True)
    s_hi = jnp.sum(jnp.where(lo, 0.0, xx), axis=-1, keepdims=True)
    ms = jnp.where(lo, s_lo, s_hi) * (1.0 / 64.0)
    return x * lax.rsqrt(ms + EPS) * gain


def _partner(x, half):
    n = x.shape[-1]
    lane = lax.broadcasted_iota(jnp.int32, x.shape, 1)
    if 2 * half == n:
        return pltpu.roll(x, half, 1)
    a = pltpu.roll(x, half, 1)
    b = pltpu.roll(x, n - half, 1)
    src_a = pltpu.roll(lane, half, 1)
    want = jnp.where((lane & (2 * half - 1)) < half, lane + half, lane - half)
    return jnp.where(src_a == want, a, b)


def _rope(x, cos, sin_signed, half):
    return x * cos + _partner(x, half) * sin_signed


def _fold8(x, op, short_chains=False):
    rows, w = x.shape
    if short_chains:
        x = op(x.reshape(SUBLANES, rows // SUBLANES, w), axis=0)
        rows = rows // SUBLANES
    return op(x.reshape(rows // SUBLANES, SUBLANES, w), axis=0)


def _col_max(x):
    return jnp.max(_fold8(x, jnp.max), axis=0, keepdims=True)


def _col_sum(x):
    return jnp.sum(_fold8(x, jnp.sum), axis=0, keepdims=True)


def _proj_kernel(x_ref, g_ref, w_ref, o_ref, hn_ref):
    @pl.when(pl.program_id(1) == 0)
    def _():
        x = x_ref[...]
        hn_ref[...] = _rms128(x, g_ref[...]).astype(BF16)

    o_ref[...] = _dot(hn_ref[...], w_ref[...])


def _proj(x2, gain, w, tn):
    n, d = x2.shape
    c = w.shape[1]
    tm = min(1024, n)
    return pl.pallas_call(
        _proj_kernel,
        grid=(n // tm, c // tn),
        in_specs=[
            pl.BlockSpec((tm, d), lambda i, j: (i, 0)),
            pl.BlockSpec((1, d), lambda i, j: (0, 0)),
            pl.BlockSpec((d, tn), lambda i, j: (0, j)),
        ],
        out_specs=pl.BlockSpec((tm, tn), lambda i, j: (i, j)),
        out_shape=jax.ShapeDtypeStruct((n, c), F32),
        scratch_shapes=[pltpu.VMEM((tm, d), BF16)],
        compiler_params=_cparams(("arbitrary", "arbitrary")),
        name="proj",
    )(x2, gain.reshape(1, d), w)


def _prep_even_kernel(p_ref, cos_ref, sin_ref, cos64_ref, sin64_ref, ng_ref, dg_ref, mg_ref,
                      qa_ref, ks_ref, kw_ref, vst_ref, vwt_ref, craw_ref, bq_ref, bk_ref, bvt_ref, mq_ref):
    cos, sin = cos_ref[...], sin_ref[...]
    cos64, sin64 = cos64_ref[...], sin64_ref[...]
    rep = NSA_HEADS // NSA_KV_GROUPS
    tq = TQ_SPARSE

    def unit(u):
        return p_ref[:, u * LANES:(u + 1) * LANES]

    for h in range(NSA_HEADS):
        q = _rope(_rms128(unit(E_AQ + h), ng_ref[0:1, :]), cos, sin, 64).astype(BF16)
        g, r = divmod(h, rep)
        for qb in range(PREP_ROWS // tq):
            qa_ref[0, g, qb, r * tq:(r + 1) * tq, :] = q[qb * tq:(qb + 1) * tq, :]
    for g in range(NSA_KV_GROUPS):
        craw_ref[0, 0, g] = unit(E_AKV + 0 + g)
        craw_ref[0, 1, g] = unit(E_AKV + 2 + g)
        ks_ref[0, g] = _rope(_rms128(unit(E_AKV + 4 + g), ng_ref[2:3, :]), cos, sin, 64).astype(BF16)
        vst_ref[0, g, 0] = unit(E_AKV + 6 + g).T.astype(BF16)
        kw_ref[0, g] = _rope(_rms128(unit(E_AKV + 8 + g), ng_ref[3:4, :]), cos, sin, 64).astype(BF16)
        vw = unit(E_AKV + 10 + g)
        for kb in range(PREP_ROWS // LANES):
            vwt_ref[0, g, kb] = vw[kb * LANES:(kb + 1) * LANES, :].T.astype(BF16)
    lo = lax.broadcasted_iota(jnp.int32, cos.shape, 1) < 64
    for h in range(DIFF_HEADS):
        q = _rope(_rms64(unit(E_BQ + h), dg_ref[0:1, :]), cos64, sin64, 32)
        bq_ref[0, h, 0, 0:PREP_ROWS, :] = jnp.where(lo, q, 0.0).astype(BF16)
        bq_ref[0, h, 0, PREP_ROWS:2 * PREP_ROWS, :] = jnp.where(lo, 0.0, q).astype(BF16)
        bk_ref[0, h] = _rope(_rms64(unit(E_BK + h), dg_ref[1:2, :]), cos64, sin64, 32).astype(BF16)
        bvt_ref[0, h, 0] = unit(E_BV + h).T.astype(BF16)
    for h in range(MEM_HEADS):
        mq_ref[0, h] = _rms128(unit(E_MQ + h), mg_ref[...]).astype(BF16)


def _prep_even(proj, b, s, cos, sin, cos64, sin64, nsa_gain, diff_gain2, mem_gain):
    ts = PREP_ROWS
    assert ts == TQ_DIFF == VT_BLOCK
    nb = s // ts
    c = proj.shape[1]
    g = NSA_KV_GROUPS
    rep = NSA_HEADS // g
    row = lambda bb, i: (i, 0)
    const = lambda bb, i: (0, 0)
    seq = lambda n: pl.BlockSpec((1, n, ts, LANES), lambda bb, i: (bb, 0, i, 0))
    blk5 = lambda n, k, r, cdim: pl.BlockSpec((1, n, k, r, cdim), lambda bb, i: (bb, 0, i, 0, 0))
    return pl.pallas_call(
        _prep_even_kernel,
        grid=(b, nb),
        in_specs=[
            pl.BlockSpec((ts, c), lambda bb, i: (bb * nb + i, 0)),
            pl.BlockSpec((ts, LANES), row), pl.BlockSpec((ts, LANES), row),
            pl.BlockSpec((ts, LANES), row), pl.BlockSpec((ts, LANES), row),
            pl.BlockSpec((4, LANES), const), pl.BlockSpec((2, LANES), const), pl.BlockSpec((1, LANES), const),
        ],
        out_specs=[
            blk5(g, ts // TQ_SPARSE, rep * TQ_SPARSE, LANES),
            seq(g), seq(g),
            blk5(g, 1, LANES, VT_BLOCK),
            blk5(g, ts // LANES, LANES, LANES),
            pl.BlockSpec((1, 2, g, ts, LANES), lambda bb, i: (bb, 0, 0, i, 0)),
            blk5(DIFF_HEADS, 1, 2 * TQ_DIFF, LANES),
            seq(DIFF_HEADS),
            blk5(DIFF_HEADS, 1, LANES, VT_BLOCK),
            seq(MEM_HEADS),
        ],
        out_shape=[
            jax.ShapeDtypeStruct((b, g, s // TQ_SPARSE, rep * TQ_SPARSE, LANES), BF16),
            jax.ShapeDtypeStruct((b, g, s, LANES), BF16),
            jax.ShapeDtypeStruct((b, g, s, LANES), BF16),
            jax.ShapeDtypeStruct((b, g, s // VT_BLOCK, LANES, VT_BLOCK), BF16),
            jax.ShapeDtypeStruct((b, g, s // LANES, LANES, LANES), BF16),
            jax.ShapeDtypeStruct((b, 2, g, s, LANES), F32),
            jax.ShapeDtypeStruct((b, DIFF_HEADS, s // TQ_DIFF, 2 * TQ_DIFF, LANES), BF16),
            jax.ShapeDtypeStruct((b, DIFF_HEADS, s, LANES), BF16),
            jax.ShapeDtypeStruct((b, DIFF_HEADS, s // VT_BLOCK, LANES, VT_BLOCK), BF16),
            jax.ShapeDtypeStruct((b, MEM_HEADS, s, LANES), BF16),
        ],
        compiler_params=_cparams(("arbitrary", "arbitrary")),
        name="prep_even",
    )(proj, cos, sin, cos64, sin64, nsa_gain, diff_gain2, mem_gain)


def _compress_kernel(x_ref, pe_ref, w1_ref, w2_ref, g_ref, cos_ref, sin_ref, kc_ref, vct_ref, pad_ref, *, s, n_pad):
    for kind in range(2):
        pad_ref[0:s, :] = x_ref[0, kind, 0]
        pad_ref[s:s + NSA_CMP_LEN, :] = jnp.zeros((NSA_CMP_LEN, LANES), F32)
        acc = jnp.zeros((n_pad, NSA_CMP_HIDDEN), F32)
        for l in range(NSA_CMP_LEN):
            rows = pad_ref[pl.ds(l, n_pad, stride=NSA_CMP_STRIDE), :] + pe_ref[kind, l:l + 1, :]
            acc = acc + _dot(rows.astype(BF16), w1_ref[kind, l])
        out = _dot(_silu(acc).astype(BF16), w2_ref[kind])
        if kind == 0:
            kc_ref[0, 0] = _rope(_rms128(out, g_ref[...]), cos_ref[...], sin_ref[...], 64).astype(BF16)
        else:
            vct_ref[0, 0] = out.T.astype(BF16)


def _compress(craw, pe, w1, w2, gain, cos_c, sin_c):
    b, _, g, s, _ = craw.shape
    n_pad = s // NSA_CMP_STRIDE
    kern = functools.partial(_compress_kernel, s=s, n_pad=n_pad)
    whole = lambda shape: pl.BlockSpec(shape, lambda bb, gg: (0,) * len(shape))
    return pl.pallas_call(
        kern,
        grid=(b, g),
        in_specs=[
            pl.BlockSpec((1, 2, 1, s, LANES), lambda bb, gg: (bb, 0, gg, 0, 0)),
            whole((2, NSA_CMP_LEN, LANES)),
            whole((2, NSA_CMP_LEN, LANES, NSA_CMP_HIDDEN)),
            whole((2, NSA_CMP_HIDDEN, LANES)),
            whole((1, LANES)), whole((n_pad, LANES)), whole((n_pad, LANES)),
        ],
        out_specs=[
            pl.BlockSpec((1, 1, n_pad, LANES), lambda bb, gg: (bb, gg, 0, 0)),
            pl.BlockSpec((1, 1, LANES, n_pad), lambda bb, gg: (bb, gg, 0, 0)),
        ],
        out_shape=[
            jax.ShapeDtypeStruct((b, g, n_pad, LANES), BF16),
            jax.ShapeDtypeStruct((b, g, LANES, n_pad), BF16),
        ],
        scratch_shapes=[pltpu.VMEM((s + NSA_CMP_LEN, LANES), F32)],
        compiler_params=_cparams(("arbitrary", "arbitrary")),
        name="nsa_compress",
    )(craw, pe, w1, w2, gain, cos_c, sin_c)


def _softmax_parts_t(st, bounded):
    if not bounded:
        st = st - _col_max(st)
    e = jnp.exp2(st)
    return e.astype(BF16), 1.0 / jnp.maximum(_col_sum(e), 1e-30)


def _online_init(m_ref, l_ref, acc_ref):
    m_ref[...] = jnp.full(m_ref.shape, NEG_INIT, F32)
    l_ref[...] = jnp.zeros(l_ref.shape, F32)
    acc_ref[...] = jnp.zeros(acc_ref.shape, F32)


def _online_step(idx, st, pv, m_ref, l_ref, acc_ref, bounded):
    if bounded:
        p = jnp.exp2(st)
        l_ref[idx] = l_ref[idx] + _fold8(p, jnp.sum)
        acc_ref[idx] = acc_ref[idx] + pv(p.astype(BF16))
        return
    m_old = m_ref[idx]
    m_new = jnp.maximum(m_old, _col_max(st))
    alpha = jnp.exp2(m_old - m_new)
    p = jnp.exp2(st - m_new)
    l_ref[idx] = alpha * l_ref[idx] + _fold8(p, jnp.sum)
    acc_ref[idx] = alpha * acc_ref[idx] + pv(p.astype(BF16))
    m_ref[idx] = m_new


def _online_result(idx, l_ref, acc_ref):
    return acc_ref[idx] / jnp.maximum(jnp.sum(l_ref[idx], axis=0, keepdims=True), 1e-30)


def _flash_pairs(n_chunks, qk, soft, sa_ref, sb_ref):
    qk(0, sa_ref)
    n_pairs = (n_chunks - 1) // 2

    def pair(j, carry):
        a = 2 * j
        qk(a + 1, sb_ref)
        soft(a, sa_ref, False)
        qk(a + 2, sa_ref)
        soft(a + 1, sb_ref, False)
        return carry

    lax.fori_loop(0, n_pairs, pair, 0)
    e = 2 * n_pairs
    qk(e + 1, sb_ref)
    soft(e, sa_ref, True)
    soft(e + 1, sb_ref, True)


def _pv_blocks(vt_ref, lead, chunk, tk):
    per = tk // VT_BLOCK

    def pv(p):
        out = _dot(vt_ref[lead + (chunk * per,)], p[0:VT_BLOCK])
        for i in range(1, per):
            out = out + _dot(vt_ref[lead + (chunk * per + i,)], p[i * VT_BLOCK:(i + 1) * VT_BLOCK])
        return out

    return pv


def _top_n_mask_t(scores, n):
    row = lax.broadcasted_iota(jnp.int32, scores.shape, 0).astype(F32)
    height = float(scores.shape[0])
    work = scores
    sel = jnp.zeros(scores.shape, F32)
    for _ in range(n):
        m = jnp.max(work, axis=0, keepdims=True)
        first = jnp.min(jnp.where(work == m, row, height), axis=0, keepdims=True)
        pick = row == first
        sel = jnp.where(pick, 1.0, sel)
        work = jnp.where(pick, -jnp.inf, work)
    return sel


def _nsa_kernel(q_ref, kc_ref, vct_ref, ks_ref, vst_ref, kw_ref, vwt_ref, ovlt_ref, wbias_ref, gl_ref, z_ref, o_ref,
                m_ref, l_ref, acc_ref, sa_ref, sb_ref, selt_ref, oct_ref, owt_ref, *, tq, tk, top_n, bounded):
    rep = NSA_HEADS // NSA_KV_GROUPS
    w = rep * tq
    scale = HEAD_DIM ** -0.5
    q0 = pl.program_id(2) * tq
    t_row = q0 + (lax.broadcasted_iota(jnp.int32, (1, w), 1) & (tq - 1))
    t_one = t_row[:, 0:tq]
    q = q_ref[0, 0, 0]

    n_cmp_pad = kc_ref.shape[2]
    cmp_last = lax.broadcasted_iota(jnp.int32, (n_cmp_pad, w), 0) * NSA_CMP_STRIDE + (NSA_CMP_LEN - 1)
    st = jnp.where(cmp_last <= t_row, _dot_nt(kc_ref[0, 0], q) * (scale * LOG2E), NEG_MASK)
    p, inv = _softmax_parts_t(st, bounded)
    inv = jnp.where(t_row >= NSA_CMP_LEN - 1, inv, 0.0)
    oct_ref[...] = _dot(vct_ref[0, 0], p) * inv
    imp_heads = _dot(ovlt_ref[...], p) * inv
    imp = imp_heads[:, 0:tq]
    for r in range(1, rep):
        imp = imp + imp_heads[:, r * tq:(r + 1) * tq]

    n_sel = imp.shape[0]
    sel_shift = NSA_SEL_LEN.bit_length() - 1
    j = lax.broadcasted_iota(jnp.int32, (n_sel, tq), 0)
    cur = jnp.right_shift(t_one, sel_shift)
    visible = j <= cur
    forced = (j == 0) | (j >= cur - 1)
    imp = jnp.where(visible, jnp.where(forced, FORCED, imp), MASKED)
    selt_ref[...] = _top_n_mask_t(imp, top_n)

    span = NSA_WINDOW + tq
    start = pl.multiple_of(jnp.maximum(q0 - NSA_WINDOW, 0), tq)
    st = (_dot_nt(kw_ref[0, 0, pl.ds(start, span), :], q) * (scale * LOG2E)
          + jnp.concatenate([wbias_ref[0]] * rep, axis=1))
    p, inv = _softmax_parts_t(st, bounded)
    blk0 = start // LANES
    ow = _dot(vwt_ref[0, 0, blk0], p[0:LANES])
    for i in range(1, span // LANES):
        ow = ow + _dot(vwt_ref[0, 0, blk0 + i], p[i * LANES:(i + 1) * LANES])
    owt_ref[...] = ow * inv

    _online_init(m_ref, l_ref, acc_ref)
    n_chunks = (q0 + tq + tk - 1) // tk
    blocks_per_chunk = tk // NSA_SEL_LEN

    def qk(c, dst):
        k0 = pl.multiple_of(jnp.minimum(c, n_chunks - 1) * tk, tk)
        dst[0] = _dot_nt(ks_ref[0, 0, pl.ds(k0, tk), :], q) * (scale * LOG2E)

    def soft(c, src, tail):
        cc = jnp.minimum(c, n_chunks - 1)
        rows = selt_ref[pl.ds(pl.multiple_of(cc * blocks_per_chunk, blocks_per_chunk), blocks_per_chunk), :]
        ok = jnp.concatenate(
            [jnp.broadcast_to(rows[i:i + 1, :], (NSA_SEL_LEN, tq)) for i in range(blocks_per_chunk)], axis=0) > 0.5
        if tail:
            ok = ok & ((c * tk + lax.broadcasted_iota(jnp.int32, (tk, tq), 0)) <= t_one)
        bias = jnp.where(ok, 0.0, NEG_MASK)
        st = src[0] + jnp.concatenate([bias] * rep, axis=1)
        _online_step(0, st, _pv_blocks(vst_ref, (0, 0), cc, tk), m_ref, l_ref, acc_ref, bounded)

    _flash_pairs(n_chunks, qk, soft, sa_ref, sb_ref)

    gates_t = jax.nn.sigmoid(gl_ref[...]).T
    os_t = _online_result(0, l_ref, acc_ref)
    for r in range(rep):
        sl = slice(r * tq, (r + 1) * tq)
        out_t = (gates_t[3 * r:3 * r + 1, :] * oct_ref[:, sl] + gates_t[3 * r + 1:3 * r + 2, :] * os_t[:, sl]
                 + gates_t[3 * r + 2:3 * r + 3, :] * owt_ref[:, sl])
        o_ref[:, r * LANES:(r + 1) * LANES] = (out_t.T * _silu(z_ref[:, r * LANES:(r + 1) * LANES])).astype(BF16)


def _window_bias(tq):
    span = NSA_WINDOW + tq
    n_early = NSA_WINDOW // tq
    row = np.arange(span)[:, None]
    lane = np.arange(tq)[None, :]
    tables = []
    for i in range(n_early + 1):
        t = i * tq + lane
        s_pos = (0 if i < n_early else t[0, 0] - NSA_WINDOW) + row
        ok = (s_pos <= t) & (s_pos > t - NSA_WINDOW)
        tables.append(np.where(ok, 0.0, NEG_MASK))
    return jnp.asarray(np.stack(tables), dtype=F32)


def _nsa(qa, ks, vst, kw, vwt, kc, vct, ovlt, wbias, proj, b, s, bounded):
    tq, tk = TQ_SPARSE, TK
    nq = s // tq
    g = NSA_KV_GROUPS
    rep = NSA_HEADS // g
    w = rep * tq
    n_cmp_pad = kc.shape[2]
    n_sel = s // NSA_SEL_LEN
    kern = functools.partial(_nsa_kernel, tq=tq, tk=tk, top_n=min(NSA_SEL_TOPN, n_sel), bounded=bounded)
    per_group = lambda shape: pl.BlockSpec((1, 1) + shape, lambda bb, gg, i: (bb, gg) + (0,) * len(shape))
    return pl.pallas_call(
        kern,
        grid=(b, g, nq),
        in_specs=[
            pl.BlockSpec((1, 1, 1, w, LANES), lambda bb, gg, i: (bb, gg, i, 0, 0)),
            per_group((n_cmp_pad, LANES)), per_group((LANES, n_cmp_pad)),
            per_group((s, LANES)), per_group((s // VT_BLOCK, LANES, VT_BLOCK)),
            per_group((s, LANES)), per_group((s // LANES, LANES, LANES)),
            pl.BlockSpec((n_sel, n_cmp_pad), lambda bb, gg, i: (0, 0)),
            pl.BlockSpec((1,) + wbias.shape[1:], lambda bb, gg, i: (jnp.minimum(i, wbias.shape[0] - 1), 0, 0)),
            pl.BlockSpec((tq, LANES), lambda bb, gg, i: (bb * nq + i, E_AG + gg)),
            pl.BlockSpec((tq, rep * LANES), lambda bb, gg, i: (bb * nq + i, E_AZ // rep + gg)),
        ],
        out_specs=pl.BlockSpec((tq, rep * LANES), lambda bb, gg, i: (bb * nq + i, gg)),
        out_shape=jax.ShapeDtypeStruct((b * s, NSA_HEADS * LANES), BF16),
        scratch_shapes=[
            pltpu.VMEM((1, 1, w), F32), pltpu.VMEM((1, SUBLANES, w), F32), pltpu.VMEM((1, LANES, w), F32),
            pltpu.VMEM((1, tk, w), F32), pltpu.VMEM((1, tk, w), F32),
            pltpu.VMEM((n_sel, tq), F32), pltpu.VMEM((LANES, w), F32), pltpu.VMEM((LANES, w), F32),
        ],
        compiler_params=_cparams(("arbitrary", "arbitrary", "arbitrary")),
        name="nsa_attention",
    )(qa, kc, vct, ks, vst, kw, vwt, ovlt, wbias, proj, proj)


def _diff_kernel(q_ref, k_ref, vt_ref, lam_ref, sg_ref, z_ref, o_ref, m_ref, l_ref, acc_ref, sa_ref, sb_ref,
                 *, tq, tk, lambda_init, bounded):
    w = 2 * tq
    q0 = pl.program_id(2) * tq
    t_row = q0 + (lax.broadcasted_iota(jnp.int32, (1, w), 1) & (tq - 1))
    q = q_ref[0, 0, 0]
    _online_init(m_ref, l_ref, acc_ref)
    n_chunks = (q0 + tq + tk - 1) // tk

    def qk(c, dst):
        k0 = pl.multiple_of(jnp.minimum(c, n_chunks - 1) * tk, tk)
        dst[0] = _dot_nt(k_ref[0, 0, pl.ds(k0, tk), :], q) * (DIFF_QK_DIM ** -0.5 * LOG2E)

    def soft(c, src, tail):
        cc = jnp.minimum(c, n_chunks - 1)
        st = src[0]
        if tail:
            key = c * tk + lax.broadcasted_iota(jnp.int32, (tk, w), 0)
            st = jnp.where(key <= t_row, st, NEG_MASK)
        _online_step(0, st, _pv_blocks(vt_ref, (0, 0), cc, tk), m_ref, l_ref, acc_ref, bounded)

    _flash_pairs(n_chunks, qk, soft, sa_ref, sb_ref)

    lv = lam_ref[...]
    lam = (jnp.exp(jnp.sum(lv[0:1] * lv[1:2], axis=-1, keepdims=True))
           - jnp.exp(jnp.sum(lv[2:3] * lv[3:4], axis=-1, keepdims=True)) + lambda_init)
    o_t = _online_result(0, l_ref, acc_ref)
    o = (o_t[:, 0:tq] - lam * o_t[:, tq:w]).T
    o = _rms128(o, sg_ref[...]) * (1.0 - lambda_init)
    o_ref[...] = (o * _silu(z_ref[...])).astype(BF16)


def _diff(bq, bk, bvt, lam_vecs, subln_gain, proj, b, s, lambda_init, bounded):
    tq, tk = TQ_DIFF, TK
    nq = s // tq
    w = 2 * tq
    kern = functools.partial(_diff_kernel, tq=tq, tk=tk, lambda_init=lambda_init, bounded=bounded)
    return pl.pallas_call(
        kern,
        grid=(b, DIFF_HEADS, nq),
        in_specs=[
            pl.BlockSpec((1, 1, 1, w, LANES), lambda bb, h, i: (bb, h, i, 0, 0)),
            pl.BlockSpec((1, 1, s, LANES), lambda bb, h, i: (bb, h, 0, 0)),
            pl.BlockSpec((1, 1, s // VT_BLOCK, LANES, VT_BLOCK), lambda bb, h, i: (bb, h, 0, 0, 0)),
            pl.BlockSpec((4, DIFF_QK_DIM), lambda bb, h, i: (0, 0)),
            pl.BlockSpec((1, LANES), lambda bb, h, i: (0, 0)),
            pl.BlockSpec((tq, LANES), lambda bb, h, i: (bb * nq + i, E_BZ + h)),
        ],
        out_specs=pl.BlockSpec((tq, LANES), lambda bb, h, i: (bb * nq + i, h)),
        out_shape=jax.ShapeDtypeStruct((b * s, DIFF_HEADS * LANES), BF16),
        scratch_shapes=[
            pltpu.VMEM((1, 1, w), F32), pltpu.VMEM((1, SUBLANES, w), F32), pltpu.VMEM((1, LANES, w), F32),
            pltpu.VMEM((1, tk, w), F32), pltpu.VMEM((1, tk, w), F32),
        ],
        compiler_params=_cparams(("arbitrary", "arbitrary", "arbitrary")),
        name="diff_attention",
    )(bq, bk, bvt, lam_vecs, subln_gain, proj)


def _memkv_kernel(mem_ref, mg_ref, w_ref, kg_ref, k_ref, v_ref):
    mem_n = _rms128(mem_ref[0], mg_ref[...]).astype(BF16)
    kv = _dot(mem_n, w_ref[0])
    for h in range(MEM_HEADS):
        k_ref[0, 0, h] = _rms128(kv[:, h * LANES:(h + 1) * LANES], kg_ref[0]).astype(BF16)
        v_ref[0, 0, h] = kv[:, (MEM_HEADS + h) * LANES:(MEM_HEADS + h + 1) * LANES].astype(BF16)


def _memkv(mem, mem_gain, w_kv, k_gain):
    b, m, d = mem.shape
    depth = w_kv.shape[0]
    c = w_kv.shape[2]
    out = jax.ShapeDtypeStruct((depth, b, MEM_HEADS, m, LANES), BF16)
    ospec = pl.BlockSpec((1, 1, MEM_HEADS, m, LANES), lambda i, bb: (i, bb, 0, 0, 0))
    return pl.pallas_call(
        _memkv_kernel,
        grid=(depth, b),
        in_specs=[
            pl.BlockSpec((1, m, d), lambda i, bb: (bb, 0, 0)),
            pl.BlockSpec((1, d), lambda i, bb: (0, 0)),
            pl.BlockSpec((1, d, c), lambda i, bb: (i, 0, 0)),
            pl.BlockSpec((1, 1, LANES), lambda i, bb: (i, 0, 0)),
        ],
        out_specs=[ospec, ospec],
        out_shape=[out, out],
        compiler_params=_cparams(("arbitrary", "arbitrary")),
        name="mem_kv",
    )(mem, mem_gain.reshape(1, d), w_kv, k_gain)


def _memattn_kernel(q_ref, k_ref, v_ref, z_ref, o_ref):
    scale = HEAD_DIM ** -0.5
    for h in range(MEM_HEADS):
        s = _dot_nt(q_ref[0, h], k_ref[0, h]) * scale
        e = jnp.exp(s - jnp.max(s, axis=-1, keepdims=True))
        p = e / jnp.sum(e, axis=-1, keepdims=True)
        o = _dot(p.astype(BF16), v_ref[0, h])
        o_ref[:, h * LANES:(h + 1) * LANES] = (o * _silu(z_ref[:, h * LANES:(h + 1) * LANES])).astype(BF16)


def _memattn(mq, mk, mv, proj, b, s, z_unit):
    tq = min(512, s)
    nq = s // tq
    m = mk.shape[2]
    kvspec = pl.BlockSpec((1, MEM_HEADS, m, LANES), lambda bb, i: (bb, 0, 0, 0))
    return pl.pallas_call(
        _memattn_kernel,
        grid=(b, nq),
        in_specs=[
            pl.BlockSpec((1, MEM_HEADS, tq, LANES), lambda bb, i: (bb, 0, i, 0)),
            kvspec, kvspec,
            pl.BlockSpec((tq, MEM_HEADS * LANES), lambda bb, i: (bb * nq + i, z_unit // MEM_HEADS)),
        ],
        out_specs=pl.BlockSpec((tq, MEM_HEADS * LANES), lambda bb, i: (bb * nq + i, 0)),
        out_shape=jax.ShapeDtypeStruct((b * s, MEM_HEADS * LANES), BF16),
        compiler_params=_cparams(("arbitrary", "arbitrary")),
        name="mem_attention",
    )(mq, mk, mv, proj)


def _prep_odd_kernel(p_ref, cos_ref, sin_ref, cos64_ref, sin64_ref, cg_ref, mg_ref,
                     cq_ref, ck_ref, cvt_ref, iq_ref, ik_ref, mq_ref):
    cos, sin = cos_ref[...], sin_ref[...]
    cos64, sin64 = cos64_ref[...], sin64_ref[...]
    rep = DSA_HEADS // DSA_KV_HEADS
    tq = TQ_SPARSE
    n_qb = PREP_ROWS // tq

    def unit(u):
        return p_ref[:, u * LANES:(u + 1) * LANES]

    for h in range(DSA_HEADS):
        q = _rope(_rms128(unit(O_CQ + h), cg_ref[0:1, :]), cos, sin, 64).astype(BF16)
        g, r = divmod(h, rep)
        for qb in range(n_qb):
            cq_ref[0, g, qb, r * tq:(r + 1) * tq, :] = q[qb * tq:(qb + 1) * tq, :]
    for h in range(DSA_KV_HEADS):
        ck_ref[0, h] = _rope(_rms128(unit(O_CK + h), cg_ref[1:2, :]), cos, sin, 64).astype(BF16)
        cvt_ref[0, h, 0] = unit(O_CV + h).T.astype(BF16)
    lo = lax.broadcasted_iota(jnp.int32, cos.shape, 1) < 64
    for u in range(IDX_HEADS // 2):
        x = _rope(unit(O_IQ + u), cos64, sin64, 32)
        even = jnp.where(lo, x, 0.0).astype(BF16)
        odd = jnp.where(lo, pltpu.roll(x, 64, 1), 0.0).astype(BF16)
        for qb in range(n_qb):
            iq_ref[0, qb, (2 * u) * tq:(2 * u + 1) * tq, :] = even[qb * tq:(qb + 1) * tq, :]
            iq_ref[0, qb, (2 * u + 1) * tq:(2 * u + 2) * tq, :] = odd[qb * tq:(qb + 1) * tq, :]
    ik = _rope(unit(O_IKW), cos64, sin64, 32)
    ik_ref[0] = jnp.where(lo, ik, 0.0).astype(BF16)
    for h in range(MEM_HEADS):
        mq_ref[0, h] = _rms128(unit(O_MQ + h), mg_ref[...]).astype(BF16)


def _prep_odd(proj, b, s, cos, sin, cos64, sin64, dsa_gain, mem_gain):
    ts = PREP_ROWS
    nb = s // ts
    c = proj.shape[1]
    rep = DSA_HEADS // DSA_KV_HEADS
    row = lambda bb, i: (i, 0)
    const = lambda bb, i: (0, 0)
    seq = lambda n: pl.BlockSpec((1, n, ts, LANES), lambda bb, i: (bb, 0, i, 0))
    return pl.pallas_call(
        _prep_odd_kernel,
        grid=(b, nb),
        in_specs=[
            pl.BlockSpec((ts, c), lambda bb, i: (bb * nb + i, 0)),
            pl.BlockSpec((ts, LANES), row), pl.BlockSpec((ts, LANES), row),
            pl.BlockSpec((ts, LANES), row), pl.BlockSpec((ts, LANES), row),
            pl.BlockSpec((2, LANES), const), pl.BlockSpec((1, LANES), const),
        ],
        out_specs=[
            pl.BlockSpec((1, DSA_KV_HEADS, ts // TQ_SPARSE, rep * TQ_SPARSE, LANES), lambda bb, i: (bb, 0, i, 0, 0)),
            seq(DSA_KV_HEADS),
            pl.BlockSpec((1, DSA_KV_HEADS, 1, LANES, VT_BLOCK), lambda bb, i: (bb, 0, i, 0, 0)),
            pl.BlockSpec((1, ts // TQ_SPARSE, IDX_HEADS * TQ_SPARSE, LANES), lambda bb, i: (bb, i, 0, 0)),
            pl.BlockSpec((1, ts, LANES), lambda bb, i: (bb, i, 0)),
            seq(MEM_HEADS),
        ],
        out_shape=[
            jax.ShapeDtypeStruct((b, DSA_KV_HEADS, s // TQ_SPARSE, rep * TQ_SPARSE, LANES), BF16),
            jax.ShapeDtypeStruct((b, DSA_KV_HEADS, s, LANES), BF16),
            jax.ShapeDtypeStruct((b, DSA_KV_HEADS, s // VT_BLOCK, LANES, VT_BLOCK), BF16),
            jax.ShapeDtypeStruct((b, s // TQ_SPARSE, IDX_HEADS * TQ_SPARSE, LANES), BF16),
            jax.ShapeDtypeStruct((b, s, LANES), BF16),
            jax.ShapeDtypeStruct((b, MEM_HEADS, s, LANES), BF16),
        ],
        compiler_params=_cparams(("arbitrary", "arbitrary")),
        name="prep_odd",
    )(proj, cos, sin, cos64, sin64, dsa_gain, mem_gain)


def _sortable_key(x):
    bits = pltpu.bitcast(x + 0.0, jnp.int32)
    return jnp.where(bits < 0, bits ^ jnp.int32(0x7FFFFFFF), bits)


def _dsa_kernel(q_ref, k_ref, vt_ref, iq_ref, ik_ref, w_ref, ltri_ref, z_ref, o_ref,
                key_ref, m_ref, l_ref, acc_ref, sa_ref, sb_ref, ties_ref, *, tq, tk, top_k, bounded):
    rep = DSA_HEADS // DSA_KV_HEADS
    scale = HEAD_DIM ** -0.5
    q0 = pl.program_id(1) * tq
    t_row = q0 + lax.broadcasted_iota(jnp.int32, (1, tq), 1)
    n_chunks = (q0 + tq + tk - 1) // tk
    int_min = jnp.int32(-2147483648)
    heads_per_dot = 4

    w_t = (w_ref[...] * (IDX_HEADS ** -0.5 * IDX_DIM ** -0.5)).T

    def score_chunk(c, carry):
        k0 = pl.multiple_of(c * tk, tk)
        ik = ik_ref[0, pl.ds(k0, tk), :]
        sc = jnp.zeros((tk, tq), F32)
        for h0 in range(0, IDX_HEADS, heads_per_dot):
            x = _dot_nt(ik, iq_ref[0, 0, h0 * tq:(h0 + heads_per_dot) * tq, :])
            for hh in range(heads_per_dot):
                h = h0 + hh
                sc = sc + jnp.maximum(x[:, hh * tq:(hh + 1) * tq], 0.0) * w_t[IDX_DIM + h:IDX_DIM + h + 1, :]
        causal = (k0 + lax.broadcasted_iota(jnp.int32, (tk, tq), 0)) <= t_row
        key_ref[c] = _sortable_key(jnp.where(causal, sc, MASKED))
        return carry

    lax.fori_loop(0, n_chunks, score_chunk, 0)

    def count(pred):
        def body(c, acc):
            return acc + _fold8(pred(key_ref[c]).astype(F32), jnp.sum, short_chains=True)
        acc = lax.fori_loop(0, n_chunks, body, jnp.zeros((SUBLANES, tq), F32))
        return jnp.sum(acc, axis=0, keepdims=True)

    def search(it, thr_u):
        cand_u = thr_u | jnp.left_shift(jnp.int32(1), 31 - it)
        cand = cand_u ^ int_min
        return jnp.where(count(lambda kk: kk >= cand) >= top_k, cand_u, thr_u)

    thr = lax.fori_loop(0, 32, search, jnp.zeros((1, tq), jnp.int32)) ^ int_min
    budget = top_k - count(lambda kk: kk > thr)

    _online_init(m_ref, l_ref, acc_ref)
    ties_ref[...] = jnp.zeros(ties_ref.shape, F32)

    def qk(c, dst):
        k0 = pl.multiple_of(jnp.minimum(c, n_chunks - 1) * tk, tk)
        for g in range(DSA_KV_HEADS):
            dst[g] = _dot_nt(k_ref[0, g, pl.ds(k0, tk), :], q_ref[0, g, 0]) * (scale * LOG2E)

    def soft(c, src, tail):
        cc = jnp.minimum(c, n_chunks - 1)
        keys = key_ref[cc]
        tie = keys == thr
        rank = _dot(ltri_ref[...], tie.astype(BF16)) + ties_ref[...]
        ok = (keys > thr) | (tie & (rank < budget))
        if tail:
            ok = ok & ((c * tk + lax.broadcasted_iota(jnp.int32, (tk, tq), 0)) <= t_row)
        ties_ref[...] = ties_ref[...] + _col_sum(tie.astype(F32))
        bias = jnp.where(ok, 0.0, NEG_MASK)
        bias = jnp.concatenate([bias] * rep, axis=1)
        for g in range(DSA_KV_HEADS):
            _online_step(g, src[g] + bias, _pv_blocks(vt_ref, (0, g), cc, tk), m_ref, l_ref, acc_ref, bounded)

    _flash_pairs(n_chunks, qk, soft, sa_ref, sb_ref)

    for g in range(DSA_KV_HEADS):
        o_t = _online_result(g, l_ref, acc_ref)
        for r in range(rep):
            h = g * rep + r
            o = o_t[:, r * tq:(r + 1) * tq].T
            o_ref[:, h * LANES:(h + 1) * LANES] = (o * _silu(z_ref[:, h * LANES:(h + 1) * LANES])).astype(BF16)


def _dsa(cq, ck, cvt, iq, ik, ltri, proj, b, s, bounded):
    tq, tk = TQ_SPARSE, TK
    nq = s // tq
    g = DSA_KV_HEADS
    rep = DSA_HEADS // g
    w = rep * tq
    kern = functools.partial(_dsa_kernel, tq=tq, tk=tk, top_k=min(DSA_TOPK_MAX, s // 4), bounded=bounded)
    return pl.pallas_call(
        kern,
        grid=(b, nq),
        in_specs=[
            pl.BlockSpec((1, g, 1, w, LANES), lambda bb, i: (bb, 0, i, 0, 0)),
            pl.BlockSpec((1, g, s, LANES), lambda bb, i: (bb, 0, 0, 0), pipeline_mode=pl.Buffered(1)),
            pl.BlockSpec((1, g, s // VT_BLOCK, LANES, VT_BLOCK), lambda bb, i: (bb, 0, 0, 0, 0),
                         pipeline_mode=pl.Buffered(1)),
            pl.BlockSpec((1, 1, IDX_HEADS * tq, LANES), lambda bb, i: (bb, i, 0, 0)),
            pl.BlockSpec((1, s, LANES), lambda bb, i: (bb, 0, 0), pipeline_mode=pl.Buffered(1)),
            pl.BlockSpec((tq, LANES), lambda bb, i: (bb * nq + i, O_IKW)),
            pl.BlockSpec((tk, tk), lambda bb, i: (0, 0)),
            pl.BlockSpec((tq, DSA_HEADS * LANES), lambda bb, i: (bb * nq + i, O_CZ // DSA_HEADS)),
        ],
        out_specs=pl.BlockSpec((tq, DSA_HEADS * LANES), lambda bb, i: (bb * nq + i, 0)),
        out_shape=jax.ShapeDtypeStruct((b * s, DSA_HEADS * LANES), BF16),
        scratch_shapes=[
            pltpu.VMEM((s // tk, tk, tq), jnp.int32),
            pltpu.VMEM((g, 1, w), F32), pltpu.VMEM((g, SUBLANES, w), F32), pltpu.VMEM((g, LANES, w), F32),
            pltpu.VMEM((g, tk, w), F32), pltpu.VMEM((g, tk, w), F32),
            pltpu.VMEM((1, tq), F32),
        ],
        compiler_params=_cparams(("arbitrary", "arbitrary")),
        name="dsa_attention",
    )(cq, ck, cvt, iq, ik, proj, ltri, proj)


def _outproj_kernel(*refs, n_parts):
    x_ref = refs[0]
    y_refs = refs[1:1 + n_parts]
    w_refs = refs[1 + n_parts:1 + 2 * n_parts]
    o_ref = refs[1 + 2 * n_parts]
    acc = x_ref[...]
    for y_ref, w_ref in zip(y_refs, w_refs):
        acc = acc + _dot(y_ref[...], w_ref[...])
    o_ref[...] = acc


def _outproj(x2, ys, w_out):
    n, d = x2.shape
    tm = min(1024, n)
    tn = 1024
    widths = [y.shape[1] for y in ys]
    starts = np.cumsum([0] + widths[:-1]).tolist()
    ws = [w_out[st:st + wd] for st, wd in zip(starts, widths)]
    kern = functools.partial(_outproj_kernel, n_parts=len(ys))
    return pl.pallas_call(
        kern,
        grid=(d // tn, n // tm),
        in_specs=([pl.BlockSpec((tm, tn), lambda j, i: (i, j))]
                  + [pl.BlockSpec((tm, wd), lambda j, i: (i, 0)) for wd in widths]
                  + [pl.BlockSpec((wd, tn), lambda j, i: (0, j)) for wd in widths]),
        out_specs=pl.BlockSpec((tm, tn), lambda j, i: (i, j)),
        out_shape=jax.ShapeDtypeStruct((n, d), F32),
        compiler_params=_cparams(("arbitrary", "arbitrary")),
        name="out_proj",
    )(x2, *ys, *ws)


def _rope_tables(pos, half, reps):
    inv = ROPE_THETA ** (-jnp.arange(half, dtype=F32) / half)
    ang = pos.astype(F32)[:, None] * inv[None, :]
    cos, sin = jnp.cos(ang), jnp.sin(ang)
    return jnp.tile(jnp.concatenate([cos, cos], -1), (1, reps)), jnp.tile(jnp.concatenate([-sin, sin], -1), (1, reps))


def _split_cols(w, sizes):
    return jnp.split(w, np.cumsum(sizes)[:-1].tolist(), axis=-1)


def _even_weight(w):
    sizes = (1024, 1536, 24, 1024, 512, 512, 512, 512, 512, 512)
    a_q, a_kv, a_g, a_z, b_q, b_k, b_v, b_z, m_q, m_z = _split_cols(w, sizes)
    d = w.shape[0]
    per_group = 3 * NSA_HEADS // NSA_KV_GROUPS
    gates = [jnp.pad(a_g[:, g * per_group:(g + 1) * per_group], ((0, 0), (0, LANES - per_group)))
             for g in range(NSA_KV_GROUPS)]
    out = jnp.concatenate([a_q, a_kv, b_q, b_k, b_v, m_q, a_z, b_z, m_z] + gates, axis=-1)
    assert out.shape == (d, EVEN_UNITS * LANES)
    return out.astype(BF16)


def _odd_weight(w):
    sizes = (1536, 512, 512, 1024, 64, 16, 1536, 512, 512)
    c_q, c_k, c_v, i_q, i_k, i_w, c_z, m_q, m_z = _split_cols(w, sizes)
    d = w.shape[0]
    ikw = jnp.pad(jnp.concatenate([i_k, i_w], -1), ((0, 0), (0, LANES - IDX_DIM - IDX_HEADS)))
    out = jnp.concatenate([c_z, c_q, c_k, c_v, i_q, m_q, m_z, ikw, jnp.zeros((d, LANES), w.dtype)], axis=-1)
    assert out.shape == (d, ODD_UNITS * LANES)
    return out.astype(BF16)


def _overlap_matrix_t(s, n_cmp_pad):
    n_cmp = (s - NSA_CMP_LEN) // NSA_CMP_STRIDE + 1
    n_sel = s // NSA_SEL_LEN
    cmp_start = np.arange(n_cmp) * NSA_CMP_STRIDE
    sel_start = np.arange(n_sel) * NSA_SEL_LEN
    ov = np.clip(np.minimum(cmp_start[:, None] + NSA_CMP_LEN, sel_start[None, :] + NSA_SEL_LEN)
                 - np.maximum(cmp_start[:, None], sel_start[None, :]), 0, None) / NSA_CMP_LEN
    full = np.zeros((n_sel, n_cmp_pad), np.float32)
    full[:, :n_cmp] = ov.T
    return jnp.asarray(full, dtype=BF16)


def _score_bound(dim, gain_q, gain_k, scale):
    return dim * jnp.max(jnp.abs(gain_q)) * jnp.max(jnp.abs(gain_k)) * (scale * LOG2E * 1.02)


def _attend(bound, fn, *operands):
    return lax.cond(bound <= FAST_LOG2_BOUND,
                    lambda *a: fn(*a, bounded=True), lambda *a: fn(*a, bounded=False), *operands)


def kernel(x, mem, norm_gain, mem_norm_gain, mem_w_kv, mem_qk_gain, w_out, even_w_in, nsa_qk_gain, nsa_cmp_pos,
           nsa_cmp_w1, nsa_cmp_w2, diff_qk_gain, diff_lambda, diff_subln_gain, odd_w_in, dsa_qk_gain):
    b, s, d = x.shape
    assert d == D_MODEL and s % TK == 0 and s >= NSA_WINDOW + TQ_SPARSE
    pos = jnp.arange(s)
    cos, sin = _rope_tables(pos, HEAD_DIM // 2, 1)
    cos64, sin64 = _rope_tables(pos, DIFF_QK_DIM // 2, 2)
    n_cmp_pad = s // NSA_CMP_STRIDE
    cmp_last = jnp.arange(n_cmp_pad) * NSA_CMP_STRIDE + NSA_CMP_LEN - 1
    cos_c, sin_c = _rope_tables(cmp_last, HEAD_DIM // 2, 1)
    ovlt = _overlap_matrix_t(s, n_cmp_pad)
    ltri = jnp.asarray(np.tril(np.ones((TK, TK), np.float32), -1), dtype=BF16)
    wbias = _window_bias(TQ_SPARSE)

    mk_all, mv_all = _memkv(mem, mem_norm_gain, mem_w_kv.astype(BF16), mem_qk_gain[:, 1:2, :])
    w_out_b = w_out.astype(BF16)

    x2 = x.reshape(b * s, d)
    for i in range(DEPTH):
        mem_q_gain = mem_qk_gain[i, 0:1, :]
        if i % 2 == 0:
            e = i // 2
            proj = _proj(x2, norm_gain[i], _even_weight(even_w_in[e]), 768)
            dg2 = jnp.tile(diff_qk_gain[e], (1, 2))
            qa, ks, kw, vst, vwt, craw, bq, bk, bvt, mq = _prep_even(
                proj, b, s, cos, sin, cos64, sin64, nsa_qk_gain[e], dg2, mem_q_gain)
            w1 = nsa_cmp_w1[e].reshape(2, NSA_CMP_LEN, HEAD_DIM, NSA_CMP_HIDDEN).astype(BF16)
            kc, vct = _compress(craw, nsa_cmp_pos[e], w1, nsa_cmp_w2[e].astype(BF16),
                                nsa_qk_gain[e, 1:2, :], cos_c, sin_c)
            y_a = _attend(_score_bound(HEAD_DIM, nsa_qk_gain[e, 0], nsa_qk_gain[e, 1:4], HEAD_DIM ** -0.5),
                          functools.partial(_nsa, b=b, s=s), qa, ks, vst, kw, vwt, kc, vct, ovlt, wbias, proj)
            lambda_init = 0.8 - 0.6 * math.exp(-0.3 * i)
            y_b = _attend(_score_bound(DIFF_QK_DIM, diff_qk_gain[e, 0], diff_qk_gain[e, 1], DIFF_QK_DIM ** -0.5),
                          functools.partial(_diff, b=b, s=s, lambda_init=lambda_init),
                          bq, bk, bvt, diff_lambda[e], diff_subln_gain[e].reshape(1, LANES), proj)
            y_m = _memattn(mq, mk_all[i], mv_all[i], proj, b, s, E_MZ)
            ys = [y_a, y_b, y_m]
        else:
            o = i // 2
            proj = _proj(x2, norm_gain[i], _odd_weight(odd_w_in[o]), 640)
            cq, ck, cvt, iq, ik, mq = _prep_odd(proj, b, s, cos, sin, cos64, sin64, dsa_qk_gain[o], mem_q_gain)
            y_c = _attend(_score_bound(HEAD_DIM, dsa_qk_gain[o, 0], dsa_qk_gain[o, 1], HEAD_DIM ** -0.5),
                          functools.partial(_dsa, b=b, s=s), cq, ck, cvt, iq, ik, ltri, proj)
            y_m = _memattn(mq, mk_all[i], mv_all[i], proj, b, s, O_MZ)
            ys = [y_c, y_m]
        x2 = _outproj(x2, ys, w_out_b[i])
    return x2.reshape(b, s, d)
```

```python
import functools
import math

import jax
import jax.numpy as jnp
import numpy as np
from jax import lax
from jax.experimental import pallas as pl
from jax.experimental.pallas import tpu as pltpu

F32 = jnp.float32
BF16 = jnp.bfloat16

D_MODEL = 2048
DEPTH = 4
HEAD_DIM = 128
ROPE_THETA = 10000.0
EPS = 1e-6
MASKED = -1e30
FORCED = 1e9
NSA_HEADS = 8
NSA_KV_GROUPS = 2
NSA_CMP_LEN = 32
NSA_CMP_STRIDE = 16
NSA_CMP_HIDDEN = 256
NSA_SEL_LEN = 64
NSA_SEL_TOPN = 16
NSA_WINDOW = 512
DIFF_HEADS = 4
DIFF_QK_DIM = 64
DSA_HEADS = 12
DSA_KV_HEADS = 4
IDX_HEADS = 16
IDX_DIM = 64
DSA_TOPK_MAX = 256
MEM_HEADS = 4

LANES = 128
SUBLANES = 8
VMEM_LIMIT_BYTES = 56 * 1024 * 1024
LOG2E = 1.4426950408889634

PREP_ROWS = 256
TQ_SPARSE = 128
TQ_DIFF = 256
TK = 512
VT_BLOCK = 256
NEG_INIT = -1e30
NEG_MASK = -2e30
FAST_LOG2_BOUND = 40.0

EVEN_UNITS = 54
E_AQ, E_AKV, E_BQ, E_BK, E_BV, E_MQ, E_AZ, E_BZ, E_MZ, E_AG = 0, 8, 20, 24, 28, 32, 36, 44, 48, 52
ODD_UNITS = 50
O_CZ, O_CQ, O_CK, O_CV, O_IQ, O_MQ, O_MZ, O_IKW = 0, 12, 24, 28, 32, 40, 44, 48


def _cparams(sem):
    return pltpu.CompilerParams(dimension_semantics=sem, vmem_limit_bytes=VMEM_LIMIT_BYTES)


def _dot(a, b):
    return jnp.dot(a, b, preferred_element_type=F32)


def _dot_nt(a, b):
    return lax.dot_general(a, b, (((1,), (1,)), ((), ())), preferred_element_type=F32)


def _silu(x):
    return x * jax.nn.sigmoid(x)


def _rms128(x, gain):
    return x * lax.rsqrt(jnp.mean(x * x, axis=-1, keepdims=True) + EPS) * gain


def _rms64(x, gain):
    lo = lax.broadcasted_iota(jnp.int32, x.shape, 1) < 64
    xx = x * x
    s_lo = jnp.sum(jnp.where(lo, xx, 0.0), axis=-1, keepdims=True)
    s_hi = jnp.sum(jnp.where(lo, 0.0, xx), axis=-1, keepdims=True)
    ms = jnp.where(lo, s_lo, s_hi) * (1.0 / 64.0)
    return x * lax.rsqrt(ms + EPS) * gain


def _partner(x, half):
    n = x.shape[-1]
    lane = lax.broadcasted_iota(jnp.int32, x.shape, 1)
    if 2 * half == n:
        return pltpu.roll(x, half, 1)
    a = pltpu.roll(x, half, 1)
    b = pltpu.roll(x, n - half, 1)
    src_a = pltpu.roll(lane, half, 1)
    want = jnp.where((lane & (2 * half - 1)) < half, lane + half, lane - half)
    return jnp.where(src_a == want, a, b)


def _rope(x, cos, sin_signed, half):
    return x * cos + _partner(x, half) * sin_signed


def _fold8(x, op, short_chains=False):
    rows, w = x.shape
    if short_chains:
        x = op(x.reshape(SUBLANES, rows // SUBLANES, w), axis=0)
        rows = rows // SUBLANES
    return op(x.reshape(rows // SUBLANES, SUBLANES, w), axis=0)


def _col_max(x):
    return jnp.max(_fold8(x, jnp.max), axis=0, keepdims=True)


def _col_sum(x):
    return jnp.sum(_fold8(x, jnp.sum), axis=0, keepdims=True)


def _proj_kernel(x_ref, g_ref, w_ref, o_ref, hn_ref):
    @pl.when(pl.program_id(1) == 0)
    def _():
        x = x_ref[...]
        hn_ref[...] = _rms128(x, g_ref[...]).astype(BF16)

    o_ref[...] = _dot(hn_ref[...], w_ref[...])


def _proj(x2, gain, w, tn):
    n, d = x2.shape
    c = w.shape[1]
    tm = min(1024, n)
    return pl.pallas_call(
        _proj_kernel,
        grid=(n // tm, c // tn),
        in_specs=[
            pl.BlockSpec((tm, d), lambda i, j: (i, 0)),
            pl.BlockSpec((1, d), lambda i, j: (0, 0)),
            pl.BlockSpec((d, tn), lambda i, j: (0, j)),
        ],
        out_specs=pl.BlockSpec((tm, tn), lambda i, j: (i, j)),
        out_shape=jax.ShapeDtypeStruct((n, c), F32),
        scratch_shapes=[pltpu.VMEM((tm, d), BF16)],
        compiler_params=_cparams(("arbitrary", "arbitrary")),
        name="proj",
    )(x2, gain.reshape(1, d), w)


def _prep_even_kernel(p_ref, cos_ref, sin_ref, cos64_ref, sin64_ref, ng_ref, dg_ref, mg_ref,
                      qa_ref, ks_ref, kw_ref, vst_ref, vwt_ref, craw_ref, bq_ref, bk_ref, bvt_ref, mq_ref):
    cos, sin = cos_ref[...], sin_ref[...]
    cos64, sin64 = cos64_ref[...], sin64_ref[...]
    rep = NSA_HEADS // NSA_KV_GROUPS
    tq = TQ_SPARSE

    def unit(u):
        return p_ref[:, u * LANES:(u + 1) * LANES]

    for h in range(NSA_HEADS):
        q = _rope(_rms128(unit(E_AQ + h), ng_ref[0:1, :]), cos, sin, 64).astype(BF16)
        g, r = divmod(h, rep)
        for qb in range(PREP_ROWS // tq):
            qa_ref[0, g, qb, r * tq:(r + 1) * tq, :] = q[qb * tq:(qb + 1) * tq, :]
    for g in range(NSA_KV_GROUPS):
        craw_ref[0, 0, g] = unit(E_AKV + 0 + g)
        craw_ref[0, 1, g] = unit(E_AKV + 2 + g)
        ks_ref[0, g] = _rope(_rms128(unit(E_AKV + 4 + g), ng_ref[2:3, :]), cos, sin, 64).astype(BF16)
        vst_ref[0, g, 0] = unit(E_AKV + 6 + g).T.astype(BF16)
        kw_ref[0, g] = _rope(_rms128(unit(E_AKV + 8 + g), ng_ref[3:4, :]), cos, sin, 64).astype(BF16)
        vw = unit(E_AKV + 10 + g)
        for kb in range(PREP_ROWS // LANES):
            vwt_ref[0, g, kb] = vw[kb * LANES:(kb + 1) * LANES, :].T.astype(BF16)
    lo = lax.broadcasted_iota(jnp.int32, cos.shape, 1) < 64
    for h in range(DIFF_HEADS):
        q = _rope(_rms64(unit(E_BQ + h), dg_ref[0:1, :]), cos64, sin64, 32)
        bq_ref[0, h, 0, 0:PREP_ROWS, :] = jnp.where(lo, q, 0.0).astype(BF16)
        bq_ref[0, h, 0, PREP_ROWS:2 * PREP_ROWS, :] = jnp.where(lo, 0.0, q).astype(BF16)
        bk_ref[0, h] = _rope(_rms64(unit(E_BK + h), dg_ref[1:2, :]), cos64, sin64, 32).astype(BF16)
        bvt_ref[0, h, 0] = unit(E_BV + h).T.astype(BF16)
    for h in range(MEM_HEADS):
        mq_ref[0, h] = _rms128(unit(E_MQ + h), mg_ref[...]).astype(BF16)


def _prep_even(proj, b, s, cos, sin, cos64, sin64, nsa_gain, diff_gain2, mem_gain):
    ts = PREP_ROWS
    assert ts == TQ_DIFF == VT_BLOCK
    nb = s // ts
    c = proj.shape[1]
    g = NSA_KV_GROUPS
    rep = NSA_HEADS // g
    row = lambda bb, i: (i, 0)
    const = lambda bb, i: (0, 0)
    seq = lambda n: pl.BlockSpec((1, n, ts, LANES), lambda bb, i: (bb, 0, i, 0))
    blk5 = lambda n, k, r, cdim: pl.BlockSpec((1, n, k, r, cdim), lambda bb, i: (bb, 0, i, 0, 0))
    return pl.pallas_call(
        _prep_even_kernel,
        grid=(b, nb),
        in_specs=[
            pl.BlockSpec((ts, c), lambda bb, i: (bb * nb + i, 0)),
            pl.BlockSpec((ts, LANES), row), pl.BlockSpec((ts, LANES), row),
            pl.BlockSpec((ts, LANES), row), pl.BlockSpec((ts, LANES), row),
            pl.BlockSpec((4, LANES), const), pl.BlockSpec((2, LANES), const), pl.BlockSpec((1, LANES), const),
        ],
        out_specs=[
            blk5(g, ts // TQ_SPARSE, rep * TQ_SPARSE, LANES),
            seq(g), seq(g),
            blk5(g, 1, LANES, VT_BLOCK),
            blk5(g, ts // LANES, LANES, LANES),
            pl.BlockSpec((1, 2, g, ts, LANES), lambda bb, i: (bb, 0, 0, i, 0)),
            blk5(DIFF_HEADS, 1, 2 * TQ_DIFF, LANES),
            seq(DIFF_HEADS),
            blk5(DIFF_HEADS, 1, LANES, VT_BLOCK),
            seq(MEM_HEADS),
        ],
        out_shape=[
            jax.ShapeDtypeStruct((b, g, s // TQ_SPARSE, rep * TQ_SPARSE, LANES), BF16),
            jax.ShapeDtypeStruct((b, g, s, LANES), BF16),
            jax.ShapeDtypeStruct((b, g, s, LANES), BF16),
            jax.ShapeDtypeStruct((b, g, s // VT_BLOCK, LANES, VT_BLOCK), BF16),
            jax.ShapeDtypeStruct((b, g, s // LANES, LANES, LANES), BF16),
            jax.ShapeDtypeStruct((b, 2, g, s, LANES), F32),
            jax.ShapeDtypeStruct((b, DIFF_HEADS, s // TQ_DIFF, 2 * TQ_DIFF, LANES), BF16),
            jax.ShapeDtypeStruct((b, DIFF_HEADS, s, LANES), BF16),
            jax.ShapeDtypeStruct((b, DIFF_HEADS, s // VT_BLOCK, LANES, VT_BLOCK), BF16),
            jax.ShapeDtypeStruct((b, MEM_HEADS, s, LANES), BF16),
        ],
        compiler_params=_cparams(("arbitrary", "arbitrary")),
        name="prep_even",
    )(proj, cos, sin, cos64, sin64, nsa_gain, diff_gain2, mem_gain)


def _compress_kernel(x_ref, pe_ref, w1_ref, w2_ref, g_ref, cos_ref, sin_ref, kc_ref, vct_ref, pad_ref, *, s, n_pad):
    for kind in range(2):
        pad_ref[0:s, :] = x_ref[0, kind, 0]
        pad_ref[s:s + NSA_CMP_LEN, :] = jnp.zeros((NSA_CMP_LEN, LANES), F32)
        acc = jnp.zeros((n_pad, NSA_CMP_HIDDEN), F32)
        for l in range(NSA_CMP_LEN):
            rows = pad_ref[pl.ds(l, n_pad, stride=NSA_CMP_STRIDE), :] + pe_ref[kind, l:l + 1, :]
            acc = acc + _dot(rows.astype(BF16), w1_ref[kind, l])
        out = _dot(_silu(acc).astype(BF16), w2_ref[kind])
        if kind == 0:
            kc_ref[0, 0] = _rope(_rms128(out, g_ref[...]), cos_ref[...], sin_ref[...], 64).astype(BF16)
        else:
            vct_ref[0, 0] = out.T.astype(BF16)


def _compress(craw, pe, w1, w2, gain, cos_c, sin_c):
    b, _, g, s, _ = craw.shape
    n_pad = s // NSA_CMP_STRIDE
    kern = functools.partial(_compress_kernel, s=s, n_pad=n_pad)
    whole = lambda shape: pl.BlockSpec(shape, lambda bb, gg: (0,) * len(shape))
    return pl.pallas_call(
        kern,
        grid=(b, g),
        in_specs=[
            pl.BlockSpec((1, 2, 1, s, LANES), lambda bb, gg: (bb, 0, gg, 0, 0)),
            whole((2, NSA_CMP_LEN, LANES)),
            whole((2, NSA_CMP_LEN, LANES, NSA_CMP_HIDDEN)),
            whole((2, NSA_CMP_HIDDEN, LANES)),
            whole((1, LANES)), whole((n_pad, LANES)), whole((n_pad, LANES)),
        ],
        out_specs=[
            pl.BlockSpec((1, 1, n_pad, LANES), lambda bb, gg: (bb, gg, 0, 0)),
            pl.BlockSpec((1, 1, LANES, n_pad), lambda bb, gg: (bb, gg, 0, 0)),
        ],
        out_shape=[
            jax.ShapeDtypeStruct((b, g, n_pad, LANES), BF16),
            jax.ShapeDtypeStruct((b, g, LANES, n_pad), BF16),
        ],
        scratch_shapes=[pltpu.VMEM((s + NSA_CMP_LEN, LANES), F32)],
        compiler_params=_cparams(("arbitrary", "arbitrary")),
        name="nsa_compress",
    )(craw, pe, w1, w2, gain, cos_c, sin_c)


def _softmax_parts_t(st, bounded):
    if not bounded:
        st = st - _col_max(st)
    e = jnp.exp2(st)
    return e.astype(BF16), 1.0 / jnp.maximum(_col_sum(e), 1e-30)


def _online_init(m_ref, l_ref, acc_ref):
    m_ref[...] = jnp.full(m_ref.shape, NEG_INIT, F32)
    l_ref[...] = jnp.zeros(l_ref.shape, F32)
    acc_ref[...] = jnp.zeros(acc_ref.shape, F32)


def _online_step(idx, st, pv, m_ref, l_ref, acc_ref, bounded):
    if bounded:
        p = jnp.exp2(st)
        l_ref[idx] = l_ref[idx] + _fold8(p, jnp.sum)
        acc_ref[idx] = acc_ref[idx] + pv(p.astype(BF16))
        return
    m_old = m_ref[idx]
    m_new = jnp.maximum(m_old, _col_max(st))
    alpha = jnp.exp2(m_old - m_new)
    p = jnp.exp2(st - m_new)
    l_ref[idx] = alpha * l_ref[idx] + _fold8(p, jnp.sum)
    acc_ref[idx] = alpha * acc_ref[idx] + pv(p.astype(BF16))
    m_ref[idx] = m_new


def _online_result(idx, l_ref, acc_ref):
    return acc_ref[idx] / jnp.maximum(jnp.sum(l_ref[idx], axis=0, keepdims=True), 1e-30)


def _flash_pairs(n_chunks, n_groups, qk, mask, soft, sa_ref, sb_ref):
    for g in range(n_groups):
        qk(0, g, sa_ref)
    n_pairs = (n_chunks - 1) // 2

    def pair(j, carry):
        a = 2 * j
        bias_a = mask(a, False)
        bias_b = mask(a + 1, False)
        for g in range(n_groups):
            qk(a + 1, g, sb_ref)
            soft(a, g, sa_ref, bias_a)
            qk(a + 2, g, sa_ref)
            soft(a + 1, g, sb_ref, bias_b)
        return carry

    lax.fori_loop(0, n_pairs, pair, 0)
    e = 2 * n_pairs
    bias_a = mask(e, True)
    bias_b = mask(e + 1, True)
    for g in range(n_groups):
        qk(e + 1, g, sb_ref)
        soft(e, g, sa_ref, bias_a)
        soft(e + 1, g, sb_ref, bias_b)


def _pv_blocks(vt_ref, lead, chunk, tk):
    per = tk // VT_BLOCK

    def pv(p):
        out = _dot(vt_ref[lead + (chunk * per,)], p[0:VT_BLOCK])
        for i in range(1, per):
            out = out + _dot(vt_ref[lead + (chunk * per + i,)], p[i * VT_BLOCK:(i + 1) * VT_BLOCK])
        return out

    return pv


def _top_n_mask_t(scores, n):
    row = lax.broadcasted_iota(jnp.int32, scores.shape, 0).astype(F32)
    height = float(scores.shape[0])
    work = scores
    sel = jnp.zeros(scores.shape, F32)
    for _ in range(n):
        m = jnp.max(work, axis=0, keepdims=True)
        first = jnp.min(jnp.where(work == m, row, height), axis=0, keepdims=True)
        pick = row == first
        sel = jnp.where(pick, 1.0, sel)
        work = jnp.where(pick, -jnp.inf, work)
    return sel


def _nsa_kernel(q_ref, kc_ref, vct_ref, ks_ref, vst_ref, kw_ref, vwt_ref, ovlt_ref, wbias_ref, gl_ref, z_ref, o_ref,
                m_ref, l_ref, acc_ref, sa_ref, sb_ref, selt_ref, oct_ref, owt_ref, *, tq, tk, top_n, bounded):
    rep = NSA_HEADS // NSA_KV_GROUPS
    w = rep * tq
    scale = HEAD_DIM ** -0.5
    q0 = pl.program_id(2) * tq
    t_row = q0 + (lax.broadcasted_iota(jnp.int32, (1, w), 1) & (tq - 1))
    t_one = t_row[:, 0:tq]
    q = q_ref[0, 0, 0]

    n_cmp_pad = kc_ref.shape[2]
    cmp_last = lax.broadcasted_iota(jnp.int32, (n_cmp_pad, w), 0) * NSA_CMP_STRIDE + (NSA_CMP_LEN - 1)
    st = jnp.where(cmp_last <= t_row, _dot_nt(kc_ref[0, 0], q) * (scale * LOG2E), NEG_MASK)
    p, inv = _softmax_parts_t(st, bounded)
    inv = jnp.where(t_row >= NSA_CMP_LEN - 1, inv, 0.0)
    oct_ref[...] = _dot(vct_ref[0, 0], p) * inv
    imp_heads = _dot(ovlt_ref[...], p) * inv
    imp = imp_heads[:, 0:tq]
    for r in range(1, rep):
        imp = imp + imp_heads[:, r * tq:(r + 1) * tq]

    n_sel = imp.shape[0]
    sel_shift = NSA_SEL_LEN.bit_length() - 1
    j = lax.broadcasted_iota(jnp.int32, (n_sel, tq), 0)
    cur = jnp.right_shift(t_one, sel_shift)
    visible = j <= cur
    forced = (j == 0) | (j >= cur - 1)
    imp = jnp.where(visible, jnp.where(forced, FORCED, imp), MASKED)
    selt_ref[...] = _top_n_mask_t(imp, top_n)

    span = NSA_WINDOW + tq
    start = pl.multiple_of(jnp.maximum(q0 - NSA_WINDOW, 0), tq)
    st = (_dot_nt(kw_ref[0, 0, pl.ds(start, span), :], q) * (scale * LOG2E)
          + jnp.concatenate([wbias_ref[0]] * rep, axis=1))
    p, inv = _softmax_parts_t(st, bounded)
    blk0 = start // LANES
    ow = _dot(vwt_ref[0, 0, blk0], p[0:LANES])
    for i in range(1, span // LANES):
        ow = ow + _dot(vwt_ref[0, 0, blk0 + i], p[i * LANES:(i + 1) * LANES])
    owt_ref[...] = ow * inv

    _online_init(m_ref, l_ref, acc_ref)
    n_chunks = (q0 + tq + tk - 1) // tk
    blocks_per_chunk = tk // NSA_SEL_LEN

    def qk(c, g, dst):
        k0 = pl.multiple_of(jnp.minimum(c, n_chunks - 1) * tk, tk)
        dst[g] = _dot_nt(ks_ref[0, 0, pl.ds(k0, tk), :], q) * (scale * LOG2E)

    def mask(c, tail):
        cc = jnp.minimum(c, n_chunks - 1)
        rows = selt_ref[pl.ds(pl.multiple_of(cc * blocks_per_chunk, blocks_per_chunk), blocks_per_chunk), :]
        ok = jnp.concatenate(
            [jnp.broadcast_to(rows[i:i + 1, :], (NSA_SEL_LEN, tq)) for i in range(blocks_per_chunk)], axis=0) > 0.5
        if tail:
            ok = ok & ((c * tk + lax.broadcasted_iota(jnp.int32, (tk, tq), 0)) <= t_one)
        return jnp.concatenate([jnp.where(ok, 0.0, NEG_MASK)] * rep, axis=1)

    def soft(c, g, src, bias):
        cc = jnp.minimum(c, n_chunks - 1)
        _online_step(g, src[g] + bias, _pv_blocks(vst_ref, (0, 0), cc, tk), m_ref, l_ref, acc_ref, bounded)

    _flash_pairs(n_chunks, 1, qk, mask, soft, sa_ref, sb_ref)

    gates_t = jax.nn.sigmoid(gl_ref[...]).T
    os_t = _online_result(0, l_ref, acc_ref)
    for r in range(rep):
        sl = slice(r * tq, (r + 1) * tq)
        out_t = (gates_t[3 * r:3 * r + 1, :] * oct_ref[:, sl] + gates_t[3 * r + 1:3 * r + 2, :] * os_t[:, sl]
                 + gates_t[3 * r + 2:3 * r + 3, :] * owt_ref[:, sl])
        o_ref[:, r * LANES:(r + 1) * LANES] = (out_t.T * _silu(z_ref[:, r * LANES:(r + 1) * LANES])).astype(BF16)


def _window_bias(tq):
    span = NSA_WINDOW + tq
    n_early = NSA_WINDOW // tq
    row = np.arange(span)[:, None]
    lane = np.arange(tq)[None, :]
    tables = []
    for i in range(n_early + 1):
        t = i * tq + lane
        s_pos = (0 if i < n_early else t[0, 0] - NSA_WINDOW) + row
        ok = (s_pos <= t) & (s_pos > t - NSA_WINDOW)
        tables.append(np.where(ok, 0.0, NEG_MASK))
    return jnp.asarray(np.stack(tables), dtype=F32)


def _nsa(qa, ks, vst, kw, vwt, kc, vct, ovlt, wbias, proj, b, s, bounded):
    tq, tk = TQ_SPARSE, TK
    nq = s // tq
    g = NSA_KV_GROUPS
    rep = NSA_HEADS // g
    w = rep * tq
    n_cmp_pad = kc.shape[2]
    n_sel = s // NSA_SEL_LEN
    kern = functools.partial(_nsa_kernel, tq=tq, tk=tk, top_n=min(NSA_SEL_TOPN, n_sel), bounded=bounded)
    per_group = lambda shape: pl.BlockSpec((1, 1) + shape, lambda bb, gg, i: (bb, gg) + (0,) * len(shape))
    return pl.pallas_call(
        kern,
        grid=(b, g, nq),
        in_specs=[
            pl.BlockSpec((1, 1, 1, w, LANES), lambda bb, gg, i: (bb, gg, i, 0, 0)),
            per_group((n_cmp_pad, LANES)), per_group((LANES, n_cmp_pad)),
            per_group((s, LANES)), per_group((s // VT_BLOCK, LANES, VT_BLOCK)),
            per_group((s, LANES)), per_group((s // LANES, LANES, LANES)),
            pl.BlockSpec((n_sel, n_cmp_pad), lambda bb, gg, i: (0, 0)),
            pl.BlockSpec((1,) + wbias.shape[1:], lambda bb, gg, i: (jnp.minimum(i, wbias.shape[0] - 1), 0, 0)),
            pl.BlockSpec((tq, LANES), lambda bb, gg, i: (bb * nq + i, E_AG + gg)),
            pl.BlockSpec((tq, rep * LANES), lambda bb, gg, i: (bb * nq + i, E_AZ // rep + gg)),
        ],
        out_specs=pl.BlockSpec((tq, rep * LANES), lambda bb, gg, i: (bb * nq + i, gg)),
        out_shape=jax.ShapeDtypeStruct((b * s, NSA_HEADS * LANES), BF16),
        scratch_shapes=[
            pltpu.VMEM((1, 1, w), F32), pltpu.VMEM((1, SUBLANES, w), F32), pltpu.VMEM((1, LANES, w), F32),
            pltpu.VMEM((1, tk, w), F32), pltpu.VMEM((1, tk, w), F32),
            pltpu.VMEM((n_sel, tq), F32), pltpu.VMEM((LANES, w), F32), pltpu.VMEM((LANES, w), F32),
        ],
        compiler_params=_cparams(("arbitrary", "arbitrary", "arbitrary")),
        name="nsa_attention",
    )(qa, kc, vct, ks, vst, kw, vwt, ovlt, wbias, proj, proj)


def _diff_kernel(q_ref, k_ref, vt_ref, lam_ref, sg_ref, z_ref, o_ref, m_ref, l_ref, acc_ref, sa_ref, sb_ref,
                 *, tq, tk, lambda_init, bounded):
    w = 2 * tq
    q0 = pl.program_id(2) * tq
    t_row = q0 + (lax.broadcasted_iota(jnp.int32, (1, w), 1) & (tq - 1))
    q = q_ref[0, 0, 0]
    _online_init(m_ref, l_ref, acc_ref)
    n_chunks = (q0 + tq + tk - 1) // tk

    def qk(c, g, dst):
        k0 = pl.multiple_of(jnp.minimum(c, n_chunks - 1) * tk, tk)
        dst[g] = _dot_nt(k_ref[0, 0, pl.ds(k0, tk), :], q) * (DIFF_QK_DIM ** -0.5 * LOG2E)

    def mask(c, tail):
        if not tail:
            return None
        key = c * tk + lax.broadcasted_iota(jnp.int32, (tk, w), 0)
        return jnp.where(key <= t_row, 0.0, NEG_MASK)

    def soft(c, g, src, bias):
        cc = jnp.minimum(c, n_chunks - 1)
        st = src[g] if bias is None else src[g] + bias
        _online_step(g, st, _pv_blocks(vt_ref, (0, 0), cc, tk), m_ref, l_ref, acc_ref, bounded)

    _flash_pairs(n_chunks, 1, qk, mask, soft, sa_ref, sb_ref)

    lv = lam_ref[...]
    lam = (jnp.exp(jnp.sum(lv[0:1] * lv[1:2], axis=-1, keepdims=True))
           - jnp.exp(jnp.sum(lv[2:3] * lv[3:4], axis=-1, keepdims=True)) + lambda_init)
    o_t = _online_result(0, l_ref, acc_ref)
    o = (o_t[:, 0:tq] - lam * o_t[:, tq:w]).T
    o = _rms128(o, sg_ref[...]) * (1.0 - lambda_init)
    o_ref[...] = (o * _silu(z_ref[...])).astype(BF16)


def _diff(bq, bk, bvt, lam_vecs, subln_gain, proj, b, s, lambda_init, bounded):
    tq, tk = TQ_DIFF, TK
    nq = s // tq
    w = 2 * tq
    kern = functools.partial(_diff_kernel, tq=tq, tk=tk, lambda_init=lambda_init, bounded=bounded)
    return pl.pallas_call(
        kern,
        grid=(b, DIFF_HEADS, nq),
        in_specs=[
            pl.BlockSpec((1, 1, 1, w, LANES), lambda bb, h, i: (bb, h, i, 0, 0)),
            pl.BlockSpec((1, 1, s, LANES), lambda bb, h, i: (bb, h, 0, 0)),
            pl.BlockSpec((1, 1, s // VT_BLOCK, LANES, VT_BLOCK), lambda bb, h, i: (bb, h, 0, 0, 0)),
            pl.BlockSpec((4, DIFF_QK_DIM), lambda bb, h, i: (0, 0)),
            pl.BlockSpec((1, LANES), lambda bb, h, i: (0, 0)),
            pl.BlockSpec((tq, LANES), lambda bb, h, i: (bb * nq + i, E_BZ + h)),
        ],
        out_specs=pl.BlockSpec((tq, LANES), lambda bb, h, i: (bb * nq + i, h)),
        out_shape=jax.ShapeDtypeStruct((b * s, DIFF_HEADS * LANES), BF16),
        scratch_shapes=[
            pltpu.VMEM((1, 1, w), F32), pltpu.VMEM((1, SUBLANES, w), F32), pltpu.VMEM((1, LANES, w), F32),
            pltpu.VMEM((1, tk, w), F32), pltpu.VMEM((1, tk, w), F32),
        ],
        compiler_params=_cparams(("arbitrary", "arbitrary", "arbitrary")),
        name="diff_attention",
    )(bq, bk, bvt, lam_vecs, subln_gain, proj)


def _memkv_kernel(mem_ref, mg_ref, w_ref, kg_ref, k_ref, v_ref):
    mem_n = _rms128(mem_ref[0], mg_ref[...]).astype(BF16)
    kv = _dot(mem_n, w_ref[0])
    for h in range(MEM_HEADS):
        k_ref[0, 0, h] = _rms128(kv[:, h * LANES:(h + 1) * LANES], kg_ref[0]).astype(BF16)
        v_ref[0, 0, h] = kv[:, (MEM_HEADS + h) * LANES:(MEM_HEADS + h + 1) * LANES].astype(BF16)


def _memkv(mem, mem_gain, w_kv, k_gain):
    b, m, d = mem.shape
    depth = w_kv.shape[0]
    c = w_kv.shape[2]
    out = jax.ShapeDtypeStruct((depth, b, MEM_HEADS, m, LANES), BF16)
    ospec = pl.BlockSpec((1, 1, MEM_HEADS, m, LANES), lambda i, bb: (i, bb, 0, 0, 0))
    return pl.pallas_call(
        _memkv_kernel,
        grid=(depth, b),
        in_specs=[
            pl.BlockSpec((1, m, d), lambda i, bb: (bb, 0, 0)),
            pl.BlockSpec((1, d), lambda i, bb: (0, 0)),
            pl.BlockSpec((1, d, c), lambda i, bb: (i, 0, 0)),
            pl.BlockSpec((1, 1, LANES), lambda i, bb: (i, 0, 0)),
        ],
        out_specs=[ospec, ospec],
        out_shape=[out, out],
        compiler_params=_cparams(("arbitrary", "arbitrary")),
        name="mem_kv",
    )(mem, mem_gain.reshape(1, d), w_kv, k_gain)


def _memattn_kernel(q_ref, k_ref, v_ref, z_ref, o_ref):
    scale = HEAD_DIM ** -0.5
    for h in range(MEM_HEADS):
        s = _dot_nt(q_ref[0, h], k_ref[0, h]) * scale
        e = jnp.exp(s - jnp.max(s, axis=-1, keepdims=True))
        p = e / jnp.sum(e, axis=-1, keepdims=True)
        o = _dot(p.astype(BF16), v_ref[0, h])
        o_ref[:, h * LANES:(h + 1) * LANES] = (o * _silu(z_ref[:, h * LANES:(h + 1) * LANES])).astype(BF16)


def _memattn(mq, mk, mv, proj, b, s, z_unit):
    tq = min(512, s)
    nq = s // tq
    m = mk.shape[2]
    kvspec = pl.BlockSpec((1, MEM_HEADS, m, LANES), lambda bb, i: (bb, 0, 0, 0))
    return pl.pallas_call(
        _memattn_kernel,
        grid=(b, nq),
        in_specs=[
            pl.BlockSpec((1, MEM_HEADS, tq, LANES), lambda bb, i: (bb, 0, i, 0)),
            kvspec, kvspec,
            pl.BlockSpec((tq, MEM_HEADS * LANES), lambda bb, i: (bb * nq + i, z_unit // MEM_HEADS)),
        ],
        out_specs=pl.BlockSpec((tq, MEM_HEADS * LANES), lambda bb, i: (bb * nq + i, 0)),
        out_shape=jax.ShapeDtypeStruct((b * s, MEM_HEADS * LANES), BF16),
        compiler_params=_cparams(("arbitrary", "arbitrary")),
        name="mem_attention",
    )(mq, mk, mv, proj)


def _prep_odd_kernel(p_ref, cos_ref, sin_ref, cos64_ref, sin64_ref, cg_ref, mg_ref,
                     cq_ref, ck_ref, cvt_ref, iq_ref, ik_ref, mq_ref):
    cos, sin = cos_ref[...], sin_ref[...]
    cos64, sin64 = cos64_ref[...], sin64_ref[...]
    rep = DSA_HEADS // DSA_KV_HEADS
    tq = TQ_SPARSE
    n_qb = PREP_ROWS // tq

    def unit(u):
        return p_ref[:, u * LANES:(u + 1) * LANES]

    for h in range(DSA_HEADS):
        q = _rope(_rms128(unit(O_CQ + h), cg_ref[0:1, :]), cos, sin, 64).astype(BF16)
        g, r = divmod(h, rep)
        for qb in range(n_qb):
            cq_ref[0, g, qb, r * tq:(r + 1) * tq, :] = q[qb * tq:(qb + 1) * tq, :]
    for h in range(DSA_KV_HEADS):
        ck_ref[0, h] = _rope(_rms128(unit(O_CK + h), cg_ref[1:2, :]), cos, sin, 64).astype(BF16)
        cvt_ref[0, h, 0] = unit(O_CV + h).T.astype(BF16)
    lo = lax.broadcasted_iota(jnp.int32, cos.shape, 1) < 64
    for u in range(IDX_HEADS // 2):
        x = _rope(unit(O_IQ + u), cos64, sin64, 32)
        even = jnp.where(lo, x, 0.0).astype(BF16)
        odd = jnp.where(lo, pltpu.roll(x, 64, 1), 0.0).astype(BF16)
        for qb in range(n_qb):
            iq_ref[0, qb, (2 * u) * tq:(2 * u + 1) * tq, :] = even[qb * tq:(qb + 1) * tq, :]
            iq_ref[0, qb, (2 * u + 1) * tq:(2 * u + 2) * tq, :] = odd[qb * tq:(qb + 1) * tq, :]
    ik = _rope(unit(O_IKW), cos64, sin64, 32)
    ik_ref[0] = jnp.where(lo, ik, 0.0).astype(BF16)
    for h in range(MEM_HEADS):
        mq_ref[0, h] = _rms128(unit(O_MQ + h), mg_ref[...]).astype(BF16)


def _prep_odd(proj, b, s, cos, sin, cos64, sin64, dsa_gain, mem_gain):
    ts = PREP_ROWS
    nb = s // ts
    c = proj.shape[1]
    rep = DSA_HEADS // DSA_KV_HEADS
    row = lambda bb, i: (i, 0)
    const = lambda bb, i: (0, 0)
    seq = lambda n: pl.BlockSpec((1, n, ts, LANES), lambda bb, i: (bb, 0, i, 0))
    return pl.pallas_call(
        _prep_odd_kernel,
        grid=(b, nb),
        in_specs=[
            pl.BlockSpec((ts, c), lambda bb, i: (bb * nb + i, 0)),
            pl.BlockSpec((ts, LANES), row), pl.BlockSpec((ts, LANES), row),
            pl.BlockSpec((ts, LANES), row), pl.BlockSpec((ts, LANES), row),
            pl.BlockSpec((2, LANES), const), pl.BlockSpec((1, LANES), const),
        ],
        out_specs=[
            pl.BlockSpec((1, DSA_KV_HEADS, ts // TQ_SPARSE, rep * TQ_SPARSE, LANES), lambda bb, i: (bb, 0, i, 0, 0)),
            seq(DSA_KV_HEADS),
            pl.BlockSpec((1, DSA_KV_HEADS, 1, LANES, VT_BLOCK), lambda bb, i: (bb, 0, i, 0, 0)),
            pl.BlockSpec((1, ts // TQ_SPARSE, IDX_HEADS * TQ_SPARSE, LANES), lambda bb, i: (bb, i, 0, 0)),
            pl.BlockSpec((1, ts, LANES), lambda bb, i: (bb, i, 0)),
            seq(MEM_HEADS),
        ],
        out_shape=[
            jax.ShapeDtypeStruct((b, DSA_KV_HEADS, s // TQ_SPARSE, rep * TQ_SPARSE, LANES), BF16),
            jax.ShapeDtypeStruct((b, DSA_KV_HEADS, s, LANES), BF16),
            jax.ShapeDtypeStruct((b, DSA_KV_HEADS, s // VT_BLOCK, LANES, VT_BLOCK), BF16),
            jax.ShapeDtypeStruct((b, s // TQ_SPARSE, IDX_HEADS * TQ_SPARSE, LANES), BF16),
            jax.ShapeDtypeStruct((b, s, LANES), BF16),
            jax.ShapeDtypeStruct((b, MEM_HEADS, s, LANES), BF16),
        ],
        compiler_params=_cparams(("arbitrary", "arbitrary")),
        name="prep_odd",
    )(proj, cos, sin, cos64, sin64, dsa_gain, mem_gain)


def _sortable_key(x):
    bits = pltpu.bitcast(x + 0.0, jnp.int32)
    return jnp.where(bits < 0, bits ^ jnp.int32(0x7FFFFFFF), bits)


def _dsa_kernel(q_ref, k_ref, vt_ref, iq_ref, ik_ref, w_ref, ltri_ref, z_ref, o_ref,
                key_ref, m_ref, l_ref, acc_ref, sa_ref, sb_ref, ties_ref, *, tq, tk, top_k, bounded):
    rep = DSA_HEADS // DSA_KV_HEADS
    scale = HEAD_DIM ** -0.5
    q0 = pl.program_id(1) * tq
    t_row = q0 + lax.broadcasted_iota(jnp.int32, (1, tq), 1)
    n_chunks = (q0 + tq + tk - 1) // tk
    int_min = jnp.int32(-2147483648)
    heads_per_dot = 4

    w_t = (w_ref[...] * (IDX_HEADS ** -0.5 * IDX_DIM ** -0.5)).T

    def score_chunk(c, carry):
        k0 = pl.multiple_of(c * tk, tk)
        ik = ik_ref[0, pl.ds(k0, tk), :]
        sc = jnp.zeros((tk, tq), F32)
        for h0 in range(0, IDX_HEADS, heads_per_dot):
            x = _dot_nt(ik, iq_ref[0, 0, h0 * tq:(h0 + heads_per_dot) * tq, :])
            for hh in range(heads_per_dot):
                h = h0 + hh
                sc = sc + jnp.maximum(x[:, hh * tq:(hh + 1) * tq], 0.0) * w_t[IDX_DIM + h:IDX_DIM + h + 1, :]
        causal = (k0 + lax.broadcasted_iota(jnp.int32, (tk, tq), 0)) <= t_row
        key_ref[c] = _sortable_key(jnp.where(causal, sc, MASKED))
        return carry

    lax.fori_loop(0, n_chunks, score_chunk, 0)

    def count(pred):
        def body(c, acc):
            return acc + _fold8(pred(key_ref[c]).astype(F32), jnp.sum, short_chains=True)
        acc = lax.fori_loop(0, n_chunks, body, jnp.zeros((SUBLANES, tq), F32))
        return jnp.sum(acc, axis=0, keepdims=True)

    def search(it, thr_u):
        cand_u = thr_u | jnp.left_shift(jnp.int32(1), 31 - it)
        cand = cand_u ^ int_min
        return jnp.where(count(lambda kk: kk >= cand) >= top_k, cand_u, thr_u)

    thr = lax.fori_loop(0, 32, search, jnp.zeros((1, tq), jnp.int32)) ^ int_min
    budget = top_k - count(lambda kk: kk > thr)

    _online_init(m_ref, l_ref, acc_ref)
    ties_ref[...] = jnp.zeros(ties_ref.shape, F32)

    def qk(c, g, dst):
        k0 = pl.multiple_of(jnp.minimum(c, n_chunks - 1) * tk, tk)
        dst[g] = _dot_nt(k_ref[0, g, pl.ds(k0, tk), :], q_ref[0, g, 0]) * (scale * LOG2E)

    def mask(c, tail):
        cc = jnp.minimum(c, n_chunks - 1)
        keys = key_ref[cc]
        tie = keys == thr
        rank = _dot(ltri_ref[...], tie.astype(BF16)) + ties_ref[...]
        ok = (keys > thr) | (tie & (rank < budget))
        if tail:
            ok = ok & ((c * tk + lax.broadcasted_iota(jnp.int32, (tk, tq), 0)) <= t_row)
        ties_ref[...] = ties_ref[...] + _col_sum(tie.astype(F32))
        return jnp.concatenate([jnp.where(ok, 0.0, NEG_MASK)] * rep, axis=1)

    def soft(c, g, src, bias):
        cc = jnp.minimum(c, n_chunks - 1)
        _online_step(g, src[g] + bias, _pv_blocks(vt_ref, (0, g), cc, tk), m_ref, l_ref, acc_ref, bounded)

    _flash_pairs(n_chunks, DSA_KV_HEADS, qk, mask, soft, sa_ref, sb_ref)

    for g in range(DSA_KV_HEADS):
        o_t = _online_result(g, l_ref, acc_ref)
        for r in range(rep):
            h = g * rep + r
            o = o_t[:, r * tq:(r + 1) * tq].T
            o_ref[:, h * LANES:(h + 1) * LANES] = (o * _silu(z_ref[:, h * LANES:(h + 1) * LANES])).astype(BF16)


def _dsa(cq, ck, cvt, iq, ik, ltri, proj, b, s, bounded):
    tq, tk = TQ_SPARSE, TK
    nq = s // tq
    g = DSA_KV_HEADS
    rep = DSA_HEADS // g
    w = rep * tq
    kern = functools.partial(_dsa_kernel, tq=tq, tk=tk, top_k=min(DSA_TOPK_MAX, s // 4), bounded=bounded)
    return pl.pallas_call(
        kern,
        grid=(b, nq),
        in_specs=[
            pl.BlockSpec((1, g, 1, w, LANES), lambda bb, i: (bb, 0, i, 0, 0)),
            pl.BlockSpec((1, g, s, LANES), lambda bb, i: (bb, 0, 0, 0), pipeline_mode=pl.Buffered(1)),
            pl.BlockSpec((1, g, s // VT_BLOCK, LANES, VT_BLOCK), lambda bb, i: (bb, 0, 0, 0, 0),
                         pipeline_mode=pl.Buffered(1)),
            pl.BlockSpec((1, 1, IDX_HEADS * tq, LANES), lambda bb, i: (bb, i, 0, 0)),
            pl.BlockSpec((1, s, LANES), lambda bb, i: (bb, 0, 0), pipeline_mode=pl.Buffered(1)),
            pl.BlockSpec((tq, LANES), lambda bb, i: (bb * nq + i, O_IKW)),
            pl.BlockSpec((tk, tk), lambda bb, i: (0, 0)),
            pl.BlockSpec((tq, DSA_HEADS * LANES), lambda bb, i: (bb * nq + i, O_CZ // DSA_HEADS)),
        ],
        out_specs=pl.BlockSpec((tq, DSA_HEADS * LANES), lambda bb, i: (bb * nq + i, 0)),
        out_shape=jax.ShapeDtypeStruct((b * s, DSA_HEADS * LANES), BF16),
        scratch_shapes=[
            pltpu.VMEM((s // tk, tk, tq), jnp.int32),
            pltpu.VMEM((g, 1, w), F32), pltpu.VMEM((g, SUBLANES, w), F32), pltpu.VMEM((g, LANES, w), F32),
            pltpu.VMEM((g, tk, w), F32), pltpu.VMEM((g, tk, w), F32),
            pltpu.VMEM((1, tq), F32),
        ],
        compiler_params=_cparams(("arbitrary", "arbitrary")),
        name="dsa_attention",
    )(cq, ck, cvt, iq, ik, proj, ltri, proj)


def _outproj_kernel(*refs, n_parts):
    x_ref = refs[0]
    y_refs = refs[1:1 + n_parts]
    w_refs = refs[1 + n_parts:1 + 2 * n_parts]
    o_ref = refs[1 + 2 * n_parts]
    acc = x_ref[...]
    for y_ref, w_ref in zip(y_refs, w_refs):
        acc = acc + _dot(y_ref[...], w_ref[...])
    o_ref[...] = acc


def _outproj(x2, ys, w_out):
    n, d = x2.shape
    tm = min(1024, n)
    tn = 1024
    widths = [y.shape[1] for y in ys]
    starts = np.cumsum([0] + widths[:-1]).tolist()
    ws = [w_out[st:st + wd] for st, wd in zip(starts, widths)]
    kern = functools.partial(_outproj_kernel, n_parts=len(ys))
    return pl.pallas_call(
        kern,
        grid=(d // tn, n // tm),
        in_specs=([pl.BlockSpec((tm, tn), lambda j, i: (i, j))]
                  + [pl.BlockSpec((tm, wd), lambda j, i: (i, 0)) for wd in widths]
                  + [pl.BlockSpec((wd, tn), lambda j, i: (0, j)) for wd in widths]),
        out_specs=pl.BlockSpec((tm, tn), lambda j, i: (i, j)),
        out_shape=jax.ShapeDtypeStruct((n, d), F32),
        compiler_params=_cparams(("arbitrary", "arbitrary")),
        name="out_proj",
    )(x2, *ys, *ws)


def _rope_tables(pos, half, reps):
    inv = ROPE_THETA ** (-jnp.arange(half, dtype=F32) / half)
    ang = pos.astype(F32)[:, None] * inv[None, :]
    cos, sin = jnp.cos(ang), jnp.sin(ang)
    return jnp.tile(jnp.concatenate([cos, cos], -1), (1, reps)), jnp.tile(jnp.concatenate([-sin, sin], -1), (1, reps))


def _split_cols(w, sizes):
    return jnp.split(w, np.cumsum(sizes)[:-1].tolist(), axis=-1)


def _even_weight(w):
    sizes = (1024, 1536, 24, 1024, 512, 512, 512, 512, 512, 512)
    a_q, a_kv, a_g, a_z, b_q, b_k, b_v, b_z, m_q, m_z = _split_cols(w, sizes)
    d = w.shape[0]
    per_group = 3 * NSA_HEADS // NSA_KV_GROUPS
    gates = [jnp.pad(a_g[:, g * per_group:(g + 1) * per_group], ((0, 0), (0, LANES - per_group)))
             for g in range(NSA_KV_GROUPS)]
    out = jnp.concatenate([a_q, a_kv, b_q, b_k, b_v, m_q, a_z, b_z, m_z] + gates, axis=-1)
    assert out.shape == (d, EVEN_UNITS * LANES)
    return out.astype(BF16)


def _odd_weight(w):
    sizes = (1536, 512, 512, 1024, 64, 16, 1536, 512, 512)
    c_q, c_k, c_v, i_q, i_k, i_w, c_z, m_q, m_z = _split_cols(w, sizes)
    d = w.shape[0]
    ikw = jnp.pad(jnp.concatenate([i_k, i_w], -1), ((0, 0), (0, LANES - IDX_DIM - IDX_HEADS)))
    out = jnp.concatenate([c_z, c_q, c_k, c_v, i_q, m_q, m_z, ikw, jnp.zeros((d, LANES), w.dtype)], axis=-1)
    assert out.shape == (d, ODD_UNITS * LANES)
    return out.astype(BF16)


def _overlap_matrix_t(s, n_cmp_pad):
    n_cmp = (s - NSA_CMP_LEN) // NSA_CMP_STRIDE + 1
    n_sel = s // NSA_SEL_LEN
    cmp_start = np.arange(n_cmp) * NSA_CMP_STRIDE
    sel_start = np.arange(n_sel) * NSA_SEL_LEN
    ov = np.clip(np.minimum(cmp_start[:, None] + NSA_CMP_LEN, sel_start[None, :] + NSA_SEL_LEN)
                 - np.maximum(cmp_start[:, None], sel_start[None, :]), 0, None) / NSA_CMP_LEN
    full = np.zeros((n_sel, n_cmp_pad), np.float32)
    full[:, :n_cmp] = ov.T
    return jnp.asarray(full, dtype=BF16)


def _score_bound(dim, gain_q, gain_k, scale):
    return dim * jnp.max(jnp.abs(gain_q)) * jnp.max(jnp.abs(gain_k)) * (scale * LOG2E * 1.02)


def _attend(bound, fn, *operands):
    return lax.cond(bound <= FAST_LOG2_BOUND,
                    lambda *a: fn(*a, bounded=True), lambda *a: fn(*a, bounded=False), *operands)


def kernel(x, mem, norm_gain, mem_norm_gain, mem_w_kv, mem_qk_gain, w_out, even_w_in, nsa_qk_gain, nsa_cmp_pos,
           nsa_cmp_w1, nsa_cmp_w2, diff_qk_gain, diff_lambda, diff_subln_gain, odd_w_in, dsa_qk_gain):
    b, s, d = x.shape
    assert d == D_MODEL and s % TK == 0 and s >= NSA_WINDOW + TQ_SPARSE
    pos = jnp.arange(s)
    cos, sin = _rope_tables(pos, HEAD_DIM // 2, 1)
    cos64, sin64 = _rope_tables(pos, DIFF_QK_DIM // 2, 2)
    n_cmp_pad = s // NSA_CMP_STRIDE
    cmp_last = jnp.arange(n_cmp_pad) * NSA_CMP_STRIDE + NSA_CMP_LEN - 1
    cos_c, sin_c = _rope_tables(cmp_last, HEAD_DIM // 2, 1)
    ovlt = _overlap_matrix_t(s, n_cmp_pad)
    ltri = jnp.asarray(np.tril(np.ones((TK, TK), np.float32), -1), dtype=BF16)
    wbias = _window_bias(TQ_SPARSE)

    mk_all, mv_all = _memkv(mem, mem_norm_gain, mem_w_kv.astype(BF16), mem_qk_gain[:, 1:2, :])
    w_out_b = w_out.astype(BF16)

    x2 = x.reshape(b * s, d)
    for i in range(DEPTH):
        mem_q_gain = mem_qk_gain[i, 0:1, :]
        if i % 2 == 0:
            e = i // 2
            proj = _proj(x2, norm_gain[i], _even_weight(even_w_in[e]), 768)
            dg2 = jnp.tile(diff_qk_gain[e], (1, 2))
            qa, ks, kw, vst, vwt, craw, bq, bk, bvt, mq = _prep_even(
                proj, b, s, cos, sin, cos64, sin64, nsa_qk_gain[e], dg2, mem_q_gain)
            w1 = nsa_cmp_w1[e].reshape(2, NSA_CMP_LEN, HEAD_DIM, NSA_CMP_HIDDEN).astype(BF16)
            kc, vct = _compress(craw, nsa_cmp_pos[e], w1, nsa_cmp_w2[e].astype(BF16),
                                nsa_qk_gain[e, 1:2, :], cos_c, sin_c)
            y_a = _attend(_score_bound(HEAD_DIM, nsa_qk_gain[e, 0], nsa_qk_gain[e, 1:4], HEAD_DIM ** -0.5),
                          functools.partial(_nsa, b=b, s=s), qa, ks, vst, kw, vwt, kc, vct, ovlt, wbias, proj)
            lambda_init = 0.8 - 0.6 * math.exp(-0.3 * i)
            y_b = _attend(_score_bound(DIFF_QK_DIM, diff_qk_gain[e, 0], diff_qk_gain[e, 1], DIFF_QK_DIM ** -0.5),
                          functools.partial(_diff, b=b, s=s, lambda_init=lambda_init),
                          bq, bk, bvt, diff_lambda[e], diff_subln_gain[e].reshape(1, LANES), proj)
            y_m = _memattn(mq, mk_all[i], mv_all[i], proj, b, s, E_MZ)
            ys = [y_a, y_b, y_m]
        else:
            o = i // 2
            proj = _proj(x2, norm_gain[i], _odd_weight(odd_w_in[o]), 640)
            cq, ck, cvt, iq, ik, mq = _prep_odd(proj, b, s, cos, sin, cos64, sin64, dsa_qk_gain[o], mem_q_gain)
            y_c = _attend(_score_bound(HEAD_DIM, dsa_qk_gain[o, 0], dsa_qk_gain[o, 1], HEAD_DIM ** -0.5),
                          functools.partial(_dsa, b=b, s=s), cq, ck, cvt, iq, ik, ltri, proj)
            y_m = _memattn(mq, mk_all[i], mv_all[i], proj, b, s, O_MZ)
            ys = [y_c, y_m]
        x2 = _outproj(x2, ys, w_out_b[i])
    return x2.reshape(b, s, d)
```

```python
import functools
import math

import jax
import jax.numpy as jnp
import numpy as np
from jax import lax
from jax.experimental import pallas as pl
from jax.experimental.pallas import tpu as pltpu

F32 = jnp.float32
BF16 = jnp.bfloat16

D_MODEL = 2048
DEPTH = 4
HEAD_DIM = 128
ROPE_THETA = 10000.0
EPS = 1e-6
MASKED = -1e30
FORCED = 1e9
NSA_HEADS = 8
NSA_KV_GROUPS = 2
NSA_CMP_LEN = 32
NSA_CMP_STRIDE = 16
NSA_CMP_HIDDEN = 256
NSA_SEL_LEN = 64
NSA_SEL_TOPN = 16
NSA_WINDOW = 512
DIFF_HEADS = 4
DIFF_QK_DIM = 64
DSA_HEADS = 12
DSA_KV_HEADS = 4
IDX_HEADS = 16
IDX_DIM = 64
DSA_TOPK_MAX = 256
MEM_HEADS = 4

LANES = 128
SUBLANES = 8
VMEM_LIMIT_BYTES = 56 * 1024 * 1024
LOG2E = 1.4426950408889634

PREP_ROWS = 256
TQ_SPARSE = 128
TQ_DIFF = 256
TK = 512
VT_BLOCK = 256
NEG_INIT = -1e30
NEG_MASK = -2e30
FAST_LOG2_BOUND = 40.0

EVEN_UNITS = 54
E_AQ, E_AKV, E_BQ, E_BK, E_BV, E_MQ, E_AZ, E_BZ, E_MZ, E_AG = 0, 8, 20, 24, 28, 32, 36, 44, 48, 52
ODD_UNITS = 50
O_CZ, O_CQ, O_CK, O_CV, O_IQ, O_MQ, O_MZ, O_IKW = 0, 12, 24, 28, 32, 40, 44, 48


def _cparams(sem):
    return pltpu.CompilerParams(dimension_semantics=sem, vmem_limit_bytes=VMEM_LIMIT_BYTES)


def _dot(a, b):
    return jnp.dot(a, b, preferred_element_type=F32)


def _dot_nt(a, b):
    return lax.dot_general(a, b, (((1,), (1,)), ((), ())), preferred_element_type=F32)


def _silu(x):
    return x * jax.nn.sigmoid(x)


def _rms128(x, gain):
    return x * lax.rsqrt(jnp.mean(x * x, axis=-1, keepdims=True) + EPS) * gain


def _rms64(x, gain):
    lo = lax.broadcasted_iota(jnp.int32, x.shape, 1) < 64
    xx = x * x
    s_lo = jnp.sum(jnp.where(lo, xx, 0.0), axis=-1, keepdims=True)
    s_hi = jnp.sum(jnp.where(lo, 0.0, xx), axis=-1, keepdims=True)
    ms = jnp.where(lo, s_lo, s_hi) * (1.0 / 64.0)
    return x * lax.rsqrt(ms + EPS) * gain


def _partner(x, half):
    n = x.shape[-1]
    lane = lax.broadcasted_iota(jnp.int32, x.shape, 1)
    if 2 * half == n:
        return pltpu.roll(x, half, 1)
    a = pltpu.roll(x, half, 1)
    b = pltpu.roll(x, n - half, 1)
    src_a = pltpu.roll(lane, half, 1)
    want = jnp.where((lane & (2 * half - 1)) < half, lane + half, lane - half)
    return jnp.where(src_a == want, a, b)


def _rope(x, cos, sin_signed, half):
    return x * cos + _partner(x, half) * sin_signed


def _fold8(x, op, short_chains=False):
    rows, w = x.shape
    if short_chains:
        x = op(x.reshape(SUBLANES, rows // SUBLANES, w), axis=0)
        rows = rows // SUBLANES
    return op(x.reshape(rows // SUBLANES, SUBLANES, w), axis=0)


def _col_max(x):
    return jnp.max(_fold8(x, jnp.max), axis=0, keepdims=True)


def _col_sum(x):
    return jnp.sum(_fold8(x, jnp.sum), axis=0, keepdims=True)


def _proj_kernel(x_ref, g_ref, w_ref, o_ref, hn_ref):
    @pl.when(pl.program_id(1) == 0)
    def _():
        x = x_ref[...]
        hn_ref[...] = _rms128(x, g_ref[...]).astype(BF16)

    o_ref[...] = _dot(hn_ref[...], w_ref[...])


def _proj(x2, gain, w, tn):
    n, d = x2.shape
    c = w.shape[1]
    tm = min(1024, n)
    return pl.pallas_call(
        _proj_kernel,
        grid=(n // tm, c // tn),
        in_specs=[
            pl.BlockSpec((tm, d), lambda i, j: (i, 0)),
            pl.BlockSpec((1, d), lambda i, j: (0, 0)),
            pl.BlockSpec((d, tn), lambda i, j: (0, j)),
        ],
        out_specs=pl.BlockSpec((tm, tn), lambda i, j: (i, j)),
        out_shape=jax.ShapeDtypeStruct((n, c), F32),
        scratch_shapes=[pltpu.VMEM((tm, d), BF16)],
        compiler_params=_cparams(("arbitrary", "arbitrary")),
        name="proj",
    )(x2, gain.reshape(1, d), w)


def _prep_even_kernel(p_ref, cos_ref, sin_ref, cos64_ref, sin64_ref, ng_ref, dg_ref, mg_ref,
                      qa_ref, ks_ref, kw_ref, vst_ref, vwt_ref, craw_ref, bq_ref, bk_ref, bvt_ref, mq_ref):
    cos, sin = cos_ref[...], sin_ref[...]
    cos64, sin64 = cos64_ref[...], sin64_ref[...]
    rep = NSA_HEADS // NSA_KV_GROUPS
    tq = TQ_SPARSE

    def unit(u):
        return p_ref[:, u * LANES:(u + 1) * LANES]

    for h in range(NSA_HEADS):
        q = _rope(_rms128(unit(E_AQ + h), ng_ref[0:1, :]), cos, sin, 64).astype(BF16)
        g, r = divmod(h, rep)
        for qb in range(PREP_ROWS // tq):
            qa_ref[0, g, qb, r * tq:(r + 1) * tq, :] = q[qb * tq:(qb + 1) * tq, :]
    for g in range(NSA_KV_GROUPS):
        craw_ref[0, 0, g] = unit(E_AKV + 0 + g)
        craw_ref[0, 1, g] = unit(E_AKV + 2 + g)
        ks_ref[0, g] = _rope(_rms128(unit(E_AKV + 4 + g), ng_ref[2:3, :]), cos, sin, 64).astype(BF16)
        vst_ref[0, g, 0] = unit(E_AKV + 6 + g).T.astype(BF16)
        kw_ref[0, g] = _rope(_rms128(unit(E_AKV + 8 + g), ng_ref[3:4, :]), cos, sin, 64).astype(BF16)
        vw = unit(E_AKV + 10 + g)
        for kb in range(PREP_ROWS // LANES):
            vwt_ref[0, g, kb] = vw[kb * LANES:(kb + 1) * LANES, :].T.astype(BF16)
    lo = lax.broadcasted_iota(jnp.int32, cos.shape, 1) < 64
    for h in range(DIFF_HEADS):
        q = _rope(_rms64(unit(E_BQ + h), dg_ref[0:1, :]), cos64, sin64, 32)
        bq_ref[0, h, 0, 0:PREP_ROWS, :] = jnp.where(lo, q, 0.0).astype(BF16)
        bq_ref[0, h, 0, PREP_ROWS:2 * PREP_ROWS, :] = jnp.where(lo, 0.0, q).astype(BF16)
        bk_ref[0, h] = _rope(_rms64(unit(E_BK + h), dg_ref[1:2, :]), cos64, sin64, 32).astype(BF16)
        bvt_ref[0, h, 0] = unit(E_BV + h).T.astype(BF16)
    for h in range(MEM_HEADS):
        mq_ref[0, h] = _rms128(unit(E_MQ + h), mg_ref[...]).astype(BF16)


def _prep_even(proj, b, s, cos, sin, cos64, sin64, nsa_gain, diff_gain2, mem_gain):
    ts = PREP_ROWS
    assert ts == TQ_DIFF == VT_BLOCK
    nb = s // ts
    c = proj.shape[1]
    g = NSA_KV_GROUPS
    rep = NSA_HEADS // g
    row = lambda bb, i: (i, 0)
    const = lambda bb, i: (0, 0)
    seq = lambda n: pl.BlockSpec((1, n, ts, LANES), lambda bb, i: (bb, 0, i, 0))
    blk5 = lambda n, k, r, cdim: pl.BlockSpec((1, n, k, r, cdim), lambda bb, i: (bb, 0, i, 0, 0))
    return pl.pallas_call(
        _prep_even_kernel,
        grid=(b, nb),
        in_specs=[
            pl.BlockSpec((ts, c), lambda bb, i: (bb * nb + i, 0)),
            pl.BlockSpec((ts, LANES), row), pl.BlockSpec((ts, LANES), row),
            pl.BlockSpec((ts, LANES), row), pl.BlockSpec((ts, LANES), row),
            pl.BlockSpec((4, LANES), const), pl.BlockSpec((2, LANES), const), pl.BlockSpec((1, LANES), const),
        ],
        out_specs=[
            blk5(g, ts // TQ_SPARSE, rep * TQ_SPARSE, LANES),
            seq(g), seq(g),
            blk5(g, 1, LANES, VT_BLOCK),
            blk5(g, ts // LANES, LANES, LANES),
            pl.BlockSpec((1, 2, g, ts, LANES), lambda bb, i: (bb, 0, 0, i, 0)),
            blk5(DIFF_HEADS, 1, 2 * TQ_DIFF, LANES),
            seq(DIFF_HEADS),
            blk5(DIFF_HEADS, 1, LANES, VT_BLOCK),
            seq(MEM_HEADS),
        ],
        out_shape=[
            jax.ShapeDtypeStruct((b, g, s // TQ_SPARSE, rep * TQ_SPARSE, LANES), BF16),
            jax.ShapeDtypeStruct((b, g, s, LANES), BF16),
            jax.ShapeDtypeStruct((b, g, s, LANES), BF16),
            jax.ShapeDtypeStruct((b, g, s // VT_BLOCK, LANES, VT_BLOCK), BF16),
            jax.ShapeDtypeStruct((b, g, s // LANES, LANES, LANES), BF16),
            jax.ShapeDtypeStruct((b, 2, g, s, LANES), F32),
            jax.ShapeDtypeStruct((b, DIFF_HEADS, s // TQ_DIFF, 2 * TQ_DIFF, LANES), BF16),
            jax.ShapeDtypeStruct((b, DIFF_HEADS, s, LANES), BF16),
            jax.ShapeDtypeStruct((b, DIFF_HEADS, s // VT_BLOCK, LANES, VT_BLOCK), BF16),
            jax.ShapeDtypeStruct((b, MEM_HEADS, s, LANES), BF16),
        ],
        compiler_params=_cparams(("arbitrary", "arbitrary")),
        name="prep_even",
    )(proj, cos, sin, cos64, sin64, nsa_gain, diff_gain2, mem_gain)


def _compress_kernel(x_ref, pe_ref, w1_ref, w2_ref, g_ref, cos_ref, sin_ref, kc_ref, vct_ref, pad_ref, *, s, n_pad):
    for kind in range(2):
        pad_ref[0:s, :] = x_ref[0, kind, 0]
        pad_ref[s:s + NSA_CMP_LEN, :] = jnp.zeros((NSA_CMP_LEN, LANES), F32)
        acc = jnp.zeros((n_pad, NSA_CMP_HIDDEN), F32)
        for l in range(NSA_CMP_LEN):
            rows = pad_ref[pl.ds(l, n_pad, stride=NSA_CMP_STRIDE), :] + pe_ref[kind, l:l + 1, :]
            acc = acc + _dot(rows.astype(BF16), w1_ref[kind, l])
        out = _dot(_silu(acc).astype(BF16), w2_ref[kind])
        if kind == 0:
            kc_ref[0, 0] = _rope(_rms128(out, g_ref[...]), cos_ref[...], sin_ref[...], 64).astype(BF16)
        else:
            vct_ref[0, 0] = out.T.astype(BF16)


def _compress(craw, pe, w1, w2, gain, cos_c, sin_c):
    b, _, g, s, _ = craw.shape
    n_pad = s // NSA_CMP_STRIDE
    kern = functools.partial(_compress_kernel, s=s, n_pad=n_pad)
    whole = lambda shape: pl.BlockSpec(shape, lambda bb, gg: (0,) * len(shape))
    return pl.pallas_call(
        kern,
        grid=(b, g),
        in_specs=[
            pl.BlockSpec((1, 2, 1, s, LANES), lambda bb, gg: (bb, 0, gg, 0, 0)),
            whole((2, NSA_CMP_LEN, LANES)),
            whole((2, NSA_CMP_LEN, LANES, NSA_CMP_HIDDEN)),
            whole((2, NSA_CMP_HIDDEN, LANES)),
            whole((1, LANES)), whole((n_pad, LANES)), whole((n_pad, LANES)),
        ],
        out_specs=[
            pl.BlockSpec((1, 1, n_pad, LANES), lambda bb, gg: (bb, gg, 0, 0)),
            pl.BlockSpec((1, 1, LANES, n_pad), lambda bb, gg: (bb, gg, 0, 0)),
        ],
        out_shape=[
            jax.ShapeDtypeStruct((b, g, n_pad, LANES), BF16),
            jax.ShapeDtypeStruct((b, g, LANES, n_pad), BF16),
        ],
        scratch_shapes=[pltpu.VMEM((s + NSA_CMP_LEN, LANES), F32)],
        compiler_params=_cparams(("arbitrary", "arbitrary")),
        name="nsa_compress",
    )(craw, pe, w1, w2, gain, cos_c, sin_c)


def _softmax_parts_t(st, bounded):
    if not bounded:
        st = st - _col_max(st)
    e = jnp.exp2(st)
    return e.astype(BF16), 1.0 / jnp.maximum(_col_sum(e), 1e-30)


def _online_init(m_ref, l_ref, acc_ref):
    m_ref[...] = jnp.full(m_ref.shape, NEG_INIT, F32)
    l_ref[...] = jnp.zeros(l_ref.shape, F32)
    acc_ref[...] = jnp.zeros(acc_ref.shape, F32)


def _online_step(idx, st, pv, m_ref, l_ref, acc_ref, bounded):
    if bounded:
        p = jnp.exp2(st)
        l_ref[idx] = l_ref[idx] + _fold8(p, jnp.sum)
        acc_ref[idx] = acc_ref[idx] + pv(p.astype(BF16))
        return
    m_old = m_ref[idx]
    m_new = jnp.maximum(m_old, _col_max(st))
    alpha = jnp.exp2(m_old - m_new)
    p = jnp.exp2(st - m_new)
    l_ref[idx] = alpha * l_ref[idx] + _fold8(p, jnp.sum)
    acc_ref[idx] = alpha * acc_ref[idx] + pv(p.astype(BF16))
    m_ref[idx] = m_new


def _online_result(idx, l_ref, acc_ref):
    return acc_ref[idx] / jnp.maximum(jnp.sum(l_ref[idx], axis=0, keepdims=True), 1e-30)


def _flash_pairs(n_chunks, n_groups, qk, mask, soft, sa_ref, sb_ref):
    for g in range(n_groups):
        qk(0, g, sa_ref)
    n_pairs = (n_chunks - 1) // 2

    def pair(j, carry):
        a = 2 * j
        bias_a = mask(a, False)
        bias_b = mask(a + 1, False)
        for g in range(n_groups):
            qk(a + 1, g, sb_ref)
            soft(a, g, sa_ref, bias_a)
            qk(a + 2, g, sa_ref)
            soft(a + 1, g, sb_ref, bias_b)
        return carry

    lax.fori_loop(0, n_pairs, pair, 0)
    e = 2 * n_pairs

    @pl.when(e + 1 < n_chunks)
    def _():
        bias_a = mask(e, True)
        bias_b = mask(e + 1, True)
        for g in range(n_groups):
            qk(e + 1, g, sb_ref)
            soft(e, g, sa_ref, bias_a)
            soft(e + 1, g, sb_ref, bias_b)

    @pl.when(e + 1 >= n_chunks)
    def _():
        bias_a = mask(e, True)
        for g in range(n_groups):
            soft(e, g, sa_ref, bias_a)


def _pv_blocks(vt_ref, lead, chunk, tk):
    per = tk // VT_BLOCK

    def pv(p):
        out = _dot(vt_ref[lead + (chunk * per,)], p[0:VT_BLOCK])
        for i in range(1, per):
            out = out + _dot(vt_ref[lead + (chunk * per + i,)], p[i * VT_BLOCK:(i + 1) * VT_BLOCK])
        return out

    return pv


def _top_n_mask_t(scores, n):
    row = lax.broadcasted_iota(jnp.int32, scores.shape, 0).astype(F32)
    height = float(scores.shape[0])
    work = scores
    sel = jnp.zeros(scores.shape, F32)
    for _ in range(n):
        m = jnp.max(work, axis=0, keepdims=True)
        first = jnp.min(jnp.where(work == m, row, height), axis=0, keepdims=True)
        pick = row == first
        sel = jnp.where(pick, 1.0, sel)
        work = jnp.where(pick, -jnp.inf, work)
    return sel


def _nsa_kernel(q_ref, kc_ref, vct_ref, ks_ref, vst_ref, kw_ref, vwt_ref, ovlt_ref, wbias_ref, gl_ref, z_ref, o_ref,
                m_ref, l_ref, acc_ref, sa_ref, sb_ref, selt_ref, oct_ref, owt_ref, *, tq, tk, top_n, bounded):
    rep = NSA_HEADS // NSA_KV_GROUPS
    w = rep * tq
    scale = HEAD_DIM ** -0.5
    q0 = pl.program_id(2) * tq
    t_row = q0 + (lax.broadcasted_iota(jnp.int32, (1, w), 1) & (tq - 1))
    t_one = t_row[:, 0:tq]
    q = q_ref[0, 0, 0]

    n_cmp_pad = kc_ref.shape[2]
    cmp_last = lax.broadcasted_iota(jnp.int32, (n_cmp_pad, w), 0) * NSA_CMP_STRIDE + (NSA_CMP_LEN - 1)
    st = jnp.where(cmp_last <= t_row, _dot_nt(kc_ref[0, 0], q) * (scale * LOG2E), NEG_MASK)
    p, inv = _softmax_parts_t(st, bounded)
    inv = jnp.where(t_row >= NSA_CMP_LEN - 1, inv, 0.0)
    oct_ref[...] = _dot(vct_ref[0, 0], p) * inv
    imp_heads = _dot(ovlt_ref[...], p) * inv
    imp = imp_heads[:, 0:tq]
    for r in range(1, rep):
        imp = imp + imp_heads[:, r * tq:(r + 1) * tq]

    n_sel = imp.shape[0]
    sel_shift = NSA_SEL_LEN.bit_length() - 1
    j = lax.broadcasted_iota(jnp.int32, (n_sel, tq), 0)
    cur = jnp.right_shift(t_one, sel_shift)
    visible = j <= cur
    forced = (j == 0) | (j >= cur - 1)
    imp = jnp.where(visible, jnp.where(forced, FORCED, imp), MASKED)
    selt_ref[...] = _top_n_mask_t(imp, top_n)

    span = NSA_WINDOW + tq
    start = pl.multiple_of(jnp.maximum(q0 - NSA_WINDOW, 0), tq)
    st = (_dot_nt(kw_ref[0, 0, pl.ds(start, span), :], q) * (scale * LOG2E)
          + jnp.concatenate([wbias_ref[0]] * rep, axis=1))
    p, inv = _softmax_parts_t(st, bounded)
    blk0 = start // LANES
    ow = _dot(vwt_ref[0, 0, blk0], p[0:LANES])
    for i in range(1, span // LANES):
        ow = ow + _dot(vwt_ref[0, 0, blk0 + i], p[i * LANES:(i + 1) * LANES])
    owt_ref[...] = ow * inv

    _online_init(m_ref, l_ref, acc_ref)
    n_chunks = (q0 + tq + tk - 1) // tk
    blocks_per_chunk = tk // NSA_SEL_LEN

    def qk(c, g, dst):
        dst[g] = _dot_nt(ks_ref[0, 0, pl.ds(pl.multiple_of(c * tk, tk), tk), :], q) * (scale * LOG2E)

    def mask(c, tail):
        rows = selt_ref[pl.ds(pl.multiple_of(c * blocks_per_chunk, blocks_per_chunk), blocks_per_chunk), :]
        ok = jnp.concatenate(
            [jnp.broadcast_to(rows[i:i + 1, :], (NSA_SEL_LEN, tq)) for i in range(blocks_per_chunk)], axis=0) > 0.5
        if tail:
            ok = ok & ((c * tk + lax.broadcasted_iota(jnp.int32, (tk, tq), 0)) <= t_one)
        return jnp.concatenate([jnp.where(ok, 0.0, NEG_MASK)] * rep, axis=1)

    def soft(c, g, src, bias):
        _online_step(g, src[g] + bias, _pv_blocks(vst_ref, (0, 0), c, tk), m_ref, l_ref, acc_ref, bounded)

    _flash_pairs(n_chunks, 1, qk, mask, soft, sa_ref, sb_ref)

    gates_t = jax.nn.sigmoid(gl_ref[...]).T
    os_t = _online_result(0, l_ref, acc_ref)
    for r in range(rep):
        sl = slice(r * tq, (r + 1) * tq)
        out_t = (gates_t[3 * r:3 * r + 1, :] * oct_ref[:, sl] + gates_t[3 * r + 1:3 * r + 2, :] * os_t[:, sl]
                 + gates_t[3 * r + 2:3 * r + 3, :] * owt_ref[:, sl])
        o_ref[:, r * LANES:(r + 1) * LANES] = (out_t.T * _silu(z_ref[:, r * LANES:(r + 1) * LANES])).astype(BF16)


def _window_bias(tq):
    span = NSA_WINDOW + tq
    n_early = NSA_WINDOW // tq
    row = np.arange(span)[:, None]
    lane = np.arange(tq)[None, :]
    tables = []
    for i in range(n_early + 1):
        t = i * tq + lane
        s_pos = (0 if i < n_early else t[0, 0] - NSA_WINDOW) + row
        ok = (s_pos <= t) & (s_pos > t - NSA_WINDOW)
        tables.append(np.where(ok, 0.0, NEG_MASK))
    return jnp.asarray(np.stack(tables), dtype=F32)


def _nsa(qa, ks, vst, kw, vwt, kc, vct, ovlt, wbias, proj, b, s, bounded):
    tq, tk = TQ_SPARSE, TK
    nq = s // tq
    g = NSA_KV_GROUPS
    rep = NSA_HEADS // g
    w = rep * tq
    n_cmp_pad = kc.shape[2]
    n_sel = s // NSA_SEL_LEN
    kern = functools.partial(_nsa_kernel, tq=tq, tk=tk, top_n=min(NSA_SEL_TOPN, n_sel), bounded=bounded)
    per_group = lambda shape: pl.BlockSpec((1, 1) + shape, lambda bb, gg, i: (bb, gg) + (0,) * len(shape))
    return pl.pallas_call(
        kern,
        grid=(b, g, nq),
        in_specs=[
            pl.BlockSpec((1, 1, 1, w, LANES), lambda bb, gg, i: (bb, gg, i, 0, 0)),
            per_group((n_cmp_pad, LANES)), per_group((LANES, n_cmp_pad)),
            per_group((s, LANES)), per_group((s // VT_BLOCK, LANES, VT_BLOCK)),
            per_group((s, LANES)), per_group((s // LANES, LANES, LANES)),
            pl.BlockSpec((n_sel, n_cmp_pad), lambda bb, gg, i: (0, 0)),
            pl.BlockSpec((1,) + wbias.shape[1:], lambda bb, gg, i: (jnp.minimum(i, wbias.shape[0] - 1), 0, 0)),
            pl.BlockSpec((tq, LANES), lambda bb, gg, i: (bb * nq + i, E_AG + gg)),
            pl.BlockSpec((tq, rep * LANES), lambda bb, gg, i: (bb * nq + i, E_AZ // rep + gg)),
        ],
        out_specs=pl.BlockSpec((tq, rep * LANES), lambda bb, gg, i: (bb * nq + i, gg)),
        out_shape=jax.ShapeDtypeStruct((b * s, NSA_HEADS * LANES), BF16),
        scratch_shapes=[
            pltpu.VMEM((1, 1, w), F32), pltpu.VMEM((1, SUBLANES, w), F32), pltpu.VMEM((1, LANES, w), F32),
            pltpu.VMEM((1, tk, w), F32), pltpu.VMEM((1, tk, w), F32),
            pltpu.VMEM((n_sel, tq), F32), pltpu.VMEM((LANES, w), F32), pltpu.VMEM((LANES, w), F32),
        ],
        compiler_params=_cparams(("arbitrary", "arbitrary", "arbitrary")),
        name="nsa_attention",
    )(qa, kc, vct, ks, vst, kw, vwt, ovlt, wbias, proj, proj)


def _diff_kernel(q_ref, k_ref, vt_ref, lam_ref, sg_ref, z_ref, o_ref, m_ref, l_ref, acc_ref, sa_ref, sb_ref,
                 *, tq, tk, lambda_init, bounded):
    w = 2 * tq
    q0 = pl.program_id(2) * tq
    t_row = q0 + (lax.broadcasted_iota(jnp.int32, (1, w), 1) & (tq - 1))
    q = q_ref[0, 0, 0]
    _online_init(m_ref, l_ref, acc_ref)
    n_chunks = (q0 + tq + tk - 1) // tk

    def qk(c, g, dst):
        dst[g] = (_dot_nt(k_ref[0, 0, pl.ds(pl.multiple_of(c * tk, tk), tk), :], q)
                  * (DIFF_QK_DIM ** -0.5 * LOG2E))

    def mask(c, tail):
        if not tail:
            return None
        key = c * tk + lax.broadcasted_iota(jnp.int32, (tk, w), 0)
        return jnp.where(key <= t_row, 0.0, NEG_MASK)

    def soft(c, g, src, bias):
        st = src[g] if bias is None else src[g] + bias
        _online_step(g, st, _pv_blocks(vt_ref, (0, 0), c, tk), m_ref, l_ref, acc_ref, bounded)

    _flash_pairs(n_chunks, 1, qk, mask, soft, sa_ref, sb_ref)

    lv = lam_ref[...]
    lam = (jnp.exp(jnp.sum(lv[0:1] * lv[1:2], axis=-1, keepdims=True))
           - jnp.exp(jnp.sum(lv[2:3] * lv[3:4], axis=-1, keepdims=True)) + lambda_init)
    o_t = _online_result(0, l_ref, acc_ref)
    o = (o_t[:, 0:tq] - lam * o_t[:, tq:w]).T
    o = _rms128(o, sg_ref[...]) * (1.0 - lambda_init)
    o_ref[...] = (o * _silu(z_ref[...])).astype(BF16)


def _diff(bq, bk, bvt, lam_vecs, subln_gain, proj, b, s, lambda_init, bounded):
    tq, tk = TQ_DIFF, TK
    nq = s // tq
    w = 2 * tq
    kern = functools.partial(_diff_kernel, tq=tq, tk=tk, lambda_init=lambda_init, bounded=bounded)
    return pl.pallas_call(
        kern,
        grid=(b, DIFF_HEADS, nq),
        in_specs=[
            pl.BlockSpec((1, 1, 1, w, LANES), lambda bb, h, i: (bb, h, i, 0, 0)),
            pl.BlockSpec((1, 1, s, LANES), lambda bb, h, i: (bb, h, 0, 0)),
            pl.BlockSpec((1, 1, s // VT_BLOCK, LANES, VT_BLOCK), lambda bb, h, i: (bb, h, 0, 0, 0)),
            pl.BlockSpec((4, DIFF_QK_DIM), lambda bb, h, i: (0, 0)),
            pl.BlockSpec((1, LANES), lambda bb, h, i: (0, 0)),
            pl.BlockSpec((tq, LANES), lambda bb, h, i: (bb * nq + i, E_BZ + h)),
        ],
        out_specs=pl.BlockSpec((tq, LANES), lambda bb, h, i: (bb * nq + i, h)),
        out_shape=jax.ShapeDtypeStruct((b * s, DIFF_HEADS * LANES), BF16),
        scratch_shapes=[
            pltpu.VMEM((1, 1, w), F32), pltpu.VMEM((1, SUBLANES, w), F32), pltpu.VMEM((1, LANES, w), F32),
            pltpu.VMEM((1, tk, w), F32), pltpu.VMEM((1, tk, w), F32),
        ],
        compiler_params=_cparams(("arbitrary", "arbitrary", "arbitrary")),
        name="diff_attention",
    )(bq, bk, bvt, lam_vecs, subln_gain, proj)


def _memkv_kernel(mem_ref, mg_ref, w_ref, kg_ref, k_ref, v_ref):
    mem_n = _rms128(mem_ref[0], mg_ref[...]).astype(BF16)
    kv = _dot(mem_n, w_ref[0])
    for h in range(MEM_HEADS):
        k_ref[0, 0, h] = _rms128(kv[:, h * LANES:(h + 1) * LANES], kg_ref[0]).astype(BF16)
        v_ref[0, 0, h] = kv[:, (MEM_HEADS + h) * LANES:(MEM_HEADS + h + 1) * LANES].astype(BF16)


def _memkv(mem, mem_gain, w_kv, k_gain):
    b, m, d = mem.shape
    depth = w_kv.shape[0]
    c = w_kv.shape[2]
    out = jax.ShapeDtypeStruct((depth, b, MEM_HEADS, m, LANES), BF16)
    ospec = pl.BlockSpec((1, 1, MEM_HEADS, m, LANES), lambda i, bb: (i, bb, 0, 0, 0))
    return pl.pallas_call(
        _memkv_kernel,
        grid=(depth, b),
        in_specs=[
            pl.BlockSpec((1, m, d), lambda i, bb: (bb, 0, 0)),
            pl.BlockSpec((1, d), lambda i, bb: (0, 0)),
            pl.BlockSpec((1, d, c), lambda i, bb: (i, 0, 0)),
            pl.BlockSpec((1, 1, LANES), lambda i, bb: (i, 0, 0)),
        ],
        out_specs=[ospec, ospec],
        out_shape=[out, out],
        compiler_params=_cparams(("arbitrary", "arbitrary")),
        name="mem_kv",
    )(mem, mem_gain.reshape(1, d), w_kv, k_gain)


def _memattn_kernel(q_ref, k_ref, v_ref, z_ref, o_ref):
    scale = HEAD_DIM ** -0.5
    for h in range(MEM_HEADS):
        s = _dot_nt(q_ref[0, h], k_ref[0, h]) * scale
        e = jnp.exp(s - jnp.max(s, axis=-1, keepdims=True))
        p = e / jnp.sum(e, axis=-1, keepdims=True)
        o = _dot(p.astype(BF16), v_ref[0, h])
        o_ref[:, h * LANES:(h + 1) * LANES] = (o * _silu(z_ref[:, h * LANES:(h + 1) * LANES])).astype(BF16)


def _memattn(mq, mk, mv, proj, b, s, z_unit):
    tq = min(512, s)
    nq = s // tq
    m = mk.shape[2]
    kvspec = pl.BlockSpec((1, MEM_HEADS, m, LANES), lambda bb, i: (bb, 0, 0, 0))
    return pl.pallas_call(
        _memattn_kernel,
        grid=(b, nq),
        in_specs=[
            pl.BlockSpec((1, MEM_HEADS, tq, LANES), lambda bb, i: (bb, 0, i, 0)),
            kvspec, kvspec,
            pl.BlockSpec((tq, MEM_HEADS * LANES), lambda bb, i: (bb * nq + i, z_unit // MEM_HEADS)),
        ],
        out_specs=pl.BlockSpec((tq, MEM_HEADS * LANES), lambda bb, i: (bb * nq + i, 0)),
        out_shape=jax.ShapeDtypeStruct((b * s, MEM_HEADS * LANES), BF16),
        compiler_params=_cparams(("arbitrary", "arbitrary")),
        name="mem_attention",
    )(mq, mk, mv, proj)


def _prep_odd_kernel(p_ref, cos_ref, sin_ref, cos64_ref, sin64_ref, cg_ref, mg_ref,
                     cq_ref, ck_ref, cvt_ref, iq_ref, ik_ref, mq_ref):
    cos, sin = cos_ref[...], sin_ref[...]
    cos64, sin64 = cos64_ref[...], sin64_ref[...]
    rep = DSA_HEADS // DSA_KV_HEADS
    tq = TQ_SPARSE
    n_qb = PREP_ROWS // tq

    def unit(u):
        return p_ref[:, u * LANES:(u + 1) * LANES]

    for h in range(DSA_HEADS):
        q = _rope(_rms128(unit(O_CQ + h), cg_ref[0:1, :]), cos, sin, 64).astype(BF16)
        g, r = divmod(h, rep)
        for qb in range(n_qb):
            cq_ref[0, g, qb, r * tq:(r + 1) * tq, :] = q[qb * tq:(qb + 1) * tq, :]
    for h in range(DSA_KV_HEADS):
        ck_ref[0, h] = _rope(_rms128(unit(O_CK + h), cg_ref[1:2, :]), cos, sin, 64).astype(BF16)
        cvt_ref[0, h, 0] = unit(O_CV + h).T.astype(BF16)
    lo = lax.broadcasted_iota(jnp.int32, cos.shape, 1) < 64
    for u in range(IDX_HEADS // 2):
        x = _rope(unit(O_IQ + u), cos64, sin64, 32)
        even = jnp.where(lo, x, 0.0).astype(BF16)
        odd = jnp.where(lo, pltpu.roll(x, 64, 1), 0.0).astype(BF16)
        for qb in range(n_qb):
            iq_ref[0, qb, (2 * u) * tq:(2 * u + 1) * tq, :] = even[qb * tq:(qb + 1) * tq, :]
            iq_ref[0, qb, (2 * u + 1) * tq:(2 * u + 2) * tq, :] = odd[qb * tq:(qb + 1) * tq, :]
    ik = _rope(unit(O_IKW), cos64, sin64, 32)
    ik_ref[0] = jnp.where(lo, ik, 0.0).astype(BF16)
    for h in range(MEM_HEADS):
        mq_ref[0, h] = _rms128(unit(O_MQ + h), mg_ref[...]).astype(BF16)


def _prep_odd(proj, b, s, cos, sin, cos64, sin64, dsa_gain, mem_gain):
    ts = PREP_ROWS
    nb = s // ts
    c = proj.shape[1]
    rep = DSA_HEADS // DSA_KV_HEADS
    row = lambda bb, i: (i, 0)
    const = lambda bb, i: (0, 0)
    seq = lambda n: pl.BlockSpec((1, n, ts, LANES), lambda bb, i: (bb, 0, i, 0))
    return pl.pallas_call(
        _prep_odd_kernel,
        grid=(b, nb),
        in_specs=[
            pl.BlockSpec((ts, c), lambda bb, i: (bb * nb + i, 0)),
            pl.BlockSpec((ts, LANES), row), pl.BlockSpec((ts, LANES), row),
            pl.BlockSpec((ts, LANES), row), pl.BlockSpec((ts, LANES), row),
            pl.BlockSpec((2, LANES), const), pl.BlockSpec((1, LANES), const),
        ],
        out_specs=[
            pl.BlockSpec((1, DSA_KV_HEADS, ts // TQ_SPARSE, rep * TQ_SPARSE, LANES), lambda bb, i: (bb, 0, i, 0, 0)),
            seq(DSA_KV_HEADS),
            pl.BlockSpec((1, DSA_KV_HEADS, 1, LANES, VT_BLOCK), lambda bb, i: (bb, 0, i, 0, 0)),
            pl.BlockSpec((1, ts // TQ_SPARSE, IDX_HEADS * TQ_SPARSE, LANES), lambda bb, i: (bb, i, 0, 0)),
            pl.BlockSpec((1, ts, LANES), lambda bb, i: (bb, i, 0)),
            seq(MEM_HEADS),
        ],
        out_shape=[
            jax.ShapeDtypeStruct((b, DSA_KV_HEADS, s // TQ_SPARSE, rep * TQ_SPARSE, LANES), BF16),
            jax.ShapeDtypeStruct((b, DSA_KV_HEADS, s, LANES), BF16),
            jax.ShapeDtypeStruct((b, DSA_KV_HEADS, s // VT_BLOCK, LANES, VT_BLOCK), BF16),
            jax.ShapeDtypeStruct((b, s // TQ_SPARSE, IDX_HEADS * TQ_SPARSE, LANES), BF16),
            jax.ShapeDtypeStruct((b, s, LANES), BF16),
            jax.ShapeDtypeStruct((b, MEM_HEADS, s, LANES), BF16),
        ],
        compiler_params=_cparams(("arbitrary", "arbitrary")),
        name="prep_odd",
    )(proj, cos, sin, cos64, sin64, dsa_gain, mem_gain)


def _sortable_key(x):
    bits = pltpu.bitcast(x + 0.0, jnp.int32)
    return jnp.where(bits < 0, bits ^ jnp.int32(0x7FFFFFFF), bits)


def _dsa_kernel(q_ref, k_ref, vt_ref, iq_ref, ik_ref, w_ref, ltri_ref, z_ref, o_ref,
                key_ref, m_ref, l_ref, acc_ref, sa_ref, sb_ref, ties_ref, *, tq, tk, top_k, bounded):
    rep = DSA_HEADS // DSA_KV_HEADS
    scale = HEAD_DIM ** -0.5
    q0 = pl.program_id(1) * tq
    t_row = q0 + lax.broadcasted_iota(jnp.int32, (1, tq), 1)
    n_chunks = (q0 + tq + tk - 1) // tk
    int_min = jnp.int32(-2147483648)
    heads_per_dot = 4

    w_t = (w_ref[...] * (IDX_HEADS ** -0.5 * IDX_DIM ** -0.5)).T

    def score_chunk(c, carry):
        k0 = pl.multiple_of(c * tk, tk)
        ik = ik_ref[0, pl.ds(k0, tk), :]
        sc = jnp.zeros((tk, tq), F32)
        for h0 in range(0, IDX_HEADS, heads_per_dot):
            x = _dot_nt(ik, iq_ref[0, 0, h0 * tq:(h0 + heads_per_dot) * tq, :])
            for hh in range(heads_per_dot):
                h = h0 + hh
                sc = sc + jnp.maximum(x[:, hh * tq:(hh + 1) * tq], 0.0) * w_t[IDX_DIM + h:IDX_DIM + h + 1, :]
        causal = (k0 + lax.broadcasted_iota(jnp.int32, (tk, tq), 0)) <= t_row
        key_ref[c] = _sortable_key(jnp.where(causal, sc, MASKED))
        return carry

    lax.fori_loop(0, n_chunks, score_chunk, 0)

    def count(pred):
        def body(c, acc):
            return acc + _fold8(pred(key_ref[c]).astype(F32), jnp.sum, short_chains=True)
        acc = lax.fori_loop(0, n_chunks, body, jnp.zeros((SUBLANES, tq), F32))
        return jnp.sum(acc, axis=0, keepdims=True)

    def search(it, thr_u):
        cand_u = thr_u | jnp.left_shift(jnp.int32(1), 31 - it)
        cand = cand_u ^ int_min
        return jnp.where(count(lambda kk: kk >= cand) >= top_k, cand_u, thr_u)

    thr = lax.fori_loop(0, 32, search, jnp.zeros((1, tq), jnp.int32)) ^ int_min
    budget = top_k - count(lambda kk: kk > thr)

    _online_init(m_ref, l_ref, acc_ref)
    ties_ref[...] = jnp.zeros(ties_ref.shape, F32)

    def qk(c, g, dst):
        dst[g] = _dot_nt(k_ref[0, g, pl.ds(pl.multiple_of(c * tk, tk), tk), :], q_ref[0, g, 0]) * (scale * LOG2E)

    def mask(c, tail):
        keys = key_ref[c]
        tie = keys == thr
        rank = _dot(ltri_ref[...], tie.astype(BF16)) + ties_ref[...]
        ok = (keys > thr) | (tie & (rank < budget))
        if tail:
            ok = ok & ((c * tk + lax.broadcasted_iota(jnp.int32, (tk, tq), 0)) <= t_row)
        ties_ref[...] = ties_ref[...] + _col_sum(tie.astype(F32))
        return jnp.concatenate([jnp.where(ok, 0.0, NEG_MASK)] * rep, axis=1)

    def soft(c, g, src, bias):
        _online_step(g, src[g] + bias, _pv_blocks(vt_ref, (0, g), c, tk), m_ref, l_ref, acc_ref, bounded)

    _flash_pairs(n_chunks, DSA_KV_HEADS, qk, mask, soft, sa_ref, sb_ref)

    for g in range(DSA_KV_HEADS):
        o_t = _online_result(g, l_ref, acc_ref)
        for r in range(rep):
            h = g * rep + r
            o = o_t[:, r * tq:(r + 1) * tq].T
            o_ref[:, h * LANES:(h + 1) * LANES] = (o * _silu(z_ref[:, h * LANES:(h + 1) * LANES])).astype(BF16)


def _dsa(cq, ck, cvt, iq, ik, ltri, proj, b, s, bounded):
    tq, tk = TQ_SPARSE, TK
    nq = s // tq
    g = DSA_KV_HEADS
    rep = DSA_HEADS // g
    w = rep * tq
    kern = functools.partial(_dsa_kernel, tq=tq, tk=tk, top_k=min(DSA_TOPK_MAX, s // 4), bounded=bounded)
    return pl.pallas_call(
        kern,
        grid=(b, nq),
        in_specs=[
            pl.BlockSpec((1, g, 1, w, LANES), lambda bb, i: (bb, 0, i, 0, 0)),
            pl.BlockSpec((1, g, s, LANES), lambda bb, i: (bb, 0, 0, 0), pipeline_mode=pl.Buffered(1)),
            pl.BlockSpec((1, g, s // VT_BLOCK, LANES, VT_BLOCK), lambda bb, i: (bb, 0, 0, 0, 0),
                         pipeline_mode=pl.Buffered(1)),
            pl.BlockSpec((1, 1, IDX_HEADS * tq, LANES), lambda bb, i: (bb, i, 0, 0)),
            pl.BlockSpec((1, s, LANES), lambda bb, i: (bb, 0, 0), pipeline_mode=pl.Buffered(1)),
            pl.BlockSpec((tq, LANES), lambda bb, i: (bb * nq + i, O_IKW)),
            pl.BlockSpec((tk, tk), lambda bb, i: (0, 0)),
            pl.BlockSpec((tq, DSA_HEADS * LANES), lambda bb, i: (bb * nq + i, O_CZ // DSA_HEADS)),
        ],
        out_specs=pl.BlockSpec((tq, DSA_HEADS * LANES), lambda bb, i: (bb * nq + i, 0)),
        out_shape=jax.ShapeDtypeStruct((b * s, DSA_HEADS * LANES), BF16),
        scratch_shapes=[
            pltpu.VMEM((s // tk, tk, tq), jnp.int32),
            pltpu.VMEM((g, 1, w), F32), pltpu.VMEM((g, SUBLANES, w), F32), pltpu.VMEM((g, LANES, w), F32),
            pltpu.VMEM((g, tk, w), F32), pltpu.VMEM((g, tk, w), F32),
            pltpu.VMEM((1, tq), F32),
        ],
        compiler_params=_cparams(("arbitrary", "arbitrary")),
        name="dsa_attention",
    )(cq, ck, cvt, iq, ik, proj, ltri, proj)


def _outproj_kernel(*refs, n_parts):
    x_ref = refs[0]
    y_refs = refs[1:1 + n_parts]
    w_refs = refs[1 + n_parts:1 + 2 * n_parts]
    o_ref = refs[1 + 2 * n_parts]
    acc = x_ref[...]
    for y_ref, w_ref in zip(y_refs, w_refs):
        acc = acc + _dot(y_ref[...], w_ref[...])
    o_ref[...] = acc


def _outproj(x2, ys, w_out):
    n, d = x2.shape
    tm = min(1024, n)
    tn = 1024
    widths = [y.shape[1] for y in ys]
    starts = np.cumsum([0] + widths[:-1]).tolist()
    ws = [w_out[st:st + wd] for st, wd in zip(starts, widths)]
    kern = functools.partial(_outproj_kernel, n_parts=len(ys))
    return pl.pallas_call(
        kern,
        grid=(d // tn, n // tm),
        in_specs=([pl.BlockSpec((tm, tn), lambda j, i: (i, j))]
                  + [pl.BlockSpec((tm, wd), lambda j, i: (i, 0)) for wd in widths]
                  + [pl.BlockSpec((wd, tn), lambda j, i: (0, j)) for wd in widths]),
        out_specs=pl.BlockSpec((tm, tn), lambda j, i: (i, j)),
        out_shape=jax.ShapeDtypeStruct((n, d), F32),
        compiler_params=_cparams(("arbitrary", "arbitrary")),
        name="out_proj",
    )(x2, *ys, *ws)


def _rope_tables(pos, half, reps):
    inv = ROPE_THETA ** (-jnp.arange(half, dtype=F32) / half)
    ang = pos.astype(F32)[:, None] * inv[None, :]
    cos, sin = jnp.cos(ang), jnp.sin(ang)
    return jnp.tile(jnp.concatenate([cos, cos], -1), (1, reps)), jnp.tile(jnp.concatenate([-sin, sin], -1), (1, reps))


def _split_cols(w, sizes):
    return jnp.split(w, np.cumsum(sizes)[:-1].tolist(), axis=-1)


def _even_weight(w):
    sizes = (1024, 1536, 24, 1024, 512, 512, 512, 512, 512, 512)
    a_q, a_kv, a_g, a_z, b_q, b_k, b_v, b_z, m_q, m_z = _split_cols(w, sizes)
    d = w.shape[0]
    per_group = 3 * NSA_HEADS // NSA_KV_GROUPS
    gates = [jnp.pad(a_g[:, g * per_group:(g + 1) * per_group], ((0, 0), (0, LANES - per_group)))
             for g in range(NSA_KV_GROUPS)]
    out = jnp.concatenate([a_q, a_kv, b_q, b_k, b_v, m_q, a_z, b_z, m_z] + gates, axis=-1)
    assert out.shape == (d, EVEN_UNITS * LANES)
    return out.astype(BF16)


def _odd_weight(w):
    sizes = (1536, 512, 512, 1024, 64, 16, 1536, 512, 512)
    c_q, c_k, c_v, i_q, i_k, i_w, c_z, m_q, m_z = _split_cols(w, sizes)
    d = w.shape[0]
    ikw = jnp.pad(jnp.concatenate([i_k, i_w], -1), ((0, 0), (0, LANES - IDX_DIM - IDX_HEADS)))
    out = jnp.concatenate([c_z, c_q, c_k, c_v, i_q, m_q, m_z, ikw, jnp.zeros((d, LANES), w.dtype)], axis=-1)
    assert out.shape == (d, ODD_UNITS * LANES)
    return out.astype(BF16)


def _overlap_matrix_t(s, n_cmp_pad):
    n_cmp = (s - NSA_CMP_LEN) // NSA_CMP_STRIDE + 1
    n_sel = s // NSA_SEL_LEN
    cmp_start = np.arange(n_cmp) * NSA_CMP_STRIDE
    sel_start = np.arange(n_sel) * NSA_SEL_LEN
    ov = np.clip(np.minimum(cmp_start[:, None] + NSA_CMP_LEN, sel_start[None, :] + NSA_SEL_LEN)
                 - np.maximum(cmp_start[:, None], sel_start[None, :]), 0, None) / NSA_CMP_LEN
    full = np.zeros((n_sel, n_cmp_pad), np.float32)
    full[:, :n_cmp] = ov.T
    return jnp.asarray(full, dtype=BF16)


def _score_bound(dim, gain_q, gain_k, scale):
    return dim * jnp.max(jnp.abs(gain_q)) * jnp.max(jnp.abs(gain_k)) * (scale * LOG2E * 1.02)


def _attend(bound, fn, *operands):
    return lax.cond(bound <= FAST_LOG2_BOUND,
                    lambda *a: fn(*a, bounded=True), lambda *a: fn(*a, bounded=False), *operands)


def kernel(x, mem, norm_gain, mem_norm_gain, mem_w_kv, mem_qk_gain, w_out, even_w_in, nsa_qk_gain, nsa_cmp_pos,
           nsa_cmp_w1, nsa_cmp_w2, diff_qk_gain, diff_lambda, diff_subln_gain, odd_w_in, dsa_qk_gain):
    b, s, d = x.shape
    assert d == D_MODEL and s % TK == 0 and s >= NSA_WINDOW + TQ_SPARSE
    pos = jnp.arange(s)
    cos, sin = _rope_tables(pos, HEAD_DIM // 2, 1)
    cos64, sin64 = _rope_tables(pos, DIFF_QK_DIM // 2, 2)
    n_cmp_pad = s // NSA_CMP_STRIDE
    cmp_last = jnp.arange(n_cmp_pad) * NSA_CMP_STRIDE + NSA_CMP_LEN - 1
    cos_c, sin_c = _rope_tables(cmp_last, HEAD_DIM // 2, 1)
    ovlt = _overlap_matrix_t(s, n_cmp_pad)
    ltri = jnp.asarray(np.tril(np.ones((TK, TK), np.float32), -1), dtype=BF16)
    wbias = _window_bias(TQ_SPARSE)

    mk_all, mv_all = _memkv(mem, mem_norm_gain, mem_w_kv.astype(BF16), mem_qk_gain[:, 1:2, :])
    w_out_b = w_out.astype(BF16)

    x2 = x.reshape(b * s, d)
    for i in range(DEPTH):
        mem_q_gain = mem_qk_gain[i, 0:1, :]
        if i % 2 == 0:
            e = i // 2
            proj = _proj(x2, norm_gain[i], _even_weight(even_w_in[e]), 768)
            dg2 = jnp.tile(diff_qk_gain[e], (1, 2))
            qa, ks, kw, vst, vwt, craw, bq, bk, bvt, mq = _prep_even(
                proj, b, s, cos, sin, cos64, sin64, nsa_qk_gain[e], dg2, mem_q_gain)
            w1 = nsa_cmp_w1[e].reshape(2, NSA_CMP_LEN, HEAD_DIM, NSA_CMP_HIDDEN).astype(BF16)
            kc, vct = _compress(craw, nsa_cmp_pos[e], w1, nsa_cmp_w2[e].astype(BF16),
                                nsa_qk_gain[e, 1:2, :], cos_c, sin_c)
            y_a = _attend(_score_bound(HEAD_DIM, nsa_qk_gain[e, 0], nsa_qk_gain[e, 1:4], HEAD_DIM ** -0.5),
                          functools.partial(_nsa, b=b, s=s), qa, ks, vst, kw, vwt, kc, vct, ovlt, wbias, proj)
            lambda_init = 0.8 - 0.6 * math.exp(-0.3 * i)
            y_b = _attend(_score_bound(DIFF_QK_DIM, diff_qk_gain[e, 0], diff_qk_gain[e, 1], DIFF_QK_DIM ** -0.5),
                          functools.partial(_diff, b=b, s=s, lambda_init=lambda_init),
                          bq, bk, bvt, diff_lambda[e], diff_subln_gain[e].reshape(1, LANES), proj)
            y_m = _memattn(mq, mk_all[i], mv_all[i], proj, b, s, E_MZ)
            ys = [y_a, y_b, y_m]
        else:
            o = i // 2
            proj = _proj(x2, norm_gain[i], _odd_weight(odd_w_in[o]), 640)
            cq, ck, cvt, iq, ik, mq = _prep_odd(proj, b, s, cos, sin, cos64, sin64, dsa_qk_gain[o], mem_q_gain)
            y_c = _attend(_score_bound(HEAD_DIM, dsa_qk_gain[o, 0], dsa_qk_gain[o, 1], HEAD_DIM ** -0.5),
                          functools.partial(_dsa, b=b, s=s), cq, ck, cvt, iq, ik, ltri, proj)
            y_m = _memattn(mq, mk_all[i], mv_all[i], proj, b, s, O_MZ)
            ys = [y_c, y_m]
        x2 = _outproj(x2, ys, w_out_b[i])
    return x2.reshape(b, s, d)
```

```python
import functools
import math

import jax
import jax.numpy as jnp
import numpy as np
from jax import lax
from jax.experimental import pallas as pl
from jax.experimental.pallas import tpu as pltpu

F32 = jnp.float32
BF16 = jnp.bfloat16

D_MODEL = 2048
DEPTH = 4
HEAD_DIM = 128
ROPE_THETA = 10000.0
EPS = 1e-6
MASKED = -1e30
FORCED = 1e9
NSA_HEADS = 8
NSA_KV_GROUPS = 2
NSA_CMP_LEN = 32
NSA_CMP_STRIDE = 16
NSA_CMP_HIDDEN = 256
NSA_SEL_LEN = 64
NSA_SEL_TOPN = 16
NSA_WINDOW = 512
DIFF_HEADS = 4
DIFF_QK_DIM = 64
DSA_HEADS = 12
DSA_KV_HEADS = 4
IDX_HEADS = 16
IDX_DIM = 64
DSA_TOPK_MAX = 256
MEM_HEADS = 4

LANES = 128
SUBLANES = 8
VMEM_LIMIT_BYTES = 56 * 1024 * 1024
LOG2E = 1.4426950408889634

PREP_ROWS = 256
TQ_SPARSE = 128
TQ_DIFF = 256
TK = 512
VT_BLOCK = 256
NEG_INIT = -1e30
NEG_MASK = -2e30
FAST_LOG2_BOUND = 40.0

EVEN_UNITS = 54
E_AQ, E_AKV, E_BQ, E_BK, E_BV, E_MQ, E_AZ, E_BZ, E_MZ, E_AG = 0, 8, 20, 24, 28, 32, 36, 44, 48, 52
EVEN_PREP_UNITS = E_AZ
ODD_UNITS = 52
O_CQ, O_CK, O_CV, O_IQ, O_MQ, O_IKW, O_CZ, O_MZ = 0, 12, 16, 20, 28, 32, 36, 48
ODD_PREP_UNITS = O_IKW + 1


def _cparams(sem):
    return pltpu.CompilerParams(dimension_semantics=sem, vmem_limit_bytes=VMEM_LIMIT_BYTES)


def _dot(a, b):
    return jnp.dot(a, b, preferred_element_type=F32)


def _dot_nt(a, b):
    return lax.dot_general(a, b, (((1,), (1,)), ((), ())), preferred_element_type=F32)


def _silu(x):
    return x * jax.nn.sigmoid(x)


def _rms128(x, gain):
    return x * lax.rsqrt(jnp.mean(x * x, axis=-1, keepdims=True) + EPS) * gain


def _rms64(x, gain):
    lo = lax.broadcasted_iota(jnp.int32, x.shape, 1) < 64
    xx = x * x
    s_lo = jnp.sum(jnp.where(lo, xx, 0.0), axis=-1, keepdims=True)
    s_hi = jnp.sum(jnp.where(lo, 0.0, xx), axis=-1, keepdims=True)
    ms = jnp.where(lo, s_lo, s_hi) * (1.0 / 64.0)
    return x * lax.rsqrt(ms + EPS) * gain


def _partner(x, half):
    n = x.shape[-1]
    lane = lax.broadcasted_iota(jnp.int32, x.shape, 1)
    if 2 * half == n:
        return pltpu.roll(x, half, 1)
    a = pltpu.roll(x, half, 1)
    b = pltpu.roll(x, n - half, 1)
    src_a = pltpu.roll(lane, half, 1)
    want = jnp.where((lane & (2 * half - 1)) < half, lane + half, lane - half)
    return jnp.where(src_a == want, a, b)


def _rope(x, cos, sin_signed, half):
    return x * cos + _partner(x, half) * sin_signed


def _fold8(x, op, short_chains=False):
    rows, w = x.shape
    if short_chains:
        x = op(x.reshape(SUBLANES, rows // SUBLANES, w), axis=0)
        rows = rows // SUBLANES
    return op(x.reshape(rows // SUBLANES, SUBLANES, w), axis=0)


def _col_max(x):
    return jnp.max(_fold8(x, jnp.max), axis=0, keepdims=True)


def _col_sum(x):
    return jnp.sum(_fold8(x, jnp.sum), axis=0, keepdims=True)


def _proj_kernel(x_ref, g_ref, w_ref, o_ref, hn_ref):
    @pl.when(pl.program_id(1) == 0)
    def _():
        x = x_ref[...]
        hn_ref[...] = _rms128(x, g_ref[...]).astype(BF16)

    o_ref[...] = _dot(hn_ref[...], w_ref[...])


def _proj(x2, gain, w, tn):
    n, d = x2.shape
    c = w.shape[1]
    tm = min(1024, n)
    return pl.pallas_call(
        _proj_kernel,
        grid=(n // tm, c // tn),
        in_specs=[
            pl.BlockSpec((tm, d), lambda i, j: (i, 0)),
            pl.BlockSpec((1, d), lambda i, j: (0, 0)),
            pl.BlockSpec((d, tn), lambda i, j: (0, j)),
        ],
        out_specs=pl.BlockSpec((tm, tn), lambda i, j: (i, j)),
        out_shape=jax.ShapeDtypeStruct((n, c), F32),
        scratch_shapes=[pltpu.VMEM((tm, d), BF16)],
        compiler_params=_cparams(("arbitrary", "arbitrary")),
        name="proj",
    )(x2, gain.reshape(1, d), w)


def _prep_even_kernel(p_ref, cos_ref, sin_ref, cos64_ref, sin64_ref, ng_ref, dg_ref, mg_ref,
                      qa_ref, ks_ref, kw_ref, vst_ref, vwt_ref, craw_ref, bq_ref, bk_ref, bvt_ref, mq_ref):
    cos, sin = cos_ref[...], sin_ref[...]
    cos64, sin64 = cos64_ref[...], sin64_ref[...]
    rep = NSA_HEADS // NSA_KV_GROUPS
    tq = TQ_SPARSE

    def unit(u):
        return p_ref[:, u * LANES:(u + 1) * LANES]

    for h in range(NSA_HEADS):
        q = _rope(_rms128(unit(E_AQ + h), ng_ref[0:1, :]), cos, sin, 64).astype(BF16)
        g, r = divmod(h, rep)
        for qb in range(PREP_ROWS // tq):
            qa_ref[0, g, qb, r * tq:(r + 1) * tq, :] = q[qb * tq:(qb + 1) * tq, :]
    for g in range(NSA_KV_GROUPS):
        craw_ref[0, 0, g] = unit(E_AKV + 0 + g)
        craw_ref[0, 1, g] = unit(E_AKV + 2 + g)
        ks_ref[0, g] = _rope(_rms128(unit(E_AKV + 4 + g), ng_ref[2:3, :]), cos, sin, 64).astype(BF16)
        vst_ref[0, g, 0] = unit(E_AKV + 6 + g).T.astype(BF16)
        kw_ref[0, g] = _rope(_rms128(unit(E_AKV + 8 + g), ng_ref[3:4, :]), cos, sin, 64).astype(BF16)
        vw = unit(E_AKV + 10 + g)
        for kb in range(PREP_ROWS // LANES):
            vwt_ref[0, g, kb] = vw[kb * LANES:(kb + 1) * LANES, :].T.astype(BF16)
    lo = lax.broadcasted_iota(jnp.int32, cos.shape, 1) < 64
    for h in range(DIFF_HEADS):
        q = _rope(_rms64(unit(E_BQ + h), dg_ref[0:1, :]), cos64, sin64, 32)
        bq_ref[0, h, 0, 0:PREP_ROWS, :] = jnp.where(lo, q, 0.0).astype(BF16)
        bq_ref[0, h, 0, PREP_ROWS:2 * PREP_ROWS, :] = jnp.where(lo, 0.0, q).astype(BF16)
        bk_ref[0, h] = _rope(_rms64(unit(E_BK + h), dg_ref[1:2, :]), cos64, sin64, 32).astype(BF16)
        bvt_ref[0, h, 0] = unit(E_BV + h).T.astype(BF16)
    for h in range(MEM_HEADS):
        mq_ref[0, h] = _rms128(unit(E_MQ + h), mg_ref[...]).astype(BF16)


def _prep_even(proj, b, s, cos, sin, cos64, sin64, nsa_gain, diff_gain2, mem_gain):
    ts = PREP_ROWS
    assert ts == TQ_DIFF == VT_BLOCK
    nb = s // ts
    c = EVEN_PREP_UNITS * LANES
    g = NSA_KV_GROUPS
    rep = NSA_HEADS // g
    row = lambda bb, i: (i, 0)
    const = lambda bb, i: (0, 0)
    seq = lambda n: pl.BlockSpec((1, n, ts, LANES), lambda bb, i: (bb, 0, i, 0))
    blk5 = lambda n, k, r, cdim: pl.BlockSpec((1, n, k, r, cdim), lambda bb, i: (bb, 0, i, 0, 0))
    return pl.pallas_call(
        _prep_even_kernel,
        grid=(b, nb),
        in_specs=[
            pl.BlockSpec((ts, c), lambda bb, i: (bb * nb + i, 0)),
            pl.BlockSpec((ts, LANES), row), pl.BlockSpec((ts, LANES), row),
            pl.BlockSpec((ts, LANES), row), pl.BlockSpec((ts, LANES), row),
            pl.BlockSpec((4, LANES), const), pl.BlockSpec((2, LANES), const), pl.BlockSpec((1, LANES), const),
        ],
        out_specs=[
            blk5(g, ts // TQ_SPARSE, rep * TQ_SPARSE, LANES),
            seq(g), seq(g),
            blk5(g, 1, LANES, VT_BLOCK),
            blk5(g, ts // LANES, LANES, LANES),
            pl.BlockSpec((1, 2, g, ts, LANES), lambda bb, i: (bb, 0, 0, i, 0)),
            blk5(DIFF_HEADS, 1, 2 * TQ_DIFF, LANES),
            seq(DIFF_HEADS),
            blk5(DIFF_HEADS, 1, LANES, VT_BLOCK),
            seq(MEM_HEADS),
        ],
        out_shape=[
            jax.ShapeDtypeStruct((b, g, s // TQ_SPARSE, rep * TQ_SPARSE, LANES), BF16),
            jax.ShapeDtypeStruct((b, g, s, LANES), BF16),
            jax.ShapeDtypeStruct((b, g, s, LANES), BF16),
            jax.ShapeDtypeStruct((b, g, s // VT_BLOCK, LANES, VT_BLOCK), BF16),
            jax.ShapeDtypeStruct((b, g, s // LANES, LANES, LANES), BF16),
            jax.ShapeDtypeStruct((b, 2, g, s, LANES), F32),
            jax.ShapeDtypeStruct((b, DIFF_HEADS, s // TQ_DIFF, 2 * TQ_DIFF, LANES), BF16),
            jax.ShapeDtypeStruct((b, DIFF_HEADS, s, LANES), BF16),
            jax.ShapeDtypeStruct((b, DIFF_HEADS, s // VT_BLOCK, LANES, VT_BLOCK), BF16),
            jax.ShapeDtypeStruct((b, MEM_HEADS, s, LANES), BF16),
        ],
        compiler_params=_cparams(("arbitrary", "arbitrary")),
        name="prep_even",
    )(proj, cos, sin, cos64, sin64, nsa_gain, diff_gain2, mem_gain)


def _compress_kernel(x_ref, pe_ref, w1_ref, w2_ref, g_ref, cos_ref, sin_ref, kc_ref, vct_ref, pad_ref, *, s, n_pad):
    for kind in range(2):
        pad_ref[0:s, :] = x_ref[0, kind, 0]
        pad_ref[s:s + NSA_CMP_LEN, :] = jnp.zeros((NSA_CMP_LEN, LANES), F32)
        acc = jnp.zeros((n_pad, NSA_CMP_HIDDEN), F32)
        for l in range(NSA_CMP_LEN):
            rows = pad_ref[pl.ds(l, n_pad, stride=NSA_CMP_STRIDE), :] + pe_ref[kind, l:l + 1, :]
            acc = acc + _dot(rows.astype(BF16), w1_ref[kind, l])
        out = _dot(_silu(acc).astype(BF16), w2_ref[kind])
        if kind == 0:
            kc_ref[0, 0] = _rope(_rms128(out, g_ref[...]), cos_ref[...], sin_ref[...], 64).astype(BF16)
        else:
            vct_ref[0, 0] = out.T.astype(BF16)


def _compress(craw, pe, w1, w2, gain, cos_c, sin_c):
    b, _, g, s, _ = craw.shape
    n_pad = s // NSA_CMP_STRIDE
    kern = functools.partial(_compress_kernel, s=s, n_pad=n_pad)
    whole = lambda shape: pl.BlockSpec(shape, lambda bb, gg: (0,) * len(shape))
    return pl.pallas_call(
        kern,
        grid=(b, g),
        in_specs=[
            pl.BlockSpec((1, 2, 1, s, LANES), lambda bb, gg: (bb, 0, gg, 0, 0)),
            whole((2, NSA_CMP_LEN, LANES)),
            whole((2, NSA_CMP_LEN, LANES, NSA_CMP_HIDDEN)),
            whole((2, NSA_CMP_HIDDEN, LANES)),
            whole((1, LANES)), whole((n_pad, LANES)), whole((n_pad, LANES)),
        ],
        out_specs=[
            pl.BlockSpec((1, 1, n_pad, LANES), lambda bb, gg: (bb, gg, 0, 0)),
            pl.BlockSpec((1, 1, LANES, n_pad), lambda bb, gg: (bb, gg, 0, 0)),
        ],
        out_shape=[
            jax.ShapeDtypeStruct((b, g, n_pad, LANES), BF16),
            jax.ShapeDtypeStruct((b, g, LANES, n_pad), BF16),
        ],
        scratch_shapes=[pltpu.VMEM((s + NSA_CMP_LEN, LANES), F32)],
        compiler_params=_cparams(("arbitrary", "arbitrary")),
        name="nsa_compress",
    )(craw, pe, w1, w2, gain, cos_c, sin_c)


def _softmax_parts_t(st, bounded):
    if not bounded:
        st = st - _col_max(st)
    e = jnp.exp2(st)
    return e.astype(BF16), 1.0 / jnp.maximum(_col_sum(e), 1e-30)


def _online_init(m_ref, l_ref, acc_ref):
    m_ref[...] = jnp.full(m_ref.shape, NEG_INIT, F32)
    l_ref[...] = jnp.zeros(l_ref.shape, F32)
    acc_ref[...] = jnp.zeros(acc_ref.shape, F32)


def _online_step(idx, st, pv, m_ref, l_ref, acc_ref, bounded):
    if bounded:
        p = jnp.exp2(st)
        l_ref[idx] = l_ref[idx] + _fold8(p, jnp.sum)
        acc_ref[idx] = acc_ref[idx] + pv(p.astype(BF16))
        return
    m_old = m_ref[idx]
    m_new = jnp.maximum(m_old, _col_max(st))
    alpha = jnp.exp2(m_old - m_new)
    p = jnp.exp2(st - m_new)
    l_ref[idx] = alpha * l_ref[idx] + _fold8(p, jnp.sum)
    acc_ref[idx] = alpha * acc_ref[idx] + pv(p.astype(BF16))
    m_ref[idx] = m_new


def _online_result(idx, l_ref, acc_ref):
    return acc_ref[idx] / jnp.maximum(jnp.sum(l_ref[idx], axis=0, keepdims=True), 1e-30)


def _flash_pairs(n_chunks, n_groups, qk, mask, soft, sa_ref, sb_ref):
    for g in range(n_groups):
        qk(0, g, sa_ref)
    n_pairs = (n_chunks - 1) // 2

    def pair(j, carry):
        a = 2 * j
        bias_a = mask(a, False)
        bias_b = mask(a + 1, False)
        for g in range(n_groups):
            qk(a + 1, g, sb_ref)
            soft(a, g, sa_ref, bias_a)
            qk(a + 2, g, sa_ref)
            soft(a + 1, g, sb_ref, bias_b)
        return carry

    lax.fori_loop(0, n_pairs, pair, 0)
    e = 2 * n_pairs

    @pl.when(e + 1 < n_chunks)
    def _():
        bias_a = mask(e, True)
        bias_b = mask(e + 1, True)
        for g in range(n_groups):
            qk(e + 1, g, sb_ref)
            soft(e, g, sa_ref, bias_a)
            soft(e + 1, g, sb_ref, bias_b)

    @pl.when(e + 1 >= n_chunks)
    def _():
        bias_a = mask(e, True)
        for g in range(n_groups):
            soft(e, g, sa_ref, bias_a)


def _pv_blocks(vt_ref, lead, chunk, tk):
    per = tk // VT_BLOCK

    def pv(p):
        out = _dot(vt_ref[lead + (chunk * per,)], p[0:VT_BLOCK])
        for i in range(1, per):
            out = out + _dot(vt_ref[lead + (chunk * per + i,)], p[i * VT_BLOCK:(i + 1) * VT_BLOCK])
        return out

    return pv


def _top_n_mask_t(scores, n):
    row = lax.broadcasted_iota(jnp.int32, scores.shape, 0).astype(F32)
    height = float(scores.shape[0])
    work = scores
    sel = jnp.zeros(scores.shape, F32)
    for _ in range(n):
        m = jnp.max(work, axis=0, keepdims=True)
        first = jnp.min(jnp.where(work == m, row, height), axis=0, keepdims=True)
        pick = row == first
        sel = jnp.where(pick, 1.0, sel)
        work = jnp.where(pick, -jnp.inf, work)
    return sel


def _nsa_kernel(q_ref, kc_ref, vct_ref, ks_ref, vst_ref, kw_ref, vwt_ref, ovlt_ref, wbias_ref, gl_ref, z_ref, o_ref,
                m_ref, l_ref, acc_ref, sa_ref, sb_ref, selt_ref, oct_ref, owt_ref, *, tq, tk, top_n, bounded):
    rep = NSA_HEADS // NSA_KV_GROUPS
    w = rep * tq
    scale = HEAD_DIM ** -0.5
    q0 = pl.program_id(2) * tq
    t_row = q0 + (lax.broadcasted_iota(jnp.int32, (1, w), 1) & (tq - 1))
    t_one = t_row[:, 0:tq]
    q = q_ref[0, 0, 0]

    n_cmp_pad = kc_ref.shape[2]
    cmp_last = lax.broadcasted_iota(jnp.int32, (n_cmp_pad, w), 0) * NSA_CMP_STRIDE + (NSA_CMP_LEN - 1)
    st = jnp.where(cmp_last <= t_row, _dot_nt(kc_ref[0, 0], q) * (scale * LOG2E), NEG_MASK)
    p, inv = _softmax_parts_t(st, bounded)
    inv = jnp.where(t_row >= NSA_CMP_LEN - 1, inv, 0.0)
    oct_ref[...] = _dot(vct_ref[0, 0], p) * inv
    imp_heads = _dot(ovlt_ref[...], p) * inv
    imp = imp_heads[:, 0:tq]
    for r in range(1, rep):
        imp = imp + imp_heads[:, r * tq:(r + 1) * tq]

    n_sel = imp.shape[0]
    sel_shift = NSA_SEL_LEN.bit_length() - 1
    j = lax.broadcasted_iota(jnp.int32, (n_sel, tq), 0)
    cur = jnp.right_shift(t_one, sel_shift)
    visible = j <= cur
    forced = (j == 0) | (j >= cur - 1)
    imp = jnp.where(visible, jnp.where(forced, FORCED, imp), MASKED)
    selt_ref[...] = _top_n_mask_t(imp, top_n)

    span = NSA_WINDOW + tq
    start = pl.multiple_of(jnp.maximum(q0 - NSA_WINDOW, 0), tq)
    st = (_dot_nt(kw_ref[0, 0, pl.ds(start, span), :], q) * (scale * LOG2E)
          + jnp.concatenate([wbias_ref[0]] * rep, axis=1))
    p, inv = _softmax_parts_t(st, bounded)
    blk0 = start // LANES
    ow = _dot(vwt_ref[0, 0, blk0], p[0:LANES])
    for i in range(1, span // LANES):
        ow = ow + _dot(vwt_ref[0, 0, blk0 + i], p[i * LANES:(i + 1) * LANES])
    owt_ref[...] = ow * inv

    _online_init(m_ref, l_ref, acc_ref)
    n_chunks = (q0 + tq + tk - 1) // tk
    blocks_per_chunk = tk // NSA_SEL_LEN

    def qk(c, g, dst):
        dst[g] = _dot_nt(ks_ref[0, 0, pl.ds(pl.multiple_of(c * tk, tk), tk), :], q) * (scale * LOG2E)

    def mask(c, tail):
        rows = selt_ref[pl.ds(pl.multiple_of(c * blocks_per_chunk, blocks_per_chunk), blocks_per_chunk), :]
        ok = jnp.concatenate(
            [jnp.broadcast_to(rows[i:i + 1, :], (NSA_SEL_LEN, tq)) for i in range(blocks_per_chunk)], axis=0) > 0.5
        if tail:
            ok = ok & ((c * tk + lax.broadcasted_iota(jnp.int32, (tk, tq), 0)) <= t_one)
        return jnp.concatenate([jnp.where(ok, 0.0, NEG_MASK)] * rep, axis=1)

    def soft(c, g, src, bias):
        _online_step(g, src[g] + bias, _pv_blocks(vst_ref, (0, 0), c, tk), m_ref, l_ref, acc_ref, bounded)

    _flash_pairs(n_chunks, 1, qk, mask, soft, sa_ref, sb_ref)

    gates_t = jax.nn.sigmoid(gl_ref[...]).T
    os_t = _online_result(0, l_ref, acc_ref)
    for r in range(rep):
        sl = slice(r * tq, (r + 1) * tq)
        out_t = (gates_t[3 * r:3 * r + 1, :] * oct_ref[:, sl] + gates_t[3 * r + 1:3 * r + 2, :] * os_t[:, sl]
                 + gates_t[3 * r + 2:3 * r + 3, :] * owt_ref[:, sl])
        o_ref[:, r * LANES:(r + 1) * LANES] = (out_t.T * _silu(z_ref[:, r * LANES:(r + 1) * LANES])).astype(BF16)


def _window_bias(tq):
    span = NSA_WINDOW + tq
    n_early = NSA_WINDOW // tq
    row = np.arange(span)[:, None]
    lane = np.arange(tq)[None, :]
    tables = []
    for i in range(n_early + 1):
        t = i * tq + lane
        s_pos = (0 if i < n_early else t[0, 0] - NSA_WINDOW) + row
        ok = (s_pos <= t) & (s_pos > t - NSA_WINDOW)
        tables.append(np.where(ok, 0.0, NEG_MASK))
    return jnp.asarray(np.stack(tables), dtype=F32)


def _nsa(qa, ks, vst, kw, vwt, kc, vct, ovlt, wbias, proj, b, s, bounded):
    tq, tk = TQ_SPARSE, TK
    nq = s // tq
    g = NSA_KV_GROUPS
    rep = NSA_HEADS // g
    w = rep * tq
    n_cmp_pad = kc.shape[2]
    n_sel = s // NSA_SEL_LEN
    kern = functools.partial(_nsa_kernel, tq=tq, tk=tk, top_n=min(NSA_SEL_TOPN, n_sel), bounded=bounded)
    per_group = lambda shape: pl.BlockSpec((1, 1) + shape, lambda bb, gg, i: (bb, gg) + (0,) * len(shape))
    return pl.pallas_call(
        kern,
        grid=(b, g, nq),
        in_specs=[
            pl.BlockSpec((1, 1, 1, w, LANES), lambda bb, gg, i: (bb, gg, i, 0, 0)),
            per_group((n_cmp_pad, LANES)), per_group((LANES, n_cmp_pad)),
            per_group((s, LANES)), per_group((s // VT_BLOCK, LANES, VT_BLOCK)),
            per_group((s, LANES)), per_group((s // LANES, LANES, LANES)),
            pl.BlockSpec((n_sel, n_cmp_pad), lambda bb, gg, i: (0, 0)),
            pl.BlockSpec((1,) + wbias.shape[1:], lambda bb, gg, i: (jnp.minimum(i, wbias.shape[0] - 1), 0, 0)),
            pl.BlockSpec((tq, LANES), lambda bb, gg, i: (bb * nq + i, E_AG + gg)),
            pl.BlockSpec((tq, rep * LANES), lambda bb, gg, i: (bb * nq + i, E_AZ // rep + gg)),
        ],
        out_specs=pl.BlockSpec((tq, rep * LANES), lambda bb, gg, i: (bb * nq + i, gg)),
        out_shape=jax.ShapeDtypeStruct((b * s, NSA_HEADS * LANES), BF16),
        scratch_shapes=[
            pltpu.VMEM((1, 1, w), F32), pltpu.VMEM((1, SUBLANES, w), F32), pltpu.VMEM((1, LANES, w), F32),
            pltpu.VMEM((1, tk, w), F32), pltpu.VMEM((1, tk, w), F32),
            pltpu.VMEM((n_sel, tq), F32), pltpu.VMEM((LANES, w), F32), pltpu.VMEM((LANES, w), F32),
        ],
        compiler_params=_cparams(("arbitrary", "arbitrary", "arbitrary")),
        name="nsa_attention",
    )(qa, kc, vct, ks, vst, kw, vwt, ovlt, wbias, proj, proj)


def _diff_kernel(q_ref, k_ref, vt_ref, lam_ref, sg_ref, z_ref, o_ref, m_ref, l_ref, acc_ref, sa_ref, sb_ref,
                 *, tq, tk, lambda_init, bounded):
    w = 2 * tq
    q0 = pl.program_id(2) * tq
    t_row = q0 + (lax.broadcasted_iota(jnp.int32, (1, w), 1) & (tq - 1))
    q = q_ref[0, 0, 0]
    _online_init(m_ref, l_ref, acc_ref)
    n_chunks = (q0 + tq + tk - 1) // tk

    def qk(c, g, dst):
        dst[g] = (_dot_nt(k_ref[0, 0, pl.ds(pl.multiple_of(c * tk, tk), tk), :], q)
                  * (DIFF_QK_DIM ** -0.5 * LOG2E))

    def mask(c, tail):
        if not tail:
            return None
        key = c * tk + lax.broadcasted_iota(jnp.int32, (tk, w), 0)
        return jnp.where(key <= t_row, 0.0, NEG_MASK)

    def soft(c, g, src, bias):
        st = src[g] if bias is None else src[g] + bias
        _online_step(g, st, _pv_blocks(vt_ref, (0, 0), c, tk), m_ref, l_ref, acc_ref, bounded)

    _flash_pairs(n_chunks, 1, qk, mask, soft, sa_ref, sb_ref)

    lv = lam_ref[...]
    lam = (jnp.exp(jnp.sum(lv[0:1] * lv[1:2], axis=-1, keepdims=True))
           - jnp.exp(jnp.sum(lv[2:3] * lv[3:4], axis=-1, keepdims=True)) + lambda_init)
    o_t = _online_result(0, l_ref, acc_ref)
    o = (o_t[:, 0:tq] - lam * o_t[:, tq:w]).T
    o = _rms128(o, sg_ref[...]) * (1.0 - lambda_init)
    o_ref[...] = (o * _silu(z_ref[...])).astype(BF16)


def _diff(bq, bk, bvt, lam_vecs, subln_gain, proj, b, s, lambda_init, bounded):
    tq, tk = TQ_DIFF, TK
    nq = s // tq
    w = 2 * tq
    kern = functools.partial(_diff_kernel, tq=tq, tk=tk, lambda_init=lambda_init, bounded=bounded)
    return pl.pallas_call(
        kern,
        grid=(b, DIFF_HEADS, nq),
        in_specs=[
            pl.BlockSpec((1, 1, 1, w, LANES), lambda bb, h, i: (bb, h, i, 0, 0)),
            pl.BlockSpec((1, 1, s, LANES), lambda bb, h, i: (bb, h, 0, 0)),
            pl.BlockSpec((1, 1, s // VT_BLOCK, LANES, VT_BLOCK), lambda bb, h, i: (bb, h, 0, 0, 0)),
            pl.BlockSpec((4, DIFF_QK_DIM), lambda bb, h, i: (0, 0)),
            pl.BlockSpec((1, LANES), lambda bb, h, i: (0, 0)),
            pl.BlockSpec((tq, LANES), lambda bb, h, i: (bb * nq + i, E_BZ + h)),
        ],
        out_specs=pl.BlockSpec((tq, LANES), lambda bb, h, i: (bb * nq + i, h)),
        out_shape=jax.ShapeDtypeStruct((b * s, DIFF_HEADS * LANES), BF16),
        scratch_shapes=[
            pltpu.VMEM((1, 1, w), F32), pltpu.VMEM((1, SUBLANES, w), F32), pltpu.VMEM((1, LANES, w), F32),
            pltpu.VMEM((1, tk, w), F32), pltpu.VMEM((1, tk, w), F32),
        ],
        compiler_params=_cparams(("arbitrary", "arbitrary", "arbitrary")),
        name="diff_attention",
    )(bq, bk, bvt, lam_vecs, subln_gain, proj)


def _memkv_kernel(mem_ref, mg_ref, w_ref, kg_ref, k_ref, v_ref):
    mem_n = _rms128(mem_ref[0], mg_ref[...]).astype(BF16)
    kv = _dot(mem_n, w_ref[0])
    for h in range(MEM_HEADS):
        k_ref[0, 0, h] = _rms128(kv[:, h * LANES:(h + 1) * LANES], kg_ref[0]).astype(BF16)
        v_ref[0, 0, h] = kv[:, (MEM_HEADS + h) * LANES:(MEM_HEADS + h + 1) * LANES].astype(BF16)


def _memkv(mem, mem_gain, w_kv, k_gain):
    b, m, d = mem.shape
    depth = w_kv.shape[0]
    c = w_kv.shape[2]
    out = jax.ShapeDtypeStruct((depth, b, MEM_HEADS, m, LANES), BF16)
    ospec = pl.BlockSpec((1, 1, MEM_HEADS, m, LANES), lambda i, bb: (i, bb, 0, 0, 0))
    return pl.pallas_call(
        _memkv_kernel,
        grid=(depth, b),
        in_specs=[
            pl.BlockSpec((1, m, d), lambda i, bb: (bb, 0, 0)),
            pl.BlockSpec((1, d), lambda i, bb: (0, 0)),
            pl.BlockSpec((1, d, c), lambda i, bb: (i, 0, 0)),
            pl.BlockSpec((1, 1, LANES), lambda i, bb: (i, 0, 0)),
        ],
        out_specs=[ospec, ospec],
        out_shape=[out, out],
        compiler_params=_cparams(("arbitrary", "arbitrary")),
        name="mem_kv",
    )(mem, mem_gain.reshape(1, d), w_kv, k_gain)


def _memattn_kernel(q_ref, k_ref, v_ref, z_ref, o_ref):
    scale = HEAD_DIM ** -0.5
    for h in range(MEM_HEADS):
        s = _dot_nt(q_ref[0, h], k_ref[0, h]) * scale
        e = jnp.exp(s - jnp.max(s, axis=-1, keepdims=True))
        p = e / jnp.sum(e, axis=-1, keepdims=True)
        o = _dot(p.astype(BF16), v_ref[0, h])
        o_ref[:, h * LANES:(h + 1) * LANES] = (o * _silu(z_ref[:, h * LANES:(h + 1) * LANES])).astype(BF16)


def _memattn(mq, mk, mv, proj, b, s, z_unit):
    tq = min(512, s)
    nq = s // tq
    m = mk.shape[2]
    kvspec = pl.BlockSpec((1, MEM_HEADS, m, LANES), lambda bb, i: (bb, 0, 0, 0))
    return pl.pallas_call(
        _memattn_kernel,
        grid=(b, nq),
        in_specs=[
            pl.BlockSpec((1, MEM_HEADS, tq, LANES), lambda bb, i: (bb, 0, i, 0)),
            kvspec, kvspec,
            pl.BlockSpec((tq, MEM_HEADS * LANES), lambda bb, i: (bb * nq + i, z_unit // MEM_HEADS)),
        ],
        out_specs=pl.BlockSpec((tq, MEM_HEADS * LANES), lambda bb, i: (bb * nq + i, 0)),
        out_shape=jax.ShapeDtypeStruct((b * s, MEM_HEADS * LANES), BF16),
        compiler_params=_cparams(("arbitrary", "arbitrary")),
        name="mem_attention",
    )(mq, mk, mv, proj)


def _prep_odd_kernel(p_ref, cos_ref, sin_ref, cos64_ref, sin64_ref, cg_ref, mg_ref,
                     cq_ref, ck_ref, cvt_ref, iq_ref, ik_ref, mq_ref):
    cos, sin = cos_ref[...], sin_ref[...]
    cos64, sin64 = cos64_ref[...], sin64_ref[...]
    rep = DSA_HEADS // DSA_KV_HEADS
    tq = TQ_SPARSE
    n_qb = PREP_ROWS // tq

    def unit(u):
        return p_ref[:, u * LANES:(u + 1) * LANES]

    for h in range(DSA_HEADS):
        q = _rope(_rms128(unit(O_CQ + h), cg_ref[0:1, :]), cos, sin, 64).astype(BF16)
        g, r = divmod(h, rep)
        for qb in range(n_qb):
            cq_ref[0, g, qb, r * tq:(r + 1) * tq, :] = q[qb * tq:(qb + 1) * tq, :]
    for h in range(DSA_KV_HEADS):
        ck_ref[0, h] = _rope(_rms128(unit(O_CK + h), cg_ref[1:2, :]), cos, sin, 64).astype(BF16)
        cvt_ref[0, h, 0] = unit(O_CV + h).T.astype(BF16)
    lo = lax.broadcasted_iota(jnp.int32, cos.shape, 1) < 64
    for u in range(IDX_HEADS // 2):
        x = _rope(unit(O_IQ + u), cos64, sin64, 32)
        even = jnp.where(lo, x, 0.0).astype(BF16)
        odd = jnp.where(lo, pltpu.roll(x, 64, 1), 0.0).astype(BF16)
        for qb in range(n_qb):
            iq_ref[0, qb, (2 * u) * tq:(2 * u + 1) * tq, :] = even[qb * tq:(qb + 1) * tq, :]
            iq_ref[0, qb, (2 * u + 1) * tq:(2 * u + 2) * tq, :] = odd[qb * tq:(qb + 1) * tq, :]
    ik = _rope(unit(O_IKW), cos64, sin64, 32)
    ik_ref[0] = jnp.where(lo, ik, 0.0).astype(BF16)
    for h in range(MEM_HEADS):
        mq_ref[0, h] = _rms128(unit(O_MQ + h), mg_ref[...]).astype(BF16)


def _prep_odd(proj, b, s, cos, sin, cos64, sin64, dsa_gain, mem_gain):
    ts = PREP_ROWS
    nb = s // ts
    c = ODD_PREP_UNITS * LANES
    rep = DSA_HEADS // DSA_KV_HEADS
    row = lambda bb, i: (i, 0)
    const = lambda bb, i: (0, 0)
    seq = lambda n: pl.BlockSpec((1, n, ts, LANES), lambda bb, i: (bb, 0, i, 0))
    return pl.pallas_call(
        _prep_odd_kernel,
        grid=(b, nb),
        in_specs=[
            pl.BlockSpec((ts, c), lambda bb, i: (bb * nb + i, 0)),
            pl.BlockSpec((ts, LANES), row), pl.BlockSpec((ts, LANES), row),
            pl.BlockSpec((ts, LANES), row), pl.BlockSpec((ts, LANES), row),
            pl.BlockSpec((2, LANES), const), pl.BlockSpec((1, LANES), const),
        ],
        out_specs=[
            pl.BlockSpec((1, DSA_KV_HEADS, ts // TQ_SPARSE, rep * TQ_SPARSE, LANES), lambda bb, i: (bb, 0, i, 0, 0)),
            seq(DSA_KV_HEADS),
            pl.BlockSpec((1, DSA_KV_HEADS, 1, LANES, VT_BLOCK), lambda bb, i: (bb, 0, i, 0, 0)),
            pl.BlockSpec((1, ts // TQ_SPARSE, IDX_HEADS * TQ_SPARSE, LANES), lambda bb, i: (bb, i, 0, 0)),
            pl.BlockSpec((1, ts, LANES), lambda bb, i: (bb, i, 0)),
            seq(MEM_HEADS),
        ],
        out_shape=[
            jax.ShapeDtypeStruct((b, DSA_KV_HEADS, s // TQ_SPARSE, rep * TQ_SPARSE, LANES), BF16),
            jax.ShapeDtypeStruct((b, DSA_KV_HEADS, s, LANES), BF16),
            jax.ShapeDtypeStruct((b, DSA_KV_HEADS, s // VT_BLOCK, LANES, VT_BLOCK), BF16),
            jax.ShapeDtypeStruct((b, s // TQ_SPARSE, IDX_HEADS * TQ_SPARSE, LANES), BF16),
            jax.ShapeDtypeStruct((b, s, LANES), BF16),
            jax.ShapeDtypeStruct((b, MEM_HEADS, s, LANES), BF16),
        ],
        compiler_params=_cparams(("arbitrary", "arbitrary")),
        name="prep_odd",
    )(proj, cos, sin, cos64, sin64, dsa_gain, mem_gain)


def _sortable_key(x):
    bits = pltpu.bitcast(x + 0.0, jnp.int32)
    return jnp.where(bits < 0, bits ^ jnp.int32(0x7FFFFFFF), bits)


def _dsa_kernel(q_ref, k_ref, vt_ref, iq_ref, ik_ref, w_ref, ltri_ref, z_ref, o_ref,
                key_ref, m_ref, l_ref, acc_ref, sa_ref, sb_ref, ties_ref, *, tq, tk, top_k, bounded):
    rep = DSA_HEADS // DSA_KV_HEADS
    scale = HEAD_DIM ** -0.5
    q0 = pl.program_id(1) * tq
    t_row = q0 + lax.broadcasted_iota(jnp.int32, (1, tq), 1)
    n_chunks = (q0 + tq + tk - 1) // tk
    int_min = jnp.int32(-2147483648)
    heads_per_dot = 4

    w_t = (w_ref[...] * (IDX_HEADS ** -0.5 * IDX_DIM ** -0.5)).T

    def score_chunk(c, carry):
        k0 = pl.multiple_of(c * tk, tk)
        ik = ik_ref[0, pl.ds(k0, tk), :]
        sc = jnp.zeros((tk, tq), F32)
        for h0 in range(0, IDX_HEADS, heads_per_dot):
            x = _dot_nt(ik, iq_ref[0, 0, h0 * tq:(h0 + heads_per_dot) * tq, :])
            for hh in range(heads_per_dot):
                h = h0 + hh
                sc = sc + jnp.maximum(x[:, hh * tq:(hh + 1) * tq], 0.0) * w_t[IDX_DIM + h:IDX_DIM + h + 1, :]
        causal = (k0 + lax.broadcasted_iota(jnp.int32, (tk, tq), 0)) <= t_row
        key_ref[c] = _sortable_key(jnp.where(causal, sc, MASKED))
        return carry

    lax.fori_loop(0, n_chunks, score_chunk, 0)

    def count(pred):
        def body(c, acc):
            return acc + _fold8(pred(key_ref[c]).astype(F32), jnp.sum, short_chains=True)
        acc = lax.fori_loop(0, n_chunks, body, jnp.zeros((SUBLANES, tq), F32))
        return jnp.sum(acc, axis=0, keepdims=True)

    def search(it, thr_u):
        cand_u = thr_u | jnp.left_shift(jnp.int32(1), 31 - it)
        cand = cand_u ^ int_min
        return jnp.where(count(lambda kk: kk >= cand) >= top_k, cand_u, thr_u)

    thr = lax.fori_loop(0, 32, search, jnp.zeros((1, tq), jnp.int32)) ^ int_min
    budget = top_k - count(lambda kk: kk > thr)

    _online_init(m_ref, l_ref, acc_ref)
    ties_ref[...] = jnp.zeros(ties_ref.shape, F32)

    def qk(c, g, dst):
        dst[g] = _dot_nt(k_ref[0, g, pl.ds(pl.multiple_of(c * tk, tk), tk), :], q_ref[0, g, 0]) * (scale * LOG2E)

    def mask(c, tail):
        keys = key_ref[c]
        tie = keys == thr
        rank = _dot(ltri_ref[...], tie.astype(BF16)) + ties_ref[...]
        ok = (keys > thr) | (tie & (rank < budget))
        if tail:
            ok = ok & ((c * tk + lax.broadcasted_iota(jnp.int32, (tk, tq), 0)) <= t_row)
        ties_ref[...] = ties_ref[...] + _col_sum(tie.astype(F32))
        return jnp.concatenate([jnp.where(ok, 0.0, NEG_MASK)] * rep, axis=1)

    def soft(c, g, src, bias):
        _online_step(g, src[g] + bias, _pv_blocks(vt_ref, (0, g), c, tk), m_ref, l_ref, acc_ref, bounded)

    _flash_pairs(n_chunks, DSA_KV_HEADS, qk, mask, soft, sa_ref, sb_ref)

    for g in range(DSA_KV_HEADS):
        o_t = _online_result(g, l_ref, acc_ref)
        for r in range(rep):
            h = g * rep + r
            o = o_t[:, r * tq:(r + 1) * tq].T
            o_ref[:, h * LANES:(h + 1) * LANES] = (o * _silu(z_ref[:, h * LANES:(h + 1) * LANES])).astype(BF16)


def _dsa(cq, ck, cvt, iq, ik, ltri, proj, b, s, bounded):
    tq, tk = TQ_SPARSE, TK
    nq = s // tq
    g = DSA_KV_HEADS
    rep = DSA_HEADS // g
    w = rep * tq
    kern = functools.partial(_dsa_kernel, tq=tq, tk=tk, top_k=min(DSA_TOPK_MAX, s // 4), bounded=bounded)
    return pl.pallas_call(
        kern,
        grid=(b, nq),
        in_specs=[
            pl.BlockSpec((1, g, 1, w, LANES), lambda bb, i: (bb, 0, i, 0, 0)),
            pl.BlockSpec((1, g, s, LANES), lambda bb, i: (bb, 0, 0, 0), pipeline_mode=pl.Buffered(1)),
            pl.BlockSpec((1, g, s // VT_BLOCK, LANES, VT_BLOCK), lambda bb, i: (bb, 0, 0, 0, 0),
                         pipeline_mode=pl.Buffered(1)),
            pl.BlockSpec((1, 1, IDX_HEADS * tq, LANES), lambda bb, i: (bb, i, 0, 0)),
            pl.BlockSpec((1, s, LANES), lambda bb, i: (bb, 0, 0), pipeline_mode=pl.Buffered(1)),
            pl.BlockSpec((tq, LANES), lambda bb, i: (bb * nq + i, O_IKW)),
            pl.BlockSpec((tk, tk), lambda bb, i: (0, 0)),
            pl.BlockSpec((tq, DSA_HEADS * LANES), lambda bb, i: (bb * nq + i, O_CZ // DSA_HEADS)),
        ],
        out_specs=pl.BlockSpec((tq, DSA_HEADS * LANES), lambda bb, i: (bb * nq + i, 0)),
        out_shape=jax.ShapeDtypeStruct((b * s, DSA_HEADS * LANES), BF16),
        scratch_shapes=[
            pltpu.VMEM((s // tk, tk, tq), jnp.int32),
            pltpu.VMEM((g, 1, w), F32), pltpu.VMEM((g, SUBLANES, w), F32), pltpu.VMEM((g, LANES, w), F32),
            pltpu.VMEM((g, tk, w), F32), pltpu.VMEM((g, tk, w), F32),
            pltpu.VMEM((1, tq), F32),
        ],
        compiler_params=_cparams(("arbitrary", "arbitrary")),
        name="dsa_attention",
    )(cq, ck, cvt, iq, ik, proj, ltri, proj)


def _outproj_kernel(*refs, n_parts):
    x_ref = refs[0]
    y_refs = refs[1:1 + n_parts]
    w_refs = refs[1 + n_parts:1 + 2 * n_parts]
    o_ref = refs[1 + 2 * n_parts]
    acc = x_ref[...]
    for y_ref, w_ref in zip(y_refs, w_refs):
        acc = acc + _dot(y_ref[...], w_ref[...])
    o_ref[...] = acc


def _outproj(x2, ys, w_out):
    n, d = x2.shape
    tm = min(1024, n)
    tn = 1024
    widths = [y.shape[1] for y in ys]
    starts = np.cumsum([0] + widths[:-1]).tolist()
    ws = [w_out[st:st + wd] for st, wd in zip(starts, widths)]
    kern = functools.partial(_outproj_kernel, n_parts=len(ys))
    return pl.pallas_call(
        kern,
        grid=(d // tn, n // tm),
        in_specs=([pl.BlockSpec((tm, tn), lambda j, i: (i, j))]
                  + [pl.BlockSpec((tm, wd), lambda j, i: (i, 0)) for wd in widths]
                  + [pl.BlockSpec((wd, tn), lambda j, i: (0, j)) for wd in widths]),
        out_specs=pl.BlockSpec((tm, tn), lambda j, i: (i, j)),
        out_shape=jax.ShapeDtypeStruct((n, d), F32),
        compiler_params=_cparams(("arbitrary", "arbitrary")),
        name="out_proj",
    )(x2, *ys, *ws)


def _rope_tables(pos, half, reps):
    inv = ROPE_THETA ** (-jnp.arange(half, dtype=F32) / half)
    ang = pos.astype(F32)[:, None] * inv[None, :]
    cos, sin = jnp.cos(ang), jnp.sin(ang)
    return jnp.tile(jnp.concatenate([cos, cos], -1), (1, reps)), jnp.tile(jnp.concatenate([-sin, sin], -1), (1, reps))


def _split_cols(w, sizes):
    return jnp.split(w, np.cumsum(sizes)[:-1].tolist(), axis=-1)


def _even_weight(w):
    sizes = (1024, 1536, 24, 1024, 512, 512, 512, 512, 512, 512)
    a_q, a_kv, a_g, a_z, b_q, b_k, b_v, b_z, m_q, m_z = _split_cols(w, sizes)
    d = w.shape[0]
    per_group = 3 * NSA_HEADS // NSA_KV_GROUPS
    gates = [jnp.pad(a_g[:, g * per_group:(g + 1) * per_group], ((0, 0), (0, LANES - per_group)))
             for g in range(NSA_KV_GROUPS)]
    out = jnp.concatenate([a_q, a_kv, b_q, b_k, b_v, m_q, a_z, b_z, m_z] + gates, axis=-1)
    assert out.shape == (d, EVEN_UNITS * LANES)
    return out.astype(BF16)


def _odd_weight(w):
    sizes = (1536, 512, 512, 1024, 64, 16, 1536, 512, 512)
    c_q, c_k, c_v, i_q, i_k, i_w, c_z, m_q, m_z = _split_cols(w, sizes)
    d = w.shape[0]
    ikw = jnp.pad(jnp.concatenate([i_k, i_w], -1), ((0, 0), (0, LANES - IDX_DIM - IDX_HEADS)))
    pad = jnp.zeros((d, (O_CZ - ODD_PREP_UNITS) * LANES), w.dtype)
    out = jnp.concatenate([c_q, c_k, c_v, i_q, m_q, ikw, pad, c_z, m_z], axis=-1)
    assert out.shape == (d, ODD_UNITS * LANES)
    return out.astype(BF16)


def _overlap_matrix_t(s, n_cmp_pad):
    n_cmp = (s - NSA_CMP_LEN) // NSA_CMP_STRIDE + 1
    n_sel = s // NSA_SEL_LEN
    cmp_start = np.arange(n_cmp) * NSA_CMP_STRIDE
    sel_start = np.arange(n_sel) * NSA_SEL_LEN
    ov = np.clip(np.minimum(cmp_start[:, None] + NSA_CMP_LEN, sel_start[None, :] + NSA_SEL_LEN)
                 - np.maximum(cmp_start[:, None], sel_start[None, :]), 0, None) / NSA_CMP_LEN
    full = np.zeros((n_sel, n_cmp_pad), np.float32)
    full[:, :n_cmp] = ov.T
    return jnp.asarray(full, dtype=BF16)


def _score_bound(dim, gain_q, gain_k, scale):
    return dim * jnp.max(jnp.abs(gain_q)) * jnp.max(jnp.abs(gain_k)) * (scale * LOG2E * 1.02)


def _attend(bound, fn, *operands):
    return lax.cond(bound <= FAST_LOG2_BOUND,
                    lambda *a: fn(*a, bounded=True), lambda *a: fn(*a, bounded=False), *operands)


def kernel(x, mem, norm_gain, mem_norm_gain, mem_w_kv, mem_qk_gain, w_out, even_w_in, nsa_qk_gain, nsa_cmp_pos,
           nsa_cmp_w1, nsa_cmp_w2, diff_qk_gain, diff_lambda, diff_subln_gain, odd_w_in, dsa_qk_gain):
    b, s, d = x.shape
    assert d == D_MODEL and s % TK == 0 and s >= NSA_WINDOW + TQ_SPARSE
    pos = jnp.arange(s)
    cos, sin = _rope_tables(pos, HEAD_DIM // 2, 1)
    cos64, sin64 = _rope_tables(pos, DIFF_QK_DIM // 2, 2)
    n_cmp_pad = s // NSA_CMP_STRIDE
    cmp_last = jnp.arange(n_cmp_pad) * NSA_CMP_STRIDE + NSA_CMP_LEN - 1
    cos_c, sin_c = _rope_tables(cmp_last, HEAD_DIM // 2, 1)
    ovlt = _overlap_matrix_t(s, n_cmp_pad)
    ltri = jnp.asarray(np.tril(np.ones((TK, TK), np.float32), -1), dtype=BF16)
    wbias = _window_bias(TQ_SPARSE)

    mk_all, mv_all = _memkv(mem, mem_norm_gain, mem_w_kv.astype(BF16), mem_qk_gain[:, 1:2, :])
    w_out_b = w_out.astype(BF16)

    x2 = x.reshape(b * s, d)
    for i in range(DEPTH):
        mem_q_gain = mem_qk_gain[i, 0:1, :]
        if i % 2 == 0:
            e = i // 2
            proj = _proj(x2, norm_gain[i], _even_weight(even_w_in[e]), 768)
            dg2 = jnp.tile(diff_qk_gain[e], (1, 2))
            qa, ks, kw, vst, vwt, craw, bq, bk, bvt, mq = _prep_even(
                proj, b, s, cos, sin, cos64, sin64, nsa_qk_gain[e], dg2, mem_q_gain)
            w1 = nsa_cmp_w1[e].reshape(2, NSA_CMP_LEN, HEAD_DIM, NSA_CMP_HIDDEN).astype(BF16)
            kc, vct = _compress(craw, nsa_cmp_pos[e], w1, nsa_cmp_w2[e].astype(BF16),
                                nsa_qk_gain[e, 1:2, :], cos_c, sin_c)
            y_a = _attend(_score_bound(HEAD_DIM, nsa_qk_gain[e, 0], nsa_qk_gain[e, 1:4], HEAD_DIM ** -0.5),
                          functools.partial(_nsa, b=b, s=s), qa, ks, vst, kw, vwt, kc, vct, ovlt, wbias, proj)
            lambda_init = 0.8 - 0.6 * math.exp(-0.3 * i)
            y_b = _attend(_score_bound(DIFF_QK_DIM, diff_qk_gain[e, 0], diff_qk_gain[e, 1], DIFF_QK_DIM ** -0.5),
                          functools.partial(_diff, b=b, s=s, lambda_init=lambda_init),
                          bq, bk, bvt, diff_lambda[e], diff_subln_gain[e].reshape(1, LANES), proj)
            y_m = _memattn(mq, mk_all[i], mv_all[i], proj, b, s, E_MZ)
            ys = [y_a, y_b, y_m]
        else:
            o = i // 2
            proj = _proj(x2, norm_gain[i], _odd_weight(odd_w_in[o]), 512)
            cq, ck, cvt, iq, ik, mq = _prep_odd(proj, b, s, cos, sin, cos64, sin64, dsa_qk_gain[o], mem_q_gain)
            y_c = _attend(_score_bound(HEAD_DIM, dsa_qk_gain[o, 0], dsa_qk_gain[o, 1], HEAD_DIM ** -0.5),
                          functools.partial(_dsa, b=b, s=s), cq, ck, cvt, iq, ik, ltri, proj)
            y_m = _memattn(mq, mk_all[i], mv_all[i], proj, b, s, O_MZ)
            ys = [y_c, y_m]
        x2 = _outproj(x2, ys, w_out_b[i])
    return x2.reshape(b, s, d)
```

```python
import functools
import math

import jax
import jax.numpy as jnp
import numpy as np
from jax import lax
from jax.experimental import pallas as pl
from jax.experimental.pallas import tpu as pltpu

F32 = jnp.float32
BF16 = jnp.bfloat16

D_MODEL = 2048
DEPTH = 4
HEAD_DIM = 128
ROPE_THETA = 10000.0
EPS = 1e-6
MASKED = -1e30
FORCED = 1e9
NSA_HEADS = 8
NSA_KV_GROUPS = 2
NSA_CMP_LEN = 32
NSA_CMP_STRIDE = 16
NSA_CMP_HIDDEN = 256
NSA_SEL_LEN = 64
NSA_SEL_TOPN = 16
NSA_WINDOW = 512
DIFF_HEADS = 4
DIFF_QK_DIM = 64
DSA_HEADS = 12
DSA_KV_HEADS = 4
IDX_HEADS = 16
IDX_DIM = 64
DSA_TOPK_MAX = 256
MEM_HEADS = 4

LANES = 128
SUBLANES = 8
VMEM_LIMIT_BYTES = 56 * 1024 * 1024
LOG2E = 1.4426950408889634

PREP_ROWS = 256
TQ_SPARSE = 128
TQ_DIFF = 256
TK = 512
VT_BLOCK = 256
NEG_INIT = -1e30
NEG_MASK = -2e30
FAST_LOG2_BOUND = 40.0

EVEN_UNITS = 54
E_AQ, E_AKV, E_BQ, E_BK, E_BV, E_MQ, E_AZ, E_BZ, E_MZ, E_AG = 0, 8, 20, 24, 28, 32, 36, 44, 48, 52
EVEN_PREP_UNITS = E_AZ
ODD_UNITS = 52
O_CQ, O_CK, O_CV, O_IQ, O_MQ, O_IKW, O_CZ, O_MZ = 0, 12, 16, 20, 28, 32, 36, 48
ODD_PREP_UNITS = O_IKW + 1


def _cparams(sem):
    return pltpu.CompilerParams(dimension_semantics=sem, vmem_limit_bytes=VMEM_LIMIT_BYTES)


def _dot(a, b):
    return jnp.dot(a, b, preferred_element_type=F32)


def _dot_nt(a, b):
    return lax.dot_general(a, b, (((1,), (1,)), ((), ())), preferred_element_type=F32)


def _silu(x):
    return x * jax.nn.sigmoid(x)


def _rms128(x, gain):
    return x * lax.rsqrt(jnp.mean(x * x, axis=-1, keepdims=True) + EPS) * gain


def _rms64(x, gain):
    lo = lax.broadcasted_iota(jnp.int32, x.shape, 1) < 64
    xx = x * x
    s_lo = jnp.sum(jnp.where(lo, xx, 0.0), axis=-1, keepdims=True)
    s_hi = jnp.sum(jnp.where(lo, 0.0, xx), axis=-1, keepdims=True)
    ms = jnp.where(lo, s_lo, s_hi) * (1.0 / 64.0)
    return x * lax.rsqrt(ms + EPS) * gain


def _partner(x, half):
    n = x.shape[-1]
    lane = lax.broadcasted_iota(jnp.int32, x.shape, 1)
    if 2 * half == n:
        return pltpu.roll(x, half, 1)
    a = pltpu.roll(x, half, 1)
    b = pltpu.roll(x, n - half, 1)
    src_a = pltpu.roll(lane, half, 1)
    want = jnp.where((lane & (2 * half - 1)) < half, lane + half, lane - half)
    return jnp.where(src_a == want, a, b)


def _rope(x, cos, sin_signed, half):
    return x * cos + _partner(x, half) * sin_signed


def _fold8(x, op, short_chains=False):
    rows, w = x.shape
    if short_chains:
        x = op(x.reshape(SUBLANES, rows // SUBLANES, w), axis=0)
        rows = rows // SUBLANES
    return op(x.reshape(rows // SUBLANES, SUBLANES, w), axis=0)


def _col_max(x):
    return jnp.max(_fold8(x, jnp.max), axis=0, keepdims=True)


def _col_sum(x):
    return jnp.sum(_fold8(x, jnp.sum), axis=0, keepdims=True)


def _proj_kernel(x_ref, g_ref, w_ref, o_ref, hn_ref):
    @pl.when(pl.program_id(1) == 0)
    def _():
        x = x_ref[...]
        hn_ref[...] = _rms128(x, g_ref[...]).astype(BF16)

    o_ref[...] = _dot(hn_ref[...], w_ref[...])


def _proj(x2, gain, w, tn):
    n, d = x2.shape
    c = w.shape[1]
    tm = min(1024, n)
    return pl.pallas_call(
        _proj_kernel,
        grid=(n // tm, c // tn),
        in_specs=[
            pl.BlockSpec((tm, d), lambda i, j: (i, 0)),
            pl.BlockSpec((1, d), lambda i, j: (0, 0)),
            pl.BlockSpec((d, tn), lambda i, j: (0, j)),
        ],
        out_specs=pl.BlockSpec((tm, tn), lambda i, j: (i, j)),
        out_shape=jax.ShapeDtypeStruct((n, c), F32),
        scratch_shapes=[pltpu.VMEM((tm, d), BF16)],
        compiler_params=_cparams(("arbitrary", "arbitrary")),
        name="proj",
    )(x2, gain.reshape(1, d), w)


def _prep_even_kernel(p_ref, cos_ref, sin_ref, cos64_ref, sin64_ref, ng_ref, dg_ref, mg_ref,
                      qa_ref, ks_ref, kw_ref, vst_ref, vwt_ref, craw_ref, bq_ref, bk_ref, bvt_ref, mq_ref):
    cos, sin = cos_ref[...], sin_ref[...]
    cos64, sin64 = cos64_ref[...], sin64_ref[...]
    rep = NSA_HEADS // NSA_KV_GROUPS
    tq = TQ_SPARSE

    def unit(u):
        return p_ref[:, u * LANES:(u + 1) * LANES]

    for h in range(NSA_HEADS):
        q = _rope(_rms128(unit(E_AQ + h), ng_ref[0:1, :]), cos, sin, 64).astype(BF16)
        g, r = divmod(h, rep)
        for qb in range(PREP_ROWS // tq):
            qa_ref[0, g, qb, r * tq:(r + 1) * tq, :] = q[qb * tq:(qb + 1) * tq, :]
    for g in range(NSA_KV_GROUPS):
        craw_ref[0, 0, g] = unit(E_AKV + 0 + g)
        craw_ref[0, 1, g] = unit(E_AKV + 2 + g)
        ks_ref[0, g] = _rope(_rms128(unit(E_AKV + 4 + g), ng_ref[2:3, :]), cos, sin, 64).astype(BF16)
        vst_ref[0, g, 0] = unit(E_AKV + 6 + g).T.astype(BF16)
        kw_ref[0, g] = _rope(_rms128(unit(E_AKV + 8 + g), ng_ref[3:4, :]), cos, sin, 64).astype(BF16)
        vw = unit(E_AKV + 10 + g)
        for kb in range(PREP_ROWS // LANES):
            vwt_ref[0, g, kb] = vw[kb * LANES:(kb + 1) * LANES, :].T.astype(BF16)
    lo = lax.broadcasted_iota(jnp.int32, cos.shape, 1) < 64
    for h in range(DIFF_HEADS):
        q = _rope(_rms64(unit(E_BQ + h), dg_ref[0:1, :]), cos64, sin64, 32)
        bq_ref[0, h, 0, 0:PREP_ROWS, :] = jnp.where(lo, q, 0.0).astype(BF16)
        bq_ref[0, h, 0, PREP_ROWS:2 * PREP_ROWS, :] = jnp.where(lo, 0.0, q).astype(BF16)
        bk_ref[0, h] = _rope(_rms64(unit(E_BK + h), dg_ref[1:2, :]), cos64, sin64, 32).astype(BF16)
        bvt_ref[0, h, 0] = unit(E_BV + h).T.astype(BF16)
    for h in range(MEM_HEADS):
        mq_ref[0, h] = _rms128(unit(E_MQ + h), mg_ref[...]).astype(BF16)


def _prep_even(proj, b, s, cos, sin, cos64, sin64, nsa_gain, diff_gain2, mem_gain):
    ts = PREP_ROWS
    assert ts == TQ_DIFF == VT_BLOCK
    nb = s // ts
    c = EVEN_PREP_UNITS * LANES
    g = NSA_KV_GROUPS
    rep = NSA_HEADS // g
    row = lambda bb, i: (i, 0)
    const = lambda bb, i: (0, 0)
    seq = lambda n: pl.BlockSpec((1, n, ts, LANES), lambda bb, i: (bb, 0, i, 0))
    blk5 = lambda n, k, r, cdim: pl.BlockSpec((1, n, k, r, cdim), lambda bb, i: (bb, 0, i, 0, 0))
    return pl.pallas_call(
        _prep_even_kernel,
        grid=(b, nb),
        in_specs=[
            pl.BlockSpec((ts, c), lambda bb, i: (bb * nb + i, 0)),
            pl.BlockSpec((ts, LANES), row), pl.BlockSpec((ts, LANES), row),
            pl.BlockSpec((ts, LANES), row), pl.BlockSpec((ts, LANES), row),
            pl.BlockSpec((4, LANES), const), pl.BlockSpec((2, LANES), const), pl.BlockSpec((1, LANES), const),
        ],
        out_specs=[
            blk5(g, ts // TQ_SPARSE, rep * TQ_SPARSE, LANES),
            seq(g), seq(g),
            blk5(g, 1, LANES, VT_BLOCK),
            blk5(g, ts // LANES, LANES, LANES),
            pl.BlockSpec((1, 2, g, ts, LANES), lambda bb, i: (bb, 0, 0, i, 0)),
            blk5(DIFF_HEADS, 1, 2 * TQ_DIFF, LANES),
            seq(DIFF_HEADS),
            blk5(DIFF_HEADS, 1, LANES, VT_BLOCK),
            seq(MEM_HEADS),
        ],
        out_shape=[
            jax.ShapeDtypeStruct((b, g, s // TQ_SPARSE, rep * TQ_SPARSE, LANES), BF16),
            jax.ShapeDtypeStruct((b, g, s, LANES), BF16),
            jax.ShapeDtypeStruct((b, g, s, LANES), BF16),
            jax.ShapeDtypeStruct((b, g, s // VT_BLOCK, LANES, VT_BLOCK), BF16),
            jax.ShapeDtypeStruct((b, g, s // LANES, LANES, LANES), BF16),
            jax.ShapeDtypeStruct((b, 2, g, s, LANES), F32),
            jax.ShapeDtypeStruct((b, DIFF_HEADS, s // TQ_DIFF, 2 * TQ_DIFF, LANES), BF16),
            jax.ShapeDtypeStruct((b, DIFF_HEADS, s, LANES), BF16),
            jax.ShapeDtypeStruct((b, DIFF_HEADS, s // VT_BLOCK, LANES, VT_BLOCK), BF16),
            jax.ShapeDtypeStruct((b, MEM_HEADS, s, LANES), BF16),
        ],
        compiler_params=_cparams(("arbitrary", "arbitrary")),
        name="prep_even",
    )(proj, cos, sin, cos64, sin64, nsa_gain, diff_gain2, mem_gain)


def _compress_kernel(x_ref, pe_ref, w1_ref, w2_ref, g_ref, cos_ref, sin_ref, kc_ref, vct_ref, pad_ref, *, s, n_pad):
    for kind in range(2):
        pad_ref[0:s, :] = x_ref[0, kind, 0]
        pad_ref[s:s + NSA_CMP_LEN, :] = jnp.zeros((NSA_CMP_LEN, LANES), F32)
        acc = jnp.zeros((n_pad, NSA_CMP_HIDDEN), F32)
        for l in range(NSA_CMP_LEN):
            rows = pad_ref[pl.ds(l, n_pad, stride=NSA_CMP_STRIDE), :] + pe_ref[kind, l:l + 1, :]
            acc = acc + _dot(rows.astype(BF16), w1_ref[kind, l])
        out = _dot(_silu(acc).astype(BF16), w2_ref[kind])
        if kind == 0:
            kc_ref[0, 0] = _rope(_rms128(out, g_ref[...]), cos_ref[...], sin_ref[...], 64).astype(BF16)
        else:
            vct_ref[0, 0] = out.T.astype(BF16)


def _compress(craw, pe, w1, w2, gain, cos_c, sin_c):
    b, _, g, s, _ = craw.shape
    n_pad = s // NSA_CMP_STRIDE
    kern = functools.partial(_compress_kernel, s=s, n_pad=n_pad)
    whole = lambda shape: pl.BlockSpec(shape, lambda bb, gg: (0,) * len(shape))
    return pl.pallas_call(
        kern,
        grid=(b, g),
        in_specs=[
            pl.BlockSpec((1, 2, 1, s, LANES), lambda bb, gg: (bb, 0, gg, 0, 0)),
            whole((2, NSA_CMP_LEN, LANES)),
            whole((2, NSA_CMP_LEN, LANES, NSA_CMP_HIDDEN)),
            whole((2, NSA_CMP_HIDDEN, LANES)),
            whole((1, LANES)), whole((n_pad, LANES)), whole((n_pad, LANES)),
        ],
        out_specs=[
            pl.BlockSpec((1, 1, n_pad, LANES), lambda bb, gg: (bb, gg, 0, 0)),
            pl.BlockSpec((1, 1, LANES, n_pad), lambda bb, gg: (bb, gg, 0, 0)),
        ],
        out_shape=[
            jax.ShapeDtypeStruct((b, g, n_pad, LANES), BF16),
            jax.ShapeDtypeStruct((b, g, LANES, n_pad), BF16),
        ],
        scratch_shapes=[pltpu.VMEM((s + NSA_CMP_LEN, LANES), F32)],
        compiler_params=_cparams(("arbitrary", "arbitrary")),
        name="nsa_compress",
    )(craw, pe, w1, w2, gain, cos_c, sin_c)


def _softmax_parts_t(st, bounded):
    if not bounded:
        st = st - _col_max(st)
    e = jnp.exp2(st)
    return e.astype(BF16), 1.0 / jnp.maximum(_col_sum(e), 1e-30)


def _online_init(m_ref, l_ref, acc_ref):
    m_ref[...] = jnp.full(m_ref.shape, NEG_INIT, F32)
    l_ref[...] = jnp.zeros(l_ref.shape, F32)
    acc_ref[...] = jnp.zeros(acc_ref.shape, F32)


def _online_step(idx, st, pv, m_ref, l_ref, acc_ref, bounded):
    if bounded:
        p = jnp.exp2(st)
        l_ref[idx] = l_ref[idx] + _fold8(p, jnp.sum)
        acc_ref[idx] = acc_ref[idx] + pv(p.astype(BF16))
        return
    m_old = m_ref[idx]
    m_new = jnp.maximum(m_old, _col_max(st))
    alpha = jnp.exp2(m_old - m_new)
    p = jnp.exp2(st - m_new)
    l_ref[idx] = alpha * l_ref[idx] + _fold8(p, jnp.sum)
    acc_ref[idx] = alpha * acc_ref[idx] + pv(p.astype(BF16))
    m_ref[idx] = m_new


def _online_result(idx, l_ref, acc_ref):
    return acc_ref[idx] / jnp.maximum(jnp.sum(l_ref[idx], axis=0, keepdims=True), 1e-30)


def _flash_pairs(n_chunks, n_groups, qk, mask, soft, sa_ref, sb_ref):
    for g in range(n_groups):
        qk(0, g, sa_ref)
    n_pairs = (n_chunks - 1) // 2

    def pair(j, carry):
        a = 2 * j
        bias_a = mask(a, False)
        bias_b = mask(a + 1, False)
        for g in range(n_groups):
            qk(a + 1, g, sb_ref)
            soft(a, g, sa_ref, bias_a)
            qk(a + 2, g, sa_ref)
            soft(a + 1, g, sb_ref, bias_b)
        return carry

    lax.fori_loop(0, n_pairs, pair, 0)
    e = 2 * n_pairs

    @pl.when(e + 1 < n_chunks)
    def _():
        bias_a = mask(e, True)
        bias_b = mask(e + 1, True)
        for g in range(n_groups):
            qk(e + 1, g, sb_ref)
            soft(e, g, sa_ref, bias_a)
            soft(e + 1, g, sb_ref, bias_b)

    @pl.when(e + 1 >= n_chunks)
    def _():
        bias_a = mask(e, True)
        for g in range(n_groups):
            soft(e, g, sa_ref, bias_a)


def _pv_blocks(vt_ref, lead, chunk, tk):
    per = tk // VT_BLOCK

    def pv(p):
        out = _dot(vt_ref[lead + (chunk * per,)], p[0:VT_BLOCK])
        for i in range(1, per):
            out = out + _dot(vt_ref[lead + (chunk * per + i,)], p[i * VT_BLOCK:(i + 1) * VT_BLOCK])
        return out

    return pv


def _top_n_mask_t(scores, n):
    row = lax.broadcasted_iota(jnp.int32, scores.shape, 0).astype(F32)
    height = float(scores.shape[0])
    work = scores
    sel = jnp.zeros(scores.shape, F32)
    for _ in range(n):
        m = jnp.max(work, axis=0, keepdims=True)
        first = jnp.min(jnp.where(work == m, row, height), axis=0, keepdims=True)
        pick = row == first
        sel = jnp.where(pick, 1.0, sel)
        work = jnp.where(pick, -jnp.inf, work)
    return sel


def _nsa_kernel(q_ref, kc_ref, vct_ref, ks_ref, vst_ref, kw_ref, vwt_ref, ovlt_ref, wbias_ref, gl_ref, z_ref, o_ref,
                m_ref, l_ref, acc_ref, sa_ref, sb_ref, selt_ref, oct_ref, owt_ref, *, tq, tk, top_n, bounded):
    rep = NSA_HEADS // NSA_KV_GROUPS
    w = rep * tq
    scale = HEAD_DIM ** -0.5
    q0 = pl.program_id(2) * tq
    t_row = q0 + (lax.broadcasted_iota(jnp.int32, (1, w), 1) & (tq - 1))
    t_one = t_row[:, 0:tq]
    q = q_ref[0, 0, 0]

    n_cmp_pad = kc_ref.shape[2]
    cmp_last = lax.broadcasted_iota(jnp.int32, (n_cmp_pad, w), 0) * NSA_CMP_STRIDE + (NSA_CMP_LEN - 1)
    st = jnp.where(cmp_last <= t_row, _dot_nt(kc_ref[0, 0], q) * (scale * LOG2E), NEG_MASK)
    p, inv = _softmax_parts_t(st, bounded)
    inv = jnp.where(t_row >= NSA_CMP_LEN - 1, inv, 0.0)
    oct_ref[...] = _dot(vct_ref[0, 0], p) * inv
    imp_heads = _dot(ovlt_ref[...], p) * inv
    imp = imp_heads[:, 0:tq]
    for r in range(1, rep):
        imp = imp + imp_heads[:, r * tq:(r + 1) * tq]

    n_sel = imp.shape[0]
    sel_shift = NSA_SEL_LEN.bit_length() - 1
    j = lax.broadcasted_iota(jnp.int32, (n_sel, tq), 0)
    cur = jnp.right_shift(t_one, sel_shift)
    visible = j <= cur
    forced = (j == 0) | (j >= cur - 1)
    imp = jnp.where(visible, jnp.where(forced, FORCED, imp), MASKED)
    selt_ref[...] = _top_n_mask_t(imp, top_n)

    span = NSA_WINDOW + tq
    start = pl.multiple_of(jnp.maximum(q0 - NSA_WINDOW, 0), tq)
    st = (_dot_nt(kw_ref[0, 0, pl.ds(start, span), :], q) * (scale * LOG2E)
          + jnp.concatenate([wbias_ref[0]] * rep, axis=1))
    p, inv = _softmax_parts_t(st, bounded)
    blk0 = start // LANES
    ow = _dot(vwt_ref[0, 0, blk0], p[0:LANES])
    for i in range(1, span // LANES):
        ow = ow + _dot(vwt_ref[0, 0, blk0 + i], p[i * LANES:(i + 1) * LANES])
    owt_ref[...] = ow * inv

    _online_init(m_ref, l_ref, acc_ref)
    n_chunks = (q0 + tq + tk - 1) // tk
    blocks_per_chunk = tk // NSA_SEL_LEN

    def qk(c, g, dst):
        dst[g] = _dot_nt(ks_ref[0, 0, pl.ds(pl.multiple_of(c * tk, tk), tk), :], q) * (scale * LOG2E)

    def mask(c, tail):
        rows = selt_ref[pl.ds(pl.multiple_of(c * blocks_per_chunk, blocks_per_chunk), blocks_per_chunk), :]
        ok = jnp.concatenate(
            [jnp.broadcast_to(rows[i:i + 1, :], (NSA_SEL_LEN, tq)) for i in range(blocks_per_chunk)], axis=0) > 0.5
        if tail:
            ok = ok & ((c * tk + lax.broadcasted_iota(jnp.int32, (tk, tq), 0)) <= t_one)
        return jnp.concatenate([jnp.where(ok, 0.0, NEG_MASK)] * rep, axis=1)

    def soft(c, g, src, bias):
        _online_step(g, src[g] + bias, _pv_blocks(vst_ref, (0, 0), c, tk), m_ref, l_ref, acc_ref, bounded)

    _flash_pairs(n_chunks, 1, qk, mask, soft, sa_ref, sb_ref)

    gates_t = jax.nn.sigmoid(gl_ref[...]).T
    os_t = _online_result(0, l_ref, acc_ref)
    for r in range(rep):
        sl = slice(r * tq, (r + 1) * tq)
        out_t = (gates_t[3 * r:3 * r + 1, :] * oct_ref[:, sl] + gates_t[3 * r + 1:3 * r + 2, :] * os_t[:, sl]
                 + gates_t[3 * r + 2:3 * r + 3, :] * owt_ref[:, sl])
        o_ref[:, r * LANES:(r + 1) * LANES] = (out_t.T * _silu(z_ref[:, r * LANES:(r + 1) * LANES])).astype(BF16)


def _window_bias(tq):
    span = NSA_WINDOW + tq
    n_early = NSA_WINDOW // tq
    row = np.arange(span)[:, None]
    lane = np.arange(tq)[None, :]
    tables = []
    for i in range(n_early + 1):
        t = i * tq + lane
        s_pos = (0 if i < n_early else t[0, 0] - NSA_WINDOW) + row
        ok = (s_pos <= t) & (s_pos > t - NSA_WINDOW)
        tables.append(np.where(ok, 0.0, NEG_MASK))
    return jnp.asarray(np.stack(tables), dtype=F32)


def _nsa(qa, ks, vst, kw, vwt, kc, vct, ovlt, wbias, proj, b, s, bounded):
    tq, tk = TQ_SPARSE, TK
    nq = s // tq
    g = NSA_KV_GROUPS
    rep = NSA_HEADS // g
    w = rep * tq
    n_cmp_pad = kc.shape[2]
    n_sel = s // NSA_SEL_LEN
    kern = functools.partial(_nsa_kernel, tq=tq, tk=tk, top_n=min(NSA_SEL_TOPN, n_sel), bounded=bounded)
    per_group = lambda shape: pl.BlockSpec((1, 1) + shape, lambda bb, gg, i: (bb, gg) + (0,) * len(shape))
    return pl.pallas_call(
        kern,
        grid=(b, g, nq),
        in_specs=[
            pl.BlockSpec((1, 1, 1, w, LANES), lambda bb, gg, i: (bb, gg, i, 0, 0)),
            per_group((n_cmp_pad, LANES)), per_group((LANES, n_cmp_pad)),
            per_group((s, LANES)), per_group((s // VT_BLOCK, LANES, VT_BLOCK)),
            per_group((s, LANES)), per_group((s // LANES, LANES, LANES)),
            pl.BlockSpec((n_sel, n_cmp_pad), lambda bb, gg, i: (0, 0)),
            pl.BlockSpec((1,) + wbias.shape[1:], lambda bb, gg, i: (jnp.minimum(i, wbias.shape[0] - 1), 0, 0)),
            pl.BlockSpec((tq, LANES), lambda bb, gg, i: (bb * nq + i, E_AG + gg)),
            pl.BlockSpec((tq, rep * LANES), lambda bb, gg, i: (bb * nq + i, E_AZ // rep + gg)),
        ],
        out_specs=pl.BlockSpec((tq, rep * LANES), lambda bb, gg, i: (bb * nq + i, gg)),
        out_shape=jax.ShapeDtypeStruct((b * s, NSA_HEADS * LANES), BF16),
        scratch_shapes=[
            pltpu.VMEM((1, 1, w), F32), pltpu.VMEM((1, SUBLANES, w), F32), pltpu.VMEM((1, LANES, w), F32),
            pltpu.VMEM((1, tk, w), F32), pltpu.VMEM((1, tk, w), F32),
            pltpu.VMEM((n_sel, tq), F32), pltpu.VMEM((LANES, w), F32), pltpu.VMEM((LANES, w), F32),
        ],
        compiler_params=_cparams(("arbitrary", "arbitrary", "arbitrary")),
        name="nsa_attention",
    )(qa, kc, vct, ks, vst, kw, vwt, ovlt, wbias, proj, proj)


def _diff_kernel(q_ref, k_ref, vt_ref, lam_ref, sg_ref, z_ref, o_ref, m_ref, l_ref, acc_ref, sa_ref, sb_ref,
                 *, tq, tk, lambda_init, bounded):
    w = 2 * tq
    q0 = pl.program_id(2) * tq
    t_row = q0 + (lax.broadcasted_iota(jnp.int32, (1, w), 1) & (tq - 1))
    q = q_ref[0, 0, 0]
    _online_init(m_ref, l_ref, acc_ref)
    n_chunks = (q0 + tq + tk - 1) // tk

    def qk(c, g, dst):
        dst[g] = (_dot_nt(k_ref[0, 0, pl.ds(pl.multiple_of(c * tk, tk), tk), :], q)
                  * (DIFF_QK_DIM ** -0.5 * LOG2E))

    def mask(c, tail):
        if not tail:
            return None
        key = c * tk + lax.broadcasted_iota(jnp.int32, (tk, w), 0)
        return jnp.where(key <= t_row, 0.0, NEG_MASK)

    def soft(c, g, src, bias):
        st = src[g] if bias is None else src[g] + bias
        _online_step(g, st, _pv_blocks(vt_ref, (0, 0), c, tk), m_ref, l_ref, acc_ref, bounded)

    _flash_pairs(n_chunks, 1, qk, mask, soft, sa_ref, sb_ref)

    lv = lam_ref[...]
    lam = (jnp.exp(jnp.sum(lv[0:1] * lv[1:2], axis=-1, keepdims=True))
           - jnp.exp(jnp.sum(lv[2:3] * lv[3:4], axis=-1, keepdims=True)) + lambda_init)
    o_t = _online_result(0, l_ref, acc_ref)
    o = (o_t[:, 0:tq] - lam * o_t[:, tq:w]).T
    o = _rms128(o, sg_ref[...]) * (1.0 - lambda_init)
    o_ref[...] = (o * _silu(z_ref[...])).astype(BF16)


def _diff(bq, bk, bvt, lam_vecs, subln_gain, proj, b, s, lambda_init, bounded):
    tq, tk = TQ_DIFF, TK
    nq = s // tq
    w = 2 * tq
    kern = functools.partial(_diff_kernel, tq=tq, tk=tk, lambda_init=lambda_init, bounded=bounded)
    return pl.pallas_call(
        kern,
        grid=(b, DIFF_HEADS, nq),
        in_specs=[
            pl.BlockSpec((1, 1, 1, w, LANES), lambda bb, h, i: (bb, h, i, 0, 0)),
            pl.BlockSpec((1, 1, s, LANES), lambda bb, h, i: (bb, h, 0, 0)),
            pl.BlockSpec((1, 1, s // VT_BLOCK, LANES, VT_BLOCK), lambda bb, h, i: (bb, h, 0, 0, 0)),
            pl.BlockSpec((4, DIFF_QK_DIM), lambda bb, h, i: (0, 0)),
            pl.BlockSpec((1, LANES), lambda bb, h, i: (0, 0)),
            pl.BlockSpec((tq, LANES), lambda bb, h, i: (bb * nq + i, E_BZ + h)),
        ],
        out_specs=pl.BlockSpec((tq, LANES), lambda bb, h, i: (bb * nq + i, h)),
        out_shape=jax.ShapeDtypeStruct((b * s, DIFF_HEADS * LANES), BF16),
        scratch_shapes=[
            pltpu.VMEM((1, 1, w), F32), pltpu.VMEM((1, SUBLANES, w), F32), pltpu.VMEM((1, LANES, w), F32),
            pltpu.VMEM((1, tk, w), F32), pltpu.VMEM((1, tk, w), F32),
        ],
        compiler_params=_cparams(("arbitrary", "arbitrary", "arbitrary")),
        name="diff_attention",
    )(bq, bk, bvt, lam_vecs, subln_gain, proj)


def _memkv_kernel(mem_ref, mg_ref, w_ref, kg_ref, k_ref, v_ref):
    mem_n = _rms128(mem_ref[0], mg_ref[...]).astype(BF16)
    kv = _dot(mem_n, w_ref[0])
    for h in range(MEM_HEADS):
        k_ref[0, 0, h] = _rms128(kv[:, h * LANES:(h + 1) * LANES], kg_ref[0]).astype(BF16)
        v_ref[0, 0, h] = kv[:, (MEM_HEADS + h) * LANES:(MEM_HEADS + h + 1) * LANES].astype(BF16)


def _memkv(mem, mem_gain, w_kv, k_gain):
    b, m, d = mem.shape
    depth = w_kv.shape[0]
    c = w_kv.shape[2]
    out = jax.ShapeDtypeStruct((depth, b, MEM_HEADS, m, LANES), BF16)
    ospec = pl.BlockSpec((1, 1, MEM_HEADS, m, LANES), lambda i, bb: (i, bb, 0, 0, 0))
    return pl.pallas_call(
        _memkv_kernel,
        grid=(depth, b),
        in_specs=[
            pl.BlockSpec((1, m, d), lambda i, bb: (bb, 0, 0)),
            pl.BlockSpec((1, d), lambda i, bb: (0, 0)),
            pl.BlockSpec((1, d, c), lambda i, bb: (i, 0, 0)),
            pl.BlockSpec((1, 1, LANES), lambda i, bb: (i, 0, 0)),
        ],
        out_specs=[ospec, ospec],
        out_shape=[out, out],
        compiler_params=_cparams(("arbitrary", "arbitrary")),
        name="mem_kv",
    )(mem, mem_gain.reshape(1, d), w_kv, k_gain)


def _memattn_kernel(q_ref, k_ref, v_ref, z_ref, o_ref):
    scale = HEAD_DIM ** -0.5
    for h in range(MEM_HEADS):
        s = _dot_nt(q_ref[0, h], k_ref[0, h]) * scale
        e = jnp.exp(s - jnp.max(s, axis=-1, keepdims=True))
        p = e / jnp.sum(e, axis=-1, keepdims=True)
        o = _dot(p.astype(BF16), v_ref[0, h])
        o_ref[:, h * LANES:(h + 1) * LANES] = (o * _silu(z_ref[:, h * LANES:(h + 1) * LANES])).astype(BF16)


def _memattn(mq, mk, mv, proj, b, s, z_unit):
    tq = min(512, s)
    nq = s // tq
    m = mk.shape[2]
    kvspec = pl.BlockSpec((1, MEM_HEADS, m, LANES), lambda bb, i: (bb, 0, 0, 0))
    return pl.pallas_call(
        _memattn_kernel,
        grid=(b, nq),
        in_specs=[
            pl.BlockSpec((1, MEM_HEADS, tq, LANES), lambda bb, i: (bb, 0, i, 0)),
            kvspec, kvspec,
            pl.BlockSpec((tq, MEM_HEADS * LANES), lambda bb, i: (bb * nq + i, z_unit // MEM_HEADS)),
        ],
        out_specs=pl.BlockSpec((tq, MEM_HEADS * LANES), lambda bb, i: (bb * nq + i, 0)),
        out_shape=jax.ShapeDtypeStruct((b * s, MEM_HEADS * LANES), BF16),
        compiler_params=_cparams(("arbitrary", "arbitrary")),
        name="mem_attention",
    )(mq, mk, mv, proj)


def _prep_odd_kernel(p_ref, cos_ref, sin_ref, cos64_ref, sin64_ref, cg_ref, mg_ref,
                     cq_ref, ck_ref, cvt_ref, iq_ref, ik_ref, mq_ref):
    cos, sin = cos_ref[...], sin_ref[...]
    cos64, sin64 = cos64_ref[...], sin64_ref[...]
    rep = DSA_HEADS // DSA_KV_HEADS
    tq = TQ_SPARSE
    n_qb = PREP_ROWS // tq

    def unit(u):
        return p_ref[:, u * LANES:(u + 1) * LANES]

    for h in range(DSA_HEADS):
        q = _rope(_rms128(unit(O_CQ + h), cg_ref[0:1, :]), cos, sin, 64).astype(BF16)
        g, r = divmod(h, rep)
        for qb in range(n_qb):
            cq_ref[0, g, qb, r * tq:(r + 1) * tq, :] = q[qb * tq:(qb + 1) * tq, :]
    for h in range(DSA_KV_HEADS):
        ck_ref[0, h] = _rope(_rms128(unit(O_CK + h), cg_ref[1:2, :]), cos, sin, 64).astype(BF16)
        cvt_ref[0, h, 0] = unit(O_CV + h).T.astype(BF16)
    lo = lax.broadcasted_iota(jnp.int32, cos.shape, 1) < 64
    for u in range(IDX_HEADS // 2):
        x = _rope(unit(O_IQ + u), cos64, sin64, 32)
        even = jnp.where(lo, x, 0.0).astype(BF16)
        odd = jnp.where(lo, pltpu.roll(x, 64, 1), 0.0).astype(BF16)
        for qb in range(n_qb):
            iq_ref[0, qb, (2 * u) * tq:(2 * u + 1) * tq, :] = even[qb * tq:(qb + 1) * tq, :]
            iq_ref[0, qb, (2 * u + 1) * tq:(2 * u + 2) * tq, :] = odd[qb * tq:(qb + 1) * tq, :]
    ik = _rope(unit(O_IKW), cos64, sin64, 32)
    ik_ref[0] = jnp.where(lo, ik, 0.0).astype(BF16)
    for h in range(MEM_HEADS):
        mq_ref[0, h] = _rms128(unit(O_MQ + h), mg_ref[...]).astype(BF16)


def _prep_odd(proj, b, s, cos, sin, cos64, sin64, dsa_gain, mem_gain):
    ts = PREP_ROWS
    nb = s // ts
    c = ODD_PREP_UNITS * LANES
    rep = DSA_HEADS // DSA_KV_HEADS
    row = lambda bb, i: (i, 0)
    const = lambda bb, i: (0, 0)
    seq = lambda n: pl.BlockSpec((1, n, ts, LANES), lambda bb, i: (bb, 0, i, 0))
    return pl.pallas_call(
        _prep_odd_kernel,
        grid=(b, nb),
        in_specs=[
            pl.BlockSpec((ts, c), lambda bb, i: (bb * nb + i, 0)),
            pl.BlockSpec((ts, LANES), row), pl.BlockSpec((ts, LANES), row),
            pl.BlockSpec((ts, LANES), row), pl.BlockSpec((ts, LANES), row),
            pl.BlockSpec((2, LANES), const), pl.BlockSpec((1, LANES), const),
        ],
        out_specs=[
            pl.BlockSpec((1, DSA_KV_HEADS, ts // TQ_SPARSE, rep * TQ_SPARSE, LANES), lambda bb, i: (bb, 0, i, 0, 0)),
            seq(DSA_KV_HEADS),
            pl.BlockSpec((1, DSA_KV_HEADS, 1, LANES, VT_BLOCK), lambda bb, i: (bb, 0, i, 0, 0)),
            pl.BlockSpec((1, ts // TQ_SPARSE, IDX_HEADS * TQ_SPARSE, LANES), lambda bb, i: (bb, i, 0, 0)),
            pl.BlockSpec((1, ts, LANES), lambda bb, i: (bb, i, 0)),
            seq(MEM_HEADS),
        ],
        out_shape=[
            jax.ShapeDtypeStruct((b, DSA_KV_HEADS, s // TQ_SPARSE, rep * TQ_SPARSE, LANES), BF16),
            jax.ShapeDtypeStruct((b, DSA_KV_HEADS, s, LANES), BF16),
            jax.ShapeDtypeStruct((b, DSA_KV_HEADS, s // VT_BLOCK, LANES, VT_BLOCK), BF16),
            jax.ShapeDtypeStruct((b, s // TQ_SPARSE, IDX_HEADS * TQ_SPARSE, LANES), BF16),
            jax.ShapeDtypeStruct((b, s, LANES), BF16),
            jax.ShapeDtypeStruct((b, MEM_HEADS, s, LANES), BF16),
        ],
        compiler_params=_cparams(("arbitrary", "arbitrary")),
        name="prep_odd",
    )(proj, cos, sin, cos64, sin64, dsa_gain, mem_gain)


def _sortable_key(x):
    bits = pltpu.bitcast(x + 0.0, jnp.int32)
    return jnp.where(bits < 0, bits ^ jnp.int32(0x7FFFFFFF), bits)


def _dsa_kernel(q_ref, k_ref, vt_ref, iq_ref, ik_ref, w_ref, ltri_ref, z_ref, o_ref,
                key_ref, m_ref, l_ref, acc_ref, sa_ref, sb_ref, ties_ref, *, tq, tk, top_k, bounded):
    rep = DSA_HEADS // DSA_KV_HEADS
    scale = HEAD_DIM ** -0.5
    q0 = pl.program_id(1) * tq
    t_row = q0 + lax.broadcasted_iota(jnp.int32, (1, tq), 1)
    n_chunks = (q0 + tq + tk - 1) // tk
    int_min = jnp.int32(-2147483648)
    heads_per_dot = 4

    w_t = (w_ref[...] * (IDX_HEADS ** -0.5 * IDX_DIM ** -0.5)).T

    def score_chunk(c, carry):
        k0 = pl.multiple_of(c * tk, tk)
        ik = ik_ref[0, pl.ds(k0, tk), :]
        sc = jnp.zeros((tk, tq), F32)
        for h0 in range(0, IDX_HEADS, heads_per_dot):
            x = _dot_nt(ik, iq_ref[0, 0, h0 * tq:(h0 + heads_per_dot) * tq, :])
            for hh in range(heads_per_dot):
                h = h0 + hh
                sc = sc + jnp.maximum(x[:, hh * tq:(hh + 1) * tq], 0.0) * w_t[IDX_DIM + h:IDX_DIM + h + 1, :]
        causal = (k0 + lax.broadcasted_iota(jnp.int32, (tk, tq), 0)) <= t_row
        key_ref[c] = _sortable_key(jnp.where(causal, sc, MASKED))
        return carry

    lax.fori_loop(0, n_chunks, score_chunk, 0)

    def count(pred):
        def body(c, acc):
            return acc + _fold8(pred(key_ref[c]).astype(F32), jnp.sum, short_chains=True)
        acc = lax.fori_loop(0, n_chunks, body, jnp.zeros((SUBLANES, tq), F32))
        return jnp.sum(acc, axis=0, keepdims=True)

    def search(it, state):
        thr_u, above = state
        cand_u = thr_u | jnp.left_shift(jnp.int32(1), 31 - it)
        cand = cand_u ^ int_min
        cnt = count(lambda kk: kk >= cand)
        take = cnt >= top_k
        return jnp.where(take, cand_u, thr_u), jnp.where(take, above, cnt)

    thr_u, above = lax.fori_loop(0, 32, search, (jnp.zeros((1, tq), jnp.int32), jnp.zeros((1, tq), F32)))
    thr = thr_u ^ int_min
    budget = top_k - above

    _online_init(m_ref, l_ref, acc_ref)
    ties_ref[...] = jnp.zeros(ties_ref.shape, F32)

    def qk(c, g, dst):
        dst[g] = _dot_nt(k_ref[0, g, pl.ds(pl.multiple_of(c * tk, tk), tk), :], q_ref[0, g, 0]) * (scale * LOG2E)

    def mask(c, tail):
        keys = key_ref[c]
        tie = keys == thr
        rank = _dot(ltri_ref[...], tie.astype(BF16)) + ties_ref[...]
        ok = (keys > thr) | (tie & (rank < budget))
        if tail:
            ok = ok & ((c * tk + lax.broadcasted_iota(jnp.int32, (tk, tq), 0)) <= t_row)
        ties_ref[...] = ties_ref[...] + _col_sum(tie.astype(F32))
        return jnp.concatenate([jnp.where(ok, 0.0, NEG_MASK)] * rep, axis=1)

    def soft(c, g, src, bias):
        _online_step(g, src[g] + bias, _pv_blocks(vt_ref, (0, g), c, tk), m_ref, l_ref, acc_ref, bounded)

    _flash_pairs(n_chunks, DSA_KV_HEADS, qk, mask, soft, sa_ref, sb_ref)

    for g in range(DSA_KV_HEADS):
        o_t = _online_result(g, l_ref, acc_ref)
        for r in range(rep):
            h = g * rep + r
            o = o_t[:, r * tq:(r + 1) * tq].T
            o_ref[:, h * LANES:(h + 1) * LANES] = (o * _silu(z_ref[:, h * LANES:(h + 1) * LANES])).astype(BF16)


def _dsa(cq, ck, cvt, iq, ik, ltri, proj, b, s, bounded):
    tq, tk = TQ_SPARSE, TK
    nq = s // tq
    g = DSA_KV_HEADS
    rep = DSA_HEADS // g
    w = rep * tq
    kern = functools.partial(_dsa_kernel, tq=tq, tk=tk, top_k=min(DSA_TOPK_MAX, s // 4), bounded=bounded)
    return pl.pallas_call(
        kern,
        grid=(b, nq),
        in_specs=[
            pl.BlockSpec((1, g, 1, w, LANES), lambda bb, i: (bb, 0, i, 0, 0)),
            pl.BlockSpec((1, g, s, LANES), lambda bb, i: (bb, 0, 0, 0), pipeline_mode=pl.Buffered(1)),
            pl.BlockSpec((1, g, s // VT_BLOCK, LANES, VT_BLOCK), lambda bb, i: (bb, 0, 0, 0, 0),
                         pipeline_mode=pl.Buffered(1)),
            pl.BlockSpec((1, 1, IDX_HEADS * tq, LANES), lambda bb, i: (bb, i, 0, 0)),
            pl.BlockSpec((1, s, LANES), lambda bb, i: (bb, 0, 0), pipeline_mode=pl.Buffered(1)),
            pl.BlockSpec((tq, LANES), lambda bb, i: (bb * nq + i, O_IKW)),
            pl.BlockSpec((tk, tk), lambda bb, i: (0, 0)),
            pl.BlockSpec((tq, DSA_HEADS * LANES), lambda bb, i: (bb * nq + i, O_CZ // DSA_HEADS)),
        ],
        out_specs=pl.BlockSpec((tq, DSA_HEADS * LANES), lambda bb, i: (bb * nq + i, 0)),
        out_shape=jax.ShapeDtypeStruct((b * s, DSA_HEADS * LANES), BF16),
        scratch_shapes=[
            pltpu.VMEM((s // tk, tk, tq), jnp.int32),
            pltpu.VMEM((g, 1, w), F32), pltpu.VMEM((g, SUBLANES, w), F32), pltpu.VMEM((g, LANES, w), F32),
            pltpu.VMEM((g, tk, w), F32), pltpu.VMEM((g, tk, w), F32),
            pltpu.VMEM((1, tq), F32),
        ],
        compiler_params=_cparams(("arbitrary", "arbitrary")),
        name="dsa_attention",
    )(cq, ck, cvt, iq, ik, proj, ltri, proj)


def _outproj_kernel(*refs, n_parts):
    x_ref = refs[0]
    y_refs = refs[1:1 + n_parts]
    w_refs = refs[1 + n_parts:1 + 2 * n_parts]
    o_ref = refs[1 + 2 * n_parts]
    acc = x_ref[...]
    for y_ref, w_ref in zip(y_refs, w_refs):
        acc = acc + _dot(y_ref[...], w_ref[...])
    o_ref[...] = acc


def _outproj(x2, ys, w_out):
    n, d = x2.shape
    tm = min(1024, n)
    tn = 1024
    widths = [y.shape[1] for y in ys]
    starts = np.cumsum([0] + widths[:-1]).tolist()
    ws = [w_out[st:st + wd] for st, wd in zip(starts, widths)]
    kern = functools.partial(_outproj_kernel, n_parts=len(ys))
    return pl.pallas_call(
        kern,
        grid=(d // tn, n // tm),
        in_specs=([pl.BlockSpec((tm, tn), lambda j, i: (i, j))]
                  + [pl.BlockSpec((tm, wd), lambda j, i: (i, 0)) for wd in widths]
                  + [pl.BlockSpec((wd, tn), lambda j, i: (0, j)) for wd in widths]),
        out_specs=pl.BlockSpec((tm, tn), lambda j, i: (i, j)),
        out_shape=jax.ShapeDtypeStruct((n, d), F32),
        compiler_params=_cparams(("arbitrary", "arbitrary")),
        name="out_proj",
    )(x2, *ys, *ws)


def _rope_tables(pos, half, reps):
    inv = ROPE_THETA ** (-jnp.arange(half, dtype=F32) / half)
    ang = pos.astype(F32)[:, None] * inv[None, :]
    cos, sin = jnp.cos(ang), jnp.sin(ang)
    return jnp.tile(jnp.concatenate([cos, cos], -1), (1, reps)), jnp.tile(jnp.concatenate([-sin, sin], -1), (1, reps))


def _split_cols(w, sizes):
    return jnp.split(w, np.cumsum(sizes)[:-1].tolist(), axis=-1)


def _even_weight(w):
    sizes = (1024, 1536, 24, 1024, 512, 512, 512, 512, 512, 512)
    a_q, a_kv, a_g, a_z, b_q, b_k, b_v, b_z, m_q, m_z = _split_cols(w.astype(BF16), sizes)
    d = w.shape[0]
    per_group = 3 * NSA_HEADS // NSA_KV_GROUPS
    gates = [jnp.pad(a_g[:, g * per_group:(g + 1) * per_group], ((0, 0), (0, LANES - per_group)))
             for g in range(NSA_KV_GROUPS)]
    out = jnp.concatenate([a_q, a_kv, b_q, b_k, b_v, m_q, a_z, b_z, m_z] + gates, axis=-1)
    assert out.shape == (d, EVEN_UNITS * LANES)
    return out


def _odd_weight(w):
    sizes = (1536, 512, 512, 1024, 64, 16, 1536, 512, 512)
    c_q, c_k, c_v, i_q, i_k, i_w, c_z, m_q, m_z = _split_cols(w.astype(BF16), sizes)
    d = w.shape[0]
    ikw = jnp.pad(jnp.concatenate([i_k, i_w], -1), ((0, 0), (0, LANES - IDX_DIM - IDX_HEADS)))
    pad = jnp.zeros((d, (O_CZ - ODD_PREP_UNITS) * LANES), BF16)
    out = jnp.concatenate([c_q, c_k, c_v, i_q, m_q, ikw, pad, c_z, m_z], axis=-1)
    assert out.shape == (d, ODD_UNITS * LANES)
    return out


def _overlap_matrix_t(s, n_cmp_pad):
    n_cmp = (s - NSA_CMP_LEN) // NSA_CMP_STRIDE + 1
    n_sel = s // NSA_SEL_LEN
    cmp_start = np.arange(n_cmp) * NSA_CMP_STRIDE
    sel_start = np.arange(n_sel) * NSA_SEL_LEN
    ov = np.clip(np.minimum(cmp_start[:, None] + NSA_CMP_LEN, sel_start[None, :] + NSA_SEL_LEN)
                 - np.maximum(cmp_start[:, None], sel_start[None, :]), 0, None) / NSA_CMP_LEN
    full = np.zeros((n_sel, n_cmp_pad), np.float32)
    full[:, :n_cmp] = ov.T
    return jnp.asarray(full, dtype=BF16)


def _score_bound(dim, gain_q, gain_k, scale):
    return dim * jnp.max(jnp.abs(gain_q)) * jnp.max(jnp.abs(gain_k)) * (scale * LOG2E * 1.02)


def _attend(bound, fn, *operands):
    return lax.cond(bound <= FAST_LOG2_BOUND,
                    lambda *a: fn(*a, bounded=True), lambda *a: fn(*a, bounded=False), *operands)


def kernel(x, mem, norm_gain, mem_norm_gain, mem_w_kv, mem_qk_gain, w_out, even_w_in, nsa_qk_gain, nsa_cmp_pos,
           nsa_cmp_w1, nsa_cmp_w2, diff_qk_gain, diff_lambda, diff_subln_gain, odd_w_in, dsa_qk_gain):
    b, s, d = x.shape
    assert d == D_MODEL and s % TK == 0 and s >= NSA_WINDOW + TQ_SPARSE
    pos = jnp.arange(s)
    cos, sin = _rope_tables(pos, HEAD_DIM // 2, 1)
    cos64, sin64 = _rope_tables(pos, DIFF_QK_DIM // 2, 2)
    n_cmp_pad = s // NSA_CMP_STRIDE
    cmp_last = jnp.arange(n_cmp_pad) * NSA_CMP_STRIDE + NSA_CMP_LEN - 1
    cos_c, sin_c = _rope_tables(cmp_last, HEAD_DIM // 2, 1)
    ovlt = _overlap_matrix_t(s, n_cmp_pad)
    ltri = jnp.asarray(np.tril(np.ones((TK, TK), np.float32), -1), dtype=BF16)
    wbias = _window_bias(TQ_SPARSE)

    mk_all, mv_all = _memkv(mem, mem_norm_gain, mem_w_kv.astype(BF16), mem_qk_gain[:, 1:2, :])
    w_out_b = w_out.astype(BF16)

    x2 = x.reshape(b * s, d)
    for i in range(DEPTH):
        mem_q_gain = mem_qk_gain[i, 0:1, :]
        if i % 2 == 0:
            e = i // 2
            proj = _proj(x2, norm_gain[i], _even_weight(even_w_in[e]), 768)
            dg2 = jnp.tile(diff_qk_gain[e], (1, 2))
            qa, ks, kw, vst, vwt, craw, bq, bk, bvt, mq = _prep_even(
                proj, b, s, cos, sin, cos64, sin64, nsa_qk_gain[e], dg2, mem_q_gain)
            w1 = nsa_cmp_w1[e].reshape(2, NSA_CMP_LEN, HEAD_DIM, NSA_CMP_HIDDEN).astype(BF16)
            kc, vct = _compress(craw, nsa_cmp_pos[e], w1, nsa_cmp_w2[e].astype(BF16),
                                nsa_qk_gain[e, 1:2, :], cos_c, sin_c)
            y_a = _attend(_score_bound(HEAD_DIM, nsa_qk_gain[e, 0], nsa_qk_gain[e, 1:4], HEAD_DIM ** -0.5),
                          functools.partial(_nsa, b=b, s=s), qa, ks, vst, kw, vwt, kc, vct, ovlt, wbias, proj)
            lambda_init = 0.8 - 0.6 * math.exp(-0.3 * i)
            y_b = _attend(_score_bound(DIFF_QK_DIM, diff_qk_gain[e, 0], diff_qk_gain[e, 1], DIFF_QK_DIM ** -0.5),
                          functools.partial(_diff, b=b, s=s, lambda_init=lambda_init),
                          bq, bk, bvt, diff_lambda[e], diff_subln_gain[e].reshape(1, LANES), proj)
            y_m = _memattn(mq, mk_all[i], mv_all[i], proj, b, s, E_MZ)
            ys = [y_a, y_b, y_m]
        else:
            o = i // 2
            proj = _proj(x2, norm_gain[i], _odd_weight(odd_w_in[o]), 512)
            cq, ck, cvt, iq, ik, mq = _prep_odd(proj, b, s, cos, sin, cos64, sin64, dsa_qk_gain[o], mem_q_gain)
            y_c = _attend(_score_bound(HEAD_DIM, dsa_qk_gain[o, 0], dsa_qk_gain[o, 1], HEAD_DIM ** -0.5),
                          functools.partial(_dsa, b=b, s=s), cq, ck, cvt, iq, ik, ltri, proj)
            y_m = _memattn(mq, mk_all[i], mv_all[i], proj, b, s, O_MZ)
            ys = [y_c, y_m]
        x2 = _outproj(x2, ys, w_out_b[i])
    return x2.reshape(b, s, d)
```

```python
import functools
import math

import jax
import jax.numpy as jnp
import numpy as np
from jax import lax
from jax.experimental import pallas as pl
from jax.experimental.pallas import tpu as pltpu

F32 = jnp.float32
BF16 = jnp.bfloat16

D_MODEL = 2048
DEPTH = 4
HEAD_DIM = 128
ROPE_THETA = 10000.0
EPS = 1e-6
MASKED = -1e30
FORCED = 1e9
NSA_HEADS = 8
NSA_KV_GROUPS = 2
NSA_CMP_LEN = 32
NSA_CMP_STRIDE = 16
NSA_CMP_HIDDEN = 256
NSA_SEL_LEN = 64
NSA_SEL_TOPN = 16
NSA_WINDOW = 512
DIFF_HEADS = 4
DIFF_QK_DIM = 64
DSA_HEADS = 12
DSA_KV_HEADS = 4
IDX_HEADS = 16
IDX_DIM = 64
DSA_TOPK_MAX = 256
MEM_HEADS = 4

LANES = 128
SUBLANES = 8
VMEM_LIMIT_BYTES = 56 * 1024 * 1024
LOG2E = 1.4426950408889634

PREP_ROWS = 256
TQ_SPARSE = 128
TQ_DIFF = 256
TK = 512
VT_BLOCK = 256
NEG_INIT = -1e30
NEG_MASK = -2e30
FAST_LOG2_BOUND = 40.0

EVEN_UNITS = 54
E_AQ, E_AKV, E_BQ, E_BK, E_BV, E_MQ, E_AZ, E_BZ, E_MZ, E_AG = 0, 8, 20, 24, 28, 32, 36, 44, 48, 52
EVEN_PREP_UNITS = E_AZ
ODD_UNITS = 52
O_CQ, O_CK, O_CV, O_IQ, O_MQ, O_IKW, O_CZ, O_MZ = 0, 12, 16, 20, 28, 32, 36, 48
ODD_PREP_UNITS = O_IKW + 1


def _cparams(sem):
    return pltpu.CompilerParams(dimension_semantics=sem, vmem_limit_bytes=VMEM_LIMIT_BYTES)


def _dot(a, b):
    return jnp.dot(a, b, preferred_element_type=F32)


def _dot_nt(a, b):
    return lax.dot_general(a, b, (((1,), (1,)), ((), ())), preferred_element_type=F32)


def _silu(x):
    return x * jax.nn.sigmoid(x)


def _rms128(x, gain):
    return x * lax.rsqrt(jnp.mean(x * x, axis=-1, keepdims=True) + EPS) * gain


def _rms64(x, gain):
    lo = lax.broadcasted_iota(jnp.int32, x.shape, 1) < 64
    xx = x * x
    s_lo = jnp.sum(jnp.where(lo, xx, 0.0), axis=-1, keepdims=True)
    s_hi = jnp.sum(jnp.where(lo, 0.0, xx), axis=-1, keepdims=True)
    ms = jnp.where(lo, s_lo, s_hi) * (1.0 / 64.0)
    return x * lax.rsqrt(ms + EPS) * gain


def _partner(x, half):
    n = x.shape[-1]
    lane = lax.broadcasted_iota(jnp.int32, x.shape, 1)
    if 2 * half == n:
        return pltpu.roll(x, half, 1)
    a = pltpu.roll(x, half, 1)
    b = pltpu.roll(x, n - half, 1)
    src_a = pltpu.roll(lane, half, 1)
    want = jnp.where((lane & (2 * half - 1)) < half, lane + half, lane - half)
    return jnp.where(src_a == want, a, b)


def _rope(x, cos, sin_signed, half):
    return x * cos + _partner(x, half) * sin_signed


def _transpose_bf16(x):
    n = x.shape[1]
    eye = (lax.broadcasted_iota(jnp.int32, (n, n), 0) == lax.broadcasted_iota(jnp.int32, (n, n), 1)).astype(BF16)
    return _dot_nt(eye, x.astype(BF16)).astype(BF16)


def _fold8(x, op, short_chains=False):
    rows, w = x.shape
    if short_chains:
        x = op(x.reshape(SUBLANES, rows // SUBLANES, w), axis=0)
        rows = rows // SUBLANES
    return op(x.reshape(rows // SUBLANES, SUBLANES, w), axis=0)


def _col_max(x):
    return jnp.max(_fold8(x, jnp.max), axis=0, keepdims=True)


def _col_sum(x):
    return jnp.sum(_fold8(x, jnp.sum), axis=0, keepdims=True)


def _proj_kernel(x_ref, g_ref, w_ref, o_ref, hn_ref):
    @pl.when(pl.program_id(1) == 0)
    def _():
        x = x_ref[...]
        hn_ref[...] = _rms128(x, g_ref[...]).astype(BF16)

    o_ref[...] = _dot(hn_ref[...], w_ref[...])


def _proj(x2, gain, w, tn):
    n, d = x2.shape
    c = w.shape[1]
    tm = min(1024, n)
    return pl.pallas_call(
        _proj_kernel,
        grid=(n // tm, c // tn),
        in_specs=[
            pl.BlockSpec((tm, d), lambda i, j: (i, 0)),
            pl.BlockSpec((1, d), lambda i, j: (0, 0)),
            pl.BlockSpec((d, tn), lambda i, j: (0, j)),
        ],
        out_specs=pl.BlockSpec((tm, tn), lambda i, j: (i, j)),
        out_shape=jax.ShapeDtypeStruct((n, c), F32),
        scratch_shapes=[pltpu.VMEM((tm, d), BF16)],
        compiler_params=_cparams(("arbitrary", "arbitrary")),
        name="proj",
    )(x2, gain.reshape(1, d), w)


def _prep_even_kernel(p_ref, cos_ref, sin_ref, cos64_ref, sin64_ref, ng_ref, dg_ref, mg_ref,
                      qa_ref, ks_ref, kw_ref, vst_ref, vwt_ref, craw_ref, bq_ref, bk_ref, bvt_ref, mq_ref):
    cos, sin = cos_ref[...], sin_ref[...]
    cos64, sin64 = cos64_ref[...], sin64_ref[...]
    rep = NSA_HEADS // NSA_KV_GROUPS
    tq = TQ_SPARSE

    def unit(u):
        return p_ref[:, u * LANES:(u + 1) * LANES]

    for h in range(NSA_HEADS):
        q = _rope(_rms128(unit(E_AQ + h), ng_ref[0:1, :]), cos, sin, 64).astype(BF16)
        g, r = divmod(h, rep)
        for qb in range(PREP_ROWS // tq):
            qa_ref[0, g, qb, r * tq:(r + 1) * tq, :] = q[qb * tq:(qb + 1) * tq, :]
    for g in range(NSA_KV_GROUPS):
        craw_ref[0, 0, g] = unit(E_AKV + 0 + g)
        craw_ref[0, 1, g] = unit(E_AKV + 2 + g)
        ks_ref[0, g] = _rope(_rms128(unit(E_AKV + 4 + g), ng_ref[2:3, :]), cos, sin, 64).astype(BF16)
        vst_ref[0, g, 0] = _transpose_bf16(unit(E_AKV + 6 + g))
        kw_ref[0, g] = _rope(_rms128(unit(E_AKV + 8 + g), ng_ref[3:4, :]), cos, sin, 64).astype(BF16)
        vw = unit(E_AKV + 10 + g)
        for kb in range(PREP_ROWS // LANES):
            vwt_ref[0, g, kb] = _transpose_bf16(vw[kb * LANES:(kb + 1) * LANES, :])
    lo = lax.broadcasted_iota(jnp.int32, cos.shape, 1) < 64
    for h in range(DIFF_HEADS):
        q = _rope(_rms64(unit(E_BQ + h), dg_ref[0:1, :]), cos64, sin64, 32)
        bq_ref[0, h, 0, 0:PREP_ROWS, :] = jnp.where(lo, q, 0.0).astype(BF16)
        bq_ref[0, h, 0, PREP_ROWS:2 * PREP_ROWS, :] = jnp.where(lo, 0.0, q).astype(BF16)
        bk_ref[0, h] = _rope(_rms64(unit(E_BK + h), dg_ref[1:2, :]), cos64, sin64, 32).astype(BF16)
        bvt_ref[0, h, 0] = _transpose_bf16(unit(E_BV + h))
    for h in range(MEM_HEADS):
        mq_ref[0, h] = _rms128(unit(E_MQ + h), mg_ref[...]).astype(BF16)


def _prep_even(proj, b, s, cos, sin, cos64, sin64, nsa_gain, diff_gain2, mem_gain):
    ts = PREP_ROWS
    assert ts == TQ_DIFF == VT_BLOCK
    nb = s // ts
    c = EVEN_PREP_UNITS * LANES
    g = NSA_KV_GROUPS
    rep = NSA_HEADS // g
    row = lambda bb, i: (i, 0)
    const = lambda bb, i: (0, 0)
    seq = lambda n: pl.BlockSpec((1, n, ts, LANES), lambda bb, i: (bb, 0, i, 0))
    blk5 = lambda n, k, r, cdim: pl.BlockSpec((1, n, k, r, cdim), lambda bb, i: (bb, 0, i, 0, 0))
    return pl.pallas_call(
        _prep_even_kernel,
        grid=(b, nb),
        in_specs=[
            pl.BlockSpec((ts, c), lambda bb, i: (bb * nb + i, 0)),
            pl.BlockSpec((ts, LANES), row), pl.BlockSpec((ts, LANES), row),
            pl.BlockSpec((ts, LANES), row), pl.BlockSpec((ts, LANES), row),
            pl.BlockSpec((4, LANES), const), pl.BlockSpec((2, LANES), const), pl.BlockSpec((1, LANES), const),
        ],
        out_specs=[
            blk5(g, ts // TQ_SPARSE, rep * TQ_SPARSE, LANES),
            seq(g), seq(g),
            blk5(g, 1, LANES, VT_BLOCK),
            blk5(g, ts // LANES, LANES, LANES),
            pl.BlockSpec((1, 2, g, ts, LANES), lambda bb, i: (bb, 0, 0, i, 0)),
            blk5(DIFF_HEADS, 1, 2 * TQ_DIFF, LANES),
            seq(DIFF_HEADS),
            blk5(DIFF_HEADS, 1, LANES, VT_BLOCK),
            seq(MEM_HEADS),
        ],
        out_shape=[
            jax.ShapeDtypeStruct((b, g, s // TQ_SPARSE, rep * TQ_SPARSE, LANES), BF16),
            jax.ShapeDtypeStruct((b, g, s, LANES), BF16),
            jax.ShapeDtypeStruct((b, g, s, LANES), BF16),
            jax.ShapeDtypeStruct((b, g, s // VT_BLOCK, LANES, VT_BLOCK), BF16),
            jax.ShapeDtypeStruct((b, g, s // LANES, LANES, LANES), BF16),
            jax.ShapeDtypeStruct((b, 2, g, s, LANES), F32),
            jax.ShapeDtypeStruct((b, DIFF_HEADS, s // TQ_DIFF, 2 * TQ_DIFF, LANES), BF16),
            jax.ShapeDtypeStruct((b, DIFF_HEADS, s, LANES), BF16),
            jax.ShapeDtypeStruct((b, DIFF_HEADS, s // VT_BLOCK, LANES, VT_BLOCK), BF16),
            jax.ShapeDtypeStruct((b, MEM_HEADS, s, LANES), BF16),
        ],
        compiler_params=_cparams(("arbitrary", "arbitrary")),
        name="prep_even",
    )(proj, cos, sin, cos64, sin64, nsa_gain, diff_gain2, mem_gain)


def _compress_kernel(x_ref, pe_ref, w1_ref, w2_ref, g_ref, cos_ref, sin_ref, kc_ref, vct_ref, pad_ref, *, s, n_pad):
    for kind in range(2):
        pad_ref[0:s, :] = x_ref[0, kind, 0]
        pad_ref[s:s + NSA_CMP_LEN, :] = jnp.zeros((NSA_CMP_LEN, LANES), F32)
        acc = jnp.zeros((n_pad, NSA_CMP_HIDDEN), F32)
        for l in range(NSA_CMP_LEN):
            rows = pad_ref[pl.ds(l, n_pad, stride=NSA_CMP_STRIDE), :] + pe_ref[kind, l:l + 1, :]
            acc = acc + _dot(rows.astype(BF16), w1_ref[kind, l])
        out = _dot(_silu(acc).astype(BF16), w2_ref[kind])
        if kind == 0:
            kc_ref[0, 0] = _rope(_rms128(out, g_ref[...]), cos_ref[...], sin_ref[...], 64).astype(BF16)
        else:
            vct_ref[0, 0] = out.T.astype(BF16)


def _compress(craw, pe, w1, w2, gain, cos_c, sin_c):
    b, _, g, s, _ = craw.shape
    n_pad = s // NSA_CMP_STRIDE
    kern = functools.partial(_compress_kernel, s=s, n_pad=n_pad)
    whole = lambda shape: pl.BlockSpec(shape, lambda bb, gg: (0,) * len(shape))
    return pl.pallas_call(
        kern,
        grid=(b, g),
        in_specs=[
            pl.BlockSpec((1, 2, 1, s, LANES), lambda bb, gg: (bb, 0, gg, 0, 0)),
            whole((2, NSA_CMP_LEN, LANES)),
            whole((2, NSA_CMP_LEN, LANES, NSA_CMP_HIDDEN)),
            whole((2, NSA_CMP_HIDDEN, LANES)),
            whole((1, LANES)), whole((n_pad, LANES)), whole((n_pad, LANES)),
        ],
        out_specs=[
            pl.BlockSpec((1, 1, n_pad, LANES), lambda bb, gg: (bb, gg, 0, 0)),
            pl.BlockSpec((1, 1, LANES, n_pad), lambda bb, gg: (bb, gg, 0, 0)),
        ],
        out_shape=[
            jax.ShapeDtypeStruct((b, g, n_pad, LANES), BF16),
            jax.ShapeDtypeStruct((b, g, LANES, n_pad), BF16),
        ],
        scratch_shapes=[pltpu.VMEM((s + NSA_CMP_LEN, LANES), F32)],
        compiler_params=_cparams(("arbitrary", "arbitrary")),
        name="nsa_compress",
    )(craw, pe, w1, w2, gain, cos_c, sin_c)


def _softmax_parts_t(st, bounded):
    if not bounded:
        st = st - _col_max(st)
    e = jnp.exp2(st)
    return e.astype(BF16), 1.0 / jnp.maximum(_col_sum(e), 1e-30)


def _online_init(m_ref, l_ref, acc_ref):
    m_ref[...] = jnp.full(m_ref.shape, NEG_INIT, F32)
    l_ref[...] = jnp.zeros(l_ref.shape, F32)
    acc_ref[...] = jnp.zeros(acc_ref.shape, F32)


def _online_step(idx, st, pv, m_ref, l_ref, acc_ref, bounded):
    if bounded:
        p = jnp.exp2(st)
        l_ref[idx] = l_ref[idx] + _fold8(p, jnp.sum)
        acc_ref[idx] = acc_ref[idx] + pv(p.astype(BF16))
        return
    m_old = m_ref[idx]
    m_new = jnp.maximum(m_old, _col_max(st))
    alpha = jnp.exp2(m_old - m_new)
    p = jnp.exp2(st - m_new)
    l_ref[idx] = alpha * l_ref[idx] + _fold8(p, jnp.sum)
    acc_ref[idx] = alpha * acc_ref[idx] + pv(p.astype(BF16))
    m_ref[idx] = m_new


def _online_result(idx, l_ref, acc_ref):
    return acc_ref[idx] / jnp.maximum(jnp.sum(l_ref[idx], axis=0, keepdims=True), 1e-30)


def _flash_pairs(n_chunks, n_groups, qk, mask, soft, sa_ref, sb_ref):
    for g in range(n_groups):
        qk(0, g, sa_ref)
    n_pairs = (n_chunks - 1) // 2

    def pair(j, carry):
        a = 2 * j
        bias_a = mask(a, False)
        bias_b = mask(a + 1, False)
        for g in range(n_groups):
            qk(a + 1, g, sb_ref)
            soft(a, g, sa_ref, bias_a)
            qk(a + 2, g, sa_ref)
            soft(a + 1, g, sb_ref, bias_b)
        return carry

    lax.fori_loop(0, n_pairs, pair, 0)
    e = 2 * n_pairs

    @pl.when(e + 1 < n_chunks)
    def _():
        bias_a = mask(e, True)
        bias_b = mask(e + 1, True)
        for g in range(n_groups):
            qk(e + 1, g, sb_ref)
            soft(e, g, sa_ref, bias_a)
            soft(e + 1, g, sb_ref, bias_b)

    @pl.when(e + 1 >= n_chunks)
    def _():
        bias_a = mask(e, True)
        for g in range(n_groups):
            soft(e, g, sa_ref, bias_a)


def _pv_blocks(vt_ref, lead, chunk, tk):
    per = tk // VT_BLOCK

    def pv(p):
        out = _dot(vt_ref[lead + (chunk * per,)], p[0:VT_BLOCK])
        for i in range(1, per):
            out = out + _dot(vt_ref[lead + (chunk * per + i,)], p[i * VT_BLOCK:(i + 1) * VT_BLOCK])
        return out

    return pv


def _top_n_mask_t(scores, n):
    row = lax.broadcasted_iota(jnp.int32, scores.shape, 0).astype(F32)
    height = float(scores.shape[0])
    work = scores
    sel = jnp.zeros(scores.shape, F32)
    for _ in range(n):
        m = jnp.max(work, axis=0, keepdims=True)
        first = jnp.min(jnp.where(work == m, row, height), axis=0, keepdims=True)
        pick = row == first
        sel = jnp.where(pick, 1.0, sel)
        work = jnp.where(pick, -jnp.inf, work)
    return sel


def _nsa_kernel(q_ref, kc_ref, vct_ref, ks_ref, vst_ref, kw_ref, vwt_ref, ovlt_ref, wbias_ref, gl_ref, z_ref, o_ref,
                m_ref, l_ref, acc_ref, sa_ref, sb_ref, selt_ref, oct_ref, owt_ref, *, tq, tk, top_n, bounded):
    rep = NSA_HEADS // NSA_KV_GROUPS
    w = rep * tq
    scale = HEAD_DIM ** -0.5
    q0 = pl.program_id(2) * tq
    t_row = q0 + (lax.broadcasted_iota(jnp.int32, (1, w), 1) & (tq - 1))
    t_one = t_row[:, 0:tq]
    q = q_ref[0, 0, 0]

    n_cmp_pad = kc_ref.shape[2]
    cmp_last = lax.broadcasted_iota(jnp.int32, (n_cmp_pad, w), 0) * NSA_CMP_STRIDE + (NSA_CMP_LEN - 1)
    st = jnp.where(cmp_last <= t_row, _dot_nt(kc_ref[0, 0], q) * (scale * LOG2E), NEG_MASK)
    p, inv = _softmax_parts_t(st, bounded)
    inv = jnp.where(t_row >= NSA_CMP_LEN - 1, inv, 0.0)
    oct_ref[...] = _dot(vct_ref[0, 0], p) * inv
    imp_heads = _dot(ovlt_ref[...], p) * inv
    imp = imp_heads[:, 0:tq]
    for r in range(1, rep):
        imp = imp + imp_heads[:, r * tq:(r + 1) * tq]

    n_sel = imp.shape[0]
    sel_shift = NSA_SEL_LEN.bit_length() - 1
    j = lax.broadcasted_iota(jnp.int32, (n_sel, tq), 0)
    cur = jnp.right_shift(t_one, sel_shift)
    visible = j <= cur
    forced = (j == 0) | (j >= cur - 1)
    imp = jnp.where(visible, jnp.where(forced, FORCED, imp), MASKED)
    selt_ref[...] = _top_n_mask_t(imp, top_n)

    span = NSA_WINDOW + tq
    start = pl.multiple_of(jnp.maximum(q0 - NSA_WINDOW, 0), tq)
    st = (_dot_nt(kw_ref[0, 0, pl.ds(start, span), :], q) * (scale * LOG2E)
          + jnp.concatenate([wbias_ref[0]] * rep, axis=1))
    p, inv = _softmax_parts_t(st, bounded)
    blk0 = start // LANES
    ow = _dot(vwt_ref[0, 0, blk0], p[0:LANES])
    for i in range(1, span // LANES):
        ow = ow + _dot(vwt_ref[0, 0, blk0 + i], p[i * LANES:(i + 1) * LANES])
    owt_ref[...] = ow * inv

    _online_init(m_ref, l_ref, acc_ref)
    n_chunks = (q0 + tq + tk - 1) // tk
    blocks_per_chunk = tk // NSA_SEL_LEN

    def qk(c, g, dst):
        dst[g] = _dot_nt(ks_ref[0, 0, pl.ds(pl.multiple_of(c * tk, tk), tk), :], q) * (scale * LOG2E)

    def mask(c, tail):
        rows = selt_ref[pl.ds(pl.multiple_of(c * blocks_per_chunk, blocks_per_chunk), blocks_per_chunk), :]
        ok = jnp.concatenate(
            [jnp.broadcast_to(rows[i:i + 1, :], (NSA_SEL_LEN, tq)) for i in range(blocks_per_chunk)], axis=0) > 0.5
        if tail:
            ok = ok & ((c * tk + lax.broadcasted_iota(jnp.int32, (tk, tq), 0)) <= t_one)
        return jnp.concatenate([jnp.where(ok, 0.0, NEG_MASK)] * rep, axis=1)

    def soft(c, g, src, bias):
        _online_step(g, src[g] + bias, _pv_blocks(vst_ref, (0, 0), c, tk), m_ref, l_ref, acc_ref, bounded)

    _flash_pairs(n_chunks, 1, qk, mask, soft, sa_ref, sb_ref)

    gates_t = jax.nn.sigmoid(gl_ref[...]).T
    os_t = _online_result(0, l_ref, acc_ref)
    for r in range(rep):
        sl = slice(r * tq, (r + 1) * tq)
        out_t = (gates_t[3 * r:3 * r + 1, :] * oct_ref[:, sl] + gates_t[3 * r + 1:3 * r + 2, :] * os_t[:, sl]
                 + gates_t[3 * r + 2:3 * r + 3, :] * owt_ref[:, sl])
        o_ref[:, r * LANES:(r + 1) * LANES] = (out_t.T * _silu(z_ref[:, r * LANES:(r + 1) * LANES])).astype(BF16)


def _window_bias(tq):
    span = NSA_WINDOW + tq
    n_early = NSA_WINDOW // tq
    row = np.arange(span)[:, None]
    lane = np.arange(tq)[None, :]
    tables = []
    for i in range(n_early + 1):
        t = i * tq + lane
        s_pos = (0 if i < n_early else t[0, 0] - NSA_WINDOW) + row
        ok = (s_pos <= t) & (s_pos > t - NSA_WINDOW)
        tables.append(np.where(ok, 0.0, NEG_MASK))
    return jnp.asarray(np.stack(tables), dtype=F32)


def _nsa(qa, ks, vst, kw, vwt, kc, vct, ovlt, wbias, proj, b, s, bounded):
    tq, tk = TQ_SPARSE, TK
    nq = s // tq
    g = NSA_KV_GROUPS
    rep = NSA_HEADS // g
    w = rep * tq
    n_cmp_pad = kc.shape[2]
    n_sel = s // NSA_SEL_LEN
    kern = functools.partial(_nsa_kernel, tq=tq, tk=tk, top_n=min(NSA_SEL_TOPN, n_sel), bounded=bounded)
    per_group = lambda shape: pl.BlockSpec((1, 1) + shape, lambda bb, gg, i: (bb, gg) + (0,) * len(shape))
    return pl.pallas_call(
        kern,
        grid=(b, g, nq),
        in_specs=[
            pl.BlockSpec((1, 1, 1, w, LANES), lambda bb, gg, i: (bb, gg, i, 0, 0)),
            per_group((n_cmp_pad, LANES)), per_group((LANES, n_cmp_pad)),
            per_group((s, LANES)), per_group((s // VT_BLOCK, LANES, VT_BLOCK)),
            per_group((s, LANES)), per_group((s // LANES, LANES, LANES)),
            pl.BlockSpec((n_sel, n_cmp_pad), lambda bb, gg, i: (0, 0)),
            pl.BlockSpec((1,) + wbias.shape[1:], lambda bb, gg, i: (jnp.minimum(i, wbias.shape[0] - 1), 0, 0)),
            pl.BlockSpec((tq, LANES), lambda bb, gg, i: (bb * nq + i, E_AG + gg)),
            pl.BlockSpec((tq, rep * LANES), lambda bb, gg, i: (bb * nq + i, E_AZ // rep + gg)),
        ],
        out_specs=pl.BlockSpec((tq, rep * LANES), lambda bb, gg, i: (bb * nq + i, gg)),
        out_shape=jax.ShapeDtypeStruct((b * s, NSA_HEADS * LANES), BF16),
        scratch_shapes=[
            pltpu.VMEM((1, 1, w), F32), pltpu.VMEM((1, SUBLANES, w), F32), pltpu.VMEM((1, LANES, w), F32),
            pltpu.VMEM((1, tk, w), F32), pltpu.VMEM((1, tk, w), F32),
            pltpu.VMEM((n_sel, tq), F32), pltpu.VMEM((LANES, w), F32), pltpu.VMEM((LANES, w), F32),
        ],
        compiler_params=_cparams(("arbitrary", "arbitrary", "arbitrary")),
        name="nsa_attention",
    )(qa, kc, vct, ks, vst, kw, vwt, ovlt, wbias, proj, proj)


def _diff_kernel(q_ref, k_ref, vt_ref, lam_ref, sg_ref, z_ref, o_ref, m_ref, l_ref, acc_ref, sa_ref, sb_ref,
                 *, tq, tk, lambda_init, bounded):
    w = 2 * tq
    q0 = pl.program_id(2) * tq
    t_row = q0 + (lax.broadcasted_iota(jnp.int32, (1, w), 1) & (tq - 1))
    q = q_ref[0, 0, 0]
    _online_init(m_ref, l_ref, acc_ref)
    n_chunks = (q0 + tq + tk - 1) // tk

    def qk(c, g, dst):
        dst[g] = (_dot_nt(k_ref[0, 0, pl.ds(pl.multiple_of(c * tk, tk), tk), :], q)
                  * (DIFF_QK_DIM ** -0.5 * LOG2E))

    def mask(c, tail):
        if not tail:
            return None
        key = c * tk + lax.broadcasted_iota(jnp.int32, (tk, w), 0)
        return jnp.where(key <= t_row, 0.0, NEG_MASK)

    def soft(c, g, src, bias):
        st = src[g] if bias is None else src[g] + bias
        _online_step(g, st, _pv_blocks(vt_ref, (0, 0), c, tk), m_ref, l_ref, acc_ref, bounded)

    _flash_pairs(n_chunks, 1, qk, mask, soft, sa_ref, sb_ref)

    lv = lam_ref[...]
    lam = (jnp.exp(jnp.sum(lv[0:1] * lv[1:2], axis=-1, keepdims=True))
           - jnp.exp(jnp.sum(lv[2:3] * lv[3:4], axis=-1, keepdims=True)) + lambda_init)
    o_t = _online_result(0, l_ref, acc_ref)
    o = (o_t[:, 0:tq] - lam * o_t[:, tq:w]).T
    o = _rms128(o, sg_ref[...]) * (1.0 - lambda_init)
    o_ref[...] = (o * _silu(z_ref[...])).astype(BF16)


def _diff(bq, bk, bvt, lam_vecs, subln_gain, proj, b, s, lambda_init, bounded):
    tq, tk = TQ_DIFF, TK
    nq = s // tq
    w = 2 * tq
    kern = functools.partial(_diff_kernel, tq=tq, tk=tk, lambda_init=lambda_init, bounded=bounded)
    return pl.pallas_call(
        kern,
        grid=(b, DIFF_HEADS, nq),
        in_specs=[
            pl.BlockSpec((1, 1, 1, w, LANES), lambda bb, h, i: (bb, h, i, 0, 0)),
            pl.BlockSpec((1, 1, s, LANES), lambda bb, h, i: (bb, h, 0, 0)),
            pl.BlockSpec((1, 1, s // VT_BLOCK, LANES, VT_BLOCK), lambda bb, h, i: (bb, h, 0, 0, 0)),
            pl.BlockSpec((4, DIFF_QK_DIM), lambda bb, h, i: (0, 0)),
            pl.BlockSpec((1, LANES), lambda bb, h, i: (0, 0)),
            pl.BlockSpec((tq, LANES), lambda bb, h, i: (bb * nq + i, E_BZ + h)),
        ],
        out_specs=pl.BlockSpec((tq, LANES), lambda bb, h, i: (bb * nq + i, h)),
        out_shape=jax.ShapeDtypeStruct((b * s, DIFF_HEADS * LANES), BF16),
        scratch_shapes=[
            pltpu.VMEM((1, 1, w), F32), pltpu.VMEM((1, SUBLANES, w), F32), pltpu.VMEM((1, LANES, w), F32),
            pltpu.VMEM((1, tk, w), F32), pltpu.VMEM((1, tk, w), F32),
        ],
        compiler_params=_cparams(("arbitrary", "arbitrary", "arbitrary")),
        name="diff_attention",
    )(bq, bk, bvt, lam_vecs, subln_gain, proj)


def _memkv_kernel(mem_ref, mg_ref, w_ref, kg_ref, k_ref, v_ref):
    mem_n = _rms128(mem_ref[0], mg_ref[...]).astype(BF16)
    kv = _dot(mem_n, w_ref[0])
    for h in range(MEM_HEADS):
        k_ref[0, 0, h] = _rms128(kv[:, h * LANES:(h + 1) * LANES], kg_ref[0]).astype(BF16)
        v_ref[0, 0, h] = kv[:, (MEM_HEADS + h) * LANES:(MEM_HEADS + h + 1) * LANES].astype(BF16)


def _memkv(mem, mem_gain, w_kv, k_gain):
    b, m, d = mem.shape
    depth = w_kv.shape[0]
    c = w_kv.shape[2]
    out = jax.ShapeDtypeStruct((depth, b, MEM_HEADS, m, LANES), BF16)
    ospec = pl.BlockSpec((1, 1, MEM_HEADS, m, LANES), lambda i, bb: (i, bb, 0, 0, 0))
    return pl.pallas_call(
        _memkv_kernel,
        grid=(depth, b),
        in_specs=[
            pl.BlockSpec((1, m, d), lambda i, bb: (bb, 0, 0)),
            pl.BlockSpec((1, d), lambda i, bb: (0, 0)),
            pl.BlockSpec((1, d, c), lambda i, bb: (i, 0, 0)),
            pl.BlockSpec((1, 1, LANES), lambda i, bb: (i, 0, 0)),
        ],
        out_specs=[ospec, ospec],
        out_shape=[out, out],
        compiler_params=_cparams(("arbitrary", "arbitrary")),
        name="mem_kv",
    )(mem, mem_gain.reshape(1, d), w_kv, k_gain)


def _memattn_kernel(q_ref, k_ref, v_ref, z_ref, o_ref):
    scale = HEAD_DIM ** -0.5
    for h in range(MEM_HEADS):
        s = _dot_nt(q_ref[0, h], k_ref[0, h]) * scale
        e = jnp.exp(s - jnp.max(s, axis=-1, keepdims=True))
        p = e / jnp.sum(e, axis=-1, keepdims=True)
        o = _dot(p.astype(BF16), v_ref[0, h])
        o_ref[:, h * LANES:(h + 1) * LANES] = (o * _silu(z_ref[:, h * LANES:(h + 1) * LANES])).astype(BF16)


def _memattn(mq, mk, mv, proj, b, s, z_unit):
    tq = min(512, s)
    nq = s // tq
    m = mk.shape[2]
    kvspec = pl.BlockSpec((1, MEM_HEADS, m, LANES), lambda bb, i: (bb, 0, 0, 0))
    return pl.pallas_call(
        _memattn_kernel,
        grid=(b, nq),
        in_specs=[
            pl.BlockSpec((1, MEM_HEADS, tq, LANES), lambda bb, i: (bb, 0, i, 0)),
            kvspec, kvspec,
            pl.BlockSpec((tq, MEM_HEADS * LANES), lambda bb, i: (bb * nq + i, z_unit // MEM_HEADS)),
        ],
        out_specs=pl.BlockSpec((tq, MEM_HEADS * LANES), lambda bb, i: (bb * nq + i, 0)),
        out_shape=jax.ShapeDtypeStruct((b * s, MEM_HEADS * LANES), BF16),
        compiler_params=_cparams(("arbitrary", "arbitrary")),
        name="mem_attention",
    )(mq, mk, mv, proj)


def _prep_odd_kernel(p_ref, cos_ref, sin_ref, cos64_ref, sin64_ref, cg_ref, mg_ref,
                     cq_ref, ck_ref, cvt_ref, iq_ref, ik_ref, mq_ref):
    cos, sin = cos_ref[...], sin_ref[...]
    cos64, sin64 = cos64_ref[...], sin64_ref[...]
    rep = DSA_HEADS // DSA_KV_HEADS
    tq = TQ_SPARSE
    n_qb = PREP_ROWS // tq

    def unit(u):
        return p_ref[:, u * LANES:(u + 1) * LANES]

    for h in range(DSA_HEADS):
        q = _rope(_rms128(unit(O_CQ + h), cg_ref[0:1, :]), cos, sin, 64).astype(BF16)
        g, r = divmod(h, rep)
        for qb in range(n_qb):
            cq_ref[0, g, qb, r * tq:(r + 1) * tq, :] = q[qb * tq:(qb + 1) * tq, :]
    for h in range(DSA_KV_HEADS):
        ck_ref[0, h] = _rope(_rms128(unit(O_CK + h), cg_ref[1:2, :]), cos, sin, 64).astype(BF16)
        cvt_ref[0, h, 0] = _transpose_bf16(unit(O_CV + h))
    lo = lax.broadcasted_iota(jnp.int32, cos.shape, 1) < 64
    for u in range(IDX_HEADS // 2):
        x = _rope(unit(O_IQ + u), cos64, sin64, 32)
        even = jnp.where(lo, x, 0.0).astype(BF16)
        odd = jnp.where(lo, pltpu.roll(x, 64, 1), 0.0).astype(BF16)
        for qb in range(n_qb):
            iq_ref[0, qb, (2 * u) * tq:(2 * u + 1) * tq, :] = even[qb * tq:(qb + 1) * tq, :]
            iq_ref[0, qb, (2 * u + 1) * tq:(2 * u + 2) * tq, :] = odd[qb * tq:(qb + 1) * tq, :]
    ik = _rope(unit(O_IKW), cos64, sin64, 32)
    ik_ref[0] = jnp.where(lo, ik, 0.0).astype(BF16)
    for h in range(MEM_HEADS):
        mq_ref[0, h] = _rms128(unit(O_MQ + h), mg_ref[...]).astype(BF16)


def _prep_odd(proj, b, s, cos, sin, cos64, sin64, dsa_gain, mem_gain):
    ts = PREP_ROWS
    nb = s // ts
    c = ODD_PREP_UNITS * LANES
    rep = DSA_HEADS // DSA_KV_HEADS
    row = lambda bb, i: (i, 0)
    const = lambda bb, i: (0, 0)
    seq = lambda n: pl.BlockSpec((1, n, ts, LANES), lambda bb, i: (bb, 0, i, 0))
    return pl.pallas_call(
        _prep_odd_kernel,
        grid=(b, nb),
        in_specs=[
            pl.BlockSpec((ts, c), lambda bb, i: (bb * nb + i, 0)),
            pl.BlockSpec((ts, LANES), row), pl.BlockSpec((ts, LANES), row),
            pl.BlockSpec((ts, LANES), row), pl.BlockSpec((ts, LANES), row),
            pl.BlockSpec((2, LANES), const), pl.BlockSpec((1, LANES), const),
        ],
        out_specs=[
            pl.BlockSpec((1, DSA_KV_HEADS, ts // TQ_SPARSE, rep * TQ_SPARSE, LANES), lambda bb, i: (bb, 0, i, 0, 0)),
            seq(DSA_KV_HEADS),
            pl.BlockSpec((1, DSA_KV_HEADS, 1, LANES, VT_BLOCK), lambda bb, i: (bb, 0, i, 0, 0)),
            pl.BlockSpec((1, ts // TQ_SPARSE, IDX_HEADS * TQ_SPARSE, LANES), lambda bb, i: (bb, i, 0, 0)),
            pl.BlockSpec((1, ts, LANES), lambda bb, i: (bb, i, 0)),
            seq(MEM_HEADS),
        ],
        out_shape=[
            jax.ShapeDtypeStruct((b, DSA_KV_HEADS, s // TQ_SPARSE, rep * TQ_SPARSE, LANES), BF16),
            jax.ShapeDtypeStruct((b, DSA_KV_HEADS, s, LANES), BF16),
            jax.ShapeDtypeStruct((b, DSA_KV_HEADS, s // VT_BLOCK, LANES, VT_BLOCK), BF16),
            jax.ShapeDtypeStruct((b, s // TQ_SPARSE, IDX_HEADS * TQ_SPARSE, LANES), BF16),
            jax.ShapeDtypeStruct((b, s, LANES), BF16),
            jax.ShapeDtypeStruct((b, MEM_HEADS, s, LANES), BF16),
        ],
        compiler_params=_cparams(("arbitrary", "arbitrary")),
        name="prep_odd",
    )(proj, cos, sin, cos64, sin64, dsa_gain, mem_gain)


def _sortable_key(x):
    bits = pltpu.bitcast(x + 0.0, jnp.int32)
    return jnp.where(bits < 0, bits ^ jnp.int32(0x7FFFFFFF), bits)


def _dsa_kernel(q_ref, k_ref, vt_ref, iq_ref, ik_ref, w_ref, ltri_ref, z_ref, o_ref,
                key_ref, m_ref, l_ref, acc_ref, sa_ref, sb_ref, ties_ref, *, tq, tk, top_k, bounded):
    rep = DSA_HEADS // DSA_KV_HEADS
    scale = HEAD_DIM ** -0.5
    q0 = pl.program_id(1) * tq
    t_row = q0 + lax.broadcasted_iota(jnp.int32, (1, tq), 1)
    n_chunks = (q0 + tq + tk - 1) // tk
    int_min = jnp.int32(-2147483648)
    heads_per_dot = 4

    w_t = (w_ref[...] * (IDX_HEADS ** -0.5 * IDX_DIM ** -0.5)).T

    def score_chunk(c, carry):
        k0 = pl.multiple_of(c * tk, tk)
        ik = ik_ref[0, pl.ds(k0, tk), :]
        sc = jnp.zeros((tk, tq), F32)
        for h0 in range(0, IDX_HEADS, heads_per_dot):
            x = _dot_nt(ik, iq_ref[0, 0, h0 * tq:(h0 + heads_per_dot) * tq, :])
            for hh in range(heads_per_dot):
                h = h0 + hh
                sc = sc + jnp.maximum(x[:, hh * tq:(hh + 1) * tq], 0.0) * w_t[IDX_DIM + h:IDX_DIM + h + 1, :]
        causal = (k0 + lax.broadcasted_iota(jnp.int32, (tk, tq), 0)) <= t_row
        key_ref[c] = _sortable_key(jnp.where(causal, sc, MASKED))
        return carry

    lax.fori_loop(0, n_chunks, score_chunk, 0)

    def count(pred):
        def one(c):
            return _fold8(pred(key_ref[c]).astype(F32), jnp.sum, short_chains=True)

        def two(j, acc):
            return acc + (one(2 * j) + one(2 * j + 1))

        acc = lax.fori_loop(0, n_chunks // 2, two, jnp.zeros((SUBLANES, tq), F32))
        acc = lax.fori_loop(n_chunks - n_chunks % 2, n_chunks, lambda c, a: a + one(c), acc)
        return jnp.sum(acc, axis=0, keepdims=True)

    def search(it, state):
        thr_u, above = state
        cand_u = thr_u | jnp.left_shift(jnp.int32(1), 31 - it)
        cand = cand_u ^ int_min
        cnt = count(lambda kk: kk >= cand)
        take = cnt >= top_k
        return jnp.where(take, cand_u, thr_u), jnp.where(take, above, cnt)

    thr_u, above = lax.fori_loop(0, 32, search, (jnp.zeros((1, tq), jnp.int32), jnp.zeros((1, tq), F32)))
    thr = thr_u ^ int_min
    budget = top_k - above

    _online_init(m_ref, l_ref, acc_ref)
    ties_ref[...] = jnp.zeros(ties_ref.shape, F32)

    def qk(c, g, dst):
        dst[g] = _dot_nt(k_ref[0, g, pl.ds(pl.multiple_of(c * tk, tk), tk), :], q_ref[0, g, 0]) * (scale * LOG2E)

    def mask(c, tail):
        keys = key_ref[c]
        tie = keys == thr
        rank = _dot(ltri_ref[...], tie.astype(BF16)) + ties_ref[...]
        ok = (keys > thr) | (tie & (rank < budget))
        if tail:
            ok = ok & ((c * tk + lax.broadcasted_iota(jnp.int32, (tk, tq), 0)) <= t_row)
        ties_ref[...] = ties_ref[...] + _col_sum(tie.astype(F32))
        return jnp.concatenate([jnp.where(ok, 0.0, NEG_MASK)] * rep, axis=1)

    def soft(c, g, src, bias):
        _online_step(g, src[g] + bias, _pv_blocks(vt_ref, (0, g), c, tk), m_ref, l_ref, acc_ref, bounded)

    _flash_pairs(n_chunks, DSA_KV_HEADS, qk, mask, soft, sa_ref, sb_ref)

    for g in range(DSA_KV_HEADS):
        o_t = _online_result(g, l_ref, acc_ref)
        for r in range(rep):
            h = g * rep + r
            o = o_t[:, r * tq:(r + 1) * tq].T
            o_ref[:, h * LANES:(h + 1) * LANES] = (o * _silu(z_ref[:, h * LANES:(h + 1) * LANES])).astype(BF16)


def _dsa(cq, ck, cvt, iq, ik, ltri, proj, b, s, bounded):
    tq, tk = TQ_SPARSE, TK
    nq = s // tq
    g = DSA_KV_HEADS
    rep = DSA_HEADS // g
    w = rep * tq
    kern = functools.partial(_dsa_kernel, tq=tq, tk=tk, top_k=min(DSA_TOPK_MAX, s // 4), bounded=bounded)
    return pl.pallas_call(
        kern,
        grid=(b, nq),
        in_specs=[
            pl.BlockSpec((1, g, 1, w, LANES), lambda bb, i: (bb, 0, i, 0, 0)),
            pl.BlockSpec((1, g, s, LANES), lambda bb, i: (bb, 0, 0, 0), pipeline_mode=pl.Buffered(1)),
            pl.BlockSpec((1, g, s // VT_BLOCK, LANES, VT_BLOCK), lambda bb, i: (bb, 0, 0, 0, 0),
                         pipeline_mode=pl.Buffered(1)),
            pl.BlockSpec((1, 1, IDX_HEADS * tq, LANES), lambda bb, i: (bb, i, 0, 0)),
            pl.BlockSpec((1, s, LANES), lambda bb, i: (bb, 0, 0), pipeline_mode=pl.Buffered(1)),
            pl.BlockSpec((tq, LANES), lambda bb, i: (bb * nq + i, O_IKW)),
            pl.BlockSpec((tk, tk), lambda bb, i: (0, 0)),
            pl.BlockSpec((tq, DSA_HEADS * LANES), lambda bb, i: (bb * nq + i, O_CZ // DSA_HEADS)),
        ],
        out_specs=pl.BlockSpec((tq, DSA_HEADS * LANES), lambda bb, i: (bb * nq + i, 0)),
        out_shape=jax.ShapeDtypeStruct((b * s, DSA_HEADS * LANES), BF16),
        scratch_shapes=[
            pltpu.VMEM((s // tk, tk, tq), jnp.int32),
            pltpu.VMEM((g, 1, w), F32), pltpu.VMEM((g, SUBLANES, w), F32), pltpu.VMEM((g, LANES, w), F32),
            pltpu.VMEM((g, tk, w), F32), pltpu.VMEM((g, tk, w), F32),
            pltpu.VMEM((1, tq), F32),
        ],
        compiler_params=_cparams(("arbitrary", "arbitrary")),
        name="dsa_attention",
    )(cq, ck, cvt, iq, ik, proj, ltri, proj)


def _outproj_kernel(*refs, n_parts):
    x_ref = refs[0]
    y_refs = refs[1:1 + n_parts]
    w_refs = refs[1 + n_parts:1 + 2 * n_parts]
    o_ref = refs[1 + 2 * n_parts]
    acc = x_ref[...]
    for y_ref, w_ref in zip(y_refs, w_refs):
        acc = acc + _dot(y_ref[...], w_ref[...])
    o_ref[...] = acc


def _outproj(x2, ys, w_out):
    n, d = x2.shape
    tm = min(1024, n)
    tn = 1024
    widths = [y.shape[1] for y in ys]
    starts = np.cumsum([0] + widths[:-1]).tolist()
    ws = [w_out[st:st + wd] for st, wd in zip(starts, widths)]
    kern = functools.partial(_outproj_kernel, n_parts=len(ys))
    return pl.pallas_call(
        kern,
        grid=(d // tn, n // tm),
        in_specs=([pl.BlockSpec((tm, tn), lambda j, i: (i, j))]
                  + [pl.BlockSpec((tm, wd), lambda j, i: (i, 0)) for wd in widths]
                  + [pl.BlockSpec((wd, tn), lambda j, i: (0, j)) for wd in widths]),
        out_specs=pl.BlockSpec((tm, tn), lambda j, i: (i, j)),
        out_shape=jax.ShapeDtypeStruct((n, d), F32),
        compiler_params=_cparams(("arbitrary", "arbitrary")),
        name="out_proj",
    )(x2, *ys, *ws)


def _rope_tables(pos, half, reps):
    inv = ROPE_THETA ** (-jnp.arange(half, dtype=F32) / half)
    ang = pos.astype(F32)[:, None] * inv[None, :]
    cos, sin = jnp.cos(ang), jnp.sin(ang)
    return jnp.tile(jnp.concatenate([cos, cos], -1), (1, reps)), jnp.tile(jnp.concatenate([-sin, sin], -1), (1, reps))


def _split_cols(w, sizes):
    return jnp.split(w, np.cumsum(sizes)[:-1].tolist(), axis=-1)


def _even_weight(w):
    sizes = (1024, 1536, 24, 1024, 512, 512, 512, 512, 512, 512)
    a_q, a_kv, a_g, a_z, b_q, b_k, b_v, b_z, m_q, m_z = _split_cols(w.astype(BF16), sizes)
    d = w.shape[0]
    per_group = 3 * NSA_HEADS // NSA_KV_GROUPS
    gates = [jnp.pad(a_g[:, g * per_group:(g + 1) * per_group], ((0, 0), (0, LANES - per_group)))
             for g in range(NSA_KV_GROUPS)]
    out = jnp.concatenate([a_q, a_kv, b_q, b_k, b_v, m_q, a_z, b_z, m_z] + gates, axis=-1)
    assert out.shape == (d, EVEN_UNITS * LANES)
    return out


def _odd_weight(w):
    sizes = (1536, 512, 512, 1024, 64, 16, 1536, 512, 512)
    c_q, c_k, c_v, i_q, i_k, i_w, c_z, m_q, m_z = _split_cols(w.astype(BF16), sizes)
    d = w.shape[0]
    ikw = jnp.pad(jnp.concatenate([i_k, i_w], -1), ((0, 0), (0, LANES - IDX_DIM - IDX_HEADS)))
    pad = jnp.zeros((d, (O_CZ - ODD_PREP_UNITS) * LANES), BF16)
    out = jnp.concatenate([c_q, c_k, c_v, i_q, m_q, ikw, pad, c_z, m_z], axis=-1)
    assert out.shape == (d, ODD_UNITS * LANES)
    return out


def _overlap_matrix_t(s, n_cmp_pad):
    n_cmp = (s - NSA_CMP_LEN) // NSA_CMP_STRIDE + 1
    n_sel = s // NSA_SEL_LEN
    cmp_start = np.arange(n_cmp) * NSA_CMP_STRIDE
    sel_start = np.arange(n_sel) * NSA_SEL_LEN
    ov = np.clip(np.minimum(cmp_start[:, None] + NSA_CMP_LEN, sel_start[None, :] + NSA_SEL_LEN)
                 - np.maximum(cmp_start[:, None], sel_start[None, :]), 0, None) / NSA_CMP_LEN
    full = np.zeros((n_sel, n_cmp_pad), np.float32)
    full[:, :n_cmp] = ov.T
    return jnp.asarray(full, dtype=BF16)


def _score_bound(dim, gain_q, gain_k, scale):
    return dim * jnp.max(jnp.abs(gain_q)) * jnp.max(jnp.abs(gain_k)) * (scale * LOG2E * 1.02)


def _attend(bound, fn, *operands):
    return lax.cond(bound <= FAST_LOG2_BOUND,
                    lambda *a: fn(*a, bounded=True), lambda *a: fn(*a, bounded=False), *operands)


def kernel(x, mem, norm_gain, mem_norm_gain, mem_w_kv, mem_qk_gain, w_out, even_w_in, nsa_qk_gain, nsa_cmp_pos,
           nsa_cmp_w1, nsa_cmp_w2, diff_qk_gain, diff_lambda, diff_subln_gain, odd_w_in, dsa_qk_gain):
    b, s, d = x.shape
    assert d == D_MODEL and s % TK == 0 and s >= NSA_WINDOW + TQ_SPARSE
    pos = jnp.arange(s)
    cos, sin = _rope_tables(pos, HEAD_DIM // 2, 1)
    cos64, sin64 = _rope_tables(pos, DIFF_QK_DIM // 2, 2)
    n_cmp_pad = s // NSA_CMP_STRIDE
    cmp_last = jnp.arange(n_cmp_pad) * NSA_CMP_STRIDE + NSA_CMP_LEN - 1
    cos_c, sin_c = _rope_tables(cmp_last, HEAD_DIM // 2, 1)
    ovlt = _overlap_matrix_t(s, n_cmp_pad)
    ltri = jnp.asarray(np.tril(np.ones((TK, TK), np.float32), -1), dtype=BF16)
    wbias = _window_bias(TQ_SPARSE)

    mk_all, mv_all = _memkv(mem, mem_norm_gain, mem_w_kv.astype(BF16), mem_qk_gain[:, 1:2, :])
    w_out_b = w_out.astype(BF16)

    x2 = x.reshape(b * s, d)
    for i in range(DEPTH):
        mem_q_gain = mem_qk_gain[i, 0:1, :]
        if i % 2 == 0:
            e = i // 2
            proj = _proj(x2, norm_gain[i], _even_weight(even_w_in[e]), 768)
            dg2 = jnp.tile(diff_qk_gain[e], (1, 2))
            qa, ks, kw, vst, vwt, craw, bq, bk, bvt, mq = _prep_even(
                proj, b, s, cos, sin, cos64, sin64, nsa_qk_gain[e], dg2, mem_q_gain)
            w1 = nsa_cmp_w1[e].reshape(2, NSA_CMP_LEN, HEAD_DIM, NSA_CMP_HIDDEN).astype(BF16)
            kc, vct = _compress(craw, nsa_cmp_pos[e], w1, nsa_cmp_w2[e].astype(BF16),
                                nsa_qk_gain[e, 1:2, :], cos_c, sin_c)
            y_a = _attend(_score_bound(HEAD_DIM, nsa_qk_gain[e, 0], nsa_qk_gain[e, 1:4], HEAD_DIM ** -0.5),
                          functools.partial(_nsa, b=b, s=s), qa, ks, vst, kw, vwt, kc, vct, ovlt, wbias, proj)
            lambda_init = 0.8 - 0.6 * math.exp(-0.3 * i)
            y_b = _attend(_score_bound(DIFF_QK_DIM, diff_qk_gain[e, 0], diff_qk_gain[e, 1], DIFF_QK_DIM ** -0.5),
                          functools.partial(_diff, b=b, s=s, lambda_init=lambda_init),
                          bq, bk, bvt, diff_lambda[e], diff_subln_gain[e].reshape(1, LANES), proj)
            y_m = _memattn(mq, mk_all[i], mv_all[i], proj, b, s, E_MZ)
            ys = [y_a, y_b, y_m]
        else:
            o = i // 2
            proj = _proj(x2, norm_gain[i], _odd_weight(odd_w_in[o]), 512)
            cq, ck, cvt, iq, ik, mq = _prep_odd(proj, b, s, cos, sin, cos64, sin64, dsa_qk_gain[o], mem_q_gain)
            y_c = _attend(_score_bound(HEAD_DIM, dsa_qk_gain[o, 0], dsa_qk_gain[o, 1], HEAD_DIM ** -0.5),
                          functools.partial(_dsa, b=b, s=s), cq, ck, cvt, iq, ik, ltri, proj)
            y_m = _memattn(mq, mk_all[i], mv_all[i], proj, b, s, O_MZ)
            ys = [y_c, y_m]
        x2 = _outproj(x2, ys, w_out_b[i])
    return x2.reshape(b, s, d)
```

```python
import functools
import math

import jax
import jax.numpy as jnp
import numpy as np
from jax import lax
from jax.experimental import pallas as pl
from jax.experimental.pallas import tpu as pltpu

F32 = jnp.float32
BF16 = jnp.bfloat16

D_MODEL = 2048
DEPTH = 4
HEAD_DIM = 128
ROPE_THETA = 10000.0
EPS = 1e-6
MASKED = -1e30
FORCED = 1e9
NSA_HEADS = 8
NSA_KV_GROUPS = 2
NSA_CMP_LEN = 32
NSA_CMP_STRIDE = 16
NSA_CMP_HIDDEN = 256
NSA_SEL_LEN = 64
NSA_SEL_TOPN = 16
NSA_WINDOW = 512
DIFF_HEADS = 4
DIFF_QK_DIM = 64
DSA_HEADS = 12
DSA_KV_HEADS = 4
IDX_HEADS = 16
IDX_DIM = 64
DSA_TOPK_MAX = 256
MEM_HEADS = 4

LANES = 128
SUBLANES = 8
VMEM_LIMIT_BYTES = 56 * 1024 * 1024
LOG2E = 1.4426950408889634

PREP_ROWS = 256
TQ_SPARSE = 128
TQ_DIFF = 256
TK = 512
VT_BLOCK = 256
NEG_INIT = -1e30
NEG_MASK = -2e30
FAST_LOG2_BOUND = 40.0

EVEN_UNITS = 54
E_AQ, E_AKV, E_BQ, E_BK, E_BV, E_MQ, E_AZ, E_BZ, E_MZ, E_AG = 0, 8, 20, 24, 28, 32, 36, 44, 48, 52
EVEN_PREP_UNITS = E_AZ
ODD_UNITS = 52
O_CQ, O_CK, O_CV, O_IQ, O_MQ, O_IKW, O_CZ, O_MZ = 0, 12, 16, 20, 28, 32, 36, 48
ODD_PREP_UNITS = O_IKW + 1


def _cparams(sem):
    return pltpu.CompilerParams(dimension_semantics=sem, vmem_limit_bytes=VMEM_LIMIT_BYTES)


def _dot(a, b):
    return jnp.dot(a, b, preferred_element_type=F32)


def _dot_nt(a, b):
    return lax.dot_general(a, b, (((1,), (1,)), ((), ())), preferred_element_type=F32)


def _silu(x):
    return x * jax.nn.sigmoid(x)


def _rms128(x, gain):
    return x * lax.rsqrt(jnp.mean(x * x, axis=-1, keepdims=True) + EPS) * gain


def _rms64(x, gain):
    lo = lax.broadcasted_iota(jnp.int32, x.shape, 1) < 64
    xx = x * x
    s_lo = jnp.sum(jnp.where(lo, xx, 0.0), axis=-1, keepdims=True)
    s_hi = jnp.sum(jnp.where(lo, 0.0, xx), axis=-1, keepdims=True)
    ms = jnp.where(lo, s_lo, s_hi) * (1.0 / 64.0)
    return x * lax.rsqrt(ms + EPS) * gain


def _partner(x, half):
    n = x.shape[-1]
    lane = lax.broadcasted_iota(jnp.int32, x.shape, 1)
    if 2 * half == n:
        return pltpu.roll(x, half, 1)
    a = pltpu.roll(x, half, 1)
    b = pltpu.roll(x, n - half, 1)
    src_a = pltpu.roll(lane, half, 1)
    want = jnp.where((lane & (2 * half - 1)) < half, lane + half, lane - half)
    return jnp.where(src_a == want, a, b)


def _rope(x, cos, sin_signed, half):
    return x * cos + _partner(x, half) * sin_signed


def _transpose_bf16(x):
    n = x.shape[1]
    eye = (lax.broadcasted_iota(jnp.int32, (n, n), 0) == lax.broadcasted_iota(jnp.int32, (n, n), 1)).astype(BF16)
    return _dot_nt(eye, x.astype(BF16)).astype(BF16)


def _fold8(x, op, short_chains=False):
    rows, w = x.shape
    if short_chains:
        x = op(x.reshape(SUBLANES, rows // SUBLANES, w), axis=0)
        rows = rows // SUBLANES
    return op(x.reshape(rows // SUBLANES, SUBLANES, w), axis=0)


def _col_max(x):
    return jnp.max(_fold8(x, jnp.max), axis=0, keepdims=True)


def _col_sum(x):
    return jnp.sum(_fold8(x, jnp.sum), axis=0, keepdims=True)


def _proj_kernel(x_ref, g_ref, w_ref, o_ref, hn_ref):
    @pl.when(pl.program_id(1) == 0)
    def _():
        x = x_ref[...]
        hn_ref[...] = _rms128(x, g_ref[...]).astype(BF16)

    o_ref[...] = _dot(hn_ref[...], w_ref[...])


def _proj(x2, gain, w, tn):
    n, d = x2.shape
    c = w.shape[1]
    tm = min(1024, n)
    return pl.pallas_call(
        _proj_kernel,
        grid=(n // tm, c // tn),
        in_specs=[
            pl.BlockSpec((tm, d), lambda i, j: (i, 0)),
            pl.BlockSpec((1, d), lambda i, j: (0, 0)),
            pl.BlockSpec((d, tn), lambda i, j: (0, j)),
        ],
        out_specs=pl.BlockSpec((tm, tn), lambda i, j: (i, j)),
        out_shape=jax.ShapeDtypeStruct((n, c), F32),
        scratch_shapes=[pltpu.VMEM((tm, d), BF16)],
        compiler_params=_cparams(("arbitrary", "arbitrary")),
        name="proj",
    )(x2, gain.reshape(1, d), w)


def _prep_even_kernel(p_ref, cos_ref, sin_ref, cos64_ref, sin64_ref, ng_ref, dg_ref, mg_ref,
                      qa_ref, ks_ref, kw_ref, vst_ref, vwt_ref, craw_ref, bq_ref, bk_ref, bvt_ref, mq_ref):
    cos, sin = cos_ref[...], sin_ref[...]
    cos64, sin64 = cos64_ref[...], sin64_ref[...]
    rep = NSA_HEADS // NSA_KV_GROUPS
    tq = TQ_SPARSE

    def unit(u):
        return p_ref[:, u * LANES:(u + 1) * LANES]

    for h in range(NSA_HEADS):
        q = _rope(_rms128(unit(E_AQ + h), ng_ref[0:1, :]), cos, sin, 64).astype(BF16)
        g, r = divmod(h, rep)
        for qb in range(PREP_ROWS // tq):
            qa_ref[0, g, qb, r * tq:(r + 1) * tq, :] = q[qb * tq:(qb + 1) * tq, :]
    for g in range(NSA_KV_GROUPS):
        craw_ref[0, 0, g] = unit(E_AKV + 0 + g)
        craw_ref[0, 1, g] = unit(E_AKV + 2 + g)
        ks_ref[0, g] = _rope(_rms128(unit(E_AKV + 4 + g), ng_ref[2:3, :]), cos, sin, 64).astype(BF16)
        vst_ref[0, g, 0] = _transpose_bf16(unit(E_AKV + 6 + g))
        kw_ref[0, g] = _rope(_rms128(unit(E_AKV + 8 + g), ng_ref[3:4, :]), cos, sin, 64).astype(BF16)
        vw = unit(E_AKV + 10 + g)
        for kb in range(PREP_ROWS // LANES):
            vwt_ref[0, g, kb] = _transpose_bf16(vw[kb * LANES:(kb + 1) * LANES, :])
    lo = lax.broadcasted_iota(jnp.int32, cos.shape, 1) < 64
    for h in range(DIFF_HEADS):
        q = _rope(_rms64(unit(E_BQ + h), dg_ref[0:1, :]), cos64, sin64, 32)
        bq_ref[0, h, 0, 0:PREP_ROWS, :] = jnp.where(lo, q, 0.0).astype(BF16)
        bq_ref[0, h, 0, PREP_ROWS:2 * PREP_ROWS, :] = jnp.where(lo, 0.0, q).astype(BF16)
        bk_ref[0, h] = _rope(_rms64(unit(E_BK + h), dg_ref[1:2, :]), cos64, sin64, 32).astype(BF16)
        bvt_ref[0, h, 0] = _transpose_bf16(unit(E_BV + h))
    for h in range(MEM_HEADS):
        mq_ref[0, h] = _rms128(unit(E_MQ + h), mg_ref[...]).astype(BF16)


def _prep_even(proj, b, s, cos, sin, cos64, sin64, nsa_gain, diff_gain2, mem_gain):
    ts = PREP_ROWS
    assert ts == TQ_DIFF == VT_BLOCK
    nb = s // ts
    c = EVEN_PREP_UNITS * LANES
    g = NSA_KV_GROUPS
    rep = NSA_HEADS // g
    row = lambda bb, i: (i, 0)
    const = lambda bb, i: (0, 0)
    seq = lambda n: pl.BlockSpec((1, n, ts, LANES), lambda bb, i: (bb, 0, i, 0))
    blk5 = lambda n, k, r, cdim: pl.BlockSpec((1, n, k, r, cdim), lambda bb, i: (bb, 0, i, 0, 0))
    return pl.pallas_call(
        _prep_even_kernel,
        grid=(b, nb),
        in_specs=[
            pl.BlockSpec((ts, c), lambda bb, i: (bb * nb + i, 0)),
            pl.BlockSpec((ts, LANES), row), pl.BlockSpec((ts, LANES), row),
            pl.BlockSpec((ts, LANES), row), pl.BlockSpec((ts, LANES), row),
            pl.BlockSpec((4, LANES), const), pl.BlockSpec((2, LANES), const), pl.BlockSpec((1, LANES), const),
        ],
        out_specs=[
            blk5(g, ts // TQ_SPARSE, rep * TQ_SPARSE, LANES),
            seq(g), seq(g),
            blk5(g, 1, LANES, VT_BLOCK),
            blk5(g, ts // LANES, LANES, LANES),
            pl.BlockSpec((1, 2, g, ts, LANES), lambda bb, i: (bb, 0, 0, i, 0)),
            blk5(DIFF_HEADS, 1, 2 * TQ_DIFF, LANES),
            seq(DIFF_HEADS),
            blk5(DIFF_HEADS, 1, LANES, VT_BLOCK),
            seq(MEM_HEADS),
        ],
        out_shape=[
            jax.ShapeDtypeStruct((b, g, s // TQ_SPARSE, rep * TQ_SPARSE, LANES), BF16),
            jax.ShapeDtypeStruct((b, g, s, LANES), BF16),
            jax.ShapeDtypeStruct((b, g, s, LANES), BF16),
            jax.ShapeDtypeStruct((b, g, s // VT_BLOCK, LANES, VT_BLOCK), BF16),
            jax.ShapeDtypeStruct((b, g, s // LANES, LANES, LANES), BF16),
            jax.ShapeDtypeStruct((b, 2, g, s, LANES), F32),
            jax.ShapeDtypeStruct((b, DIFF_HEADS, s // TQ_DIFF, 2 * TQ_DIFF, LANES), BF16),
            jax.ShapeDtypeStruct((b, DIFF_HEADS, s, LANES), BF16),
            jax.ShapeDtypeStruct((b, DIFF_HEADS, s // VT_BLOCK, LANES, VT_BLOCK), BF16),
            jax.ShapeDtypeStruct((b, MEM_HEADS, s, LANES), BF16),
        ],
        compiler_params=_cparams(("arbitrary", "arbitrary")),
        name="prep_even",
    )(proj, cos, sin, cos64, sin64, nsa_gain, diff_gain2, mem_gain)


def _compress_kernel(x_ref, pe_ref, w1_ref, w2_ref, g_ref, cos_ref, sin_ref, kc_ref, vct_ref, pad_ref, *, s, n_pad):
    for kind in range(2):
        pad_ref[0:s, :] = x_ref[0, kind, 0]
        pad_ref[s:s + NSA_CMP_LEN, :] = jnp.zeros((NSA_CMP_LEN, LANES), F32)
        acc = jnp.zeros((n_pad, NSA_CMP_HIDDEN), F32)
        for l in range(NSA_CMP_LEN):
            rows = pad_ref[pl.ds(l, n_pad, stride=NSA_CMP_STRIDE), :] + pe_ref[kind, l:l + 1, :]
            acc = acc + _dot(rows.astype(BF16), w1_ref[kind, l])
        out = _dot(_silu(acc).astype(BF16), w2_ref[kind])
        if kind == 0:
            kc_ref[0, 0] = _rope(_rms128(out, g_ref[...]), cos_ref[...], sin_ref[...], 64).astype(BF16)
        else:
            vct_ref[0, 0] = out.T.astype(BF16)


def _compress(craw, pe, w1, w2, gain, cos_c, sin_c):
    b, _, g, s, _ = craw.shape
    n_pad = s // NSA_CMP_STRIDE
    kern = functools.partial(_compress_kernel, s=s, n_pad=n_pad)
    whole = lambda shape: pl.BlockSpec(shape, lambda bb, gg: (0,) * len(shape))
    return pl.pallas_call(
        kern,
        grid=(b, g),
        in_specs=[
            pl.BlockSpec((1, 2, 1, s, LANES), lambda bb, gg: (bb, 0, gg, 0, 0)),
            whole((2, NSA_CMP_LEN, LANES)),
            whole((2, NSA_CMP_LEN, LANES, NSA_CMP_HIDDEN)),
            whole((2, NSA_CMP_HIDDEN, LANES)),
            whole((1, LANES)), whole((n_pad, LANES)), whole((n_pad, LANES)),
        ],
        out_specs=[
            pl.BlockSpec((1, 1, n_pad, LANES), lambda bb, gg: (bb, gg, 0, 0)),
            pl.BlockSpec((1, 1, LANES, n_pad), lambda bb, gg: (bb, gg, 0, 0)),
        ],
        out_shape=[
            jax.ShapeDtypeStruct((b, g, n_pad, LANES), BF16),
            jax.ShapeDtypeStruct((b, g, LANES, n_pad), BF16),
        ],
        scratch_shapes=[pltpu.VMEM((s + NSA_CMP_LEN, LANES), F32)],
        compiler_params=_cparams(("arbitrary", "arbitrary")),
        name="nsa_compress",
    )(craw, pe, w1, w2, gain, cos_c, sin_c)


def _softmax_parts_t(st, bounded):
    if not bounded:
        st = st - _col_max(st)
    e = jnp.exp2(st)
    return e.astype(BF16), 1.0 / jnp.maximum(_col_sum(e), 1e-30)


def _online_init(m_ref, l_ref, acc_ref):
    m_ref[...] = jnp.full(m_ref.shape, NEG_INIT, F32)
    l_ref[...] = jnp.zeros(l_ref.shape, F32)
    acc_ref[...] = jnp.zeros(acc_ref.shape, F32)


def _online_step(idx, st, pv, m_ref, l_ref, acc_ref, bounded):
    if bounded:
        p = jnp.exp2(st)
        l_ref[idx] = l_ref[idx] + _fold8(p, jnp.sum)
        acc_ref[idx] = acc_ref[idx] + pv(p.astype(BF16))
        return
    m_old = m_ref[idx]
    m_new = jnp.maximum(m_old, _col_max(st))
    alpha = jnp.exp2(m_old - m_new)
    p = jnp.exp2(st - m_new)
    l_ref[idx] = alpha * l_ref[idx] + _fold8(p, jnp.sum)
    acc_ref[idx] = alpha * acc_ref[idx] + pv(p.astype(BF16))
    m_ref[idx] = m_new


def _online_result(idx, l_ref, acc_ref):
    return acc_ref[idx] / jnp.maximum(jnp.sum(l_ref[idx], axis=0, keepdims=True), 1e-30)


def _flash_pairs(n_chunks, n_groups, qk, mask, soft, sa_ref, sb_ref):
    for g in range(n_groups):
        qk(0, g, sa_ref)
    n_pairs = (n_chunks - 1) // 2

    def pair(j, carry):
        a = 2 * j
        bias_a = mask(a, False)
        bias_b = mask(a + 1, False)
        for g in range(n_groups):
            qk(a + 1, g, sb_ref)
            soft(a, g, sa_ref, bias_a)
            qk(a + 2, g, sa_ref)
            soft(a + 1, g, sb_ref, bias_b)
        return carry

    lax.fori_loop(0, n_pairs, pair, 0)
    e = 2 * n_pairs

    @pl.when(e + 1 < n_chunks)
    def _():
        bias_a = mask(e, True)
        bias_b = mask(e + 1, True)
        for g in range(n_groups):
            qk(e + 1, g, sb_ref)
            soft(e, g, sa_ref, bias_a)
            soft(e + 1, g, sb_ref, bias_b)

    @pl.when(e + 1 >= n_chunks)
    def _():
        bias_a = mask(e, True)
        for g in range(n_groups):
            soft(e, g, sa_ref, bias_a)


def _pv_blocks(vt_ref, lead, chunk, tk):
    per = tk // VT_BLOCK

    def pv(p):
        out = _dot(vt_ref[lead + (chunk * per,)], p[0:VT_BLOCK])
        for i in range(1, per):
            out = out + _dot(vt_ref[lead + (chunk * per + i,)], p[i * VT_BLOCK:(i + 1) * VT_BLOCK])
        return out

    return pv


def _top_n_mask_t(scores, n):
    row = lax.broadcasted_iota(jnp.int32, scores.shape, 0).astype(F32)
    height = float(scores.shape[0])
    work = scores
    sel = jnp.zeros(scores.shape, F32)
    for _ in range(n):
        m = jnp.max(work, axis=0, keepdims=True)
        first = jnp.min(jnp.where(work == m, row, height), axis=0, keepdims=True)
        pick = row == first
        sel = jnp.where(pick, 1.0, sel)
        work = jnp.where(pick, -jnp.inf, work)
    return sel


def _nsa_kernel(q_ref, kc_ref, vct_ref, ks_ref, vst_ref, kw_ref, vwt_ref, ovlt_ref, wbias_ref, gl_ref, z_ref, o_ref,
                m_ref, l_ref, acc_ref, sa_ref, sb_ref, selt_ref, oct_ref, owt_ref, *, tq, tk, top_n, bounded):
    rep = NSA_HEADS // NSA_KV_GROUPS
    w = rep * tq
    scale = HEAD_DIM ** -0.5
    q0 = pl.program_id(2) * tq
    t_row = q0 + (lax.broadcasted_iota(jnp.int32, (1, w), 1) & (tq - 1))
    t_one = t_row[:, 0:tq]
    q = q_ref[0, 0, 0]

    n_cmp_pad = kc_ref.shape[2]
    cmp_last = lax.broadcasted_iota(jnp.int32, (n_cmp_pad, w), 0) * NSA_CMP_STRIDE + (NSA_CMP_LEN - 1)
    st = jnp.where(cmp_last <= t_row, _dot_nt(kc_ref[0, 0], q) * (scale * LOG2E), NEG_MASK)
    p, inv = _softmax_parts_t(st, bounded)
    inv = jnp.where(t_row >= NSA_CMP_LEN - 1, inv, 0.0)
    oct_ref[...] = _dot(vct_ref[0, 0], p) * inv
    imp_heads = _dot(ovlt_ref[...], p) * inv
    imp = imp_heads[:, 0:tq]
    for r in range(1, rep):
        imp = imp + imp_heads[:, r * tq:(r + 1) * tq]

    n_sel = imp.shape[0]
    sel_shift = NSA_SEL_LEN.bit_length() - 1
    j = lax.broadcasted_iota(jnp.int32, (n_sel, tq), 0)
    cur = jnp.right_shift(t_one, sel_shift)
    visible = j <= cur
    forced = (j == 0) | (j >= cur - 1)
    imp = jnp.where(visible, jnp.where(forced, FORCED, imp), MASKED)
    selt_ref[...] = _top_n_mask_t(imp, top_n)

    span = NSA_WINDOW + tq
    start = pl.multiple_of(jnp.maximum(q0 - NSA_WINDOW, 0), tq)
    st = (_dot_nt(kw_ref[0, 0, pl.ds(start, span), :], q) * (scale * LOG2E)
          + jnp.concatenate([wbias_ref[0]] * rep, axis=1))
    p, inv = _softmax_parts_t(st, bounded)
    blk0 = start // LANES
    ow = _dot(vwt_ref[0, 0, blk0], p[0:LANES])
    for i in range(1, span // LANES):
        ow = ow + _dot(vwt_ref[0, 0, blk0 + i], p[i * LANES:(i + 1) * LANES])
    owt_ref[...] = ow * inv

    _online_init(m_ref, l_ref, acc_ref)
    n_chunks = (q0 + tq + tk - 1) // tk
    blocks_per_chunk = tk // NSA_SEL_LEN

    def qk(c, g, dst):
        dst[g] = _dot_nt(ks_ref[0, 0, pl.ds(pl.multiple_of(c * tk, tk), tk), :], q) * (scale * LOG2E)

    def mask(c, tail):
        rows = selt_ref[pl.ds(pl.multiple_of(c * blocks_per_chunk, blocks_per_chunk), blocks_per_chunk), :]
        ok = jnp.concatenate(
            [jnp.broadcast_to(rows[i:i + 1, :], (NSA_SEL_LEN, tq)) for i in range(blocks_per_chunk)], axis=0) > 0.5
        if tail:
            ok = ok & ((c * tk + lax.broadcasted_iota(jnp.int32, (tk, tq), 0)) <= t_one)
        return jnp.concatenate([jnp.where(ok, 0.0, NEG_MASK)] * rep, axis=1)

    def soft(c, g, src, bias):
        _online_step(g, src[g] + bias, _pv_blocks(vst_ref, (0, 0), c, tk), m_ref, l_ref, acc_ref, bounded)

    _flash_pairs(n_chunks, 1, qk, mask, soft, sa_ref, sb_ref)

    gates_t = jax.nn.sigmoid(gl_ref[...]).T
    os_t = _online_result(0, l_ref, acc_ref)
    for r in range(rep):
        sl = slice(r * tq, (r + 1) * tq)
        out_t = (gates_t[3 * r:3 * r + 1, :] * oct_ref[:, sl] + gates_t[3 * r + 1:3 * r + 2, :] * os_t[:, sl]
                 + gates_t[3 * r + 2:3 * r + 3, :] * owt_ref[:, sl])
        o_ref[:, r * LANES:(r + 1) * LANES] = (out_t.T * _silu(z_ref[:, r * LANES:(r + 1) * LANES])).astype(BF16)


def _window_bias(tq):
    span = NSA_WINDOW + tq
    n_early = NSA_WINDOW // tq
    row = np.arange(span)[:, None]
    lane = np.arange(tq)[None, :]
    tables = []
    for i in range(n_early + 1):
        t = i * tq + lane
        s_pos = (0 if i < n_early else t[0, 0] - NSA_WINDOW) + row
        ok = (s_pos <= t) & (s_pos > t - NSA_WINDOW)
        tables.append(np.where(ok, 0.0, NEG_MASK))
    return jnp.asarray(np.stack(tables), dtype=F32)


def _nsa(qa, ks, vst, kw, vwt, kc, vct, ovlt, wbias, proj, b, s, bounded):
    tq, tk = TQ_SPARSE, TK
    nq = s // tq
    g = NSA_KV_GROUPS
    rep = NSA_HEADS // g
    w = rep * tq
    n_cmp_pad = kc.shape[2]
    n_sel = s // NSA_SEL_LEN
    kern = functools.partial(_nsa_kernel, tq=tq, tk=tk, top_n=min(NSA_SEL_TOPN, n_sel), bounded=bounded)
    per_group = lambda shape: pl.BlockSpec((1, 1) + shape, lambda bb, gg, i: (bb, gg) + (0,) * len(shape))
    return pl.pallas_call(
        kern,
        grid=(b, g, nq),
        in_specs=[
            pl.BlockSpec((1, 1, 1, w, LANES), lambda bb, gg, i: (bb, gg, i, 0, 0)),
            per_group((n_cmp_pad, LANES)), per_group((LANES, n_cmp_pad)),
            per_group((s, LANES)), per_group((s // VT_BLOCK, LANES, VT_BLOCK)),
            per_group((s, LANES)), per_group((s // LANES, LANES, LANES)),
            pl.BlockSpec((n_sel, n_cmp_pad), lambda bb, gg, i: (0, 0)),
            pl.BlockSpec((1,) + wbias.shape[1:], lambda bb, gg, i: (jnp.minimum(i, wbias.shape[0] - 1), 0, 0)),
            pl.BlockSpec((tq, LANES), lambda bb, gg, i: (bb * nq + i, E_AG + gg)),
            pl.BlockSpec((tq, rep * LANES), lambda bb, gg, i: (bb * nq + i, E_AZ // rep + gg)),
        ],
        out_specs=pl.BlockSpec((tq, rep * LANES), lambda bb, gg, i: (bb * nq + i, gg)),
        out_shape=jax.ShapeDtypeStruct((b * s, NSA_HEADS * LANES), BF16),
        scratch_shapes=[
            pltpu.VMEM((1, 1, w), F32), pltpu.VMEM((1, SUBLANES, w), F32), pltpu.VMEM((1, LANES, w), F32),
            pltpu.VMEM((1, tk, w), F32), pltpu.VMEM((1, tk, w), F32),
            pltpu.VMEM((n_sel, tq), F32), pltpu.VMEM((LANES, w), F32), pltpu.VMEM((LANES, w), F32),
        ],
        compiler_params=_cparams(("arbitrary", "arbitrary", "arbitrary")),
        name="nsa_attention",
    )(qa, kc, vct, ks, vst, kw, vwt, ovlt, wbias, proj, proj)


def _diff_kernel(q_ref, k_ref, vt_ref, lam_ref, sg_ref, z_ref, o_ref, m_ref, l_ref, acc_ref, sa_ref, sb_ref,
                 *, tq, tk, lambda_init, bounded):
    w = 2 * tq
    q0 = pl.program_id(2) * tq
    t_row = q0 + (lax.broadcasted_iota(jnp.int32, (1, w), 1) & (tq - 1))
    q = q_ref[0, 0, 0]
    _online_init(m_ref, l_ref, acc_ref)
    n_chunks = (q0 + tq + tk - 1) // tk

    def qk(c, g, dst):
        dst[g] = (_dot_nt(k_ref[0, 0, pl.ds(pl.multiple_of(c * tk, tk), tk), :], q)
                  * (DIFF_QK_DIM ** -0.5 * LOG2E))

    def mask(c, tail):
        if not tail:
            return None
        key = c * tk + lax.broadcasted_iota(jnp.int32, (tk, w), 0)
        return jnp.where(key <= t_row, 0.0, NEG_MASK)

    def soft(c, g, src, bias):
        st = src[g] if bias is None else src[g] + bias
        _online_step(g, st, _pv_blocks(vt_ref, (0, 0), c, tk), m_ref, l_ref, acc_ref, bounded)

    _flash_pairs(n_chunks, 1, qk, mask, soft, sa_ref, sb_ref)

    lv = lam_ref[...]
    lam = (jnp.exp(jnp.sum(lv[0:1] * lv[1:2], axis=-1, keepdims=True))
           - jnp.exp(jnp.sum(lv[2:3] * lv[3:4], axis=-1, keepdims=True)) + lambda_init)
    o_t = _online_result(0, l_ref, acc_ref)
    o = (o_t[:, 0:tq] - lam * o_t[:, tq:w]).T
    o = _rms128(o, sg_ref[...]) * (1.0 - lambda_init)
    o_ref[...] = (o * _silu(z_ref[...])).astype(BF16)


def _diff(bq, bk, bvt, lam_vecs, subln_gain, proj, b, s, lambda_init, bounded):
    tq, tk = TQ_DIFF, TK
    nq = s // tq
    w = 2 * tq
    kern = functools.partial(_diff_kernel, tq=tq, tk=tk, lambda_init=lambda_init, bounded=bounded)
    return pl.pallas_call(
        kern,
        grid=(b, DIFF_HEADS, nq),
        in_specs=[
            pl.BlockSpec((1, 1, 1, w, LANES), lambda bb, h, i: (bb, h, i, 0, 0)),
            pl.BlockSpec((1, 1, s, LANES), lambda bb, h, i: (bb, h, 0, 0)),
            pl.BlockSpec((1, 1, s // VT_BLOCK, LANES, VT_BLOCK), lambda bb, h, i: (bb, h, 0, 0, 0)),
            pl.BlockSpec((4, DIFF_QK_DIM), lambda bb, h, i: (0, 0)),
            pl.BlockSpec((1, LANES), lambda bb, h, i: (0, 0)),
            pl.BlockSpec((tq, LANES), lambda bb, h, i: (bb * nq + i, E_BZ + h)),
        ],
        out_specs=pl.BlockSpec((tq, LANES), lambda bb, h, i: (bb * nq + i, h)),
        out_shape=jax.ShapeDtypeStruct((b * s, DIFF_HEADS * LANES), BF16),
        scratch_shapes=[
            pltpu.VMEM((1, 1, w), F32), pltpu.VMEM((1, SUBLANES, w), F32), pltpu.VMEM((1, LANES, w), F32),
            pltpu.VMEM((1, tk, w), F32), pltpu.VMEM((1, tk, w), F32),
        ],
        compiler_params=_cparams(("arbitrary", "arbitrary", "arbitrary")),
        name="diff_attention",
    )(bq, bk, bvt, lam_vecs, subln_gain, proj)


def _memkv_kernel(mem_ref, mg_ref, w_ref, kg_ref, k_ref, v_ref):
    mem_n = _rms128(mem_ref[0], mg_ref[...]).astype(BF16)
    kv = _dot(mem_n, w_ref[0])
    for h in range(MEM_HEADS):
        k_ref[0, 0, h] = _rms128(kv[:, h * LANES:(h + 1) * LANES], kg_ref[0]).astype(BF16)
        v_ref[0, 0, h] = kv[:, (MEM_HEADS + h) * LANES:(MEM_HEADS + h + 1) * LANES].astype(BF16)


def _memkv(mem, mem_gain, w_kv, k_gain):
    b, m, d = mem.shape
    depth = w_kv.shape[0]
    c = w_kv.shape[2]
    out = jax.ShapeDtypeStruct((depth, b, MEM_HEADS, m, LANES), BF16)
    ospec = pl.BlockSpec((1, 1, MEM_HEADS, m, LANES), lambda i, bb: (i, bb, 0, 0, 0))
    return pl.pallas_call(
        _memkv_kernel,
        grid=(depth, b),
        in_specs=[
            pl.BlockSpec((1, m, d), lambda i, bb: (bb, 0, 0)),
            pl.BlockSpec((1, d), lambda i, bb: (0, 0)),
            pl.BlockSpec((1, d, c), lambda i, bb: (i, 0, 0)),
            pl.BlockSpec((1, 1, LANES), lambda i, bb: (i, 0, 0)),
        ],
        out_specs=[ospec, ospec],
        out_shape=[out, out],
        compiler_params=_cparams(("arbitrary", "arbitrary")),
        name="mem_kv",
    )(mem, mem_gain.reshape(1, d), w_kv, k_gain)


def _memattn_kernel(q_ref, k_ref, v_ref, z_ref, o_ref):
    scale = HEAD_DIM ** -0.5
    for h in range(MEM_HEADS):
        s = _dot_nt(q_ref[0, h], k_ref[0, h]) * scale
        e = jnp.exp(s - jnp.max(s, axis=-1, keepdims=True))
        p = e / jnp.sum(e, axis=-1, keepdims=True)
        o = _dot(p.astype(BF16), v_ref[0, h])
        o_ref[:, h * LANES:(h + 1) * LANES] = (o * _silu(z_ref[:, h * LANES:(h + 1) * LANES])).astype(BF16)


def _memattn(mq, mk, mv, proj, b, s, z_unit):
    tq = min(512, s)
    nq = s // tq
    m = mk.shape[2]
    kvspec = pl.BlockSpec((1, MEM_HEADS, m, LANES), lambda bb, i: (bb, 0, 0, 0))
    return pl.pallas_call(
        _memattn_kernel,
        grid=(b, nq),
        in_specs=[
            pl.BlockSpec((1, MEM_HEADS, tq, LANES), lambda bb, i: (bb, 0, i, 0)),
            kvspec, kvspec,
            pl.BlockSpec((tq, MEM_HEADS * LANES), lambda bb, i: (bb * nq + i, z_unit // MEM_HEADS)),
        ],
        out_specs=pl.BlockSpec((tq, MEM_HEADS * LANES), lambda bb, i: (bb * nq + i, 0)),
        out_shape=jax.ShapeDtypeStruct((b * s, MEM_HEADS * LANES), BF16),
        compiler_params=_cparams(("arbitrary", "arbitrary")),
        name="mem_attention",
    )(mq, mk, mv, proj)


def _prep_odd_kernel(p_ref, cos_ref, sin_ref, cos64_ref, sin64_ref, cg_ref, mg_ref,
                     cq_ref, ck_ref, cvt_ref, iq_ref, ik_ref, mq_ref):
    cos, sin = cos_ref[...], sin_ref[...]
    cos64, sin64 = cos64_ref[...], sin64_ref[...]
    rep = DSA_HEADS // DSA_KV_HEADS
    tq = TQ_SPARSE
    n_qb = PREP_ROWS // tq

    def unit(u):
        return p_ref[:, u * LANES:(u + 1) * LANES]

    for h in range(DSA_HEADS):
        q = _rope(_rms128(unit(O_CQ + h), cg_ref[0:1, :]), cos, sin, 64).astype(BF16)
        g, r = divmod(h, rep)
        for qb in range(n_qb):
            cq_ref[0, g, qb, r * tq:(r + 1) * tq, :] = q[qb * tq:(qb + 1) * tq, :]
    for h in range(DSA_KV_HEADS):
        ck_ref[0, h] = _rope(_rms128(unit(O_CK + h), cg_ref[1:2, :]), cos, sin, 64).astype(BF16)
        cvt_ref[0, h, 0] = _transpose_bf16(unit(O_CV + h))
    lo = lax.broadcasted_iota(jnp.int32, cos.shape, 1) < 64
    for u in range(IDX_HEADS // 2):
        x = _rope(unit(O_IQ + u), cos64, sin64, 32)
        even = jnp.where(lo, x, 0.0).astype(BF16)
        odd = jnp.where(lo, pltpu.roll(x, 64, 1), 0.0).astype(BF16)
        for qb in range(n_qb):
            iq_ref[0, qb, (2 * u) * tq:(2 * u + 1) * tq, :] = even[qb * tq:(qb + 1) * tq, :]
            iq_ref[0, qb, (2 * u + 1) * tq:(2 * u + 2) * tq, :] = odd[qb * tq:(qb + 1) * tq, :]
    ik = _rope(unit(O_IKW), cos64, sin64, 32)
    ik_ref[0] = jnp.where(lo, ik, 0.0).astype(BF16)
    for h in range(MEM_HEADS):
        mq_ref[0, h] = _rms128(unit(O_MQ + h), mg_ref[...]).astype(BF16)


def _prep_odd(proj, b, s, cos, sin, cos64, sin64, dsa_gain, mem_gain):
    ts = PREP_ROWS
    nb = s // ts
    c = ODD_PREP_UNITS * LANES
    rep = DSA_HEADS // DSA_KV_HEADS
    row = lambda bb, i: (i, 0)
    const = lambda bb, i: (0, 0)
    seq = lambda n: pl.BlockSpec((1, n, ts, LANES), lambda bb, i: (bb, 0, i, 0))
    return pl.pallas_call(
        _prep_odd_kernel,
        grid=(b, nb),
        in_specs=[
            pl.BlockSpec((ts, c), lambda bb, i: (bb * nb + i, 0)),
            pl.BlockSpec((ts, LANES), row), pl.BlockSpec((ts, LANES), row),
            pl.BlockSpec((ts, LANES), row), pl.BlockSpec((ts, LANES), row),
            pl.BlockSpec((2, LANES), const), pl.BlockSpec((1, LANES), const),
        ],
        out_specs=[
            pl.BlockSpec((1, DSA_KV_HEADS, ts // TQ_SPARSE, rep * TQ_SPARSE, LANES), lambda bb, i: (bb, 0, i, 0, 0)),
            seq(DSA_KV_HEADS),
            pl.BlockSpec((1, DSA_KV_HEADS, 1, LANES, VT_BLOCK), lambda bb, i: (bb, 0, i, 0, 0)),
            pl.BlockSpec((1, ts // TQ_SPARSE, IDX_HEADS * TQ_SPARSE, LANES), lambda bb, i: (bb, i, 0, 0)),
            pl.BlockSpec((1, ts, LANES), lambda bb, i: (bb, i, 0)),
            seq(MEM_HEADS),
        ],
        out_shape=[
            jax.ShapeDtypeStruct((b, DSA_KV_HEADS, s // TQ_SPARSE, rep * TQ_SPARSE, LANES), BF16),
            jax.ShapeDtypeStruct((b, DSA_KV_HEADS, s, LANES), BF16),
            jax.ShapeDtypeStruct((b, DSA_KV_HEADS, s // VT_BLOCK, LANES, VT_BLOCK), BF16),
            jax.ShapeDtypeStruct((b, s // TQ_SPARSE, IDX_HEADS * TQ_SPARSE, LANES), BF16),
            jax.ShapeDtypeStruct((b, s, LANES), BF16),
            jax.ShapeDtypeStruct((b, MEM_HEADS, s, LANES), BF16),
        ],
        compiler_params=_cparams(("arbitrary", "arbitrary")),
        name="prep_odd",
    )(proj, cos, sin, cos64, sin64, dsa_gain, mem_gain)


def _sortable_key(x):
    bits = pltpu.bitcast(x + 0.0, jnp.int32)
    return jnp.where(bits < 0, bits ^ jnp.int32(0x7FFFFFFF), bits)


def _dsa_kernel(q_ref, k_ref, vt_ref, iq_ref, ik_ref, w_ref, ltri_ref, z_ref, o_ref,
                key_ref, m_ref, l_ref, acc_ref, sa_ref, sb_ref, ties_ref, *, tq, tk, top_k, bounded):
    rep = DSA_HEADS // DSA_KV_HEADS
    scale = HEAD_DIM ** -0.5
    q0 = pl.program_id(1) * tq
    t_row = q0 + lax.broadcasted_iota(jnp.int32, (1, tq), 1)
    n_chunks = (q0 + tq + tk - 1) // tk
    int_min = jnp.int32(-2147483648)
    heads_per_dot = 4

    w_t = (w_ref[...] * (IDX_HEADS ** -0.5 * IDX_DIM ** -0.5)).T

    def score_chunk(c, carry):
        k0 = pl.multiple_of(c * tk, tk)
        ik = ik_ref[0, pl.ds(k0, tk), :]
        sc = jnp.zeros((tk, tq), F32)
        for h0 in range(0, IDX_HEADS, heads_per_dot):
            x = _dot_nt(ik, iq_ref[0, 0, h0 * tq:(h0 + heads_per_dot) * tq, :])
            for hh in range(heads_per_dot):
                h = h0 + hh
                sc = sc + jnp.maximum(x[:, hh * tq:(hh + 1) * tq], 0.0) * w_t[IDX_DIM + h:IDX_DIM + h + 1, :]
        causal = (k0 + lax.broadcasted_iota(jnp.int32, (tk, tq), 0)) <= t_row
        key_ref[c] = _sortable_key(jnp.where(causal, sc, MASKED))
        return carry

    lax.fori_loop(0, n_chunks, score_chunk, 0)

    def count(pred):
        def one(c):
            return _fold8(pred(key_ref[c]).astype(F32), jnp.sum, short_chains=True)

        def four(j, acc):
            return acc + ((one(4 * j) + one(4 * j + 1)) + (one(4 * j + 2) + one(4 * j + 3)))

        acc = lax.fori_loop(0, n_chunks // 4, four, jnp.zeros((SUBLANES, tq), F32))
        acc = lax.fori_loop(n_chunks - n_chunks % 4, n_chunks, lambda c, a: a + one(c), acc)
        return jnp.sum(acc, axis=0, keepdims=True)

    def search(it, state):
        thr_u, above = state
        cand_u = thr_u | jnp.left_shift(jnp.int32(1), 31 - it)
        cand = cand_u ^ int_min
        cnt = count(lambda kk: kk >= cand)
        take = cnt >= top_k
        return jnp.where(take, cand_u, thr_u), jnp.where(take, above, cnt)

    thr_u, above = lax.fori_loop(0, 32, search, (jnp.zeros((1, tq), jnp.int32), jnp.zeros((1, tq), F32)))
    thr = thr_u ^ int_min
    budget = top_k - above

    _online_init(m_ref, l_ref, acc_ref)
    ties_ref[...] = jnp.zeros(ties_ref.shape, F32)

    def qk(c, g, dst):
        dst[g] = _dot_nt(k_ref[0, g, pl.ds(pl.multiple_of(c * tk, tk), tk), :], q_ref[0, g, 0]) * (scale * LOG2E)

    def mask(c, tail):
        keys = key_ref[c]
        tie = keys == thr
        rank = _dot(ltri_ref[...], tie.astype(BF16)) + ties_ref[...]
        ok = (keys > thr) | (tie & (rank < budget))
        if tail:
            ok = ok & ((c * tk + lax.broadcasted_iota(jnp.int32, (tk, tq), 0)) <= t_row)
        ties_ref[...] = ties_ref[...] + _col_sum(tie.astype(F32))
        return jnp.concatenate([jnp.where(ok, 0.0, NEG_MASK)] * rep, axis=1)

    def soft(c, g, src, bias):
        _online_step(g, src[g] + bias, _pv_blocks(vt_ref, (0, g), c, tk), m_ref, l_ref, acc_ref, bounded)

    _flash_pairs(n_chunks, DSA_KV_HEADS, qk, mask, soft, sa_ref, sb_ref)

    for g in range(DSA_KV_HEADS):
        o_t = _online_result(g, l_ref, acc_ref)
        for r in range(rep):
            h = g * rep + r
            o = o_t[:, r * tq:(r + 1) * tq].T
            o_ref[:, h * LANES:(h + 1) * LANES] = (o * _silu(z_ref[:, h * LANES:(h + 1) * LANES])).astype(BF16)


def _dsa(cq, ck, cvt, iq, ik, ltri, proj, b, s, bounded):
    tq, tk = TQ_SPARSE, TK
    nq = s // tq
    g = DSA_KV_HEADS
    rep = DSA_HEADS // g
    w = rep * tq
    kern = functools.partial(_dsa_kernel, tq=tq, tk=tk, top_k=min(DSA_TOPK_MAX, s // 4), bounded=bounded)
    return pl.pallas_call(
        kern,
        grid=(b, nq),
        in_specs=[
            pl.BlockSpec((1, g, 1, w, LANES), lambda bb, i: (bb, 0, i, 0, 0)),
            pl.BlockSpec((1, g, s, LANES), lambda bb, i: (bb, 0, 0, 0), pipeline_mode=pl.Buffered(1)),
            pl.BlockSpec((1, g, s // VT_BLOCK, LANES, VT_BLOCK), lambda bb, i: (bb, 0, 0, 0, 0),
                         pipeline_mode=pl.Buffered(1)),
            pl.BlockSpec((1, 1, IDX_HEADS * tq, LANES), lambda bb, i: (bb, i, 0, 0)),
            pl.BlockSpec((1, s, LANES), lambda bb, i: (bb, 0, 0), pipeline_mode=pl.Buffered(1)),
            pl.BlockSpec((tq, LANES), lambda bb, i: (bb * nq + i, O_IKW)),
            pl.BlockSpec((tk, tk), lambda bb, i: (0, 0)),
            pl.BlockSpec((tq, DSA_HEADS * LANES), lambda bb, i: (bb * nq + i, O_CZ // DSA_HEADS)),
        ],
        out_specs=pl.BlockSpec((tq, DSA_HEADS * LANES), lambda bb, i: (bb * nq + i, 0)),
        out_shape=jax.ShapeDtypeStruct((b * s, DSA_HEADS * LANES), BF16),
        scratch_shapes=[
            pltpu.VMEM((s // tk, tk, tq), jnp.int32),
            pltpu.VMEM((g, 1, w), F32), pltpu.VMEM((g, SUBLANES, w), F32), pltpu.VMEM((g, LANES, w), F32),
            pltpu.VMEM((g, tk, w), F32), pltpu.VMEM((g, tk, w), F32),
            pltpu.VMEM((1, tq), F32),
        ],
        compiler_params=_cparams(("arbitrary", "arbitrary")),
        name="dsa_attention",
    )(cq, ck, cvt, iq, ik, proj, ltri, proj)


def _outproj_kernel(*refs, n_parts):
    x_ref = refs[0]
    y_refs = refs[1:1 + n_parts]
    w_refs = refs[1 + n_parts:1 + 2 * n_parts]
    o_ref = refs[1 + 2 * n_parts]
    acc = x_ref[...]
    for y_ref, w_ref in zip(y_refs, w_refs):
        acc = acc + _dot(y_ref[...], w_ref[...])
    o_ref[...] = acc


def _outproj(x2, ys, w_out):
    n, d = x2.shape
    tm = min(1024, n)
    tn = 1024
    widths = [y.shape[1] for y in ys]
    starts = np.cumsum([0] + widths[:-1]).tolist()
    ws = [w_out[st:st + wd] for st, wd in zip(starts, widths)]
    kern = functools.partial(_outproj_kernel, n_parts=len(ys))
    return pl.pallas_call(
        kern,
        grid=(d // tn, n // tm),
        in_specs=([pl.BlockSpec((tm, tn), lambda j, i: (i, j))]
                  + [pl.BlockSpec((tm, wd), lambda j, i: (i, 0)) for wd in widths]
                  + [pl.BlockSpec((wd, tn), lambda j, i: (0, j)) for wd in widths]),
        out_specs=pl.BlockSpec((tm, tn), lambda j, i: (i, j)),
        out_shape=jax.ShapeDtypeStruct((n, d), F32),
        compiler_params=_cparams(("arbitrary", "arbitrary")),
        name="out_proj",
    )(x2, *ys, *ws)


def _rope_tables(pos, half, reps):
    inv = ROPE_THETA ** (-jnp.arange(half, dtype=F32) / half)
    ang = pos.astype(F32)[:, None] * inv[None, :]
    cos, sin = jnp.cos(ang), jnp.sin(ang)
    return jnp.tile(jnp.concatenate([cos, cos], -1), (1, reps)), jnp.tile(jnp.concatenate([-sin, sin], -1), (1, reps))


def _split_cols(w, sizes):
    return jnp.split(w, np.cumsum(sizes)[:-1].tolist(), axis=-1)


def _even_weight(w):
    sizes = (1024, 1536, 24, 1024, 512, 512, 512, 512, 512, 512)
    a_q, a_kv, a_g, a_z, b_q, b_k, b_v, b_z, m_q, m_z = _split_cols(w.astype(BF16), sizes)
    d = w.shape[0]
    per_group = 3 * NSA_HEADS // NSA_KV_GROUPS
    gates = [jnp.pad(a_g[:, g * per_group:(g + 1) * per_group], ((0, 0), (0, LANES - per_group)))
             for g in range(NSA_KV_GROUPS)]
    out = jnp.concatenate([a_q, a_kv, b_q, b_k, b_v, m_q, a_z, b_z, m_z] + gates, axis=-1)
    assert out.shape == (d, EVEN_UNITS * LANES)
    return out


def _odd_weight(w):
    sizes = (1536, 512, 512, 1024, 64, 16, 1536, 512, 512)
    c_q, c_k, c_v, i_q, i_k, i_w, c_z, m_q, m_z = _split_cols(w.astype(BF16), sizes)
    d = w.shape[0]
    ikw = jnp.pad(jnp.concatenate([i_k, i_w], -1), ((0, 0), (0, LANES - IDX_DIM - IDX_HEADS)))
    pad = jnp.zeros((d, (O_CZ - ODD_PREP_UNITS) * LANES), BF16)
    out = jnp.concatenate([c_q, c_k, c_v, i_q, m_q, ikw, pad, c_z, m_z], axis=-1)
    assert out.shape == (d, ODD_UNITS * LANES)
    return out


def _overlap_matrix_t(s, n_cmp_pad):
    n_cmp = (s - NSA_CMP_LEN) // NSA_CMP_STRIDE + 1
    n_sel = s // NSA_SEL_LEN
    cmp_start = np.arange(n_cmp) * NSA_CMP_STRIDE
    sel_start = np.arange(n_sel) * NSA_SEL_LEN
    ov = np.clip(np.minimum(cmp_start[:, None] + NSA_CMP_LEN, sel_start[None, :] + NSA_SEL_LEN)
                 - np.maximum(cmp_start[:, None], sel_start[None, :]), 0, None) / NSA_CMP_LEN
    full = np.zeros((n_sel, n_cmp_pad), np.float32)
    full[:, :n_cmp] = ov.T
    return jnp.asarray(full, dtype=BF16)


def _score_bound(dim, gain_q, gain_k, scale):
    return dim * jnp.max(jnp.abs(gain_q)) * jnp.max(jnp.abs(gain_k)) * (scale * LOG2E * 1.02)


def _attend(bound, fn, *operands):
    return lax.cond(bound <= FAST_LOG2_BOUND,
                    lambda *a: fn(*a, bounded=True), lambda *a: fn(*a, bounded=False), *operands)


def kernel(x, mem, norm_gain, mem_norm_gain, mem_w_kv, mem_qk_gain, w_out, even_w_in, nsa_qk_gain, nsa_cmp_pos,
           nsa_cmp_w1, nsa_cmp_w2, diff_qk_gain, diff_lambda, diff_subln_gain, odd_w_in, dsa_qk_gain):
    b, s, d = x.shape
    assert d == D_MODEL and s % TK == 0 and s >= NSA_WINDOW + TQ_SPARSE
    pos = jnp.arange(s)
    cos, sin = _rope_tables(pos, HEAD_DIM // 2, 1)
    cos64, sin64 = _rope_tables(pos, DIFF_QK_DIM // 2, 2)
    n_cmp_pad = s // NSA_CMP_STRIDE
    cmp_last = jnp.arange(n_cmp_pad) * NSA_CMP_STRIDE + NSA_CMP_LEN - 1
    cos_c, sin_c = _rope_tables(cmp_last, HEAD_DIM // 2, 1)
    ovlt = _overlap_matrix_t(s, n_cmp_pad)
    ltri = jnp.asarray(np.tril(np.ones((TK, TK), np.float32), -1), dtype=BF16)
    wbias = _window_bias(TQ_SPARSE)

    mk_all, mv_all = _memkv(mem, mem_norm_gain, mem_w_kv.astype(BF16), mem_qk_gain[:, 1:2, :])
    w_out_b = w_out.astype(BF16)

    x2 = x.reshape(b * s, d)
    for i in range(DEPTH):
        mem_q_gain = mem_qk_gain[i, 0:1, :]
        if i % 2 == 0:
            e = i // 2
            proj = _proj(x2, norm_gain[i], _even_weight(even_w_in[e]), 768)
            dg2 = jnp.tile(diff_qk_gain[e], (1, 2))
            qa, ks, kw, vst, vwt, craw, bq, bk, bvt, mq = _prep_even(
                proj, b, s, cos, sin, cos64, sin64, nsa_qk_gain[e], dg2, mem_q_gain)
            w1 = nsa_cmp_w1[e].reshape(2, NSA_CMP_LEN, HEAD_DIM, NSA_CMP_HIDDEN).astype(BF16)
            kc, vct = _compress(craw, nsa_cmp_pos[e], w1, nsa_cmp_w2[e].astype(BF16),
                                nsa_qk_gain[e, 1:2, :], cos_c, sin_c)
            y_a = _attend(_score_bound(HEAD_DIM, nsa_qk_gain[e, 0], nsa_qk_gain[e, 1:4], HEAD_DIM ** -0.5),
                          functools.partial(_nsa, b=b, s=s), qa, ks, vst, kw, vwt, kc, vct, ovlt, wbias, proj)
            lambda_init = 0.8 - 0.6 * math.exp(-0.3 * i)
            y_b = _attend(_score_bound(DIFF_QK_DIM, diff_qk_gain[e, 0], diff_qk_gain[e, 1], DIFF_QK_DIM ** -0.5),
                          functools.partial(_diff, b=b, s=s, lambda_init=lambda_init),
                          bq, bk, bvt, diff_lambda[e], diff_subln_gain[e].reshape(1, LANES), proj)
            y_m = _memattn(mq, mk_all[i], mv_all[i], proj, b, s, E_MZ)
            ys = [y_a, y_b, y_m]
        else:
            o = i // 2
            proj = _proj(x2, norm_gain[i], _odd_weight(odd_w_in[o]), 512)
            cq, ck, cvt, iq, ik, mq = _prep_odd(proj, b, s, cos, sin, cos64, sin64, dsa_qk_gain[o], mem_q_gain)
            y_c = _attend(_score_bound(HEAD_DIM, dsa_qk_gain[o, 0], dsa_qk_gain[o, 1], HEAD_DIM ** -0.5),
                          functools.partial(_dsa, b=b, s=s), cq, ck, cvt, iq, ik, ltri, proj)
            y_m = _memattn(mq, mk_all[i], mv_all[i], proj, b, s, O_MZ)
            ys = [y_c, y_m]
        x2 = _outproj(x2, ys, w_out_b[i])
    return x2.reshape(b, s, d)
```

```python
import functools
import math

import jax
import jax.numpy as jnp
import numpy as np
from jax import lax
from jax.experimental import pallas as pl
from jax.experimental.pallas import tpu as pltpu

F32 = jnp.float32
BF16 = jnp.bfloat16

D_MODEL = 2048
DEPTH = 4
HEAD_DIM = 128
ROPE_THETA = 10000.0
EPS = 1e-6
MASKED = -1e30
FORCED = 1e9
NSA_HEADS = 8
NSA_KV_GROUPS = 2
NSA_CMP_LEN = 32
NSA_CMP_STRIDE = 16
NSA_CMP_HIDDEN = 256
NSA_SEL_LEN = 64
NSA_SEL_TOPN = 16
NSA_WINDOW = 512
DIFF_HEADS = 4
DIFF_QK_DIM = 64
DSA_HEADS = 12
DSA_KV_HEADS = 4
IDX_HEADS = 16
IDX_DIM = 64
DSA_TOPK_MAX = 256
MEM_HEADS = 4

LANES = 128
SUBLANES = 8
VMEM_LIMIT_BYTES = 56 * 1024 * 1024
LOG2E = 1.4426950408889634

PREP_ROWS = 256
TQ_SPARSE = 128
TQ_DIFF = 256
DIFF_QBLOCKS = 2
TK = 512
VT_BLOCK = 256
NEG_INIT = -1e30
NEG_MASK = -2e30
FAST_LOG2_BOUND = 40.0

EVEN_UNITS = 54
E_AQ, E_AKV, E_BQ, E_BK, E_BV, E_MQ, E_AZ, E_BZ, E_MZ, E_AG = 0, 8, 20, 24, 28, 32, 36, 44, 48, 52
EVEN_PREP_UNITS = E_AZ
ODD_UNITS = 52
O_CQ, O_CK, O_CV, O_IQ, O_MQ, O_IKW, O_CZ, O_MZ = 0, 12, 16, 20, 28, 32, 36, 48
ODD_PREP_UNITS = O_IKW + 1


def _cparams(sem):
    return pltpu.CompilerParams(dimension_semantics=sem, vmem_limit_bytes=VMEM_LIMIT_BYTES)


def _dot(a, b):
    return jnp.dot(a, b, preferred_element_type=F32)


def _dot_nt(a, b):
    return lax.dot_general(a, b, (((1,), (1,)), ((), ())), preferred_element_type=F32)


def _silu(x):
    return x * jax.nn.sigmoid(x)


def _rms_norm(x, gain):
    return x * lax.rsqrt(jnp.mean(x * x, axis=-1, keepdims=True) + EPS) * gain


def _rms_norm_halves(x, gain):
    lo = lax.broadcasted_iota(jnp.int32, x.shape, 1) < 64
    xx = x * x
    s_lo = jnp.sum(jnp.where(lo, xx, 0.0), axis=-1, keepdims=True)
    s_hi = jnp.sum(jnp.where(lo, 0.0, xx), axis=-1, keepdims=True)
    ms = jnp.where(lo, s_lo, s_hi) * (1.0 / 64.0)
    return x * lax.rsqrt(ms + EPS) * gain


def _partner(x, half):
    n = x.shape[-1]
    lane = lax.broadcasted_iota(jnp.int32, x.shape, 1)
    if 2 * half == n:
        return pltpu.roll(x, half, 1)
    a = pltpu.roll(x, half, 1)
    b = pltpu.roll(x, n - half, 1)
    src_a = pltpu.roll(lane, half, 1)
    want = jnp.where((lane & (2 * half - 1)) < half, lane + half, lane - half)
    return jnp.where(src_a == want, a, b)


def _rope(x, cos, sin_signed, half):
    return x * cos + _partner(x, half) * sin_signed


def _transpose_bf16(x):
    n = x.shape[1]
    eye = (lax.broadcasted_iota(jnp.int32, (n, n), 0) == lax.broadcasted_iota(jnp.int32, (n, n), 1)).astype(BF16)
    return _dot_nt(eye, x.astype(BF16)).astype(BF16)


def _fold8(x, op, short_chains=False):
    rows, w = x.shape
    if short_chains:
        x = op(x.reshape(SUBLANES, rows // SUBLANES, w), axis=0)
        rows = rows // SUBLANES
    return op(x.reshape(rows // SUBLANES, SUBLANES, w), axis=0)


def _col_max(x):
    return jnp.max(_fold8(x, jnp.max), axis=0, keepdims=True)


def _col_sum(x):
    return jnp.sum(_fold8(x, jnp.sum), axis=0, keepdims=True)


def _proj_kernel(x_ref, g_ref, w_ref, o_ref, hn_ref):
    @pl.when(pl.program_id(1) == 0)
    def _():
        x = x_ref[...]
        hn_ref[...] = _rms_norm(x, g_ref[...]).astype(BF16)

    o_ref[...] = _dot(hn_ref[...], w_ref[...])


def _proj(x2, gain, w, tn):
    n, d = x2.shape
    c = w.shape[1]
    tm = min(1024, n)
    return pl.pallas_call(
        _proj_kernel,
        grid=(n // tm, c // tn),
        in_specs=[
            pl.BlockSpec((tm, d), lambda i, j: (i, 0)),
            pl.BlockSpec((1, d), lambda i, j: (0, 0)),
            pl.BlockSpec((d, tn), lambda i, j: (0, j)),
        ],
        out_specs=pl.BlockSpec((tm, tn), lambda i, j: (i, j)),
        out_shape=jax.ShapeDtypeStruct((n, c), F32),
        scratch_shapes=[pltpu.VMEM((tm, d), BF16)],
        compiler_params=_cparams(("arbitrary", "arbitrary")),
        name="proj",
    )(x2, gain.reshape(1, d), w)


def _prep_even_kernel(p_ref, cos_ref, sin_ref, cos64_ref, sin64_ref, ng_ref, dg_ref, mg_ref,
                      qa_ref, ks_ref, kw_ref, vst_ref, vwt_ref, craw_ref, bq_ref, bk_ref, bvt_ref, mq_ref):
    cos, sin = cos_ref[...], sin_ref[...]
    cos64, sin64 = cos64_ref[...], sin64_ref[...]
    rep = NSA_HEADS // NSA_KV_GROUPS
    tq = TQ_SPARSE

    def unit(u):
        return p_ref[:, u * LANES:(u + 1) * LANES]

    for h in range(NSA_HEADS):
        q = _rope(_rms_norm(unit(E_AQ + h), ng_ref[0:1, :]), cos, sin, 64).astype(BF16)
        g, r = divmod(h, rep)
        for qb in range(PREP_ROWS // tq):
            qa_ref[0, g, qb, r * tq:(r + 1) * tq, :] = q[qb * tq:(qb + 1) * tq, :]
    for g in range(NSA_KV_GROUPS):
        craw_ref[0, 0, g] = unit(E_AKV + 0 + g)
        craw_ref[0, 1, g] = unit(E_AKV + 2 + g)
        ks_ref[0, g] = _rope(_rms_norm(unit(E_AKV + 4 + g), ng_ref[2:3, :]), cos, sin, 64).astype(BF16)
        vst_ref[0, g, 0] = _transpose_bf16(unit(E_AKV + 6 + g))
        kw_ref[0, g] = _rope(_rms_norm(unit(E_AKV + 8 + g), ng_ref[3:4, :]), cos, sin, 64).astype(BF16)
        vw = unit(E_AKV + 10 + g)
        for kb in range(PREP_ROWS // LANES):
            vwt_ref[0, g, kb] = _transpose_bf16(vw[kb * LANES:(kb + 1) * LANES, :])
    lo = lax.broadcasted_iota(jnp.int32, cos.shape, 1) < 64
    for h in range(DIFF_HEADS):
        q = _rope(_rms_norm_halves(unit(E_BQ + h), dg_ref[0:1, :]), cos64, sin64, 32)
        bq_ref[0, h, 0, 0:PREP_ROWS, :] = jnp.where(lo, q, 0.0).astype(BF16)
        bq_ref[0, h, 0, PREP_ROWS:2 * PREP_ROWS, :] = jnp.where(lo, 0.0, q).astype(BF16)
        bk_ref[0, h] = _rope(_rms_norm_halves(unit(E_BK + h), dg_ref[1:2, :]), cos64, sin64, 32).astype(BF16)
        bvt_ref[0, h, 0] = _transpose_bf16(unit(E_BV + h))
    for h in range(MEM_HEADS):
        mq_ref[0, h] = _rms_norm(unit(E_MQ + h), mg_ref[...]).astype(BF16)


def _prep_even(proj, b, s, cos, sin, cos64, sin64, nsa_gain, diff_gain2, mem_gain):
    ts = PREP_ROWS
    assert ts == TQ_DIFF == VT_BLOCK
    nb = s // ts
    c = EVEN_PREP_UNITS * LANES
    g = NSA_KV_GROUPS
    rep = NSA_HEADS // g
    row = lambda bb, i: (i, 0)
    const = lambda bb, i: (0, 0)
    seq = lambda n: pl.BlockSpec((1, n, ts, LANES), lambda bb, i: (bb, 0, i, 0))
    blk5 = lambda n, k, r, cdim: pl.BlockSpec((1, n, k, r, cdim), lambda bb, i: (bb, 0, i, 0, 0))
    return pl.pallas_call(
        _prep_even_kernel,
        grid=(b, nb),
        in_specs=[
            pl.BlockSpec((ts, c), lambda bb, i: (bb * nb + i, 0)),
            pl.BlockSpec((ts, LANES), row), pl.BlockSpec((ts, LANES), row),
            pl.BlockSpec((ts, LANES), row), pl.BlockSpec((ts, LANES), row),
            pl.BlockSpec((4, LANES), const), pl.BlockSpec((2, LANES), const), pl.BlockSpec((1, LANES), const),
        ],
        out_specs=[
            blk5(g, ts // TQ_SPARSE, rep * TQ_SPARSE, LANES),
            seq(g), seq(g),
            blk5(g, 1, LANES, VT_BLOCK),
            blk5(g, ts // LANES, LANES, LANES),
            pl.BlockSpec((1, 2, g, ts, LANES), lambda bb, i: (bb, 0, 0, i, 0)),
            blk5(DIFF_HEADS, 1, 2 * TQ_DIFF, LANES),
            seq(DIFF_HEADS),
            blk5(DIFF_HEADS, 1, LANES, VT_BLOCK),
            seq(MEM_HEADS),
        ],
        out_shape=[
            jax.ShapeDtypeStruct((b, g, s // TQ_SPARSE, rep * TQ_SPARSE, LANES), BF16),
            jax.ShapeDtypeStruct((b, g, s, LANES), BF16),
            jax.ShapeDtypeStruct((b, g, s, LANES), BF16),
            jax.ShapeDtypeStruct((b, g, s // VT_BLOCK, LANES, VT_BLOCK), BF16),
            jax.ShapeDtypeStruct((b, g, s // LANES, LANES, LANES), BF16),
            jax.ShapeDtypeStruct((b, 2, g, s, LANES), F32),
            jax.ShapeDtypeStruct((b, DIFF_HEADS, s // TQ_DIFF, 2 * TQ_DIFF, LANES), BF16),
            jax.ShapeDtypeStruct((b, DIFF_HEADS, s, LANES), BF16),
            jax.ShapeDtypeStruct((b, DIFF_HEADS, s // VT_BLOCK, LANES, VT_BLOCK), BF16),
            jax.ShapeDtypeStruct((b, MEM_HEADS, s, LANES), BF16),
        ],
        compiler_params=_cparams(("arbitrary", "arbitrary")),
        name="prep_even",
    )(proj, cos, sin, cos64, sin64, nsa_gain, diff_gain2, mem_gain)


def _compress_kernel(x_ref, pe_ref, w1_ref, w2_ref, g_ref, cos_ref, sin_ref, kc_ref, vct_ref, pad_ref, *, s, n_pad):
    for kind in range(2):
        pad_ref[0:s, :] = x_ref[0, kind, 0]
        pad_ref[s:s + NSA_CMP_LEN, :] = jnp.zeros((NSA_CMP_LEN, LANES), F32)
        acc = jnp.zeros((n_pad, NSA_CMP_HIDDEN), F32)
        for l in range(NSA_CMP_LEN):
            rows = pad_ref[pl.ds(l, n_pad, stride=NSA_CMP_STRIDE), :] + pe_ref[kind, l:l + 1, :]
            acc = acc + _dot(rows.astype(BF16), w1_ref[kind, l])
        out = _dot(_silu(acc).astype(BF16), w2_ref[kind])
        if kind == 0:
            kc_ref[0, 0] = _rope(_rms_norm(out, g_ref[...]), cos_ref[...], sin_ref[...], 64).astype(BF16)
        else:
            vct_ref[0, 0] = out.T.astype(BF16)


def _compress(craw, pe, w1, w2, gain, cos_c, sin_c):
    b, _, g, s, _ = craw.shape
    n_pad = s // NSA_CMP_STRIDE
    kern = functools.partial(_compress_kernel, s=s, n_pad=n_pad)
    whole = lambda shape: pl.BlockSpec(shape, lambda bb, gg: (0,) * len(shape))
    return pl.pallas_call(
        kern,
        grid=(b, g),
        in_specs=[
            pl.BlockSpec((1, 2, 1, s, LANES), lambda bb, gg: (bb, 0, gg, 0, 0)),
            whole((2, NSA_CMP_LEN, LANES)),
            whole((2, NSA_CMP_LEN, LANES, NSA_CMP_HIDDEN)),
            whole((2, NSA_CMP_HIDDEN, LANES)),
            whole((1, LANES)), whole((n_pad, LANES)), whole((n_pad, LANES)),
        ],
        out_specs=[
            pl.BlockSpec((1, 1, n_pad, LANES), lambda bb, gg: (bb, gg, 0, 0)),
            pl.BlockSpec((1, 1, LANES, n_pad), lambda bb, gg: (bb, gg, 0, 0)),
        ],
        out_shape=[
            jax.ShapeDtypeStruct((b, g, n_pad, LANES), BF16),
            jax.ShapeDtypeStruct((b, g, LANES, n_pad), BF16),
        ],
        scratch_shapes=[pltpu.VMEM((s + NSA_CMP_LEN, LANES), F32)],
        compiler_params=_cparams(("arbitrary", "arbitrary")),
        name="nsa_compress",
    )(craw, pe, w1, w2, gain, cos_c, sin_c)


def _softmax_parts_t(st, bounded):
    if not bounded:
        st = st - _col_max(st)
    e = jnp.exp2(st)
    return e.astype(BF16), 1.0 / jnp.maximum(_col_sum(e), 1e-30)


def _online_init(m_ref, l_ref, acc_ref):
    m_ref[...] = jnp.full(m_ref.shape, NEG_INIT, F32)
    l_ref[...] = jnp.zeros(l_ref.shape, F32)
    acc_ref[...] = jnp.zeros(acc_ref.shape, F32)


def _online_step(idx, st, pv, m_ref, l_ref, acc_ref, bounded):
    if bounded:
        p = jnp.exp2(st)
        l_ref[idx] = l_ref[idx] + _fold8(p, jnp.sum)
        acc_ref[idx] = acc_ref[idx] + pv(p.astype(BF16))
        return
    m_old = m_ref[idx]
    m_new = jnp.maximum(m_old, _col_max(st))
    alpha = jnp.exp2(m_old - m_new)
    p = jnp.exp2(st - m_new)
    l_ref[idx] = alpha * l_ref[idx] + _fold8(p, jnp.sum)
    acc_ref[idx] = alpha * acc_ref[idx] + pv(p.astype(BF16))
    m_ref[idx] = m_new


def _online_result(idx, l_ref, acc_ref):
    return acc_ref[idx] / jnp.maximum(jnp.sum(l_ref[idx], axis=0, keepdims=True), 1e-30)


def _flash_pairs(n_chunks, n_groups, qk, mask, soft, sa_ref, sb_ref):
    for g in range(n_groups):
        qk(0, g, sa_ref)
    n_pairs = (n_chunks - 1) // 2

    def pair(j, carry):
        a = 2 * j
        bias_a = mask(a, False)
        bias_b = mask(a + 1, False)
        for g in range(n_groups):
            qk(a + 1, g, sb_ref)
            soft(a, g, sa_ref, bias_a)
            qk(a + 2, g, sa_ref)
            soft(a + 1, g, sb_ref, bias_b)
        return carry

    lax.fori_loop(0, n_pairs, pair, 0)
    e = 2 * n_pairs

    @pl.when(e + 1 < n_chunks)
    def _():
        bias_a = mask(e, True)
        bias_b = mask(e + 1, True)
        for g in range(n_groups):
            qk(e + 1, g, sb_ref)
            soft(e, g, sa_ref, bias_a)
            soft(e + 1, g, sb_ref, bias_b)

    @pl.when(e + 1 >= n_chunks)
    def _():
        bias_a = mask(e, True)
        for g in range(n_groups):
            soft(e, g, sa_ref, bias_a)


def _pv_blocks(vt_ref, lead, chunk, tk):
    per = tk // VT_BLOCK

    def pv(p):
        out = _dot(vt_ref[lead + (chunk * per,)], p[0:VT_BLOCK])
        for i in range(1, per):
            out = out + _dot(vt_ref[lead + (chunk * per + i,)], p[i * VT_BLOCK:(i + 1) * VT_BLOCK])
        return out

    return pv


def _top_n_mask_t(scores, n):
    row = lax.broadcasted_iota(jnp.int32, scores.shape, 0).astype(F32)
    height = float(scores.shape[0])
    work = scores
    sel = jnp.zeros(scores.shape, F32)
    for _ in range(n):
        m = jnp.max(work, axis=0, keepdims=True)
        first = jnp.min(jnp.where(work == m, row, height), axis=0, keepdims=True)
        pick = row == first
        sel = jnp.where(pick, 1.0, sel)
        work = jnp.where(pick, -jnp.inf, work)
    return sel


def _nsa_kernel(q_ref, kc_ref, vct_ref, ks_ref, vst_ref, kw_ref, vwt_ref, ovlt_ref, wbias_ref, gl_ref, z_ref, o_ref,
                m_ref, l_ref, acc_ref, sa_ref, sb_ref, selt_ref, oct_ref, owt_ref, *, tq, tk, top_n, bounded):
    rep = NSA_HEADS // NSA_KV_GROUPS
    w = rep * tq
    scale = HEAD_DIM ** -0.5
    q0 = pl.program_id(2) * tq
    t_row = q0 + (lax.broadcasted_iota(jnp.int32, (1, w), 1) & (tq - 1))
    t_one = t_row[:, 0:tq]
    q = q_ref[0, 0, 0]

    n_cmp_pad = kc_ref.shape[2]
    cmp_last = lax.broadcasted_iota(jnp.int32, (n_cmp_pad, w), 0) * NSA_CMP_STRIDE + (NSA_CMP_LEN - 1)
    st = jnp.where(cmp_last <= t_row, _dot_nt(kc_ref[0, 0], q) * (scale * LOG2E), NEG_MASK)
    p, inv = _softmax_parts_t(st, bounded)
    inv = jnp.where(t_row >= NSA_CMP_LEN - 1, inv, 0.0)
    oct_ref[...] = _dot(vct_ref[0, 0], p) * inv
    imp_heads = _dot(ovlt_ref[...], p) * inv
    imp = imp_heads[:, 0:tq]
    for r in range(1, rep):
        imp = imp + imp_heads[:, r * tq:(r + 1) * tq]

    n_sel = imp.shape[0]
    sel_shift = NSA_SEL_LEN.bit_length() - 1
    j = lax.broadcasted_iota(jnp.int32, (n_sel, tq), 0)
    cur = jnp.right_shift(t_one, sel_shift)
    visible = j <= cur
    forced = (j == 0) | (j >= cur - 1)
    imp = jnp.where(visible, jnp.where(forced, FORCED, imp), MASKED)
    selt_ref[...] = _top_n_mask_t(imp, top_n)

    span = NSA_WINDOW + tq
    start = pl.multiple_of(jnp.maximum(q0 - NSA_WINDOW, 0), tq)
    st = (_dot_nt(kw_ref[0, 0, pl.ds(start, span), :], q) * (scale * LOG2E)
          + jnp.concatenate([wbias_ref[0]] * rep, axis=1))
    p, inv = _softmax_parts_t(st, bounded)
    blk0 = start // LANES
    ow = _dot(vwt_ref[0, 0, blk0], p[0:LANES])
    for i in range(1, span // LANES):
        ow = ow + _dot(vwt_ref[0, 0, blk0 + i], p[i * LANES:(i + 1) * LANES])
    owt_ref[...] = ow * inv

    _online_init(m_ref, l_ref, acc_ref)
    n_chunks = (q0 + tq + tk - 1) // tk
    blocks_per_chunk = tk // NSA_SEL_LEN

    def qk(c, g, dst):
        dst[g] = _dot_nt(ks_ref[0, 0, pl.ds(pl.multiple_of(c * tk, tk), tk), :], q) * (scale * LOG2E)

    def mask(c, tail):
        rows = selt_ref[pl.ds(pl.multiple_of(c * blocks_per_chunk, blocks_per_chunk), blocks_per_chunk), :]
        ok = jnp.concatenate(
            [jnp.broadcast_to(rows[i:i + 1, :], (NSA_SEL_LEN, tq)) for i in range(blocks_per_chunk)], axis=0) > 0.5
        if tail:
            ok = ok & ((c * tk + lax.broadcasted_iota(jnp.int32, (tk, tq), 0)) <= t_one)
        return jnp.concatenate([jnp.where(ok, 0.0, NEG_MASK)] * rep, axis=1)

    def soft(c, g, src, bias):
        _online_step(g, src[g] + bias, _pv_blocks(vst_ref, (0, 0), c, tk), m_ref, l_ref, acc_ref, bounded)

    _flash_pairs(n_chunks, 1, qk, mask, soft, sa_ref, sb_ref)

    gates_t = jax.nn.sigmoid(gl_ref[...]).T
    os_t = _online_result(0, l_ref, acc_ref)
    for r in range(rep):
        sl = slice(r * tq, (r + 1) * tq)
        out_t = (gates_t[3 * r:3 * r + 1, :] * oct_ref[:, sl] + gates_t[3 * r + 1:3 * r + 2, :] * os_t[:, sl]
                 + gates_t[3 * r + 2:3 * r + 3, :] * owt_ref[:, sl])
        o_ref[:, r * LANES:(r + 1) * LANES] = (out_t.T * _silu(z_ref[:, r * LANES:(r + 1) * LANES])).astype(BF16)


def _window_bias(tq):
    span = NSA_WINDOW + tq
    n_early = NSA_WINDOW // tq
    row = np.arange(span)[:, None]
    lane = np.arange(tq)[None, :]
    tables = []
    for i in range(n_early + 1):
        t = i * tq + lane
        s_pos = (0 if i < n_early else t[0, 0] - NSA_WINDOW) + row
        ok = (s_pos <= t) & (s_pos > t - NSA_WINDOW)
        tables.append(np.where(ok, 0.0, NEG_MASK))
    return jnp.asarray(np.stack(tables), dtype=F32)


def _nsa(qa, ks, vst, kw, vwt, kc, vct, ovlt, wbias, proj, b, s, bounded):
    tq, tk = TQ_SPARSE, TK
    nq = s // tq
    g = NSA_KV_GROUPS
    rep = NSA_HEADS // g
    w = rep * tq
    n_cmp_pad = kc.shape[2]
    n_sel = s // NSA_SEL_LEN
    kern = functools.partial(_nsa_kernel, tq=tq, tk=tk, top_n=min(NSA_SEL_TOPN, n_sel), bounded=bounded)
    per_group = lambda shape: pl.BlockSpec((1, 1) + shape, lambda bb, gg, i: (bb, gg) + (0,) * len(shape))
    return pl.pallas_call(
        kern,
        grid=(b, g, nq),
        in_specs=[
            pl.BlockSpec((1, 1, 1, w, LANES), lambda bb, gg, i: (bb, gg, i, 0, 0)),
            per_group((n_cmp_pad, LANES)), per_group((LANES, n_cmp_pad)),
            per_group((s, LANES)), per_group((s // VT_BLOCK, LANES, VT_BLOCK)),
            per_group((s, LANES)), per_group((s // LANES, LANES, LANES)),
            pl.BlockSpec((n_sel, n_cmp_pad), lambda bb, gg, i: (0, 0)),
            pl.BlockSpec((1,) + wbias.shape[1:], lambda bb, gg, i: (jnp.minimum(i, wbias.shape[0] - 1), 0, 0)),
            pl.BlockSpec((tq, LANES), lambda bb, gg, i: (bb * nq + i, E_AG + gg)),
            pl.BlockSpec((tq, rep * LANES), lambda bb, gg, i: (bb * nq + i, E_AZ // rep + gg)),
        ],
        out_specs=pl.BlockSpec((tq, rep * LANES), lambda bb, gg, i: (bb * nq + i, gg)),
        out_shape=jax.ShapeDtypeStruct((b * s, NSA_HEADS * LANES), BF16),
        scratch_shapes=[
            pltpu.VMEM((1, 1, w), F32), pltpu.VMEM((1, SUBLANES, w), F32), pltpu.VMEM((1, LANES, w), F32),
            pltpu.VMEM((1, tk, w), F32), pltpu.VMEM((1, tk, w), F32),
            pltpu.VMEM((n_sel, tq), F32), pltpu.VMEM((LANES, w), F32), pltpu.VMEM((LANES, w), F32),
        ],
        compiler_params=_cparams(("arbitrary", "arbitrary", "arbitrary")),
        name="nsa_attention",
    )(qa, kc, vct, ks, vst, kw, vwt, ovlt, wbias, proj, proj)


def _diff_kernel(q_ref, k_ref, vt_ref, lam_ref, sg_ref, z_ref, o_ref, m_ref, l_ref, acc_ref, sa_ref, sb_ref,
                 *, tq, tk, n_qb, lambda_init, bounded):
    w = 2 * tq
    q0 = pl.program_id(2) * (n_qb * tq)
    lane_t = lax.broadcasted_iota(jnp.int32, (1, w), 1) & (tq - 1)
    _online_init(m_ref, l_ref, acc_ref)
    n_chunks = (q0 + n_qb * tq + tk - 1) // tk

    def qk(c, g, dst):
        dst[g] = (_dot_nt(k_ref[0, 0, pl.ds(pl.multiple_of(c * tk, tk), tk), :], q_ref[0, 0, g])
                  * (DIFF_QK_DIM ** -0.5 * LOG2E))

    def mask(c, tail):
        if not tail:
            return None
        key = c * tk + lax.broadcasted_iota(jnp.int32, (tk, w), 0)
        return [jnp.where(key <= q0 + g * tq + lane_t, 0.0, NEG_MASK) for g in range(n_qb)]

    def soft(c, g, src, bias):
        st = src[g] if bias is None else src[g] + bias[g]
        _online_step(g, st, _pv_blocks(vt_ref, (0, 0), c, tk), m_ref, l_ref, acc_ref, bounded)

    _flash_pairs(n_chunks, n_qb, qk, mask, soft, sa_ref, sb_ref)

    lv = lam_ref[...]
    lam = (jnp.exp(jnp.sum(lv[0:1] * lv[1:2], axis=-1, keepdims=True))
           - jnp.exp(jnp.sum(lv[2:3] * lv[3:4], axis=-1, keepdims=True)) + lambda_init)
    for g in range(n_qb):
        rows = slice(g * tq, (g + 1) * tq)
        o_t = _online_result(g, l_ref, acc_ref)
        o = (o_t[:, 0:tq] - lam * o_t[:, tq:w]).T
        o = _rms_norm(o, sg_ref[...]) * (1.0 - lambda_init)
        o_ref[rows, :] = (o * _silu(z_ref[rows, :])).astype(BF16)


def _diff(bq, bk, bvt, lam_vecs, subln_gain, proj, b, s, lambda_init, bounded):
    tq, tk, n_qb = TQ_DIFF, TK, DIFF_QBLOCKS
    assert n_qb * tq == tk
    rows = n_qb * tq
    nq = s // rows
    w = 2 * tq
    kern = functools.partial(_diff_kernel, tq=tq, tk=tk, n_qb=n_qb, lambda_init=lambda_init, bounded=bounded)
    return pl.pallas_call(
        kern,
        grid=(b, DIFF_HEADS, nq),
        in_specs=[
            pl.BlockSpec((1, 1, n_qb, w, LANES), lambda bb, h, i: (bb, h, i, 0, 0)),
            pl.BlockSpec((1, 1, s, LANES), lambda bb, h, i: (bb, h, 0, 0)),
            pl.BlockSpec((1, 1, s // VT_BLOCK, LANES, VT_BLOCK), lambda bb, h, i: (bb, h, 0, 0, 0)),
            pl.BlockSpec((4, DIFF_QK_DIM), lambda bb, h, i: (0, 0)),
            pl.BlockSpec((1, LANES), lambda bb, h, i: (0, 0)),
            pl.BlockSpec((rows, LANES), lambda bb, h, i: (bb * nq + i, E_BZ + h)),
        ],
        out_specs=pl.BlockSpec((rows, LANES), lambda bb, h, i: (bb * nq + i, h)),
        out_shape=jax.ShapeDtypeStruct((b * s, DIFF_HEADS * LANES), BF16),
        scratch_shapes=[
            pltpu.VMEM((n_qb, 1, w), F32), pltpu.VMEM((n_qb, SUBLANES, w), F32), pltpu.VMEM((n_qb, LANES, w), F32),
            pltpu.VMEM((n_qb, tk, w), F32), pltpu.VMEM((n_qb, tk, w), F32),
        ],
        compiler_params=_cparams(("arbitrary", "arbitrary", "arbitrary")),
        name="diff_attention",
    )(bq, bk, bvt, lam_vecs, subln_gain, proj)


def _memkv_kernel(mem_ref, mg_ref, w_ref, kg_ref, k_ref, v_ref):
    mem_n = _rms_norm(mem_ref[0], mg_ref[...]).astype(BF16)
    kv = _dot(mem_n, w_ref[0])
    for h in range(MEM_HEADS):
        k_ref[0, 0, h] = _rms_norm(kv[:, h * LANES:(h + 1) * LANES], kg_ref[0]).astype(BF16)
        v_ref[0, 0, h] = kv[:, (MEM_HEADS + h) * LANES:(MEM_HEADS + h + 1) * LANES].astype(BF16)


def _memkv(mem, mem_gain, w_kv, k_gain):
    b, m, d = mem.shape
    depth = w_kv.shape[0]
    c = w_kv.shape[2]
    out = jax.ShapeDtypeStruct((depth, b, MEM_HEADS, m, LANES), BF16)
    ospec = pl.BlockSpec((1, 1, MEM_HEADS, m, LANES), lambda i, bb: (i, bb, 0, 0, 0))
    return pl.pallas_call(
        _memkv_kernel,
        grid=(depth, b),
        in_specs=[
            pl.BlockSpec((1, m, d), lambda i, bb: (bb, 0, 0)),
            pl.BlockSpec((1, d), lambda i, bb: (0, 0)),
            pl.BlockSpec((1, d, c), lambda i, bb: (i, 0, 0)),
            pl.BlockSpec((1, 1, LANES), lambda i, bb: (i, 0, 0)),
        ],
        out_specs=[ospec, ospec],
        out_shape=[out, out],
        compiler_params=_cparams(("arbitrary", "arbitrary")),
        name="mem_kv",
    )(mem, mem_gain.reshape(1, d), w_kv, k_gain)


def _memattn_kernel(q_ref, k_ref, v_ref, z_ref, o_ref):
    scale = HEAD_DIM ** -0.5
    for h in range(MEM_HEADS):
        s = _dot_nt(q_ref[0, h], k_ref[0, h]) * scale
        e = jnp.exp(s - jnp.max(s, axis=-1, keepdims=True))
        p = e / jnp.sum(e, axis=-1, keepdims=True)
        o = _dot(p.astype(BF16), v_ref[0, h])
        o_ref[:, h * LANES:(h + 1) * LANES] = (o * _silu(z_ref[:, h * LANES:(h + 1) * LANES])).astype(BF16)


def _memattn(mq, mk, mv, proj, b, s, z_unit):
    tq = min(512, s)
    nq = s // tq
    m = mk.shape[2]
    kvspec = pl.BlockSpec((1, MEM_HEADS, m, LANES), lambda bb, i: (bb, 0, 0, 0))
    return pl.pallas_call(
        _memattn_kernel,
        grid=(b, nq),
        in_specs=[
            pl.BlockSpec((1, MEM_HEADS, tq, LANES), lambda bb, i: (bb, 0, i, 0)),
            kvspec, kvspec,
            pl.BlockSpec((tq, MEM_HEADS * LANES), lambda bb, i: (bb * nq + i, z_unit // MEM_HEADS)),
        ],
        out_specs=pl.BlockSpec((tq, MEM_HEADS * LANES), lambda bb, i: (bb * nq + i, 0)),
        out_shape=jax.ShapeDtypeStruct((b * s, MEM_HEADS * LANES), BF16),
        compiler_params=_cparams(("arbitrary", "arbitrary")),
        name="mem_attention",
    )(mq, mk, mv, proj)


def _prep_odd_kernel(p_ref, cos_ref, sin_ref, cos64_ref, sin64_ref, cg_ref, mg_ref,
                     cq_ref, ck_ref, cvt_ref, iq_ref, ik_ref, mq_ref):
    cos, sin = cos_ref[...], sin_ref[...]
    cos64, sin64 = cos64_ref[...], sin64_ref[...]
    rep = DSA_HEADS // DSA_KV_HEADS
    tq = TQ_SPARSE
    n_qb = PREP_ROWS // tq

    def unit(u):
        return p_ref[:, u * LANES:(u + 1) * LANES]

    for h in range(DSA_HEADS):
        q = _rope(_rms_norm(unit(O_CQ + h), cg_ref[0:1, :]), cos, sin, 64).astype(BF16)
        g, r = divmod(h, rep)
        for qb in range(n_qb):
            cq_ref[0, g, qb, r * tq:(r + 1) * tq, :] = q[qb * tq:(qb + 1) * tq, :]
    for h in range(DSA_KV_HEADS):
        ck_ref[0, h] = _rope(_rms_norm(unit(O_CK + h), cg_ref[1:2, :]), cos, sin, 64).astype(BF16)
        cvt_ref[0, h, 0] = _transpose_bf16(unit(O_CV + h))
    lo = lax.broadcasted_iota(jnp.int32, cos.shape, 1) < 64
    for u in range(IDX_HEADS // 2):
        x = _rope(unit(O_IQ + u), cos64, sin64, 32)
        even = jnp.where(lo, x, 0.0).astype(BF16)
        odd = jnp.where(lo, pltpu.roll(x, 64, 1), 0.0).astype(BF16)
        for qb in range(n_qb):
            iq_ref[0, qb, (2 * u) * tq:(2 * u + 1) * tq, :] = even[qb * tq:(qb + 1) * tq, :]
            iq_ref[0, qb, (2 * u + 1) * tq:(2 * u + 2) * tq, :] = odd[qb * tq:(qb + 1) * tq, :]
    ik = _rope(unit(O_IKW), cos64, sin64, 32)
    ik_ref[0] = jnp.where(lo, ik, 0.0).astype(BF16)
    for h in range(MEM_HEADS):
        mq_ref[0, h] = _rms_norm(unit(O_MQ + h), mg_ref[...]).astype(BF16)


def _prep_odd(proj, b, s, cos, sin, cos64, sin64, dsa_gain, mem_gain):
    ts = PREP_ROWS
    nb = s // ts
    c = ODD_PREP_UNITS * LANES
    rep = DSA_HEADS // DSA_KV_HEADS
    row = lambda bb, i: (i, 0)
    const = lambda bb, i: (0, 0)
    seq = lambda n: pl.BlockSpec((1, n, ts, LANES), lambda bb, i: (bb, 0, i, 0))
    return pl.pallas_call(
        _prep_odd_kernel,
        grid=(b, nb),
        in_specs=[
            pl.BlockSpec((ts, c), lambda bb, i: (bb * nb + i, 0)),
            pl.BlockSpec((ts, LANES), row), pl.BlockSpec((ts, LANES), row),
            pl.BlockSpec((ts, LANES), row), pl.BlockSpec((ts, LANES), row),
            pl.BlockSpec((2, LANES), const), pl.BlockSpec((1, LANES), const),
        ],
        out_specs=[
            pl.BlockSpec((1, DSA_KV_HEADS, ts // TQ_SPARSE, rep * TQ_SPARSE, LANES), lambda bb, i: (bb, 0, i, 0, 0)),
            seq(DSA_KV_HEADS),
            pl.BlockSpec((1, DSA_KV_HEADS, 1, LANES, VT_BLOCK), lambda bb, i: (bb, 0, i, 0, 0)),
            pl.BlockSpec((1, ts // TQ_SPARSE, IDX_HEADS * TQ_SPARSE, LANES), lambda bb, i: (bb, i, 0, 0)),
            pl.BlockSpec((1, ts, LANES), lambda bb, i: (bb, i, 0)),
            seq(MEM_HEADS),
        ],
        out_shape=[
            jax.ShapeDtypeStruct((b, DSA_KV_HEADS, s // TQ_SPARSE, rep * TQ_SPARSE, LANES), BF16),
            jax.ShapeDtypeStruct((b, DSA_KV_HEADS, s, LANES), BF16),
            jax.ShapeDtypeStruct((b, DSA_KV_HEADS, s // VT_BLOCK, LANES, VT_BLOCK), BF16),
            jax.ShapeDtypeStruct((b, s // TQ_SPARSE, IDX_HEADS * TQ_SPARSE, LANES), BF16),
            jax.ShapeDtypeStruct((b, s, LANES), BF16),
            jax.ShapeDtypeStruct((b, MEM_HEADS, s, LANES), BF16),
        ],
        compiler_params=_cparams(("arbitrary", "arbitrary")),
        name="prep_odd",
    )(proj, cos, sin, cos64, sin64, dsa_gain, mem_gain)


def _sortable_key(x):
    bits = pltpu.bitcast(x + 0.0, jnp.int32)
    return jnp.where(bits < 0, bits ^ jnp.int32(0x7FFFFFFF), bits)


def _dsa_kernel(q_ref, k_ref, vt_ref, iq_ref, ik_ref, w_ref, ltri_ref, z_ref, o_ref,
                key_ref, m_ref, l_ref, acc_ref, sa_ref, sb_ref, ties_ref, *, tq, tk, top_k, bounded):
    rep = DSA_HEADS // DSA_KV_HEADS
    scale = HEAD_DIM ** -0.5
    q0 = pl.program_id(1) * tq
    t_row = q0 + lax.broadcasted_iota(jnp.int32, (1, tq), 1)
    n_chunks = (q0 + tq + tk - 1) // tk
    int_min = jnp.int32(-2147483648)
    heads_per_dot = 4

    w_t = (w_ref[...] * (IDX_HEADS ** -0.5 * IDX_DIM ** -0.5)).T

    def score_chunk(c, carry):
        k0 = pl.multiple_of(c * tk, tk)
        ik = ik_ref[0, pl.ds(k0, tk), :]
        sc = jnp.zeros((tk, tq), F32)
        for h0 in range(0, IDX_HEADS, heads_per_dot):
            x = _dot_nt(ik, iq_ref[0, 0, h0 * tq:(h0 + heads_per_dot) * tq, :])
            for hh in range(heads_per_dot):
                h = h0 + hh
                sc = sc + jnp.maximum(x[:, hh * tq:(hh + 1) * tq], 0.0) * w_t[IDX_DIM + h:IDX_DIM + h + 1, :]
        causal = (k0 + lax.broadcasted_iota(jnp.int32, (tk, tq), 0)) <= t_row
        key_ref[c] = _sortable_key(jnp.where(causal, sc, MASKED))
        return carry

    lax.fori_loop(0, n_chunks, score_chunk, 0)

    def count(pred):
        def one(c):
            return _fold8(pred(key_ref[c]).astype(F32), jnp.sum, short_chains=True)

        def four(j, acc):
            return acc + ((one(4 * j) + one(4 * j + 1)) + (one(4 * j + 2) + one(4 * j + 3)))

        acc = lax.fori_loop(0, n_chunks // 4, four, jnp.zeros((SUBLANES, tq), F32))
        acc = lax.fori_loop(n_chunks - n_chunks % 4, n_chunks, lambda c, a: a + one(c), acc)
        return jnp.sum(acc, axis=0, keepdims=True)

    def search(it, state):
        thr_u, above = state
        cand_u = thr_u | jnp.left_shift(jnp.int32(1), 31 - it)
        cand = cand_u ^ int_min
        cnt = count(lambda kk: kk >= cand)
        take = cnt >= top_k
        return jnp.where(take, cand_u, thr_u), jnp.where(take, above, cnt)

    thr_u, above = lax.fori_loop(0, 32, search, (jnp.zeros((1, tq), jnp.int32), jnp.zeros((1, tq), F32)))
    thr = thr_u ^ int_min
    budget = top_k - above

    _online_init(m_ref, l_ref, acc_ref)
    ties_ref[...] = jnp.zeros(ties_ref.shape, F32)

    def qk(c, g, dst):
        dst[g] = _dot_nt(k_ref[0, g, pl.ds(pl.multiple_of(c * tk, tk), tk), :], q_ref[0, g, 0]) * (scale * LOG2E)

    def mask(c, tail):
        keys = key_ref[c]
        tie = keys == thr
        rank = _dot(ltri_ref[...], tie.astype(BF16)) + ties_ref[...]
        ok = (keys > thr) | (tie & (rank < budget))
        if tail:
            ok = ok & ((c * tk + lax.broadcasted_iota(jnp.int32, (tk, tq), 0)) <= t_row)
        ties_ref[...] = ties_ref[...] + _col_sum(tie.astype(F32))
        return jnp.concatenate([jnp.where(ok, 0.0, NEG_MASK)] * rep, axis=1)

    def soft(c, g, src, bias):
        _online_step(g, src[g] + bias, _pv_blocks(vt_ref, (0, g), c, tk), m_ref, l_ref, acc_ref, bounded)

    _flash_pairs(n_chunks, DSA_KV_HEADS, qk, mask, soft, sa_ref, sb_ref)

    for g in range(DSA_KV_HEADS):
        o_t = _online_result(g, l_ref, acc_ref)
        for r in range(rep):
            h = g * rep + r
            o = o_t[:, r * tq:(r + 1) * tq].T
            o_ref[:, h * LANES:(h + 1) * LANES] = (o * _silu(z_ref[:, h * LANES:(h + 1) * LANES])).astype(BF16)


def _dsa(cq, ck, cvt, iq, ik, ltri, proj, b, s, bounded):
    tq, tk = TQ_SPARSE, TK
    nq = s // tq
    g = DSA_KV_HEADS
    rep = DSA_HEADS // g
    w = rep * tq
    kern = functools.partial(_dsa_kernel, tq=tq, tk=tk, top_k=min(DSA_TOPK_MAX, s // 4), bounded=bounded)
    return pl.pallas_call(
        kern,
        grid=(b, nq),
        in_specs=[
            pl.BlockSpec((1, g, 1, w, LANES), lambda bb, i: (bb, 0, i, 0, 0)),
            pl.BlockSpec((1, g, s, LANES), lambda bb, i: (bb, 0, 0, 0), pipeline_mode=pl.Buffered(1)),
            pl.BlockSpec((1, g, s // VT_BLOCK, LANES, VT_BLOCK), lambda bb, i: (bb, 0, 0, 0, 0),
                         pipeline_mode=pl.Buffered(1)),
            pl.BlockSpec((1, 1, IDX_HEADS * tq, LANES), lambda bb, i: (bb, i, 0, 0)),
            pl.BlockSpec((1, s, LANES), lambda bb, i: (bb, 0, 0), pipeline_mode=pl.Buffered(1)),
            pl.BlockSpec((tq, LANES), lambda bb, i: (bb * nq + i, O_IKW)),
            pl.BlockSpec((tk, tk), lambda bb, i: (0, 0)),
            pl.BlockSpec((tq, DSA_HEADS * LANES), lambda bb, i: (bb * nq + i, O_CZ // DSA_HEADS)),
        ],
        out_specs=pl.BlockSpec((tq, DSA_HEADS * LANES), lambda bb, i: (bb * nq + i, 0)),
        out_shape=jax.ShapeDtypeStruct((b * s, DSA_HEADS * LANES), BF16),
        scratch_shapes=[
            pltpu.VMEM((s // tk, tk, tq), jnp.int32),
            pltpu.VMEM((g, 1, w), F32), pltpu.VMEM((g, SUBLANES, w), F32), pltpu.VMEM((g, LANES, w), F32),
            pltpu.VMEM((g, tk, w), F32), pltpu.VMEM((g, tk, w), F32),
            pltpu.VMEM((1, tq), F32),
        ],
        compiler_params=_cparams(("arbitrary", "arbitrary")),
        name="dsa_attention",
    )(cq, ck, cvt, iq, ik, proj, ltri, proj)


def _outproj_kernel(*refs, n_parts):
    x_ref = refs[0]
    y_refs = refs[1:1 + n_parts]
    w_refs = refs[1 + n_parts:1 + 2 * n_parts]
    o_ref = refs[1 + 2 * n_parts]
    acc = x_ref[...]
    for y_ref, w_ref in zip(y_refs, w_refs):
        acc = acc + _dot(y_ref[...], w_ref[...])
    o_ref[...] = acc


def _outproj(x2, ys, w_out):
    n, d = x2.shape
    tm = min(1024, n)
    tn = 1024
    widths = [y.shape[1] for y in ys]
    starts = np.cumsum([0] + widths[:-1]).tolist()
    ws = [w_out[st:st + wd] for st, wd in zip(starts, widths)]
    kern = functools.partial(_outproj_kernel, n_parts=len(ys))
    return pl.pallas_call(
        kern,
        grid=(d // tn, n // tm),
        in_specs=([pl.BlockSpec((tm, tn), lambda j, i: (i, j))]
                  + [pl.BlockSpec((tm, wd), lambda j, i: (i, 0)) for wd in widths]
                  + [pl.BlockSpec((wd, tn), lambda j, i: (0, j)) for wd in widths]),
        out_specs=pl.BlockSpec((tm, tn), lambda j, i: (i, j)),
        out_shape=jax.ShapeDtypeStruct((n, d), F32),
        compiler_params=_cparams(("arbitrary", "arbitrary")),
        name="out_proj",
    )(x2, *ys, *ws)


def _rope_tables(pos, half, reps):
    inv = ROPE_THETA ** (-jnp.arange(half, dtype=F32) / half)
    ang = pos.astype(F32)[:, None] * inv[None, :]
    cos, sin = jnp.cos(ang), jnp.sin(ang)
    return jnp.tile(jnp.concatenate([cos, cos], -1), (1, reps)), jnp.tile(jnp.concatenate([-sin, sin], -1), (1, reps))


def _split_cols(w, sizes):
    return jnp.split(w, np.cumsum(sizes)[:-1].tolist(), axis=-1)


def _even_weight(w):
    sizes = (1024, 1536, 24, 1024, 512, 512, 512, 512, 512, 512)
    a_q, a_kv, a_g, a_z, b_q, b_k, b_v, b_z, m_q, m_z = _split_cols(w.astype(BF16), sizes)
    d = w.shape[0]
    per_group = 3 * NSA_HEADS // NSA_KV_GROUPS
    gates = [jnp.pad(a_g[:, g * per_group:(g + 1) * per_group], ((0, 0), (0, LANES - per_group)))
             for g in range(NSA_KV_GROUPS)]
    out = jnp.concatenate([a_q, a_kv, b_q, b_k, b_v, m_q, a_z, b_z, m_z] + gates, axis=-1)
    assert out.shape == (d, EVEN_UNITS * LANES)
    return out


def _odd_weight(w):
    sizes = (1536, 512, 512, 1024, 64, 16, 1536, 512, 512)
    c_q, c_k, c_v, i_q, i_k, i_w, c_z, m_q, m_z = _split_cols(w.astype(BF16), sizes)
    d = w.shape[0]
    ikw = jnp.pad(jnp.concatenate([i_k, i_w], -1), ((0, 0), (0, LANES - IDX_DIM - IDX_HEADS)))
    pad = jnp.zeros((d, (O_CZ - ODD_PREP_UNITS) * LANES), BF16)
    out = jnp.concatenate([c_q, c_k, c_v, i_q, m_q, ikw, pad, c_z, m_z], axis=-1)
    assert out.shape == (d, ODD_UNITS * LANES)
    return out


def _overlap_matrix_t(s, n_cmp_pad):
    n_cmp = (s - NSA_CMP_LEN) // NSA_CMP_STRIDE + 1
    n_sel = s // NSA_SEL_LEN
    cmp_start = np.arange(n_cmp) * NSA_CMP_STRIDE
    sel_start = np.arange(n_sel) * NSA_SEL_LEN
    ov = np.clip(np.minimum(cmp_start[:, None] + NSA_CMP_LEN, sel_start[None, :] + NSA_SEL_LEN)
                 - np.maximum(cmp_start[:, None], sel_start[None, :]), 0, None) / NSA_CMP_LEN
    full = np.zeros((n_sel, n_cmp_pad), np.float32)
    full[:, :n_cmp] = ov.T
    return jnp.asarray(full, dtype=BF16)


def _score_bound(dim, gain_q, gain_k, scale):
    return dim * jnp.max(jnp.abs(gain_q)) * jnp.max(jnp.abs(gain_k)) * (scale * LOG2E * 1.02)


def _attend(bound, fn, *operands):
    return lax.cond(bound <= FAST_LOG2_BOUND,
                    lambda *a: fn(*a, bounded=True), lambda *a: fn(*a, bounded=False), *operands)


def kernel(x, mem, norm_gain, mem_norm_gain, mem_w_kv, mem_qk_gain, w_out, even_w_in, nsa_qk_gain, nsa_cmp_pos,
           nsa_cmp_w1, nsa_cmp_w2, diff_qk_gain, diff_lambda, diff_subln_gain, odd_w_in, dsa_qk_gain):
    b, s, d = x.shape
    assert d == D_MODEL and s % TK == 0 and s >= NSA_WINDOW + TQ_SPARSE
    pos = jnp.arange(s)
    cos, sin = _rope_tables(pos, HEAD_DIM // 2, 1)
    cos64, sin64 = _rope_tables(pos, DIFF_QK_DIM // 2, 2)
    n_cmp_pad = s // NSA_CMP_STRIDE
    cmp_last = jnp.arange(n_cmp_pad) * NSA_CMP_STRIDE + NSA_CMP_LEN - 1
    cos_c, sin_c = _rope_tables(cmp_last, HEAD_DIM // 2, 1)
    ovlt = _overlap_matrix_t(s, n_cmp_pad)
    ltri = jnp.asarray(np.tril(np.ones((TK, TK), np.float32), -1), dtype=BF16)
    wbias = _window_bias(TQ_SPARSE)

    mk_all, mv_all = _memkv(mem, mem_norm_gain, mem_w_kv.astype(BF16), mem_qk_gain[:, 1:2, :])
    w_out_b = w_out.astype(BF16)

    x2 = x.reshape(b * s, d)
    for i in range(DEPTH):
        mem_q_gain = mem_qk_gain[i, 0:1, :]
        if i % 2 == 0:
            e = i // 2
            proj = _proj(x2, norm_gain[i], _even_weight(even_w_in[e]), 768)
            dg2 = jnp.tile(diff_qk_gain[e], (1, 2))
            qa, ks, kw, vst, vwt, craw, bq, bk, bvt, mq = _prep_even(
                proj, b, s, cos, sin, cos64, sin64, nsa_qk_gain[e], dg2, mem_q_gain)
            w1 = nsa_cmp_w1[e].reshape(2, NSA_CMP_LEN, HEAD_DIM, NSA_CMP_HIDDEN).astype(BF16)
            kc, vct = _compress(craw, nsa_cmp_pos[e], w1, nsa_cmp_w2[e].astype(BF16),
                                nsa_qk_gain[e, 1:2, :], cos_c, sin_c)
            y_a = _attend(_score_bound(HEAD_DIM, nsa_qk_gain[e, 0], nsa_qk_gain[e, 1:4], HEAD_DIM ** -0.5),
                          functools.partial(_nsa, b=b, s=s), qa, ks, vst, kw, vwt, kc, vct, ovlt, wbias, proj)
            lambda_init = 0.8 - 0.6 * math.exp(-0.3 * i)
            y_b = _attend(_score_bound(DIFF_QK_DIM, diff_qk_gain[e, 0], diff_qk_gain[e, 1], DIFF_QK_DIM ** -0.5),
                          functools.partial(_diff, b=b, s=s, lambda_init=lambda_init),
                          bq, bk, bvt, diff_lambda[e], diff_subln_gain[e].reshape(1, LANES), proj)
            y_m = _memattn(mq, mk_all[i], mv_all[i], proj, b, s, E_MZ)
            ys = [y_a, y_b, y_m]
        else:
            o = i // 2
            proj = _proj(x2, norm_gain[i], _odd_weight(odd_w_in[o]), 512)
            cq, ck, cvt, iq, ik, mq = _prep_odd(proj, b, s, cos, sin, cos64, sin64, dsa_qk_gain[o], mem_q_gain)
            y_c = _attend(_score_bound(HEAD_DIM, dsa_qk_gain[o, 0], dsa_qk_gain[o, 1], HEAD_DIM ** -0.5),
                          functools.partial(_dsa, b=b, s=s), cq, ck, cvt, iq, ik, ltri, proj)
            y_m = _memattn(mq, mk_all[i], mv_all[i], proj, b, s, O_MZ)
            ys = [y_c, y_m]
        x2 = _outproj(x2, ys, w_out_b[i])
    return x2.reshape(b, s, d)
```

```python
import functools
import math

import jax
import jax.numpy as jnp
import numpy as np
from jax import lax
from jax.experimental import pallas as pl
from jax.experimental.pallas import tpu as pltpu

F32 = jnp.float32
BF16 = jnp.bfloat16

D_MODEL = 2048
DEPTH = 4
HEAD_DIM = 128
ROPE_THETA = 10000.0
EPS = 1e-6
MASKED = -1e30
FORCED = 1e9
NSA_HEADS = 8
NSA_KV_GROUPS = 2
NSA_CMP_LEN = 32
NSA_CMP_STRIDE = 16
NSA_CMP_HIDDEN = 256
NSA_SEL_LEN = 64
NSA_SEL_TOPN = 16
NSA_WINDOW = 512
DIFF_HEADS = 4
DIFF_QK_DIM = 64
DSA_HEADS = 12
DSA_KV_HEADS = 4
IDX_HEADS = 16
IDX_DIM = 64
DSA_TOPK_MAX = 256
MEM_HEADS = 4

LANES = 128
SUBLANES = 8
VMEM_LIMIT_BYTES = 56 * 1024 * 1024
LOG2E = 1.4426950408889634

PREP_ROWS = 256
TQ_SPARSE = 128
TQ_DIFF = 256
DIFF_QBLOCKS = 2
NSA_QBLOCKS = 2
TK = 512
VT_BLOCK = 256
NEG_INIT = -1e30
NEG_MASK = -2e30
FAST_LOG2_BOUND = 40.0

EVEN_UNITS = 54
E_AQ, E_AKV, E_BQ, E_BK, E_BV, E_MQ, E_AZ, E_BZ, E_MZ, E_AG = 0, 8, 20, 24, 28, 32, 36, 44, 48, 52
EVEN_PREP_UNITS = E_AZ
ODD_UNITS = 52
O_CQ, O_CK, O_CV, O_IQ, O_MQ, O_IKW, O_CZ, O_MZ = 0, 12, 16, 20, 28, 32, 36, 48
ODD_PREP_UNITS = O_IKW + 1


def _cparams(sem):
    return pltpu.CompilerParams(dimension_semantics=sem, vmem_limit_bytes=VMEM_LIMIT_BYTES)


def _dot(a, b):
    return jnp.dot(a, b, preferred_element_type=F32)


def _dot_nt(a, b):
    return lax.dot_general(a, b, (((1,), (1,)), ((), ())), preferred_element_type=F32)


def _silu(x):
    return x * jax.nn.sigmoid(x)


def _rms_norm(x, gain):
    return x * lax.rsqrt(jnp.mean(x * x, axis=-1, keepdims=True) + EPS) * gain


def _rms_norm_halves(x, gain):
    lo = lax.broadcasted_iota(jnp.int32, x.shape, 1) < 64
    xx = x * x
    s_lo = jnp.sum(jnp.where(lo, xx, 0.0), axis=-1, keepdims=True)
    s_hi = jnp.sum(jnp.where(lo, 0.0, xx), axis=-1, keepdims=True)
    ms = jnp.where(lo, s_lo, s_hi) * (1.0 / 64.0)
    return x * lax.rsqrt(ms + EPS) * gain


def _partner(x, half):
    n = x.shape[-1]
    lane = lax.broadcasted_iota(jnp.int32, x.shape, 1)
    if 2 * half == n:
        return pltpu.roll(x, half, 1)
    a = pltpu.roll(x, half, 1)
    b = pltpu.roll(x, n - half, 1)
    src_a = pltpu.roll(lane, half, 1)
    want = jnp.where((lane & (2 * half - 1)) < half, lane + half, lane - half)
    return jnp.where(src_a == want, a, b)


def _rope(x, cos, sin_signed, half):
    return x * cos + _partner(x, half) * sin_signed


def _transpose_bf16(x):
    n = x.shape[1]
    eye = (lax.broadcasted_iota(jnp.int32, (n, n), 0) == lax.broadcasted_iota(jnp.int32, (n, n), 1)).astype(BF16)
    return _dot_nt(eye, x.astype(BF16)).astype(BF16)


def _fold8(x, op, short_chains=False):
    rows, w = x.shape
    if short_chains:
        x = op(x.reshape(SUBLANES, rows // SUBLANES, w), axis=0)
        rows = rows // SUBLANES
    return op(x.reshape(rows // SUBLANES, SUBLANES, w), axis=0)


def _col_max(x):
    return jnp.max(_fold8(x, jnp.max), axis=0, keepdims=True)


def _col_sum(x):
    return jnp.sum(_fold8(x, jnp.sum), axis=0, keepdims=True)


def _proj_kernel(x_ref, g_ref, w_ref, o_ref, hn_ref):
    @pl.when(pl.program_id(1) == 0)
    def _():
        x = x_ref[...]
        hn_ref[...] = _rms_norm(x, g_ref[...]).astype(BF16)

    o_ref[...] = _dot(hn_ref[...], w_ref[...])


def _proj(x2, gain, w, tn):
    n, d = x2.shape
    c = w.shape[1]
    tm = min(1024, n)
    return pl.pallas_call(
        _proj_kernel,
        grid=(n // tm, c // tn),
        in_specs=[
            pl.BlockSpec((tm, d), lambda i, j: (i, 0)),
            pl.BlockSpec((1, d), lambda i, j: (0, 0)),
            pl.BlockSpec((d, tn), lambda i, j: (0, j)),
        ],
        out_specs=pl.BlockSpec((tm, tn), lambda i, j: (i, j)),
        out_shape=jax.ShapeDtypeStruct((n, c), F32),
        scratch_shapes=[pltpu.VMEM((tm, d), BF16)],
        compiler_params=_cparams(("arbitrary", "arbitrary")),
        name="proj",
    )(x2, gain.reshape(1, d), w)


def _prep_even_kernel(p_ref, cos_ref, sin_ref, cos64_ref, sin64_ref, ng_ref, dg_ref, mg_ref,
                      qa_ref, ks_ref, kw_ref, vst_ref, vwt_ref, craw_ref, bq_ref, bk_ref, bvt_ref, mq_ref):
    cos, sin = cos_ref[...], sin_ref[...]
    cos64, sin64 = cos64_ref[...], sin64_ref[...]
    rep = NSA_HEADS // NSA_KV_GROUPS
    tq = TQ_SPARSE

    def unit(u):
        return p_ref[:, u * LANES:(u + 1) * LANES]

    for h in range(NSA_HEADS):
        q = _rope(_rms_norm(unit(E_AQ + h), ng_ref[0:1, :]), cos, sin, 64).astype(BF16)
        g, r = divmod(h, rep)
        for qb in range(PREP_ROWS // tq):
            qa_ref[0, g, qb, r * tq:(r + 1) * tq, :] = q[qb * tq:(qb + 1) * tq, :]
    for g in range(NSA_KV_GROUPS):
        craw_ref[0, 0, g] = unit(E_AKV + 0 + g)
        craw_ref[0, 1, g] = unit(E_AKV + 2 + g)
        ks_ref[0, g] = _rope(_rms_norm(unit(E_AKV + 4 + g), ng_ref[2:3, :]), cos, sin, 64).astype(BF16)
        vst_ref[0, g, 0] = _transpose_bf16(unit(E_AKV + 6 + g))
        kw_ref[0, g] = _rope(_rms_norm(unit(E_AKV + 8 + g), ng_ref[3:4, :]), cos, sin, 64).astype(BF16)
        vw = unit(E_AKV + 10 + g)
        for kb in range(PREP_ROWS // LANES):
            vwt_ref[0, g, kb] = _transpose_bf16(vw[kb * LANES:(kb + 1) * LANES, :])
    lo = lax.broadcasted_iota(jnp.int32, cos.shape, 1) < 64
    for h in range(DIFF_HEADS):
        q = _rope(_rms_norm_halves(unit(E_BQ + h), dg_ref[0:1, :]), cos64, sin64, 32)
        bq_ref[0, h, 0, 0:PREP_ROWS, :] = jnp.where(lo, q, 0.0).astype(BF16)
        bq_ref[0, h, 0, PREP_ROWS:2 * PREP_ROWS, :] = jnp.where(lo, 0.0, q).astype(BF16)
        bk_ref[0, h] = _rope(_rms_norm_halves(unit(E_BK + h), dg_ref[1:2, :]), cos64, sin64, 32).astype(BF16)
        bvt_ref[0, h, 0] = _transpose_bf16(unit(E_BV + h))
    for h in range(MEM_HEADS):
        mq_ref[0, h] = _rms_norm(unit(E_MQ + h), mg_ref[...]).astype(BF16)


def _prep_even(proj, b, s, cos, sin, cos64, sin64, nsa_gain, diff_gain2, mem_gain):
    ts = PREP_ROWS
    assert ts == TQ_DIFF == VT_BLOCK
    nb = s // ts
    c = EVEN_PREP_UNITS * LANES
    g = NSA_KV_GROUPS
    rep = NSA_HEADS // g
    row = lambda bb, i: (i, 0)
    const = lambda bb, i: (0, 0)
    seq = lambda n: pl.BlockSpec((1, n, ts, LANES), lambda bb, i: (bb, 0, i, 0))
    blk5 = lambda n, k, r, cdim: pl.BlockSpec((1, n, k, r, cdim), lambda bb, i: (bb, 0, i, 0, 0))
    return pl.pallas_call(
        _prep_even_kernel,
        grid=(b, nb),
        in_specs=[
            pl.BlockSpec((ts, c), lambda bb, i: (bb * nb + i, 0)),
            pl.BlockSpec((ts, LANES), row), pl.BlockSpec((ts, LANES), row),
            pl.BlockSpec((ts, LANES), row), pl.BlockSpec((ts, LANES), row),
            pl.BlockSpec((4, LANES), const), pl.BlockSpec((2, LANES), const), pl.BlockSpec((1, LANES), const),
        ],
        out_specs=[
            blk5(g, ts // TQ_SPARSE, rep * TQ_SPARSE, LANES),
            seq(g), seq(g),
            blk5(g, 1, LANES, VT_BLOCK),
            blk5(g, ts // LANES, LANES, LANES),
            pl.BlockSpec((1, 2, g, ts, LANES), lambda bb, i: (bb, 0, 0, i, 0)),
            blk5(DIFF_HEADS, 1, 2 * TQ_DIFF, LANES),
            seq(DIFF_HEADS),
            blk5(DIFF_HEADS, 1, LANES, VT_BLOCK),
            seq(MEM_HEADS),
        ],
        out_shape=[
            jax.ShapeDtypeStruct((b, g, s // TQ_SPARSE, rep * TQ_SPARSE, LANES), BF16),
            jax.ShapeDtypeStruct((b, g, s, LANES), BF16),
            jax.ShapeDtypeStruct((b, g, s, LANES), BF16),
            jax.ShapeDtypeStruct((b, g, s // VT_BLOCK, LANES, VT_BLOCK), BF16),
            jax.ShapeDtypeStruct((b, g, s // LANES, LANES, LANES), BF16),
            jax.ShapeDtypeStruct((b, 2, g, s, LANES), F32),
            jax.ShapeDtypeStruct((b, DIFF_HEADS, s // TQ_DIFF, 2 * TQ_DIFF, LANES), BF16),
            jax.ShapeDtypeStruct((b, DIFF_HEADS, s, LANES), BF16),
            jax.ShapeDtypeStruct((b, DIFF_HEADS, s // VT_BLOCK, LANES, VT_BLOCK), BF16),
            jax.ShapeDtypeStruct((b, MEM_HEADS, s, LANES), BF16),
        ],
        compiler_params=_cparams(("arbitrary", "arbitrary")),
        name="prep_even",
    )(proj, cos, sin, cos64, sin64, nsa_gain, diff_gain2, mem_gain)


def _compress_kernel(x_ref, pe_ref, w1_ref, w2_ref, g_ref, cos_ref, sin_ref, kc_ref, vct_ref, pad_ref, *, s, n_pad):
    for kind in range(2):
        pad_ref[0:s, :] = x_ref[0, kind, 0]
        pad_ref[s:s + NSA_CMP_LEN, :] = jnp.zeros((NSA_CMP_LEN, LANES), F32)
        acc = jnp.zeros((n_pad, NSA_CMP_HIDDEN), F32)
        for l in range(NSA_CMP_LEN):
            rows = pad_ref[pl.ds(l, n_pad, stride=NSA_CMP_STRIDE), :] + pe_ref[kind, l:l + 1, :]
            acc = acc + _dot(rows.astype(BF16), w1_ref[kind, l])
        out = _dot(_silu(acc).astype(BF16), w2_ref[kind])
        if kind == 0:
            kc_ref[0, 0] = _rope(_rms_norm(out, g_ref[...]), cos_ref[...], sin_ref[...], 64).astype(BF16)
        else:
            vct_ref[0, 0] = out.T.astype(BF16)


def _compress(craw, pe, w1, w2, gain, cos_c, sin_c):
    b, _, g, s, _ = craw.shape
    n_pad = s // NSA_CMP_STRIDE
    kern = functools.partial(_compress_kernel, s=s, n_pad=n_pad)
    whole = lambda shape: pl.BlockSpec(shape, lambda bb, gg: (0,) * len(shape))
    return pl.pallas_call(
        kern,
        grid=(b, g),
        in_specs=[
            pl.BlockSpec((1, 2, 1, s, LANES), lambda bb, gg: (bb, 0, gg, 0, 0)),
            whole((2, NSA_CMP_LEN, LANES)),
            whole((2, NSA_CMP_LEN, LANES, NSA_CMP_HIDDEN)),
            whole((2, NSA_CMP_HIDDEN, LANES)),
            whole((1, LANES)), whole((n_pad, LANES)), whole((n_pad, LANES)),
        ],
        out_specs=[
            pl.BlockSpec((1, 1, n_pad, LANES), lambda bb, gg: (bb, gg, 0, 0)),
            pl.BlockSpec((1, 1, LANES, n_pad), lambda bb, gg: (bb, gg, 0, 0)),
        ],
        out_shape=[
            jax.ShapeDtypeStruct((b, g, n_pad, LANES), BF16),
            jax.ShapeDtypeStruct((b, g, LANES, n_pad), BF16),
        ],
        scratch_shapes=[pltpu.VMEM((s + NSA_CMP_LEN, LANES), F32)],
        compiler_params=_cparams(("arbitrary", "arbitrary")),
        name="nsa_compress",
    )(craw, pe, w1, w2, gain, cos_c, sin_c)


def _softmax_parts_t(st, bounded):
    if not bounded:
        st = st - _col_max(st)
    e = jnp.exp2(st)
    return e.astype(BF16), 1.0 / jnp.maximum(_col_sum(e), 1e-30)


def _online_init(m_ref, l_ref, acc_ref):
    m_ref[...] = jnp.full(m_ref.shape, NEG_INIT, F32)
    l_ref[...] = jnp.zeros(l_ref.shape, F32)
    acc_ref[...] = jnp.zeros(acc_ref.shape, F32)


def _online_step(idx, st, pv, m_ref, l_ref, acc_ref, bounded):
    if bounded:
        p = jnp.exp2(st)
        l_ref[idx] = l_ref[idx] + _fold8(p, jnp.sum)
        acc_ref[idx] = acc_ref[idx] + pv(p.astype(BF16))
        return
    m_old = m_ref[idx]
    m_new = jnp.maximum(m_old, _col_max(st))
    alpha = jnp.exp2(m_old - m_new)
    p = jnp.exp2(st - m_new)
    l_ref[idx] = alpha * l_ref[idx] + _fold8(p, jnp.sum)
    acc_ref[idx] = alpha * acc_ref[idx] + pv(p.astype(BF16))
    m_ref[idx] = m_new


def _online_result(idx, l_ref, acc_ref):
    return acc_ref[idx] / jnp.maximum(jnp.sum(l_ref[idx], axis=0, keepdims=True), 1e-30)


def _flash_pairs(n_chunks, n_groups, qk, mask, soft, sa_ref, sb_ref):
    for g in range(n_groups):
        qk(0, g, sa_ref)
    n_pairs = (n_chunks - 1) // 2

    def pair(j, carry):
        a = 2 * j
        bias_a = mask(a, False)
        bias_b = mask(a + 1, False)
        for g in range(n_groups):
            qk(a + 1, g, sb_ref)
            soft(a, g, sa_ref, bias_a)
            qk(a + 2, g, sa_ref)
            soft(a + 1, g, sb_ref, bias_b)
        return carry

    lax.fori_loop(0, n_pairs, pair, 0)
    e = 2 * n_pairs

    @pl.when(e + 1 < n_chunks)
    def _():
        bias_a = mask(e, True)
        bias_b = mask(e + 1, True)
        for g in range(n_groups):
            qk(e + 1, g, sb_ref)
            soft(e, g, sa_ref, bias_a)
            soft(e + 1, g, sb_ref, bias_b)

    @pl.when(e + 1 >= n_chunks)
    def _():
        bias_a = mask(e, True)
        for g in range(n_groups):
            soft(e, g, sa_ref, bias_a)


def _pv_blocks(vt_ref, lead, chunk, tk):
    per = tk // VT_BLOCK

    def pv(p):
        out = _dot(vt_ref[lead + (chunk * per,)], p[0:VT_BLOCK])
        for i in range(1, per):
            out = out + _dot(vt_ref[lead + (chunk * per + i,)], p[i * VT_BLOCK:(i + 1) * VT_BLOCK])
        return out

    return pv


def _top_n_mask_t(scores, n):
    row = lax.broadcasted_iota(jnp.int32, scores.shape, 0).astype(F32)
    height = float(scores.shape[0])
    work = scores
    sel = jnp.zeros(scores.shape, F32)
    for _ in range(n):
        m = jnp.max(work, axis=0, keepdims=True)
        first = jnp.min(jnp.where(work == m, row, height), axis=0, keepdims=True)
        pick = row == first
        sel = jnp.where(pick, 1.0, sel)
        work = jnp.where(pick, -jnp.inf, work)
    return sel


def _nsa_kernel(q_ref, kc_ref, vct_ref, ks_ref, vst_ref, kw_ref, vwt_ref, ovlt_ref, *rest, tq, tk, n_qb, top_n, bounded):
    wbias_refs = rest[:n_qb]
    gl_ref, z_ref, o_ref, m_ref, l_ref, acc_ref, sa_ref, sb_ref, selt_ref, oct_ref, owt_ref = rest[n_qb:]
    rep = NSA_HEADS // NSA_KV_GROUPS
    w = rep * tq
    scale = HEAD_DIM ** -0.5
    step0 = pl.program_id(2) * (n_qb * tq)
    lane_t = lax.broadcasted_iota(jnp.int32, (1, w), 1) & (tq - 1)
    n_cmp_pad = kc_ref.shape[2]
    n_sel = ovlt_ref.shape[0]
    sel_shift = NSA_SEL_LEN.bit_length() - 1
    span = NSA_WINDOW + tq

    for qb in range(n_qb):
        q0 = step0 + qb * tq
        t_row = q0 + lane_t
        t_one = t_row[:, 0:tq]
        q = q_ref[0, 0, qb]

        cmp_last = lax.broadcasted_iota(jnp.int32, (n_cmp_pad, w), 0) * NSA_CMP_STRIDE + (NSA_CMP_LEN - 1)
        st = jnp.where(cmp_last <= t_row, _dot_nt(kc_ref[0, 0], q) * (scale * LOG2E), NEG_MASK)
        p, inv = _softmax_parts_t(st, bounded)
        inv = jnp.where(t_row >= NSA_CMP_LEN - 1, inv, 0.0)
        oct_ref[qb] = _dot(vct_ref[0, 0], p) * inv
        imp_heads = _dot(ovlt_ref[...], p) * inv
        imp = imp_heads[:, 0:tq]
        for r in range(1, rep):
            imp = imp + imp_heads[:, r * tq:(r + 1) * tq]

        j = lax.broadcasted_iota(jnp.int32, (n_sel, tq), 0)
        cur = jnp.right_shift(t_one, sel_shift)
        visible = j <= cur
        forced = (j == 0) | (j >= cur - 1)
        imp = jnp.where(visible, jnp.where(forced, FORCED, imp), MASKED)
        selt_ref[qb] = _top_n_mask_t(imp, top_n)

        start = pl.multiple_of(jnp.maximum(q0 - NSA_WINDOW, 0), tq)
        st = (_dot_nt(kw_ref[0, 0, pl.ds(start, span), :], q) * (scale * LOG2E)
              + jnp.concatenate([wbias_refs[qb][0]] * rep, axis=1))
        p, inv = _softmax_parts_t(st, bounded)
        blk0 = start // LANES
        ow = _dot(vwt_ref[0, 0, blk0], p[0:LANES])
        for i in range(1, span // LANES):
            ow = ow + _dot(vwt_ref[0, 0, blk0 + i], p[i * LANES:(i + 1) * LANES])
        owt_ref[qb] = ow * inv

    _online_init(m_ref, l_ref, acc_ref)
    n_chunks = (step0 + n_qb * tq + tk - 1) // tk
    blocks_per_chunk = tk // NSA_SEL_LEN

    def qk(c, g, dst):
        dst[g] = _dot_nt(ks_ref[0, 0, pl.ds(pl.multiple_of(c * tk, tk), tk), :], q_ref[0, 0, g]) * (scale * LOG2E)

    def mask(c, tail):
        biases = []
        for g in range(n_qb):
            rows = selt_ref[g, pl.ds(pl.multiple_of(c * blocks_per_chunk, blocks_per_chunk), blocks_per_chunk), :]
            ok = jnp.concatenate([jnp.broadcast_to(rows[i:i + 1, :], (NSA_SEL_LEN, tq))
                                  for i in range(blocks_per_chunk)], axis=0) > 0.5
            if tail:
                key = c * tk + lax.broadcasted_iota(jnp.int32, (tk, tq), 0)
                ok = ok & (key <= step0 + g * tq + lane_t[:, 0:tq])
            biases.append(jnp.concatenate([jnp.where(ok, 0.0, NEG_MASK)] * rep, axis=1))
        return biases

    def soft(c, g, src, bias):
        _online_step(g, src[g] + bias[g], _pv_blocks(vst_ref, (0, 0), c, tk), m_ref, l_ref, acc_ref, bounded)

    _flash_pairs(n_chunks, n_qb, qk, mask, soft, sa_ref, sb_ref)

    for qb in range(n_qb):
        rows = slice(qb * tq, (qb + 1) * tq)
        gates_t = jax.nn.sigmoid(gl_ref[rows, :]).T
        os_t = _online_result(qb, l_ref, acc_ref)
        for r in range(rep):
            sl = slice(r * tq, (r + 1) * tq)
            cols = slice(r * LANES, (r + 1) * LANES)
            out_t = (gates_t[3 * r:3 * r + 1, :] * oct_ref[qb, :, sl] + gates_t[3 * r + 1:3 * r + 2, :] * os_t[:, sl]
                     + gates_t[3 * r + 2:3 * r + 3, :] * owt_ref[qb, :, sl])
            o_ref[rows, cols] = (out_t.T * _silu(z_ref[rows, cols])).astype(BF16)


def _window_bias(tq):
    span = NSA_WINDOW + tq
    n_early = NSA_WINDOW // tq
    row = np.arange(span)[:, None]
    lane = np.arange(tq)[None, :]
    tables = []
    for i in range(n_early + 1):
        t = i * tq + lane
        s_pos = (0 if i < n_early else t[0, 0] - NSA_WINDOW) + row
        ok = (s_pos <= t) & (s_pos > t - NSA_WINDOW)
        tables.append(np.where(ok, 0.0, NEG_MASK))
    return jnp.asarray(np.stack(tables), dtype=F32)


def _nsa(qa, ks, vst, kw, vwt, kc, vct, ovlt, wbias, proj, b, s, bounded):
    tq, tk, n_qb = TQ_SPARSE, TK, NSA_QBLOCKS
    assert tk % (n_qb * tq) == 0
    rows = n_qb * tq
    nq = s // rows
    g = NSA_KV_GROUPS
    rep = NSA_HEADS // g
    w = rep * tq
    n_cmp_pad = kc.shape[2]
    n_sel = s // NSA_SEL_LEN
    last_bias = wbias.shape[0] - 1
    kern = functools.partial(_nsa_kernel, tq=tq, tk=tk, n_qb=n_qb, top_n=min(NSA_SEL_TOPN, n_sel), bounded=bounded)
    per_group = lambda shape: pl.BlockSpec((1, 1) + shape, lambda bb, gg, i: (bb, gg) + (0,) * len(shape))
    bias_spec = lambda qb: pl.BlockSpec((1,) + wbias.shape[1:],
                                        lambda bb, gg, i: (jnp.minimum(n_qb * i + qb, last_bias), 0, 0))
    return pl.pallas_call(
        kern,
        grid=(b, g, nq),
        in_specs=[
            pl.BlockSpec((1, 1, n_qb, w, LANES), lambda bb, gg, i: (bb, gg, i, 0, 0)),
            per_group((n_cmp_pad, LANES)), per_group((LANES, n_cmp_pad)),
            per_group((s, LANES)), per_group((s // VT_BLOCK, LANES, VT_BLOCK)),
            per_group((s, LANES)), per_group((s // LANES, LANES, LANES)),
            pl.BlockSpec((n_sel, n_cmp_pad), lambda bb, gg, i: (0, 0)),
        ] + [bias_spec(qb) for qb in range(n_qb)] + [
            pl.BlockSpec((rows, LANES), lambda bb, gg, i: (bb * nq + i, E_AG + gg)),
            pl.BlockSpec((rows, rep * LANES), lambda bb, gg, i: (bb * nq + i, E_AZ // rep + gg)),
        ],
        out_specs=pl.BlockSpec((rows, rep * LANES), lambda bb, gg, i: (bb * nq + i, gg)),
        out_shape=jax.ShapeDtypeStruct((b * s, NSA_HEADS * LANES), BF16),
        scratch_shapes=[
            pltpu.VMEM((n_qb, 1, w), F32), pltpu.VMEM((n_qb, SUBLANES, w), F32), pltpu.VMEM((n_qb, LANES, w), F32),
            pltpu.VMEM((n_qb, tk, w), F32), pltpu.VMEM((n_qb, tk, w), F32),
            pltpu.VMEM((n_qb, n_sel, tq), F32), pltpu.VMEM((n_qb, LANES, w), F32), pltpu.VMEM((n_qb, LANES, w), F32),
        ],
        compiler_params=_cparams(("arbitrary", "arbitrary", "arbitrary")),
        name="nsa_attention",
    )(qa, kc, vct, ks, vst, kw, vwt, ovlt, *([wbias] * n_qb), proj, proj)


def _diff_kernel(q_ref, k_ref, vt_ref, lam_ref, sg_ref, z_ref, o_ref, m_ref, l_ref, acc_ref, sa_ref, sb_ref,
                 *, tq, tk, n_qb, lambda_init, bounded):
    w = 2 * tq
    q0 = pl.program_id(2) * (n_qb * tq)
    lane_t = lax.broadcasted_iota(jnp.int32, (1, w), 1) & (tq - 1)
    _online_init(m_ref, l_ref, acc_ref)
    n_chunks = (q0 + n_qb * tq + tk - 1) // tk

    def qk(c, g, dst):
        dst[g] = (_dot_nt(k_ref[0, 0, pl.ds(pl.multiple_of(c * tk, tk), tk), :], q_ref[0, 0, g])
                  * (DIFF_QK_DIM ** -0.5 * LOG2E))

    def mask(c, tail):
        if not tail:
            return None
        key = c * tk + lax.broadcasted_iota(jnp.int32, (tk, w), 0)
        return [jnp.where(key <= q0 + g * tq + lane_t, 0.0, NEG_MASK) for g in range(n_qb)]

    def soft(c, g, src, bias):
        st = src[g] if bias is None else src[g] + bias[g]
        _online_step(g, st, _pv_blocks(vt_ref, (0, 0), c, tk), m_ref, l_ref, acc_ref, bounded)

    _flash_pairs(n_chunks, n_qb, qk, mask, soft, sa_ref, sb_ref)

    lv = lam_ref[...]
    lam = (jnp.exp(jnp.sum(lv[0:1] * lv[1:2], axis=-1, keepdims=True))
           - jnp.exp(jnp.sum(lv[2:3] * lv[3:4], axis=-1, keepdims=True)) + lambda_init)
    for g in range(n_qb):
        rows = slice(g * tq, (g + 1) * tq)
        o_t = _online_result(g, l_ref, acc_ref)
        o = (o_t[:, 0:tq] - lam * o_t[:, tq:w]).T
        o = _rms_norm(o, sg_ref[...]) * (1.0 - lambda_init)
        o_ref[rows, :] = (o * _silu(z_ref[rows, :])).astype(BF16)


def _diff(bq, bk, bvt, lam_vecs, subln_gain, proj, b, s, lambda_init, bounded):
    tq, tk, n_qb = TQ_DIFF, TK, DIFF_QBLOCKS
    assert n_qb * tq == tk
    rows = n_qb * tq
    nq = s // rows
    w = 2 * tq
    kern = functools.partial(_diff_kernel, tq=tq, tk=tk, n_qb=n_qb, lambda_init=lambda_init, bounded=bounded)
    return pl.pallas_call(
        kern,
        grid=(b, DIFF_HEADS, nq),
        in_specs=[
            pl.BlockSpec((1, 1, n_qb, w, LANES), lambda bb, h, i: (bb, h, i, 0, 0)),
            pl.BlockSpec((1, 1, s, LANES), lambda bb, h, i: (bb, h, 0, 0)),
            pl.BlockSpec((1, 1, s // VT_BLOCK, LANES, VT_BLOCK), lambda bb, h, i: (bb, h, 0, 0, 0)),
            pl.BlockSpec((4, DIFF_QK_DIM), lambda bb, h, i: (0, 0)),
            pl.BlockSpec((1, LANES), lambda bb, h, i: (0, 0)),
            pl.BlockSpec((rows, LANES), lambda bb, h, i: (bb * nq + i, E_BZ + h)),
        ],
        out_specs=pl.BlockSpec((rows, LANES), lambda bb, h, i: (bb * nq + i, h)),
        out_shape=jax.ShapeDtypeStruct((b * s, DIFF_HEADS * LANES), BF16),
        scratch_shapes=[
            pltpu.VMEM((n_qb, 1, w), F32), pltpu.VMEM((n_qb, SUBLANES, w), F32), pltpu.VMEM((n_qb, LANES, w), F32),
            pltpu.VMEM((n_qb, tk, w), F32), pltpu.VMEM((n_qb, tk, w), F32),
        ],
        compiler_params=_cparams(("arbitrary", "arbitrary", "arbitrary")),
        name="diff_attention",
    )(bq, bk, bvt, lam_vecs, subln_gain, proj)


def _memkv_kernel(mem_ref, mg_ref, w_ref, kg_ref, k_ref, v_ref):
    mem_n = _rms_norm(mem_ref[0], mg_ref[...]).astype(BF16)
    kv = _dot(mem_n, w_ref[0])
    for h in range(MEM_HEADS):
        k_ref[0, 0, h] = _rms_norm(kv[:, h * LANES:(h + 1) * LANES], kg_ref[0]).astype(BF16)
        v_ref[0, 0, h] = kv[:, (MEM_HEADS + h) * LANES:(MEM_HEADS + h + 1) * LANES].astype(BF16)


def _memkv(mem, mem_gain, w_kv, k_gain):
    b, m, d = mem.shape
    depth = w_kv.shape[0]
    c = w_kv.shape[2]
    out = jax.ShapeDtypeStruct((depth, b, MEM_HEADS, m, LANES), BF16)
    ospec = pl.BlockSpec((1, 1, MEM_HEADS, m, LANES), lambda i, bb: (i, bb, 0, 0, 0))
    return pl.pallas_call(
        _memkv_kernel,
        grid=(depth, b),
        in_specs=[
            pl.BlockSpec((1, m, d), lambda i, bb: (bb, 0, 0)),
            pl.BlockSpec((1, d), lambda i, bb: (0, 0)),
            pl.BlockSpec((1, d, c), lambda i, bb: (i, 0, 0)),
            pl.BlockSpec((1, 1, LANES), lambda i, bb: (i, 0, 0)),
        ],
        out_specs=[ospec, ospec],
        out_shape=[out, out],
        compiler_params=_cparams(("arbitrary", "arbitrary")),
        name="mem_kv",
    )(mem, mem_gain.reshape(1, d), w_kv, k_gain)


def _memattn_kernel(q_ref, k_ref, v_ref, z_ref, o_ref):
    scale = HEAD_DIM ** -0.5
    for h in range(MEM_HEADS):
        s = _dot_nt(q_ref[0, h], k_ref[0, h]) * scale
        e = jnp.exp(s - jnp.max(s, axis=-1, keepdims=True))
        p = e / jnp.sum(e, axis=-1, keepdims=True)
        o = _dot(p.astype(BF16), v_ref[0, h])
        o_ref[:, h * LANES:(h + 1) * LANES] = (o * _silu(z_ref[:, h * LANES:(h + 1) * LANES])).astype(BF16)


def _memattn(mq, mk, mv, proj, b, s, z_unit):
    tq = min(512, s)
    nq = s // tq
    m = mk.shape[2]
    kvspec = pl.BlockSpec((1, MEM_HEADS, m, LANES), lambda bb, i: (bb, 0, 0, 0))
    return pl.pallas_call(
        _memattn_kernel,
        grid=(b, nq),
        in_specs=[
            pl.BlockSpec((1, MEM_HEADS, tq, LANES), lambda bb, i: (bb, 0, i, 0)),
            kvspec, kvspec,
            pl.BlockSpec((tq, MEM_HEADS * LANES), lambda bb, i: (bb * nq + i, z_unit // MEM_HEADS)),
        ],
        out_specs=pl.BlockSpec((tq, MEM_HEADS * LANES), lambda bb, i: (bb * nq + i, 0)),
        out_shape=jax.ShapeDtypeStruct((b * s, MEM_HEADS * LANES), BF16),
        compiler_params=_cparams(("arbitrary", "arbitrary")),
        name="mem_attention",
    )(mq, mk, mv, proj)


def _prep_odd_kernel(p_ref, cos_ref, sin_ref, cos64_ref, sin64_ref, cg_ref, mg_ref,
                     cq_ref, ck_ref, cvt_ref, iq_ref, ik_ref, mq_ref):
    cos, sin = cos_ref[...], sin_ref[...]
    cos64, sin64 = cos64_ref[...], sin64_ref[...]
    rep = DSA_HEADS // DSA_KV_HEADS
    tq = TQ_SPARSE
    n_qb = PREP_ROWS // tq

    def unit(u):
        return p_ref[:, u * LANES:(u + 1) * LANES]

    for h in range(DSA_HEADS):
        q = _rope(_rms_norm(unit(O_CQ + h), cg_ref[0:1, :]), cos, sin, 64).astype(BF16)
        g, r = divmod(h, rep)
        for qb in range(n_qb):
            cq_ref[0, g, qb, r * tq:(r + 1) * tq, :] = q[qb * tq:(qb + 1) * tq, :]
    for h in range(DSA_KV_HEADS):
        ck_ref[0, h] = _rope(_rms_norm(unit(O_CK + h), cg_ref[1:2, :]), cos, sin, 64).astype(BF16)
        cvt_ref[0, h, 0] = _transpose_bf16(unit(O_CV + h))
    lo = lax.broadcasted_iota(jnp.int32, cos.shape, 1) < 64
    for u in range(IDX_HEADS // 2):
        x = _rope(unit(O_IQ + u), cos64, sin64, 32)
        even = jnp.where(lo, x, 0.0).astype(BF16)
        odd = jnp.where(lo, pltpu.roll(x, 64, 1), 0.0).astype(BF16)
        for qb in range(n_qb):
            iq_ref[0, qb, (2 * u) * tq:(2 * u + 1) * tq, :] = even[qb * tq:(qb + 1) * tq, :]
            iq_ref[0, qb, (2 * u + 1) * tq:(2 * u + 2) * tq, :] = odd[qb * tq:(qb + 1) * tq, :]
    ik = _rope(unit(O_IKW), cos64, sin64, 32)
    ik_ref[0] = jnp.where(lo, ik, 0.0).astype(BF16)
    for h in range(MEM_HEADS):
        mq_ref[0, h] = _rms_norm(unit(O_MQ + h), mg_ref[...]).astype(BF16)


def _prep_odd(proj, b, s, cos, sin, cos64, sin64, dsa_gain, mem_gain):
    ts = PREP_ROWS
    nb = s // ts
    c = ODD_PREP_UNITS * LANES
    rep = DSA_HEADS // DSA_KV_HEADS
    row = lambda bb, i: (i, 0)
    const = lambda bb, i: (0, 0)
    seq = lambda n: pl.BlockSpec((1, n, ts, LANES), lambda bb, i: (bb, 0, i, 0))
    return pl.pallas_call(
        _prep_odd_kernel,
        grid=(b, nb),
        in_specs=[
            pl.BlockSpec((ts, c), lambda bb, i: (bb * nb + i, 0)),
            pl.BlockSpec((ts, LANES), row), pl.BlockSpec((ts, LANES), row),
            pl.BlockSpec((ts, LANES), row), pl.BlockSpec((ts, LANES), row),
            pl.BlockSpec((2, LANES), const), pl.BlockSpec((1, LANES), const),
        ],
        out_specs=[
            pl.BlockSpec((1, DSA_KV_HEADS, ts // TQ_SPARSE, rep * TQ_SPARSE, LANES), lambda bb, i: (bb, 0, i, 0, 0)),
            seq(DSA_KV_HEADS),
            pl.BlockSpec((1, DSA_KV_HEADS, 1, LANES, VT_BLOCK), lambda bb, i: (bb, 0, i, 0, 0)),
            pl.BlockSpec((1, ts // TQ_SPARSE, IDX_HEADS * TQ_SPARSE, LANES), lambda bb, i: (bb, i, 0, 0)),
            pl.BlockSpec((1, ts, LANES), lambda bb, i: (bb, i, 0)),
            seq(MEM_HEADS),
        ],
        out_shape=[
            jax.ShapeDtypeStruct((b, DSA_KV_HEADS, s // TQ_SPARSE, rep * TQ_SPARSE, LANES), BF16),
            jax.ShapeDtypeStruct((b, DSA_KV_HEADS, s, LANES), BF16),
            jax.ShapeDtypeStruct((b, DSA_KV_HEADS, s // VT_BLOCK, LANES, VT_BLOCK), BF16),
            jax.ShapeDtypeStruct((b, s // TQ_SPARSE, IDX_HEADS * TQ_SPARSE, LANES), BF16),
            jax.ShapeDtypeStruct((b, s, LANES), BF16),
            jax.ShapeDtypeStruct((b, MEM_HEADS, s, LANES), BF16),
        ],
        compiler_params=_cparams(("arbitrary", "arbitrary")),
        name="prep_odd",
    )(proj, cos, sin, cos64, sin64, dsa_gain, mem_gain)


def _sortable_key(x):
    bits = pltpu.bitcast(x + 0.0, jnp.int32)
    return jnp.where(bits < 0, bits ^ jnp.int32(0x7FFFFFFF), bits)


def _dsa_kernel(q_ref, k_ref, vt_ref, iq_ref, ik_ref, w_ref, ltri_ref, z_ref, o_ref,
                key_ref, m_ref, l_ref, acc_ref, sa_ref, sb_ref, ties_ref, *, tq, tk, top_k, bounded):
    rep = DSA_HEADS // DSA_KV_HEADS
    scale = HEAD_DIM ** -0.5
    q0 = pl.program_id(1) * tq
    t_row = q0 + lax.broadcasted_iota(jnp.int32, (1, tq), 1)
    n_chunks = (q0 + tq + tk - 1) // tk
    int_min = jnp.int32(-2147483648)
    heads_per_dot = 4

    w_t = (w_ref[...] * (IDX_HEADS ** -0.5 * IDX_DIM ** -0.5)).T

    def score_chunk(c, carry):
        k0 = pl.multiple_of(c * tk, tk)
        ik = ik_ref[0, pl.ds(k0, tk), :]
        sc = jnp.zeros((tk, tq), F32)
        for h0 in range(0, IDX_HEADS, heads_per_dot):
            x = _dot_nt(ik, iq_ref[0, 0, h0 * tq:(h0 + heads_per_dot) * tq, :])
            for hh in range(heads_per_dot):
                h = h0 + hh
                sc = sc + jnp.maximum(x[:, hh * tq:(hh + 1) * tq], 0.0) * w_t[IDX_DIM + h:IDX_DIM + h + 1, :]
        causal = (k0 + lax.broadcasted_iota(jnp.int32, (tk, tq), 0)) <= t_row
        key_ref[c] = _sortable_key(jnp.where(causal, sc, MASKED))
        return carry

    lax.fori_loop(0, n_chunks, score_chunk, 0)

    def count(pred):
        def one(c):
            return _fold8(pred(key_ref[c]).astype(F32), jnp.sum, short_chains=True)

        def four(j, acc):
            return acc + ((one(4 * j) + one(4 * j + 1)) + (one(4 * j + 2) + one(4 * j + 3)))

        acc = lax.fori_loop(0, n_chunks // 4, four, jnp.zeros((SUBLANES, tq), F32))
        acc = lax.fori_loop(n_chunks - n_chunks % 4, n_chunks, lambda c, a: a + one(c), acc)
        return jnp.sum(acc, axis=0, keepdims=True)

    def search(it, state):
        thr_u, above = state
        cand_u = thr_u | jnp.left_shift(jnp.int32(1), 31 - it)
        cand = cand_u ^ int_min
        cnt = count(lambda kk: kk >= cand)
        take = cnt >= top_k
        return jnp.where(take, cand_u, thr_u), jnp.where(take, above, cnt)

    thr_u, above = lax.fori_loop(0, 32, search, (jnp.zeros((1, tq), jnp.int32), jnp.zeros((1, tq), F32)))
    thr = thr_u ^ int_min
    budget = top_k - above

    _online_init(m_ref, l_ref, acc_ref)
    ties_ref[...] = jnp.zeros(ties_ref.shape, F32)

    def qk(c, g, dst):
        dst[g] = _dot_nt(k_ref[0, g, pl.ds(pl.multiple_of(c * tk, tk), tk), :], q_ref[0, g, 0]) * (scale * LOG2E)

    def mask(c, tail):
        keys = key_ref[c]
        tie = keys == thr
        rank = _dot(ltri_ref[...], tie.astype(BF16)) + ties_ref[...]
        ok = (keys > thr) | (tie & (rank < budget))
        if tail:
            ok = ok & ((c * tk + lax.broadcasted_iota(jnp.int32, (tk, tq), 0)) <= t_row)
        ties_ref[...] = ties_ref[...] + _col_sum(tie.astype(F32))
        return jnp.concatenate([jnp.where(ok, 0.0, NEG_MASK)] * rep, axis=1)

    def soft(c, g, src, bias):
        _online_step(g, src[g] + bias, _pv_blocks(vt_ref, (0, g), c, tk), m_ref, l_ref, acc_ref, bounded)

    _flash_pairs(n_chunks, DSA_KV_HEADS, qk, mask, soft, sa_ref, sb_ref)

    for g in range(DSA_KV_HEADS):
        o_t = _online_result(g, l_ref, acc_ref)
        for r in range(rep):
            h = g * rep + r
            o = o_t[:, r * tq:(r + 1) * tq].T
            o_ref[:, h * LANES:(h + 1) * LANES] = (o * _silu(z_ref[:, h * LANES:(h + 1) * LANES])).astype(BF16)


def _dsa(cq, ck, cvt, iq, ik, ltri, proj, b, s, bounded):
    tq, tk = TQ_SPARSE, TK
    nq = s // tq
    g = DSA_KV_HEADS
    rep = DSA_HEADS // g
    w = rep * tq
    kern = functools.partial(_dsa_kernel, tq=tq, tk=tk, top_k=min(DSA_TOPK_MAX, s // 4), bounded=bounded)
    return pl.pallas_call(
        kern,
        grid=(b, nq),
        in_specs=[
            pl.BlockSpec((1, g, 1, w, LANES), lambda bb, i: (bb, 0, i, 0, 0)),
            pl.BlockSpec((1, g, s, LANES), lambda bb, i: (bb, 0, 0, 0), pipeline_mode=pl.Buffered(1)),
            pl.BlockSpec((1, g, s // VT_BLOCK, LANES, VT_BLOCK), lambda bb, i: (bb, 0, 0, 0, 0),
                         pipeline_mode=pl.Buffered(1)),
            pl.BlockSpec((1, 1, IDX_HEADS * tq, LANES), lambda bb, i: (bb, i, 0, 0)),
            pl.BlockSpec((1, s, LANES), lambda bb, i: (bb, 0, 0), pipeline_mode=pl.Buffered(1)),
            pl.BlockSpec((tq, LANES), lambda bb, i: (bb * nq + i, O_IKW)),
            pl.BlockSpec((tk, tk), lambda bb, i: (0, 0)),
            pl.BlockSpec((tq, DSA_HEADS * LANES), lambda bb, i: (bb * nq + i, O_CZ // DSA_HEADS)),
        ],
        out_specs=pl.BlockSpec((tq, DSA_HEADS * LANES), lambda bb, i: (bb * nq + i, 0)),
        out_shape=jax.ShapeDtypeStruct((b * s, DSA_HEADS * LANES), BF16),
        scratch_shapes=[
            pltpu.VMEM((s // tk, tk, tq), jnp.int32),
            pltpu.VMEM((g, 1, w), F32), pltpu.VMEM((g, SUBLANES, w), F32), pltpu.VMEM((g, LANES, w), F32),
            pltpu.VMEM((g, tk, w), F32), pltpu.VMEM((g, tk, w), F32),
            pltpu.VMEM((1, tq), F32),
        ],
        compiler_params=_cparams(("arbitrary", "arbitrary")),
        name="dsa_attention",
    )(cq, ck, cvt, iq, ik, proj, ltri, proj)


def _outproj_kernel(*refs, n_parts):
    x_ref = refs[0]
    y_refs = refs[1:1 + n_parts]
    w_refs = refs[1 + n_parts:1 + 2 * n_parts]
    o_ref = refs[1 + 2 * n_parts]
    acc = x_ref[...]
    for y_ref, w_ref in zip(y_refs, w_refs):
        acc = acc + _dot(y_ref[...], w_ref[...])
    o_ref[...] = acc


def _outproj(x2, ys, w_out):
    n, d = x2.shape
    tm = min(1024, n)
    tn = 1024
    widths = [y.shape[1] for y in ys]
    starts = np.cumsum([0] + widths[:-1]).tolist()
    ws = [w_out[st:st + wd] for st, wd in zip(starts, widths)]
    kern = functools.partial(_outproj_kernel, n_parts=len(ys))
    return pl.pallas_call(
        kern,
        grid=(d // tn, n // tm),
        in_specs=([pl.BlockSpec((tm, tn), lambda j, i: (i, j))]
                  + [pl.BlockSpec((tm, wd), lambda j, i: (i, 0)) for wd in widths]
                  + [pl.BlockSpec((wd, tn), lambda j, i: (0, j)) for wd in widths]),
        out_specs=pl.BlockSpec((tm, tn), lambda j, i: (i, j)),
        out_shape=jax.ShapeDtypeStruct((n, d), F32),
        compiler_params=_cparams(("arbitrary", "arbitrary")),
        name="out_proj",
    )(x2, *ys, *ws)


def _rope_tables(pos, half, reps):
    inv = ROPE_THETA ** (-jnp.arange(half, dtype=F32) / half)
    ang = pos.astype(F32)[:, None] * inv[None, :]
    cos, sin = jnp.cos(ang), jnp.sin(ang)
    return jnp.tile(jnp.concatenate([cos, cos], -1), (1, reps)), jnp.tile(jnp.concatenate([-sin, sin], -1), (1, reps))


def _split_cols(w, sizes):
    return jnp.split(w, np.cumsum(sizes)[:-1].tolist(), axis=-1)


def _even_weight(w):
    sizes = (1024, 1536, 24, 1024, 512, 512, 512, 512, 512, 512)
    a_q, a_kv, a_g, a_z, b_q, b_k, b_v, b_z, m_q, m_z = _split_cols(w.astype(BF16), sizes)
    d = w.shape[0]
    per_group = 3 * NSA_HEADS // NSA_KV_GROUPS
    gates = [jnp.pad(a_g[:, g * per_group:(g + 1) * per_group], ((0, 0), (0, LANES - per_group)))
             for g in range(NSA_KV_GROUPS)]
    out = jnp.concatenate([a_q, a_kv, b_q, b_k, b_v, m_q, a_z, b_z, m_z] + gates, axis=-1)
    assert out.shape == (d, EVEN_UNITS * LANES)
    return out


def _odd_weight(w):
    sizes = (1536, 512, 512, 1024, 64, 16, 1536, 512, 512)
    c_q, c_k, c_v, i_q, i_k, i_w, c_z, m_q, m_z = _split_cols(w.astype(BF16), sizes)
    d = w.shape[0]
    ikw = jnp.pad(jnp.concatenate([i_k, i_w], -1), ((0, 0), (0, LANES - IDX_DIM - IDX_HEADS)))
    pad = jnp.zeros((d, (O_CZ - ODD_PREP_UNITS) * LANES), BF16)
    out = jnp.concatenate([c_q, c_k, c_v, i_q, m_q, ikw, pad, c_z, m_z], axis=-1)
    assert out.shape == (d, ODD_UNITS * LANES)
    return out


def _overlap_matrix_t(s, n_cmp_pad):
    n_cmp = (s - NSA_CMP_LEN) // NSA_CMP_STRIDE + 1
    n_sel = s // NSA_SEL_LEN
    cmp_start = np.arange(n_cmp) * NSA_CMP_STRIDE
    sel_start = np.arange(n_sel) * NSA_SEL_LEN
    ov = np.clip(np.minimum(cmp_start[:, None] + NSA_CMP_LEN, sel_start[None, :] + NSA_SEL_LEN)
                 - np.maximum(cmp_start[:, None], sel_start[None, :]), 0, None) / NSA_CMP_LEN
    full = np.zeros((n_sel, n_cmp_pad), np.float32)
    full[:, :n_cmp] = ov.T
    return jnp.asarray(full, dtype=BF16)


def _score_bound(dim, gain_q, gain_k, scale):
    return dim * jnp.max(jnp.abs(gain_q)) * jnp.max(jnp.abs(gain_k)) * (scale * LOG2E * 1.02)


def _attend(bound, fn, *operands):
    return lax.cond(bound <= FAST_LOG2_BOUND,
                    lambda *a: fn(*a, bounded=True), lambda *a: fn(*a, bounded=False), *operands)


def kernel(x, mem, norm_gain, mem_norm_gain, mem_w_kv, mem_qk_gain, w_out, even_w_in, nsa_qk_gain, nsa_cmp_pos,
           nsa_cmp_w1, nsa_cmp_w2, diff_qk_gain, diff_lambda, diff_subln_gain, odd_w_in, dsa_qk_gain):
    b, s, d = x.shape
    assert d == D_MODEL and s % TK == 0 and s >= NSA_WINDOW + TQ_SPARSE
    pos = jnp.arange(s)
    cos, sin = _rope_tables(pos, HEAD_DIM // 2, 1)
    cos64, sin64 = _rope_tables(pos, DIFF_QK_DIM // 2, 2)
    n_cmp_pad = s // NSA_CMP_STRIDE
    cmp_last = jnp.arange(n_cmp_pad) * NSA_CMP_STRIDE + NSA_CMP_LEN - 1
    cos_c, sin_c = _rope_tables(cmp_last, HEAD_DIM // 2, 1)
    ovlt = _overlap_matrix_t(s, n_cmp_pad)
    ltri = jnp.asarray(np.tril(np.ones((TK, TK), np.float32), -1), dtype=BF16)
    wbias = _window_bias(TQ_SPARSE)

    mk_all, mv_all = _memkv(mem, mem_norm_gain, mem_w_kv.astype(BF16), mem_qk_gain[:, 1:2, :])
    w_out_b = w_out.astype(BF16)

    x2 = x.reshape(b * s, d)
    for i in range(DEPTH):
        mem_q_gain = mem_qk_gain[i, 0:1, :]
        if i % 2 == 0:
            e = i // 2
            proj = _proj(x2, norm_gain[i], _even_weight(even_w_in[e]), 768)
            dg2 = jnp.tile(diff_qk_gain[e], (1, 2))
            qa, ks, kw, vst, vwt, craw, bq, bk, bvt, mq = _prep_even(
                proj, b, s, cos, sin, cos64, sin64, nsa_qk_gain[e], dg2, mem_q_gain)
            w1 = nsa_cmp_w1[e].reshape(2, NSA_CMP_LEN, HEAD_DIM, NSA_CMP_HIDDEN).astype(BF16)
            kc, vct = _compress(craw, nsa_cmp_pos[e], w1, nsa_cmp_w2[e].astype(BF16),
                                nsa_qk_gain[e, 1:2, :], cos_c, sin_c)
            y_a = _attend(_score_bound(HEAD_DIM, nsa_qk_gain[e, 0], nsa_qk_gain[e, 1:4], HEAD_DIM ** -0.5),
                          functools.partial(_nsa, b=b, s=s), qa, ks, vst, kw, vwt, kc, vct, ovlt, wbias, proj)
            lambda_init = 0.8 - 0.6 * math.exp(-0.3 * i)
            y_b = _attend(_score_bound(DIFF_QK_DIM, diff_qk_gain[e, 0], diff_qk_gain[e, 1], DIFF_QK_DIM ** -0.5),
                          functools.partial(_diff, b=b, s=s, lambda_init=lambda_init),
                          bq, bk, bvt, diff_lambda[e], diff_subln_gain[e].reshape(1, LANES), proj)
            y_m = _memattn(mq, mk_all[i], mv_all[i], proj, b, s, E_MZ)
            ys = [y_a, y_b, y_m]
        else:
            o = i // 2
            proj = _proj(x2, norm_gain[i], _odd_weight(odd_w_in[o]), 512)
            cq, ck, cvt, iq, ik, mq = _prep_odd(proj, b, s, cos, sin, cos64, sin64, dsa_qk_gain[o], mem_q_gain)
            y_c = _attend(_score_bound(HEAD_DIM, dsa_qk_gain[o, 0], dsa_qk_gain[o, 1], HEAD_DIM ** -0.5),
                          functools.partial(_dsa, b=b, s=s), cq, ck, cvt, iq, ik, ltri, proj)
            y_m = _memattn(mq, mk_all[i], mv_all[i], proj, b, s, O_MZ)
            ys = [y_c, y_m]
        x2 = _outproj(x2, ys, w_out_b[i])
    return x2.reshape(b, s, d)
```

```python
import functools
import math

import jax
import jax.numpy as jnp
import numpy as np
from jax import lax
from jax.experimental import pallas as pl
from jax.experimental.pallas import tpu as pltpu

F32 = jnp.float32
BF16 = jnp.bfloat16

D_MODEL = 2048
DEPTH = 4
HEAD_DIM = 128
ROPE_THETA = 10000.0
EPS = 1e-6
MASKED = -1e30
FORCED = 1e9
NSA_HEADS = 8
NSA_KV_GROUPS = 2
NSA_CMP_LEN = 32
NSA_CMP_STRIDE = 16
NSA_CMP_HIDDEN = 256
NSA_SEL_LEN = 64
NSA_SEL_TOPN = 16
NSA_WINDOW = 512
DIFF_HEADS = 4
DIFF_QK_DIM = 64
DSA_HEADS = 12
DSA_KV_HEADS = 4
IDX_HEADS = 16
IDX_DIM = 64
DSA_TOPK_MAX = 256
MEM_HEADS = 4

LANES = 128
SUBLANES = 8
VMEM_LIMIT_BYTES = 56 * 1024 * 1024
LOG2E = 1.4426950408889634

PREP_ROWS = 256
TQ_SPARSE = 128
TQ_DIFF = 256
DIFF_QBLOCKS = 2
NSA_QBLOCKS = 4
TK = 512
VT_BLOCK = 256
NEG_INIT = -1e30
NEG_MASK = -2e30
FAST_LOG2_BOUND = 40.0

EVEN_UNITS = 54
E_AQ, E_AKV, E_BQ, E_BK, E_BV, E_MQ, E_AZ, E_BZ, E_MZ, E_AG = 0, 8, 20, 24, 28, 32, 36, 44, 48, 52
EVEN_PREP_UNITS = E_AZ
ODD_UNITS = 52
O_CQ, O_CK, O_CV, O_IQ, O_MQ, O_IKW, O_CZ, O_MZ = 0, 12, 16, 20, 28, 32, 36, 48
ODD_PREP_UNITS = O_IKW + 1


def _cparams(sem):
    return pltpu.CompilerParams(dimension_semantics=sem, vmem_limit_bytes=VMEM_LIMIT_BYTES)


def _dot(a, b):
    return jnp.dot(a, b, preferred_element_type=F32)


def _dot_nt(a, b):
    return lax.dot_general(a, b, (((1,), (1,)), ((), ())), preferred_element_type=F32)


def _silu(x):
    return x * jax.nn.sigmoid(x)


def _rms_norm(x, gain):
    return x * lax.rsqrt(jnp.mean(x * x, axis=-1, keepdims=True) + EPS) * gain


def _rms_norm_halves(x, gain):
    lo = lax.broadcasted_iota(jnp.int32, x.shape, 1) < 64
    xx = x * x
    s_lo = jnp.sum(jnp.where(lo, xx, 0.0), axis=-1, keepdims=True)
    s_hi = jnp.sum(jnp.where(lo, 0.0, xx), axis=-1, keepdims=True)
    ms = jnp.where(lo, s_lo, s_hi) * (1.0 / 64.0)
    return x * lax.rsqrt(ms + EPS) * gain


def _partner(x, half):
    n = x.shape[-1]
    lane = lax.broadcasted_iota(jnp.int32, x.shape, 1)
    if 2 * half == n:
        return pltpu.roll(x, half, 1)
    a = pltpu.roll(x, half, 1)
    b = pltpu.roll(x, n - half, 1)
    src_a = pltpu.roll(lane, half, 1)
    want = jnp.where((lane & (2 * half - 1)) < half, lane + half, lane - half)
    return jnp.where(src_a == want, a, b)


def _rope(x, cos, sin_signed, half):
    return x * cos + _partner(x, half) * sin_signed


def _transpose_bf16(x):
    n = x.shape[1]
    eye = (lax.broadcasted_iota(jnp.int32, (n, n), 0) == lax.broadcasted_iota(jnp.int32, (n, n), 1)).astype(BF16)
    return _dot_nt(eye, x.astype(BF16)).astype(BF16)


def _fold8(x, op, short_chains=False):
    rows, w = x.shape
    if short_chains:
        x = op(x.reshape(SUBLANES, rows // SUBLANES, w), axis=0)
        rows = rows // SUBLANES
    return op(x.reshape(rows // SUBLANES, SUBLANES, w), axis=0)


def _col_max(x):
    return jnp.max(_fold8(x, jnp.max), axis=0, keepdims=True)


def _col_sum(x):
    return jnp.sum(_fold8(x, jnp.sum), axis=0, keepdims=True)


def _proj_kernel(x_ref, g_ref, w_ref, o_ref, hn_ref):
    @pl.when(pl.program_id(1) == 0)
    def _():
        x = x_ref[...]
        hn_ref[...] = _rms_norm(x, g_ref[...]).astype(BF16)

    o_ref[...] = _dot(hn_ref[...], w_ref[...])


def _proj(x2, gain, w, tn):
    n, d = x2.shape
    c = w.shape[1]
    tm = min(1024, n)
    return pl.pallas_call(
        _proj_kernel,
        grid=(n // tm, c // tn),
        in_specs=[
            pl.BlockSpec((tm, d), lambda i, j: (i, 0)),
            pl.BlockSpec((1, d), lambda i, j: (0, 0)),
            pl.BlockSpec((d, tn), lambda i, j: (0, j)),
        ],
        out_specs=pl.BlockSpec((tm, tn), lambda i, j: (i, j)),
        out_shape=jax.ShapeDtypeStruct((n, c), F32),
        scratch_shapes=[pltpu.VMEM((tm, d), BF16)],
        compiler_params=_cparams(("arbitrary", "arbitrary")),
        name="proj",
    )(x2, gain.reshape(1, d), w)


def _prep_even_kernel(p_ref, cos_ref, sin_ref, cos64_ref, sin64_ref, ng_ref, dg_ref, mg_ref,
                      qa_ref, ks_ref, kw_ref, vst_ref, vwt_ref, craw_ref, bq_ref, bk_ref, bvt_ref, mq_ref):
    cos, sin = cos_ref[...], sin_ref[...]
    cos64, sin64 = cos64_ref[...], sin64_ref[...]
    rep = NSA_HEADS // NSA_KV_GROUPS
    tq = TQ_SPARSE

    def unit(u):
        return p_ref[:, u * LANES:(u + 1) * LANES]

    for h in range(NSA_HEADS):
        q = _rope(_rms_norm(unit(E_AQ + h), ng_ref[0:1, :]), cos, sin, 64).astype(BF16)
        g, r = divmod(h, rep)
        for qb in range(PREP_ROWS // tq):
            qa_ref[0, g, qb, r * tq:(r + 1) * tq, :] = q[qb * tq:(qb + 1) * tq, :]
    for g in range(NSA_KV_GROUPS):
        craw_ref[0, 0, g] = unit(E_AKV + 0 + g)
        craw_ref[0, 1, g] = unit(E_AKV + 2 + g)
        ks_ref[0, g] = _rope(_rms_norm(unit(E_AKV + 4 + g), ng_ref[2:3, :]), cos, sin, 64).astype(BF16)
        vst_ref[0, g, 0] = _transpose_bf16(unit(E_AKV + 6 + g))
        kw_ref[0, g] = _rope(_rms_norm(unit(E_AKV + 8 + g), ng_ref[3:4, :]), cos, sin, 64).astype(BF16)
        vw = unit(E_AKV + 10 + g)
        for kb in range(PREP_ROWS // LANES):
            vwt_ref[0, g, kb] = _transpose_bf16(vw[kb * LANES:(kb + 1) * LANES, :])
    lo = lax.broadcasted_iota(jnp.int32, cos.shape, 1) < 64
    for h in range(DIFF_HEADS):
        q = _rope(_rms_norm_halves(unit(E_BQ + h), dg_ref[0:1, :]), cos64, sin64, 32)
        bq_ref[0, h, 0, 0:PREP_ROWS, :] = jnp.where(lo, q, 0.0).astype(BF16)
        bq_ref[0, h, 0, PREP_ROWS:2 * PREP_ROWS, :] = jnp.where(lo, 0.0, q).astype(BF16)
        bk_ref[0, h] = _rope(_rms_norm_halves(unit(E_BK + h), dg_ref[1:2, :]), cos64, sin64, 32).astype(BF16)
        bvt_ref[0, h, 0] = _transpose_bf16(unit(E_BV + h))
    for h in range(MEM_HEADS):
        mq_ref[0, h] = _rms_norm(unit(E_MQ + h), mg_ref[...]).astype(BF16)


def _prep_even(proj, b, s, cos, sin, cos64, sin64, nsa_gain, diff_gain2, mem_gain):
    ts = PREP_ROWS
    assert ts == TQ_DIFF == VT_BLOCK
    nb = s // ts
    c = EVEN_PREP_UNITS * LANES
    g = NSA_KV_GROUPS
    rep = NSA_HEADS // g
    row = lambda bb, i: (i, 0)
    const = lambda bb, i: (0, 0)
    seq = lambda n: pl.BlockSpec((1, n, ts, LANES), lambda bb, i: (bb, 0, i, 0))
    blk5 = lambda n, k, r, cdim: pl.BlockSpec((1, n, k, r, cdim), lambda bb, i: (bb, 0, i, 0, 0))
    return pl.pallas_call(
        _prep_even_kernel,
        grid=(b, nb),
        in_specs=[
            pl.BlockSpec((ts, c), lambda bb, i: (bb * nb + i, 0)),
            pl.BlockSpec((ts, LANES), row), pl.BlockSpec((ts, LANES), row),
            pl.BlockSpec((ts, LANES), row), pl.BlockSpec((ts, LANES), row),
            pl.BlockSpec((4, LANES), const), pl.BlockSpec((2, LANES), const), pl.BlockSpec((1, LANES), const),
        ],
        out_specs=[
            blk5(g, ts // TQ_SPARSE, rep * TQ_SPARSE, LANES),
            seq(g), seq(g),
            blk5(g, 1, LANES, VT_BLOCK),
            blk5(g, ts // LANES, LANES, LANES),
            pl.BlockSpec((1, 2, g, ts, LANES), lambda bb, i: (bb, 0, 0, i, 0)),
            blk5(DIFF_HEADS, 1, 2 * TQ_DIFF, LANES),
            seq(DIFF_HEADS),
            blk5(DIFF_HEADS, 1, LANES, VT_BLOCK),
            seq(MEM_HEADS),
        ],
        out_shape=[
            jax.ShapeDtypeStruct((b, g, s // TQ_SPARSE, rep * TQ_SPARSE, LANES), BF16),
            jax.ShapeDtypeStruct((b, g, s, LANES), BF16),
            jax.ShapeDtypeStruct((b, g, s, LANES), BF16),
            jax.ShapeDtypeStruct((b, g, s // VT_BLOCK, LANES, VT_BLOCK), BF16),
            jax.ShapeDtypeStruct((b, g, s // LANES, LANES, LANES), BF16),
            jax.ShapeDtypeStruct((b, 2, g, s, LANES), F32),
            jax.ShapeDtypeStruct((b, DIFF_HEADS, s // TQ_DIFF, 2 * TQ_DIFF, LANES), BF16),
            jax.ShapeDtypeStruct((b, DIFF_HEADS, s, LANES), BF16),
            jax.ShapeDtypeStruct((b, DIFF_HEADS, s // VT_BLOCK, LANES, VT_BLOCK), BF16),
            jax.ShapeDtypeStruct((b, MEM_HEADS, s, LANES), BF16),
        ],
        compiler_params=_cparams(("arbitrary", "arbitrary")),
        name="prep_even",
    )(proj, cos, sin, cos64, sin64, nsa_gain, diff_gain2, mem_gain)


def _compress_kernel(x_ref, pe_ref, w1_ref, w2_ref, g_ref, cos_ref, sin_ref, kc_ref, vct_ref, pad_ref, *, s, n_pad):
    for kind in range(2):
        pad_ref[0:s, :] = x_ref[0, kind, 0]
        pad_ref[s:s + NSA_CMP_LEN, :] = jnp.zeros((NSA_CMP_LEN, LANES), F32)
        acc = jnp.zeros((n_pad, NSA_CMP_HIDDEN), F32)
        for l in range(NSA_CMP_LEN):
            rows = pad_ref[pl.ds(l, n_pad, stride=NSA_CMP_STRIDE), :] + pe_ref[kind, l:l + 1, :]
            acc = acc + _dot(rows.astype(BF16), w1_ref[kind, l])
        out = _dot(_silu(acc).astype(BF16), w2_ref[kind])
        if kind == 0:
            kc_ref[0, 0] = _rope(_rms_norm(out, g_ref[...]), cos_ref[...], sin_ref[...], 64).astype(BF16)
        else:
            vct_ref[0, 0] = out.T.astype(BF16)


def _compress(craw, pe, w1, w2, gain, cos_c, sin_c):
    b, _, g, s, _ = craw.shape
    n_pad = s // NSA_CMP_STRIDE
    kern = functools.partial(_compress_kernel, s=s, n_pad=n_pad)
    whole = lambda shape: pl.BlockSpec(shape, lambda bb, gg: (0,) * len(shape))
    return pl.pallas_call(
        kern,
        grid=(b, g),
        in_specs=[
            pl.BlockSpec((1, 2, 1, s, LANES), lambda bb, gg: (bb, 0, gg, 0, 0)),
            whole((2, NSA_CMP_LEN, LANES)),
            whole((2, NSA_CMP_LEN, LANES, NSA_CMP_HIDDEN)),
            whole((2, NSA_CMP_HIDDEN, LANES)),
            whole((1, LANES)), whole((n_pad, LANES)), whole((n_pad, LANES)),
        ],
        out_specs=[
            pl.BlockSpec((1, 1, n_pad, LANES), lambda bb, gg: (bb, gg, 0, 0)),
            pl.BlockSpec((1, 1, LANES, n_pad), lambda bb, gg: (bb, gg, 0, 0)),
        ],
        out_shape=[
            jax.ShapeDtypeStruct((b, g, n_pad, LANES), BF16),
            jax.ShapeDtypeStruct((b, g, LANES, n_pad), BF16),
        ],
        scratch_shapes=[pltpu.VMEM((s + NSA_CMP_LEN, LANES), F32)],
        compiler_params=_cparams(("arbitrary", "arbitrary")),
        name="nsa_compress",
    )(craw, pe, w1, w2, gain, cos_c, sin_c)


def _softmax_parts_t(st, bounded):
    if not bounded:
        st = st - _col_max(st)
    e = jnp.exp2(st)
    return e.astype(BF16), 1.0 / jnp.maximum(_col_sum(e), 1e-30)


def _online_init(m_ref, l_ref, acc_ref):
    m_ref[...] = jnp.full(m_ref.shape, NEG_INIT, F32)
    l_ref[...] = jnp.zeros(l_ref.shape, F32)
    acc_ref[...] = jnp.zeros(acc_ref.shape, F32)


def _online_step(idx, st, pv, m_ref, l_ref, acc_ref, bounded):
    if bounded:
        p = jnp.exp2(st)
        l_ref[idx] = l_ref[idx] + _fold8(p, jnp.sum)
        acc_ref[idx] = acc_ref[idx] + pv(p.astype(BF16))
        return
    m_old = m_ref[idx]
    m_new = jnp.maximum(m_old, _col_max(st))
    alpha = jnp.exp2(m_old - m_new)
    p = jnp.exp2(st - m_new)
    l_ref[idx] = alpha * l_ref[idx] + _fold8(p, jnp.sum)
    acc_ref[idx] = alpha * acc_ref[idx] + pv(p.astype(BF16))
    m_ref[idx] = m_new


def _online_result(idx, l_ref, acc_ref):
    return acc_ref[idx] / jnp.maximum(jnp.sum(l_ref[idx], axis=0, keepdims=True), 1e-30)


def _flash_pairs(n_chunks, n_groups, qk, mask, soft, sa_ref, sb_ref):
    for g in range(n_groups):
        qk(0, g, sa_ref)
    n_pairs = (n_chunks - 1) // 2

    def pair(j, carry):
        a = 2 * j
        bias_a = mask(a, False)
        bias_b = mask(a + 1, False)
        for g in range(n_groups):
            qk(a + 1, g, sb_ref)
            soft(a, g, sa_ref, bias_a)
            qk(a + 2, g, sa_ref)
            soft(a + 1, g, sb_ref, bias_b)
        return carry

    lax.fori_loop(0, n_pairs, pair, 0)
    e = 2 * n_pairs

    @pl.when(e + 1 < n_chunks)
    def _():
        bias_a = mask(e, True)
        bias_b = mask(e + 1, True)
        for g in range(n_groups):
            qk(e + 1, g, sb_ref)
            soft(e, g, sa_ref, bias_a)
            soft(e + 1, g, sb_ref, bias_b)

    @pl.when(e + 1 >= n_chunks)
    def _():
        bias_a = mask(e, True)
        for g in range(n_groups):
            soft(e, g, sa_ref, bias_a)


def _pv_blocks(vt_ref, lead, chunk, tk):
    per = tk // VT_BLOCK

    def pv(p):
        out = _dot(vt_ref[lead + (chunk * per,)], p[0:VT_BLOCK])
        for i in range(1, per):
            out = out + _dot(vt_ref[lead + (chunk * per + i,)], p[i * VT_BLOCK:(i + 1) * VT_BLOCK])
        return out

    return pv


def _top_n_mask_t(scores, n):
    row = lax.broadcasted_iota(jnp.int32, scores.shape, 0).astype(F32)
    height = float(scores.shape[0])
    work = scores
    sel = jnp.zeros(scores.shape, F32)
    for _ in range(n):
        m = jnp.max(work, axis=0, keepdims=True)
        first = jnp.min(jnp.where(work == m, row, height), axis=0, keepdims=True)
        pick = row == first
        sel = jnp.where(pick, 1.0, sel)
        work = jnp.where(pick, -jnp.inf, work)
    return sel


def _nsa_kernel(q_ref, kc_ref, vct_ref, ks_ref, vst_ref, kw_ref, vwt_ref, ovlt_ref, *rest, tq, tk, n_qb, top_n, bounded):
    wbias_refs = rest[:n_qb]
    gl_ref, z_ref, o_ref, m_ref, l_ref, acc_ref, sa_ref, sb_ref, selt_ref, oct_ref, owt_ref = rest[n_qb:]
    rep = NSA_HEADS // NSA_KV_GROUPS
    w = rep * tq
    scale = HEAD_DIM ** -0.5
    step0 = pl.program_id(2) * (n_qb * tq)
    lane_t = lax.broadcasted_iota(jnp.int32, (1, w), 1) & (tq - 1)
    n_cmp_pad = kc_ref.shape[2]
    n_sel = ovlt_ref.shape[0]
    sel_shift = NSA_SEL_LEN.bit_length() - 1
    span = NSA_WINDOW + tq

    for qb in range(n_qb):
        q0 = step0 + qb * tq
        t_row = q0 + lane_t
        t_one = t_row[:, 0:tq]
        q = q_ref[0, 0, qb]

        cmp_last = lax.broadcasted_iota(jnp.int32, (n_cmp_pad, w), 0) * NSA_CMP_STRIDE + (NSA_CMP_LEN - 1)
        st = jnp.where(cmp_last <= t_row, _dot_nt(kc_ref[0, 0], q) * (scale * LOG2E), NEG_MASK)
        p, inv = _softmax_parts_t(st, bounded)
        inv = jnp.where(t_row >= NSA_CMP_LEN - 1, inv, 0.0)
        oct_ref[qb] = _dot(vct_ref[0, 0], p) * inv
        imp_heads = _dot(ovlt_ref[...], p) * inv
        imp = imp_heads[:, 0:tq]
        for r in range(1, rep):
            imp = imp + imp_heads[:, r * tq:(r + 1) * tq]

        j = lax.broadcasted_iota(jnp.int32, (n_sel, tq), 0)
        cur = jnp.right_shift(t_one, sel_shift)
        visible = j <= cur
        forced = (j == 0) | (j >= cur - 1)
        imp = jnp.where(visible, jnp.where(forced, FORCED, imp), MASKED)
        selt_ref[qb] = _top_n_mask_t(imp, top_n)

        start = pl.multiple_of(jnp.maximum(q0 - NSA_WINDOW, 0), tq)
        st = (_dot_nt(kw_ref[0, 0, pl.ds(start, span), :], q) * (scale * LOG2E)
              + jnp.concatenate([wbias_refs[qb][0]] * rep, axis=1))
        p, inv = _softmax_parts_t(st, bounded)
        blk0 = start // LANES
        ow = _dot(vwt_ref[0, 0, blk0], p[0:LANES])
        for i in range(1, span // LANES):
            ow = ow + _dot(vwt_ref[0, 0, blk0 + i], p[i * LANES:(i + 1) * LANES])
        owt_ref[qb] = ow * inv

    _online_init(m_ref, l_ref, acc_ref)
    n_chunks = (step0 + n_qb * tq + tk - 1) // tk
    blocks_per_chunk = tk // NSA_SEL_LEN

    def qk(c, g, dst):
        dst[g] = _dot_nt(ks_ref[0, 0, pl.ds(pl.multiple_of(c * tk, tk), tk), :], q_ref[0, 0, g]) * (scale * LOG2E)

    def mask(c, tail):
        biases = []
        for g in range(n_qb):
            rows = selt_ref[g, pl.ds(pl.multiple_of(c * blocks_per_chunk, blocks_per_chunk), blocks_per_chunk), :]
            ok = jnp.concatenate([jnp.broadcast_to(rows[i:i + 1, :], (NSA_SEL_LEN, tq))
                                  for i in range(blocks_per_chunk)], axis=0) > 0.5
            if tail:
                key = c * tk + lax.broadcasted_iota(jnp.int32, (tk, tq), 0)
                ok = ok & (key <= step0 + g * tq + lane_t[:, 0:tq])
            biases.append(jnp.concatenate([jnp.where(ok, 0.0, NEG_MASK)] * rep, axis=1))
        return biases

    def soft(c, g, src, bias):
        _online_step(g, src[g] + bias[g], _pv_blocks(vst_ref, (0, 0), c, tk), m_ref, l_ref, acc_ref, bounded)

    _flash_pairs(n_chunks, n_qb, qk, mask, soft, sa_ref, sb_ref)

    for qb in range(n_qb):
        rows = slice(qb * tq, (qb + 1) * tq)
        gates_t = jax.nn.sigmoid(gl_ref[rows, :]).T
        os_t = _online_result(qb, l_ref, acc_ref)
        for r in range(rep):
            sl = slice(r * tq, (r + 1) * tq)
            cols = slice(r * LANES, (r + 1) * LANES)
            out_t = (gates_t[3 * r:3 * r + 1, :] * oct_ref[qb, :, sl] + gates_t[3 * r + 1:3 * r + 2, :] * os_t[:, sl]
                     + gates_t[3 * r + 2:3 * r + 3, :] * owt_ref[qb, :, sl])
            o_ref[rows, cols] = (out_t.T * _silu(z_ref[rows, cols])).astype(BF16)


def _window_bias(tq):
    span = NSA_WINDOW + tq
    n_early = NSA_WINDOW // tq
    row = np.arange(span)[:, None]
    lane = np.arange(tq)[None, :]
    tables = []
    for i in range(n_early + 1):
        t = i * tq + lane
        s_pos = (0 if i < n_early else t[0, 0] - NSA_WINDOW) + row
        ok = (s_pos <= t) & (s_pos > t - NSA_WINDOW)
        tables.append(np.where(ok, 0.0, NEG_MASK))
    return jnp.asarray(np.stack(tables), dtype=F32)


def _nsa(qa, ks, vst, kw, vwt, kc, vct, ovlt, wbias, proj, b, s, bounded):
    tq, tk, n_qb = TQ_SPARSE, TK, NSA_QBLOCKS
    assert tk % (n_qb * tq) == 0
    rows = n_qb * tq
    nq = s // rows
    g = NSA_KV_GROUPS
    rep = NSA_HEADS // g
    w = rep * tq
    n_cmp_pad = kc.shape[2]
    n_sel = s // NSA_SEL_LEN
    last_bias = wbias.shape[0] - 1
    kern = functools.partial(_nsa_kernel, tq=tq, tk=tk, n_qb=n_qb, top_n=min(NSA_SEL_TOPN, n_sel), bounded=bounded)
    per_group = lambda shape: pl.BlockSpec((1, 1) + shape, lambda bb, gg, i: (bb, gg) + (0,) * len(shape))
    bias_spec = lambda qb: pl.BlockSpec((1,) + wbias.shape[1:],
                                        lambda bb, gg, i: (jnp.minimum(n_qb * i + qb, last_bias), 0, 0))
    return pl.pallas_call(
        kern,
        grid=(b, g, nq),
        in_specs=[
            pl.BlockSpec((1, 1, n_qb, w, LANES), lambda bb, gg, i: (bb, gg, i, 0, 0)),
            per_group((n_cmp_pad, LANES)), per_group((LANES, n_cmp_pad)),
            per_group((s, LANES)), per_group((s // VT_BLOCK, LANES, VT_BLOCK)),
            per_group((s, LANES)), per_group((s // LANES, LANES, LANES)),
            pl.BlockSpec((n_sel, n_cmp_pad), lambda bb, gg, i: (0, 0)),
        ] + [bias_spec(qb) for qb in range(n_qb)] + [
            pl.BlockSpec((rows, LANES), lambda bb, gg, i: (bb * nq + i, E_AG + gg)),
            pl.BlockSpec((rows, rep * LANES), lambda bb, gg, i: (bb * nq + i, E_AZ // rep + gg)),
        ],
        out_specs=pl.BlockSpec((rows, rep * LANES), lambda bb, gg, i: (bb * nq + i, gg)),
        out_shape=jax.ShapeDtypeStruct((b * s, NSA_HEADS * LANES), BF16),
        scratch_shapes=[
            pltpu.VMEM((n_qb, 1, w), F32), pltpu.VMEM((n_qb, SUBLANES, w), F32), pltpu.VMEM((n_qb, LANES, w), F32),
            pltpu.VMEM((n_qb, tk, w), F32), pltpu.VMEM((n_qb, tk, w), F32),
            pltpu.VMEM((n_qb, n_sel, tq), F32), pltpu.VMEM((n_qb, LANES, w), F32), pltpu.VMEM((n_qb, LANES, w), F32),
        ],
        compiler_params=_cparams(("arbitrary", "arbitrary", "arbitrary")),
        name="nsa_attention",
    )(qa, kc, vct, ks, vst, kw, vwt, ovlt, *([wbias] * n_qb), proj, proj)


def _diff_kernel(q_ref, k_ref, vt_ref, lam_ref, sg_ref, z_ref, o_ref, m_ref, l_ref, acc_ref, sa_ref, sb_ref,
                 *, tq, tk, n_qb, lambda_init, bounded):
    w = 2 * tq
    q0 = pl.program_id(2) * (n_qb * tq)
    lane_t = lax.broadcasted_iota(jnp.int32, (1, w), 1) & (tq - 1)
    _online_init(m_ref, l_ref, acc_ref)
    n_chunks = (q0 + n_qb * tq + tk - 1) // tk

    def qk(c, g, dst):
        dst[g] = (_dot_nt(k_ref[0, 0, pl.ds(pl.multiple_of(c * tk, tk), tk), :], q_ref[0, 0, g])
                  * (DIFF_QK_DIM ** -0.5 * LOG2E))

    def mask(c, tail):
        if not tail:
            return None
        key = c * tk + lax.broadcasted_iota(jnp.int32, (tk, w), 0)
        return [jnp.where(key <= q0 + g * tq + lane_t, 0.0, NEG_MASK) for g in range(n_qb)]

    def soft(c, g, src, bias):
        st = src[g] if bias is None else src[g] + bias[g]
        _online_step(g, st, _pv_blocks(vt_ref, (0, 0), c, tk), m_ref, l_ref, acc_ref, bounded)

    _flash_pairs(n_chunks, n_qb, qk, mask, soft, sa_ref, sb_ref)

    lv = lam_ref[...]
    lam = (jnp.exp(jnp.sum(lv[0:1] * lv[1:2], axis=-1, keepdims=True))
           - jnp.exp(jnp.sum(lv[2:3] * lv[3:4], axis=-1, keepdims=True)) + lambda_init)
    for g in range(n_qb):
        rows = slice(g * tq, (g + 1) * tq)
        o_t = _online_result(g, l_ref, acc_ref)
        o = (o_t[:, 0:tq] - lam * o_t[:, tq:w]).T
        o = _rms_norm(o, sg_ref[...]) * (1.0 - lambda_init)
        o_ref[rows, :] = (o * _silu(z_ref[rows, :])).astype(BF16)


def _diff(bq, bk, bvt, lam_vecs, subln_gain, proj, b, s, lambda_init, bounded):
    tq, tk, n_qb = TQ_DIFF, TK, DIFF_QBLOCKS
    assert n_qb * tq == tk
    rows = n_qb * tq
    nq = s // rows
    w = 2 * tq
    kern = functools.partial(_diff_kernel, tq=tq, tk=tk, n_qb=n_qb, lambda_init=lambda_init, bounded=bounded)
    return pl.pallas_call(
        kern,
        grid=(b, DIFF_HEADS, nq),
        in_specs=[
            pl.BlockSpec((1, 1, n_qb, w, LANES), lambda bb, h, i: (bb, h, i, 0, 0)),
            pl.BlockSpec((1, 1, s, LANES), lambda bb, h, i: (bb, h, 0, 0)),
            pl.BlockSpec((1, 1, s // VT_BLOCK, LANES, VT_BLOCK), lambda bb, h, i: (bb, h, 0, 0, 0)),
            pl.BlockSpec((4, DIFF_QK_DIM), lambda bb, h, i: (0, 0)),
            pl.BlockSpec((1, LANES), lambda bb, h, i: (0, 0)),
            pl.BlockSpec((rows, LANES), lambda bb, h, i: (bb * nq + i, E_BZ + h)),
        ],
        out_specs=pl.BlockSpec((rows, LANES), lambda bb, h, i: (bb * nq + i, h)),
        out_shape=jax.ShapeDtypeStruct((b * s, DIFF_HEADS * LANES), BF16),
        scratch_shapes=[
            pltpu.VMEM((n_qb, 1, w), F32), pltpu.VMEM((n_qb, SUBLANES, w), F32), pltpu.VMEM((n_qb, LANES, w), F32),
            pltpu.VMEM((n_qb, tk, w), F32), pltpu.VMEM((n_qb, tk, w), F32),
        ],
        compiler_params=_cparams(("arbitrary", "arbitrary", "arbitrary")),
        name="diff_attention",
    )(bq, bk, bvt, lam_vecs, subln_gain, proj)


def _memkv_kernel(mem_ref, mg_ref, w_ref, kg_ref, k_ref, v_ref):
    mem_n = _rms_norm(mem_ref[0], mg_ref[...]).astype(BF16)
    kv = _dot(mem_n, w_ref[0])
    for h in range(MEM_HEADS):
        k_ref[0, 0, h] = _rms_norm(kv[:, h * LANES:(h + 1) * LANES], kg_ref[0]).astype(BF16)
        v_ref[0, 0, h] = kv[:, (MEM_HEADS + h) * LANES:(MEM_HEADS + h + 1) * LANES].astype(BF16)


def _memkv(mem, mem_gain, w_kv, k_gain):
    b, m, d = mem.shape
    depth = w_kv.shape[0]
    c = w_kv.shape[2]
    out = jax.ShapeDtypeStruct((depth, b, MEM_HEADS, m, LANES), BF16)
    ospec = pl.BlockSpec((1, 1, MEM_HEADS, m, LANES), lambda i, bb: (i, bb, 0, 0, 0))
    return pl.pallas_call(
        _memkv_kernel,
        grid=(depth, b),
        in_specs=[
            pl.BlockSpec((1, m, d), lambda i, bb: (bb, 0, 0)),
            pl.BlockSpec((1, d), lambda i, bb: (0, 0)),
            pl.BlockSpec((1, d, c), lambda i, bb: (i, 0, 0)),
            pl.BlockSpec((1, 1, LANES), lambda i, bb: (i, 0, 0)),
        ],
        out_specs=[ospec, ospec],
        out_shape=[out, out],
        compiler_params=_cparams(("arbitrary", "arbitrary")),
        name="mem_kv",
    )(mem, mem_gain.reshape(1, d), w_kv, k_gain)


def _memattn_kernel(q_ref, k_ref, v_ref, z_ref, o_ref):
    scale = HEAD_DIM ** -0.5
    for h in range(MEM_HEADS):
        s = _dot_nt(q_ref[0, h], k_ref[0, h]) * scale
        e = jnp.exp(s - jnp.max(s, axis=-1, keepdims=True))
        p = e / jnp.sum(e, axis=-1, keepdims=True)
        o = _dot(p.astype(BF16), v_ref[0, h])
        o_ref[:, h * LANES:(h + 1) * LANES] = (o * _silu(z_ref[:, h * LANES:(h + 1) * LANES])).astype(BF16)


def _memattn(mq, mk, mv, proj, b, s, z_unit):
    tq = min(512, s)
    nq = s // tq
    m = mk.shape[2]
    kvspec = pl.BlockSpec((1, MEM_HEADS, m, LANES), lambda bb, i: (bb, 0, 0, 0))
    return pl.pallas_call(
        _memattn_kernel,
        grid=(b, nq),
        in_specs=[
            pl.BlockSpec((1, MEM_HEADS, tq, LANES), lambda bb, i: (bb, 0, i, 0)),
            kvspec, kvspec,
            pl.BlockSpec((tq, MEM_HEADS * LANES), lambda bb, i: (bb * nq + i, z_unit // MEM_HEADS)),
        ],
        out_specs=pl.BlockSpec((tq, MEM_HEADS * LANES), lambda bb, i: (bb * nq + i, 0)),
        out_shape=jax.ShapeDtypeStruct((b * s, MEM_HEADS * LANES), BF16),
        compiler_params=_cparams(("arbitrary", "arbitrary")),
        name="mem_attention",
    )(mq, mk, mv, proj)


def _prep_odd_kernel(p_ref, cos_ref, sin_ref, cos64_ref, sin64_ref, cg_ref, mg_ref,
                     cq_ref, ck_ref, cvt_ref, iq_ref, ik_ref, mq_ref):
    cos, sin = cos_ref[...], sin_ref[...]
    cos64, sin64 = cos64_ref[...], sin64_ref[...]
    rep = DSA_HEADS // DSA_KV_HEADS
    tq = TQ_SPARSE
    n_qb = PREP_ROWS // tq

    def unit(u):
        return p_ref[:, u * LANES:(u + 1) * LANES]

    for h in range(DSA_HEADS):
        q = _rope(_rms_norm(unit(O_CQ + h), cg_ref[0:1, :]), cos, sin, 64).astype(BF16)
        g, r = divmod(h, rep)
        for qb in range(n_qb):
            cq_ref[0, g, qb, r * tq:(r + 1) * tq, :] = q[qb * tq:(qb + 1) * tq, :]
    for h in range(DSA_KV_HEADS):
        ck_ref[0, h] = _rope(_rms_norm(unit(O_CK + h), cg_ref[1:2, :]), cos, sin, 64).astype(BF16)
        cvt_ref[0, h, 0] = _transpose_bf16(unit(O_CV + h))
    lo = lax.broadcasted_iota(jnp.int32, cos.shape, 1) < 64
    for u in range(IDX_HEADS // 2):
        x = _rope(unit(O_IQ + u), cos64, sin64, 32)
        even = jnp.where(lo, x, 0.0).astype(BF16)
        odd = jnp.where(lo, pltpu.roll(x, 64, 1), 0.0).astype(BF16)
        for qb in range(n_qb):
            iq_ref[0, qb, (2 * u) * tq:(2 * u + 1) * tq, :] = even[qb * tq:(qb + 1) * tq, :]
            iq_ref[0, qb, (2 * u + 1) * tq:(2 * u + 2) * tq, :] = odd[qb * tq:(qb + 1) * tq, :]
    ik = _rope(unit(O_IKW), cos64, sin64, 32)
    ik_ref[0] = jnp.where(lo, ik, 0.0).astype(BF16)
    for h in range(MEM_HEADS):
        mq_ref[0, h] = _rms_norm(unit(O_MQ + h), mg_ref[...]).astype(BF16)


def _prep_odd(proj, b, s, cos, sin, cos64, sin64, dsa_gain, mem_gain):
    ts = PREP_ROWS
    nb = s // ts
    c = ODD_PREP_UNITS * LANES
    rep = DSA_HEADS // DSA_KV_HEADS
    row = lambda bb, i: (i, 0)
    const = lambda bb, i: (0, 0)
    seq = lambda n: pl.BlockSpec((1, n, ts, LANES), lambda bb, i: (bb, 0, i, 0))
    return pl.pallas_call(
        _prep_odd_kernel,
        grid=(b, nb),
        in_specs=[
            pl.BlockSpec((ts, c), lambda bb, i: (bb * nb + i, 0)),
            pl.BlockSpec((ts, LANES), row), pl.BlockSpec((ts, LANES), row),
            pl.BlockSpec((ts, LANES), row), pl.BlockSpec((ts, LANES), row),
            pl.BlockSpec((2, LANES), const), pl.BlockSpec((1, LANES), const),
        ],
        out_specs=[
            pl.BlockSpec((1, DSA_KV_HEADS, ts // TQ_SPARSE, rep * TQ_SPARSE, LANES), lambda bb, i: (bb, 0, i, 0, 0)),
            seq(DSA_KV_HEADS),
            pl.BlockSpec((1, DSA_KV_HEADS, 1, LANES, VT_BLOCK), lambda bb, i: (bb, 0, i, 0, 0)),
            pl.BlockSpec((1, ts // TQ_SPARSE, IDX_HEADS * TQ_SPARSE, LANES), lambda bb, i: (bb, i, 0, 0)),
            pl.BlockSpec((1, ts, LANES), lambda bb, i: (bb, i, 0)),
            seq(MEM_HEADS),
        ],
        out_shape=[
            jax.ShapeDtypeStruct((b, DSA_KV_HEADS, s // TQ_SPARSE, rep * TQ_SPARSE, LANES), BF16),
            jax.ShapeDtypeStruct((b, DSA_KV_HEADS, s, LANES), BF16),
            jax.ShapeDtypeStruct((b, DSA_KV_HEADS, s // VT_BLOCK, LANES, VT_BLOCK), BF16),
            jax.ShapeDtypeStruct((b, s // TQ_SPARSE, IDX_HEADS * TQ_SPARSE, LANES), BF16),
            jax.ShapeDtypeStruct((b, s, LANES), BF16),
            jax.ShapeDtypeStruct((b, MEM_HEADS, s, LANES), BF16),
        ],
        compiler_params=_cparams(("arbitrary", "arbitrary")),
        name="prep_odd",
    )(proj, cos, sin, cos64, sin64, dsa_gain, mem_gain)


def _sortable_key(x):
    bits = pltpu.bitcast(x + 0.0, jnp.int32)
    return jnp.where(bits < 0, bits ^ jnp.int32(0x7FFFFFFF), bits)


def _dsa_kernel(q_ref, k_ref, vt_ref, iq_ref, ik_ref, w_ref, ltri_ref, z_ref, o_ref,
                key_ref, m_ref, l_ref, acc_ref, sa_ref, sb_ref, ties_ref, *, tq, tk, top_k, bounded):
    rep = DSA_HEADS // DSA_KV_HEADS
    scale = HEAD_DIM ** -0.5
    q0 = pl.program_id(1) * tq
    t_row = q0 + lax.broadcasted_iota(jnp.int32, (1, tq), 1)
    n_chunks = (q0 + tq + tk - 1) // tk
    int_min = jnp.int32(-2147483648)
    heads_per_dot = 4

    w_t = (w_ref[...] * (IDX_HEADS ** -0.5 * IDX_DIM ** -0.5)).T

    def score_chunk(c, carry):
        k0 = pl.multiple_of(c * tk, tk)
        ik = ik_ref[0, pl.ds(k0, tk), :]
        sc = jnp.zeros((tk, tq), F32)
        for h0 in range(0, IDX_HEADS, heads_per_dot):
            x = _dot_nt(ik, iq_ref[0, 0, h0 * tq:(h0 + heads_per_dot) * tq, :])
            for hh in range(heads_per_dot):
                h = h0 + hh
                sc = sc + jnp.maximum(x[:, hh * tq:(hh + 1) * tq], 0.0) * w_t[IDX_DIM + h:IDX_DIM + h + 1, :]
        causal = (k0 + lax.broadcasted_iota(jnp.int32, (tk, tq), 0)) <= t_row
        key_ref[c] = _sortable_key(jnp.where(causal, sc, MASKED))
        return carry

    lax.fori_loop(0, n_chunks, score_chunk, 0)

    def count(pred):
        def one(c):
            return _fold8(pred(key_ref[c]).astype(F32), jnp.sum, short_chains=True)

        def four(j, acc):
            return acc + ((one(4 * j) + one(4 * j + 1)) + (one(4 * j + 2) + one(4 * j + 3)))

        acc = lax.fori_loop(0, n_chunks // 4, four, jnp.zeros((SUBLANES, tq), F32))
        acc = lax.fori_loop(n_chunks - n_chunks % 4, n_chunks, lambda c, a: a + one(c), acc)
        return jnp.sum(acc, axis=0, keepdims=True)

    def search(it, state):
        thr_u, above = state
        cand_u = thr_u | jnp.left_shift(jnp.int32(1), 31 - it)
        cand = cand_u ^ int_min
        cnt = count(lambda kk: kk >= cand)
        take = cnt >= top_k
        return jnp.where(take, cand_u, thr_u), jnp.where(take, above, cnt)

    thr_u, above = lax.fori_loop(0, 32, search, (jnp.zeros((1, tq), jnp.int32), jnp.zeros((1, tq), F32)))
    thr = thr_u ^ int_min
    budget = top_k - above

    _online_init(m_ref, l_ref, acc_ref)
    ties_ref[...] = jnp.zeros(ties_ref.shape, F32)

    def qk(c, g, dst):
        dst[g] = _dot_nt(k_ref[0, g, pl.ds(pl.multiple_of(c * tk, tk), tk), :], q_ref[0, g, 0]) * (scale * LOG2E)

    def mask(c, tail):
        keys = key_ref[c]
        tie = keys == thr
        rank = _dot(ltri_ref[...], tie.astype(BF16)) + ties_ref[...]
        ok = (keys > thr) | (tie & (rank < budget))
        if tail:
            ok = ok & ((c * tk + lax.broadcasted_iota(jnp.int32, (tk, tq), 0)) <= t_row)
        ties_ref[...] = ties_ref[...] + _col_sum(tie.astype(F32))
        return jnp.concatenate([jnp.where(ok, 0.0, NEG_MASK)] * rep, axis=1)

    def soft(c, g, src, bias):
        _online_step(g, src[g] + bias, _pv_blocks(vt_ref, (0, g), c, tk), m_ref, l_ref, acc_ref, bounded)

    _flash_pairs(n_chunks, DSA_KV_HEADS, qk, mask, soft, sa_ref, sb_ref)

    for g in range(DSA_KV_HEADS):
        o_t = _online_result(g, l_ref, acc_ref)
        for r in range(rep):
            h = g * rep + r
            o = o_t[:, r * tq:(r + 1) * tq].T
            o_ref[:, h * LANES:(h + 1) * LANES] = (o * _silu(z_ref[:, h * LANES:(h + 1) * LANES])).astype(BF16)


def _dsa(cq, ck, cvt, iq, ik, ltri, proj, b, s, bounded):
    tq, tk = TQ_SPARSE, TK
    nq = s // tq
    g = DSA_KV_HEADS
    rep = DSA_HEADS // g
    w = rep * tq
    kern = functools.partial(_dsa_kernel, tq=tq, tk=tk, top_k=min(DSA_TOPK_MAX, s // 4), bounded=bounded)
    return pl.pallas_call(
        kern,
        grid=(b, nq),
        in_specs=[
            pl.BlockSpec((1, g, 1, w, LANES), lambda bb, i: (bb, 0, i, 0, 0)),
            pl.BlockSpec((1, g, s, LANES), lambda bb, i: (bb, 0, 0, 0), pipeline_mode=pl.Buffered(1)),
            pl.BlockSpec((1, g, s // VT_BLOCK, LANES, VT_BLOCK), lambda bb, i: (bb, 0, 0, 0, 0),
                         pipeline_mode=pl.Buffered(1)),
            pl.BlockSpec((1, 1, IDX_HEADS * tq, LANES), lambda bb, i: (bb, i, 0, 0)),
            pl.BlockSpec((1, s, LANES), lambda bb, i: (bb, 0, 0), pipeline_mode=pl.Buffered(1)),
            pl.BlockSpec((tq, LANES), lambda bb, i: (bb * nq + i, O_IKW)),
            pl.BlockSpec((tk, tk), lambda bb, i: (0, 0)),
            pl.BlockSpec((tq, DSA_HEADS * LANES), lambda bb, i: (bb * nq + i, O_CZ // DSA_HEADS)),
        ],
        out_specs=pl.BlockSpec((tq, DSA_HEADS * LANES), lambda bb, i: (bb * nq + i, 0)),
        out_shape=jax.ShapeDtypeStruct((b * s, DSA_HEADS * LANES), BF16),
        scratch_shapes=[
            pltpu.VMEM((s // tk, tk, tq), jnp.int32),
            pltpu.VMEM((g, 1, w), F32), pltpu.VMEM((g, SUBLANES, w), F32), pltpu.VMEM((g, LANES, w), F32),
            pltpu.VMEM((g, tk, w), F32), pltpu.VMEM((g, tk, w), F32),
            pltpu.VMEM((1, tq), F32),
        ],
        compiler_params=_cparams(("arbitrary", "arbitrary")),
        name="dsa_attention",
    )(cq, ck, cvt, iq, ik, proj, ltri, proj)


def _outproj_kernel(*refs, n_parts):
    x_ref = refs[0]
    y_refs = refs[1:1 + n_parts]
    w_refs = refs[1 + n_parts:1 + 2 * n_parts]
    o_ref = refs[1 + 2 * n_parts]
    acc = x_ref[...]
    for y_ref, w_ref in zip(y_refs, w_refs):
        acc = acc + _dot(y_ref[...], w_ref[...])
    o_ref[...] = acc


def _outproj(x2, ys, w_out):
    n, d = x2.shape
    tm = min(1024, n)
    tn = 1024
    widths = [y.shape[1] for y in ys]
    starts = np.cumsum([0] + widths[:-1]).tolist()
    ws = [w_out[st:st + wd] for st, wd in zip(starts, widths)]
    kern = functools.partial(_outproj_kernel, n_parts=len(ys))
    return pl.pallas_call(
        kern,
        grid=(d // tn, n // tm),
        in_specs=([pl.BlockSpec((tm, tn), lambda j, i: (i, j))]
                  + [pl.BlockSpec((tm, wd), lambda j, i: (i, 0)) for wd in widths]
                  + [pl.BlockSpec((wd, tn), lambda j, i: (0, j)) for wd in widths]),
        out_specs=pl.BlockSpec((tm, tn), lambda j, i: (i, j)),
        out_shape=jax.ShapeDtypeStruct((n, d), F32),
        compiler_params=_cparams(("arbitrary", "arbitrary")),
        name="out_proj",
    )(x2, *ys, *ws)


def _rope_tables(pos, half, reps):
    inv = ROPE_THETA ** (-jnp.arange(half, dtype=F32) / half)
    ang = pos.astype(F32)[:, None] * inv[None, :]
    cos, sin = jnp.cos(ang), jnp.sin(ang)
    return jnp.tile(jnp.concatenate([cos, cos], -1), (1, reps)), jnp.tile(jnp.concatenate([-sin, sin], -1), (1, reps))


def _split_cols(w, sizes):
    return jnp.split(w, np.cumsum(sizes)[:-1].tolist(), axis=-1)


def _even_weight(w):
    sizes = (1024, 1536, 24, 1024, 512, 512, 512, 512, 512, 512)
    a_q, a_kv, a_g, a_z, b_q, b_k, b_v, b_z, m_q, m_z = _split_cols(w.astype(BF16), sizes)
    d = w.shape[0]
    per_group = 3 * NSA_HEADS // NSA_KV_GROUPS
    gates = [jnp.pad(a_g[:, g * per_group:(g + 1) * per_group], ((0, 0), (0, LANES - per_group)))
             for g in range(NSA_KV_GROUPS)]
    out = jnp.concatenate([a_q, a_kv, b_q, b_k, b_v, m_q, a_z, b_z, m_z] + gates, axis=-1)
    assert out.shape == (d, EVEN_UNITS * LANES)
    return out


def _odd_weight(w):
    sizes = (1536, 512, 512, 1024, 64, 16, 1536, 512, 512)
    c_q, c_k, c_v, i_q, i_k, i_w, c_z, m_q, m_z = _split_cols(w.astype(BF16), sizes)
    d = w.shape[0]
    ikw = jnp.pad(jnp.concatenate([i_k, i_w], -1), ((0, 0), (0, LANES - IDX_DIM - IDX_HEADS)))
    pad = jnp.zeros((d, (O_CZ - ODD_PREP_UNITS) * LANES), BF16)
    out = jnp.concatenate([c_q, c_k, c_v, i_q, m_q, ikw, pad, c_z, m_z], axis=-1)
    assert out.shape == (d, ODD_UNITS * LANES)
    return out


def _overlap_matrix_t(s, n_cmp_pad):
    n_cmp = (s - NSA_CMP_LEN) // NSA_CMP_STRIDE + 1
    n_sel = s // NSA_SEL_LEN
    cmp_start = np.arange(n_cmp) * NSA_CMP_STRIDE
    sel_start = np.arange(n_sel) * NSA_SEL_LEN
    ov = np.clip(np.minimum(cmp_start[:, None] + NSA_CMP_LEN, sel_start[None, :] + NSA_SEL_LEN)
                 - np.maximum(cmp_start[:, None], sel_start[None, :]), 0, None) / NSA_CMP_LEN
    full = np.zeros((n_sel, n_cmp_pad), np.float32)
    full[:, :n_cmp] = ov.T
    return jnp.asarray(full, dtype=BF16)


def _score_bound(dim, gain_q, gain_k, scale):
    return dim * jnp.max(jnp.abs(gain_q)) * jnp.max(jnp.abs(gain_k)) * (scale * LOG2E * 1.02)


def _attend(bound, fn, *operands):
    return lax.cond(bound <= FAST_LOG2_BOUND,
                    lambda *a: fn(*a, bounded=True), lambda *a: fn(*a, bounded=False), *operands)


def kernel(x, mem, norm_gain, mem_norm_gain, mem_w_kv, mem_qk_gain, w_out, even_w_in, nsa_qk_gain, nsa_cmp_pos,
           nsa_cmp_w1, nsa_cmp_w2, diff_qk_gain, diff_lambda, diff_subln_gain, odd_w_in, dsa_qk_gain):
    b, s, d = x.shape
    assert d == D_MODEL and s % TK == 0 and s >= NSA_WINDOW + TQ_SPARSE
    pos = jnp.arange(s)
    cos, sin = _rope_tables(pos, HEAD_DIM // 2, 1)
    cos64, sin64 = _rope_tables(pos, DIFF_QK_DIM // 2, 2)
    n_cmp_pad = s // NSA_CMP_STRIDE
    cmp_last = jnp.arange(n_cmp_pad) * NSA_CMP_STRIDE + NSA_CMP_LEN - 1
    cos_c, sin_c = _rope_tables(cmp_last, HEAD_DIM // 2, 1)
    ovlt = _overlap_matrix_t(s, n_cmp_pad)
    ltri = jnp.asarray(np.tril(np.ones((TK, TK), np.float32), -1), dtype=BF16)
    wbias = _window_bias(TQ_SPARSE)

    mk_all, mv_all = _memkv(mem, mem_norm_gain, mem_w_kv.astype(BF16), mem_qk_gain[:, 1:2, :])
    w_out_b = w_out.astype(BF16)

    x2 = x.reshape(b * s, d)
    for i in range(DEPTH):
        mem_q_gain = mem_qk_gain[i, 0:1, :]
        if i % 2 == 0:
            e = i // 2
            proj = _proj(x2, norm_gain[i], _even_weight(even_w_in[e]), 768)
            dg2 = jnp.tile(diff_qk_gain[e], (1, 2))
            qa, ks, kw, vst, vwt, craw, bq, bk, bvt, mq = _prep_even(
                proj, b, s, cos, sin, cos64, sin64, nsa_qk_gain[e], dg2, mem_q_gain)
            w1 = nsa_cmp_w1[e].reshape(2, NSA_CMP_LEN, HEAD_DIM, NSA_CMP_HIDDEN).astype(BF16)
            kc, vct = _compress(craw, nsa_cmp_pos[e], w1, nsa_cmp_w2[e].astype(BF16),
                                nsa_qk_gain[e, 1:2, :], cos_c, sin_c)
            y_a = _attend(_score_bound(HEAD_DIM, nsa_qk_gain[e, 0], nsa_qk_gain[e, 1:4], HEAD_DIM ** -0.5),
                          functools.partial(_nsa, b=b, s=s), qa, ks, vst, kw, vwt, kc, vct, ovlt, wbias, proj)
            lambda_init = 0.8 - 0.6 * math.exp(-0.3 * i)
            y_b = _attend(_score_bound(DIFF_QK_DIM, diff_qk_gain[e, 0], diff_qk_gain[e, 1], DIFF_QK_DIM ** -0.5),
                          functools.partial(_diff, b=b, s=s, lambda_init=lambda_init),
                          bq, bk, bvt, diff_lambda[e], diff_subln_gain[e].reshape(1, LANES), proj)
            y_m = _memattn(mq, mk_all[i], mv_all[i], proj, b, s, E_MZ)
            ys = [y_a, y_b, y_m]
        else:
            o = i // 2
            proj = _proj(x2, norm_gain[i], _odd_weight(odd_w_in[o]), 512)
            cq, ck, cvt, iq, ik, mq = _prep_odd(proj, b, s, cos, sin, cos64, sin64, dsa_qk_gain[o], mem_q_gain)
            y_c = _attend(_score_bound(HEAD_DIM, dsa_qk_gain[o, 0], dsa_qk_gain[o, 1], HEAD_DIM ** -0.5),
                          functools.partial(_dsa, b=b, s=s), cq, ck, cvt, iq, ik, ltri, proj)
            y_m = _memattn(mq, mk_all[i], mv_all[i], proj, b, s, O_MZ)
            ys = [y_c, y_m]
        x2 = _outproj(x2, ys, w_out_b[i])
    return x2.reshape(b, s, d)
```

```python
import functools
import math

import jax
import jax.numpy as jnp
import numpy as np
from jax import lax
from jax.experimental import pallas as pl
from jax.experimental.pallas import tpu as pltpu

F32 = jnp.float32
BF16 = jnp.bfloat16

D_MODEL = 2048
DEPTH = 4
HEAD_DIM = 128
ROPE_THETA = 10000.0
EPS = 1e-6
MASKED = -1e30
FORCED = 1e9
NSA_HEADS = 8
NSA_KV_GROUPS = 2
NSA_CMP_LEN = 32
NSA_CMP_STRIDE = 16
NSA_CMP_HIDDEN = 256
NSA_SEL_LEN = 64
NSA_SEL_TOPN = 16
NSA_WINDOW = 512
DIFF_HEADS = 4
DIFF_QK_DIM = 64
DSA_HEADS = 12
DSA_KV_HEADS = 4
IDX_HEADS = 16
IDX_DIM = 64
DSA_TOPK_MAX = 256
MEM_HEADS = 4

LANES = 128
SUBLANES = 8
VMEM_LIMIT_BYTES = 56 * 1024 * 1024
LOG2E = 1.4426950408889634

PREP_ROWS = 256
TQ_SPARSE = 128
TQ_DIFF = 256
DIFF_QBLOCKS = 2
NSA_QBLOCKS = 4
TK = 512
VT_BLOCK = 256
NEG_INIT = -1e30
NEG_MASK = -2e30
FAST_LOG2_BOUND = 40.0

EVEN_UNITS = 54
E_AQ, E_AKV, E_BQ, E_BK, E_BV, E_MQ, E_AZ, E_BZ, E_MZ, E_AG = 0, 8, 20, 24, 28, 32, 36, 44, 48, 52
EVEN_PREP_UNITS = E_AZ
ODD_UNITS = 52
O_CQ, O_CK, O_CV, O_IQ, O_MQ, O_IKW, O_CZ, O_MZ = 0, 12, 16, 20, 28, 32, 36, 48
ODD_PREP_UNITS = O_IKW + 1


def _cparams(sem):
    return pltpu.CompilerParams(dimension_semantics=sem, vmem_limit_bytes=VMEM_LIMIT_BYTES)


def _dot(a, b):
    return jnp.dot(a, b, preferred_element_type=F32)


def _dot_nt(a, b):
    return lax.dot_general(a, b, (((1,), (1,)), ((), ())), preferred_element_type=F32)


def _silu(x):
    return x * jax.nn.sigmoid(x)


def _rms_norm(x, gain):
    return x * lax.rsqrt(jnp.mean(x * x, axis=-1, keepdims=True) + EPS) * gain


def _rms_norm_halves(x, gain):
    lo = lax.broadcasted_iota(jnp.int32, x.shape, 1) < 64
    xx = x * x
    s_lo = jnp.sum(jnp.where(lo, xx, 0.0), axis=-1, keepdims=True)
    s_hi = jnp.sum(jnp.where(lo, 0.0, xx), axis=-1, keepdims=True)
    ms = jnp.where(lo, s_lo, s_hi) * (1.0 / 64.0)
    return x * lax.rsqrt(ms + EPS) * gain


def _partner(x, half):
    n = x.shape[-1]
    lane = lax.broadcasted_iota(jnp.int32, x.shape, 1)
    if 2 * half == n:
        return pltpu.roll(x, half, 1)
    a = pltpu.roll(x, half, 1)
    b = pltpu.roll(x, n - half, 1)
    src_a = pltpu.roll(lane, half, 1)
    want = jnp.where((lane & (2 * half - 1)) < half, lane + half, lane - half)
    return jnp.where(src_a == want, a, b)


def _rope(x, cos, sin_signed, half):
    return x * cos + _partner(x, half) * sin_signed


def _transpose_bf16(x):
    n = x.shape[1]
    eye = (lax.broadcasted_iota(jnp.int32, (n, n), 0) == lax.broadcasted_iota(jnp.int32, (n, n), 1)).astype(BF16)
    return _dot_nt(eye, x.astype(BF16)).astype(BF16)


def _fold8(x, op, short_chains=False):
    rows, w = x.shape
    if short_chains:
        x = op(x.reshape(SUBLANES, rows // SUBLANES, w), axis=0)
        rows = rows // SUBLANES
    return op(x.reshape(rows // SUBLANES, SUBLANES, w), axis=0)


def _col_max(x):
    return jnp.max(_fold8(x, jnp.max), axis=0, keepdims=True)


def _col_sum(x):
    return jnp.sum(_fold8(x, jnp.sum), axis=0, keepdims=True)


def _proj_kernel(x_ref, g_ref, w_ref, o_ref, hn_ref):
    @pl.when(pl.program_id(1) == 0)
    def _():
        x = x_ref[...]
        hn_ref[...] = _rms_norm(x, g_ref[...]).astype(BF16)

    o_ref[...] = _dot(hn_ref[...], w_ref[...])


def _proj(x2, gain, w, tn):
    n, d = x2.shape
    c = w.shape[1]
    tm = min(1024, n)
    return pl.pallas_call(
        _proj_kernel,
        grid=(n // tm, c // tn),
        in_specs=[
            pl.BlockSpec((tm, d), lambda i, j: (i, 0)),
            pl.BlockSpec((1, d), lambda i, j: (0, 0)),
            pl.BlockSpec((d, tn), lambda i, j: (0, j)),
        ],
        out_specs=pl.BlockSpec((tm, tn), lambda i, j: (i, j)),
        out_shape=jax.ShapeDtypeStruct((n, c), F32),
        scratch_shapes=[pltpu.VMEM((tm, d), BF16)],
        compiler_params=_cparams(("arbitrary", "arbitrary")),
        name="proj",
    )(x2, gain.reshape(1, d), w)


def _prep_even_kernel(p_ref, cos_ref, sin_ref, cos64_ref, sin64_ref, ng_ref, dg_ref, mg_ref,
                      qa_ref, ks_ref, kw_ref, vst_ref, vwt_ref, craw_ref, bq_ref, bk_ref, bvt_ref, mq_ref):
    cos, sin = cos_ref[...], sin_ref[...]
    cos64, sin64 = cos64_ref[...], sin64_ref[...]
    rep = NSA_HEADS // NSA_KV_GROUPS
    tq = TQ_SPARSE

    def unit(u):
        return p_ref[:, u * LANES:(u + 1) * LANES]

    for h in range(NSA_HEADS):
        q = _rope(_rms_norm(unit(E_AQ + h), ng_ref[0:1, :]), cos, sin, 64).astype(BF16)
        g, r = divmod(h, rep)
        for qb in range(PREP_ROWS // tq):
            qa_ref[0, g, qb, r * tq:(r + 1) * tq, :] = q[qb * tq:(qb + 1) * tq, :]
    for g in range(NSA_KV_GROUPS):
        craw_ref[0, 0, g] = unit(E_AKV + 0 + g)
        craw_ref[0, 1, g] = unit(E_AKV + 2 + g)
        ks_ref[0, g] = _rope(_rms_norm(unit(E_AKV + 4 + g), ng_ref[2:3, :]), cos, sin, 64).astype(BF16)
        vst_ref[0, g, 0] = _transpose_bf16(unit(E_AKV + 6 + g))
        kw_ref[0, g] = _rope(_rms_norm(unit(E_AKV + 8 + g), ng_ref[3:4, :]), cos, sin, 64).astype(BF16)
        vw = unit(E_AKV + 10 + g)
        for kb in range(PREP_ROWS // LANES):
            vwt_ref[0, g, kb] = _transpose_bf16(vw[kb * LANES:(kb + 1) * LANES, :])
    lo = lax.broadcasted_iota(jnp.int32, cos.shape, 1) < 64
    for h in range(DIFF_HEADS):
        q = _rope(_rms_norm_halves(unit(E_BQ + h), dg_ref[0:1, :]), cos64, sin64, 32)
        bq_ref[0, h, 0, 0:PREP_ROWS, :] = jnp.where(lo, q, 0.0).astype(BF16)
        bq_ref[0, h, 0, PREP_ROWS:2 * PREP_ROWS, :] = jnp.where(lo, 0.0, q).astype(BF16)
        bk_ref[0, h] = _rope(_rms_norm_halves(unit(E_BK + h), dg_ref[1:2, :]), cos64, sin64, 32).astype(BF16)
        bvt_ref[0, h, 0] = _transpose_bf16(unit(E_BV + h))
    for h in range(MEM_HEADS):
        mq_ref[0, h] = _rms_norm(unit(E_MQ + h), mg_ref[...]).astype(BF16)


def _prep_even(proj, b, s, cos, sin, cos64, sin64, nsa_gain, diff_gain2, mem_gain):
    ts = PREP_ROWS
    assert ts == TQ_DIFF == VT_BLOCK
    nb = s // ts
    c = EVEN_PREP_UNITS * LANES
    g = NSA_KV_GROUPS
    rep = NSA_HEADS // g
    row = lambda bb, i: (i, 0)
    const = lambda bb, i: (0, 0)
    seq = lambda n: pl.BlockSpec((1, n, ts, LANES), lambda bb, i: (bb, 0, i, 0))
    blk5 = lambda n, k, r, cdim: pl.BlockSpec((1, n, k, r, cdim), lambda bb, i: (bb, 0, i, 0, 0))
    return pl.pallas_call(
        _prep_even_kernel,
        grid=(b, nb),
        in_specs=[
            pl.BlockSpec((ts, c), lambda bb, i: (bb * nb + i, 0)),
            pl.BlockSpec((ts, LANES), row), pl.BlockSpec((ts, LANES), row),
            pl.BlockSpec((ts, LANES), row), pl.BlockSpec((ts, LANES), row),
            pl.BlockSpec((4, LANES), const), pl.BlockSpec((2, LANES), const), pl.BlockSpec((1, LANES), const),
        ],
        out_specs=[
            blk5(g, ts // TQ_SPARSE, rep * TQ_SPARSE, LANES),
            seq(g), seq(g),
            blk5(g, 1, LANES, VT_BLOCK),
            blk5(g, ts // LANES, LANES, LANES),
            pl.BlockSpec((1, 2, g, ts, LANES), lambda bb, i: (bb, 0, 0, i, 0)),
            blk5(DIFF_HEADS, 1, 2 * TQ_DIFF, LANES),
            seq(DIFF_HEADS),
            blk5(DIFF_HEADS, 1, LANES, VT_BLOCK),
            seq(MEM_HEADS),
        ],
        out_shape=[
            jax.ShapeDtypeStruct((b, g, s // TQ_SPARSE, rep * TQ_SPARSE, LANES), BF16),
            jax.ShapeDtypeStruct((b, g, s, LANES), BF16),
            jax.ShapeDtypeStruct((b, g, s, LANES), BF16),
            jax.ShapeDtypeStruct((b, g, s // VT_BLOCK, LANES, VT_BLOCK), BF16),
            jax.ShapeDtypeStruct((b, g, s // LANES, LANES, LANES), BF16),
            jax.ShapeDtypeStruct((b, 2, g, s, LANES), F32),
            jax.ShapeDtypeStruct((b, DIFF_HEADS, s // TQ_DIFF, 2 * TQ_DIFF, LANES), BF16),
            jax.ShapeDtypeStruct((b, DIFF_HEADS, s, LANES), BF16),
            jax.ShapeDtypeStruct((b, DIFF_HEADS, s // VT_BLOCK, LANES, VT_BLOCK), BF16),
            jax.ShapeDtypeStruct((b, MEM_HEADS, s, LANES), BF16),
        ],
        compiler_params=_cparams(("arbitrary", "arbitrary")),
        name="prep_even",
    )(proj, cos, sin, cos64, sin64, nsa_gain, diff_gain2, mem_gain)


def _compress_kernel(x_ref, pe_ref, w1_ref, w2_ref, g_ref, cos_ref, sin_ref, kc_ref, vct_ref, pad_ref, *, s, n_pad):
    for kind in range(2):
        pad_ref[0:s, :] = x_ref[0, kind, 0]
        pad_ref[s:s + NSA_CMP_LEN, :] = jnp.zeros((NSA_CMP_LEN, LANES), F32)
        acc = jnp.zeros((n_pad, NSA_CMP_HIDDEN), F32)
        for l in range(NSA_CMP_LEN):
            rows = pad_ref[pl.ds(l, n_pad, stride=NSA_CMP_STRIDE), :] + pe_ref[kind, l:l + 1, :]
            acc = acc + _dot(rows.astype(BF16), w1_ref[kind, l])
        out = _dot(_silu(acc).astype(BF16), w2_ref[kind])
        if kind == 0:
            kc_ref[0, 0] = _rope(_rms_norm(out, g_ref[...]), cos_ref[...], sin_ref[...], 64).astype(BF16)
        else:
            vct_ref[0, 0] = out.T.astype(BF16)


def _compress(craw, pe, w1, w2, gain, cos_c, sin_c):
    b, _, g, s, _ = craw.shape
    n_pad = s // NSA_CMP_STRIDE
    kern = functools.partial(_compress_kernel, s=s, n_pad=n_pad)
    whole = lambda shape: pl.BlockSpec(shape, lambda bb, gg: (0,) * len(shape))
    return pl.pallas_call(
        kern,
        grid=(b, g),
        in_specs=[
            pl.BlockSpec((1, 2, 1, s, LANES), lambda bb, gg: (bb, 0, gg, 0, 0)),
            whole((2, NSA_CMP_LEN, LANES)),
            whole((2, NSA_CMP_LEN, LANES, NSA_CMP_HIDDEN)),
            whole((2, NSA_CMP_HIDDEN, LANES)),
            whole((1, LANES)), whole((n_pad, LANES)), whole((n_pad, LANES)),
        ],
        out_specs=[
            pl.BlockSpec((1, 1, n_pad, LANES), lambda bb, gg: (bb, gg, 0, 0)),
            pl.BlockSpec((1, 1, LANES, n_pad), lambda bb, gg: (bb, gg, 0, 0)),
        ],
        out_shape=[
            jax.ShapeDtypeStruct((b, g, n_pad, LANES), BF16),
            jax.ShapeDtypeStruct((b, g, LANES, n_pad), BF16),
        ],
        scratch_shapes=[pltpu.VMEM((s + NSA_CMP_LEN, LANES), F32)],
        compiler_params=_cparams(("arbitrary", "arbitrary")),
        name="nsa_compress",
    )(craw, pe, w1, w2, gain, cos_c, sin_c)


def _softmax_parts_t(st, bounded):
    if not bounded:
        st = st - _col_max(st)
    e = jnp.exp2(st)
    return e.astype(BF16), 1.0 / jnp.maximum(_col_sum(e), 1e-30)


def _online_init(m_ref, l_ref, acc_ref):
    m_ref[...] = jnp.full(m_ref.shape, NEG_INIT, F32)
    l_ref[...] = jnp.zeros(l_ref.shape, F32)
    acc_ref[...] = jnp.zeros(acc_ref.shape, F32)


def _online_step(idx, st, pv, m_ref, l_ref, acc_ref, bounded):
    if bounded:
        p = jnp.exp2(st)
        l_ref[idx] = l_ref[idx] + _fold8(p, jnp.sum)
        acc_ref[idx] = acc_ref[idx] + pv(p.astype(BF16))
        return
    m_old = m_ref[idx]
    m_new = jnp.maximum(m_old, _col_max(st))
    alpha = jnp.exp2(m_old - m_new)
    p = jnp.exp2(st - m_new)
    l_ref[idx] = alpha * l_ref[idx] + _fold8(p, jnp.sum)
    acc_ref[idx] = alpha * acc_ref[idx] + pv(p.astype(BF16))
    m_ref[idx] = m_new


def _online_result(idx, l_ref, acc_ref):
    return acc_ref[idx] / jnp.maximum(jnp.sum(l_ref[idx], axis=0, keepdims=True), 1e-30)


def _flash_pairs(n_chunks, n_groups, qk, mask, soft, sa_ref, sb_ref):
    for g in range(n_groups):
        qk(0, g, sa_ref)
    n_pairs = (n_chunks - 1) // 2

    def pair(j, carry):
        a = 2 * j
        bias_a = mask(a, False)
        bias_b = mask(a + 1, False)
        for g in range(n_groups):
            qk(a + 1, g, sb_ref)
            soft(a, g, sa_ref, bias_a)
            qk(a + 2, g, sa_ref)
            soft(a + 1, g, sb_ref, bias_b)
        return carry

    lax.fori_loop(0, n_pairs, pair, 0)
    e = 2 * n_pairs

    @pl.when(e + 1 < n_chunks)
    def _():
        bias_a = mask(e, True)
        bias_b = mask(e + 1, True)
        for g in range(n_groups):
            qk(e + 1, g, sb_ref)
            soft(e, g, sa_ref, bias_a)
            soft(e + 1, g, sb_ref, bias_b)

    @pl.when(e + 1 >= n_chunks)
    def _():
        bias_a = mask(e, True)
        for g in range(n_groups):
            soft(e, g, sa_ref, bias_a)


def _pv_blocks(vt_ref, lead, chunk, tk):
    per = tk // VT_BLOCK

    def pv(p):
        out = _dot(vt_ref[lead + (chunk * per,)], p[0:VT_BLOCK])
        for i in range(1, per):
            out = out + _dot(vt_ref[lead + (chunk * per + i,)], p[i * VT_BLOCK:(i + 1) * VT_BLOCK])
        return out

    return pv


def _top_n_mask_t(scores, n):
    row = lax.broadcasted_iota(jnp.int32, scores.shape, 0).astype(F32)
    height = float(scores.shape[0])
    work = scores
    sel = jnp.zeros(scores.shape, F32)
    for _ in range(n):
        m = jnp.max(work, axis=0, keepdims=True)
        first = jnp.min(jnp.where(work == m, row, height), axis=0, keepdims=True)
        pick = row == first
        sel = jnp.where(pick, 1.0, sel)
        work = jnp.where(pick, -jnp.inf, work)
    return sel


def _nsa_kernel(q_ref, kc_ref, vct_ref, ks_ref, vst_ref, kw_ref, vwt_ref, ovlt_ref, *rest, tq, tk, n_qb, top_n, bounded):
    wbias_refs = rest[:n_qb]
    gl_ref, z_ref, o_ref, m_ref, l_ref, acc_ref, sa_ref, sb_ref, selt_ref, oct_ref, owt_ref = rest[n_qb:]
    rep = NSA_HEADS // NSA_KV_GROUPS
    w = rep * tq
    scale = HEAD_DIM ** -0.5
    step0 = pl.program_id(2) * (n_qb * tq)
    lane_t = lax.broadcasted_iota(jnp.int32, (1, w), 1) & (tq - 1)
    n_cmp_pad = kc_ref.shape[2]
    n_sel = ovlt_ref.shape[0]
    sel_shift = NSA_SEL_LEN.bit_length() - 1
    span = NSA_WINDOW + tq

    for qb in range(n_qb):
        q0 = step0 + qb * tq
        t_row = q0 + lane_t
        t_one = t_row[:, 0:tq]
        q = q_ref[0, 0, qb]

        cmp_last = lax.broadcasted_iota(jnp.int32, (n_cmp_pad, w), 0) * NSA_CMP_STRIDE + (NSA_CMP_LEN - 1)
        st = jnp.where(cmp_last <= t_row, _dot_nt(kc_ref[0, 0], q) * (scale * LOG2E), NEG_MASK)
        p, inv = _softmax_parts_t(st, bounded)
        inv = jnp.where(t_row >= NSA_CMP_LEN - 1, inv, 0.0)
        oct_ref[qb] = _dot(vct_ref[0, 0], p) * inv
        imp_heads = _dot(ovlt_ref[...], p) * inv
        imp = imp_heads[:, 0:tq]
        for r in range(1, rep):
            imp = imp + imp_heads[:, r * tq:(r + 1) * tq]

        j = lax.broadcasted_iota(jnp.int32, (n_sel, tq), 0)
        cur = jnp.right_shift(t_one, sel_shift)
        visible = j <= cur
        forced = (j == 0) | (j >= cur - 1)
        imp = jnp.where(visible, jnp.where(forced, FORCED, imp), MASKED)
        selt_ref[qb] = _top_n_mask_t(imp, top_n)

        start = pl.multiple_of(jnp.maximum(q0 - NSA_WINDOW, 0), tq)
        st = (_dot_nt(kw_ref[0, 0, pl.ds(start, span), :], q) * (scale * LOG2E)
              + jnp.concatenate([wbias_refs[qb][0]] * rep, axis=1))
        p, inv = _softmax_parts_t(st, bounded)
        blk0 = start // LANES
        ow = _dot(vwt_ref[0, 0, blk0], p[0:LANES])
        for i in range(1, span // LANES):
            ow = ow + _dot(vwt_ref[0, 0, blk0 + i], p[i * LANES:(i + 1) * LANES])
        owt_ref[qb] = ow * inv

    _online_init(m_ref, l_ref, acc_ref)
    n_chunks = (step0 + n_qb * tq + tk - 1) // tk
    blocks_per_chunk = tk // NSA_SEL_LEN

    def qk(c, g, dst):
        dst[g] = _dot_nt(ks_ref[0, 0, pl.ds(pl.multiple_of(c * tk, tk), tk), :], q_ref[0, 0, g]) * (scale * LOG2E)

    def mask(c, tail):
        biases = []
        for g in range(n_qb):
            rows = selt_ref[g, pl.ds(pl.multiple_of(c * blocks_per_chunk, blocks_per_chunk), blocks_per_chunk), :]
            ok = jnp.concatenate([jnp.broadcast_to(rows[i:i + 1, :], (NSA_SEL_LEN, tq))
                                  for i in range(blocks_per_chunk)], axis=0) > 0.5
            if tail:
                key = c * tk + lax.broadcasted_iota(jnp.int32, (tk, tq), 0)
                ok = ok & (key <= step0 + g * tq + lane_t[:, 0:tq])
            biases.append(jnp.concatenate([jnp.where(ok, 0.0, NEG_MASK)] * rep, axis=1))
        return biases

    def soft(c, g, src, bias):
        _online_step(g, src[g] + bias[g], _pv_blocks(vst_ref, (0, 0), c, tk), m_ref, l_ref, acc_ref, bounded)

    _flash_pairs(n_chunks, n_qb, qk, mask, soft, sa_ref, sb_ref)

    for qb in range(n_qb):
        rows = slice(qb * tq, (qb + 1) * tq)
        gates_t = jax.nn.sigmoid(gl_ref[rows, :]).T
        os_t = _online_result(qb, l_ref, acc_ref)
        for r in range(rep):
            sl = slice(r * tq, (r + 1) * tq)
            cols = slice(r * LANES, (r + 1) * LANES)
            out_t = (gates_t[3 * r:3 * r + 1, :] * oct_ref[qb, :, sl] + gates_t[3 * r + 1:3 * r + 2, :] * os_t[:, sl]
                     + gates_t[3 * r + 2:3 * r + 3, :] * owt_ref[qb, :, sl])
            o_ref[rows, cols] = (out_t.T * _silu(z_ref[rows, cols])).astype(BF16)


def _window_bias(tq):
    span = NSA_WINDOW + tq
    n_early = NSA_WINDOW // tq
    row = np.arange(span)[:, None]
    lane = np.arange(tq)[None, :]
    tables = []
    for i in range(n_early + 1):
        t = i * tq + lane
        s_pos = (0 if i < n_early else t[0, 0] - NSA_WINDOW) + row
        ok = (s_pos <= t) & (s_pos > t - NSA_WINDOW)
        tables.append(np.where(ok, 0.0, NEG_MASK))
    return jnp.asarray(np.stack(tables), dtype=F32)


def _nsa(qa, ks, vst, kw, vwt, kc, vct, ovlt, wbias, proj, b, s, bounded):
    tq, tk, n_qb = TQ_SPARSE, TK, NSA_QBLOCKS
    assert tk % (n_qb * tq) == 0
    rows = n_qb * tq
    nq = s // rows
    g = NSA_KV_GROUPS
    rep = NSA_HEADS // g
    w = rep * tq
    n_cmp_pad = kc.shape[2]
    n_sel = s // NSA_SEL_LEN
    last_bias = wbias.shape[0] - 1
    kern = functools.partial(_nsa_kernel, tq=tq, tk=tk, n_qb=n_qb, top_n=min(NSA_SEL_TOPN, n_sel), bounded=bounded)
    per_group = lambda shape: pl.BlockSpec((1, 1) + shape, lambda bb, gg, i: (bb, gg) + (0,) * len(shape))
    bias_spec = lambda qb: pl.BlockSpec((1,) + wbias.shape[1:],
                                        lambda bb, gg, i: (jnp.minimum(n_qb * i + qb, last_bias), 0, 0))
    return pl.pallas_call(
        kern,
        grid=(b, g, nq),
        in_specs=[
            pl.BlockSpec((1, 1, n_qb, w, LANES), lambda bb, gg, i: (bb, gg, i, 0, 0)),
            per_group((n_cmp_pad, LANES)), per_group((LANES, n_cmp_pad)),
            per_group((s, LANES)), per_group((s // VT_BLOCK, LANES, VT_BLOCK)),
            per_group((s, LANES)), per_group((s // LANES, LANES, LANES)),
            pl.BlockSpec((n_sel, n_cmp_pad), lambda bb, gg, i: (0, 0)),
        ] + [bias_spec(qb) for qb in range(n_qb)] + [
            pl.BlockSpec((rows, LANES), lambda bb, gg, i: (bb * nq + i, E_AG + gg)),
            pl.BlockSpec((rows, rep * LANES), lambda bb, gg, i: (bb * nq + i, E_AZ // rep + gg)),
        ],
        out_specs=pl.BlockSpec((rows, rep * LANES), lambda bb, gg, i: (bb * nq + i, gg)),
        out_shape=jax.ShapeDtypeStruct((b * s, NSA_HEADS * LANES), BF16),
        scratch_shapes=[
            pltpu.VMEM((n_qb, 1, w), F32), pltpu.VMEM((n_qb, SUBLANES, w), F32), pltpu.VMEM((n_qb, LANES, w), F32),
            pltpu.VMEM((n_qb, tk, w), F32), pltpu.VMEM((n_qb, tk, w), F32),
            pltpu.VMEM((n_qb, n_sel, tq), F32), pltpu.VMEM((n_qb, LANES, w), F32), pltpu.VMEM((n_qb, LANES, w), F32),
        ],
        compiler_params=_cparams(("arbitrary", "arbitrary", "arbitrary")),
        name="nsa_attention",
    )(qa, kc, vct, ks, vst, kw, vwt, ovlt, *([wbias] * n_qb), proj, proj)


def _diff_kernel(q_ref, k_ref, vt_ref, lam_ref, sg_ref, z_ref, o_ref, m_ref, l_ref, acc_ref, sa_ref, sb_ref,
                 *, tq, tk, n_qb, lambda_init, bounded):
    w = 2 * tq
    q0 = pl.program_id(2) * (n_qb * tq)
    lane_t = lax.broadcasted_iota(jnp.int32, (1, w), 1) & (tq - 1)
    _online_init(m_ref, l_ref, acc_ref)
    n_chunks = (q0 + n_qb * tq + tk - 1) // tk

    def qk(c, g, dst):
        dst[g] = (_dot_nt(k_ref[0, 0, pl.ds(pl.multiple_of(c * tk, tk), tk), :], q_ref[0, 0, g])
                  * (DIFF_QK_DIM ** -0.5 * LOG2E))

    def mask(c, tail):
        if not tail:
            return None
        key = c * tk + lax.broadcasted_iota(jnp.int32, (tk, w), 0)
        return [jnp.where(key <= q0 + g * tq + lane_t, 0.0, NEG_MASK) for g in range(n_qb)]

    def soft(c, g, src, bias):
        st = src[g] if bias is None else src[g] + bias[g]
        _online_step(g, st, _pv_blocks(vt_ref, (0, 0), c, tk), m_ref, l_ref, acc_ref, bounded)

    _flash_pairs(n_chunks, n_qb, qk, mask, soft, sa_ref, sb_ref)

    lv = lam_ref[...]
    lam = (jnp.exp(jnp.sum(lv[0:1] * lv[1:2], axis=-1, keepdims=True))
           - jnp.exp(jnp.sum(lv[2:3] * lv[3:4], axis=-1, keepdims=True)) + lambda_init)
    for g in range(n_qb):
        rows = slice(g * tq, (g + 1) * tq)
        o_t = _online_result(g, l_ref, acc_ref)
        o = (o_t[:, 0:tq] - lam * o_t[:, tq:w]).T
        o = _rms_norm(o, sg_ref[...]) * (1.0 - lambda_init)
        o_ref[rows, :] = (o * _silu(z_ref[rows, :])).astype(BF16)


def _diff(bq, bk, bvt, lam_vecs, subln_gain, proj, b, s, lambda_init, bounded):
    tq, tk, n_qb = TQ_DIFF, TK, DIFF_QBLOCKS
    assert n_qb * tq == tk
    rows = n_qb * tq
    nq = s // rows
    w = 2 * tq
    kern = functools.partial(_diff_kernel, tq=tq, tk=tk, n_qb=n_qb, lambda_init=lambda_init, bounded=bounded)
    return pl.pallas_call(
        kern,
        grid=(b, DIFF_HEADS, nq),
        in_specs=[
            pl.BlockSpec((1, 1, n_qb, w, LANES), lambda bb, h, i: (bb, h, i, 0, 0)),
            pl.BlockSpec((1, 1, s, LANES), lambda bb, h, i: (bb, h, 0, 0)),
            pl.BlockSpec((1, 1, s // VT_BLOCK, LANES, VT_BLOCK), lambda bb, h, i: (bb, h, 0, 0, 0)),
            pl.BlockSpec((4, DIFF_QK_DIM), lambda bb, h, i: (0, 0)),
            pl.BlockSpec((1, LANES), lambda bb, h, i: (0, 0)),
            pl.BlockSpec((rows, LANES), lambda bb, h, i: (bb * nq + i, E_BZ + h)),
        ],
        out_specs=pl.BlockSpec((rows, LANES), lambda bb, h, i: (bb * nq + i, h)),
        out_shape=jax.ShapeDtypeStruct((b * s, DIFF_HEADS * LANES), BF16),
        scratch_shapes=[
            pltpu.VMEM((n_qb, 1, w), F32), pltpu.VMEM((n_qb, SUBLANES, w), F32), pltpu.VMEM((n_qb, LANES, w), F32),
            pltpu.VMEM((n_qb, tk, w), F32), pltpu.VMEM((n_qb, tk, w), F32),
        ],
        compiler_params=_cparams(("arbitrary", "arbitrary", "arbitrary")),
        name="diff_attention",
    )(bq, bk, bvt, lam_vecs, subln_gain, proj)


def _memkv_kernel(mem_ref, mg_ref, w_ref, kg_ref, k_ref, v_ref):
    mem_n = _rms_norm(mem_ref[0], mg_ref[...]).astype(BF16)
    kv = _dot(mem_n, w_ref[0])
    for h in range(MEM_HEADS):
        k_ref[0, 0, h] = _rms_norm(kv[:, h * LANES:(h + 1) * LANES], kg_ref[0]).astype(BF16)
        v_ref[0, 0, h] = kv[:, (MEM_HEADS + h) * LANES:(MEM_HEADS + h + 1) * LANES].astype(BF16)


def _memkv(mem, mem_gain, w_kv, k_gain):
    b, m, d = mem.shape
    depth = w_kv.shape[0]
    c = w_kv.shape[2]
    out = jax.ShapeDtypeStruct((depth, b, MEM_HEADS, m, LANES), BF16)
    ospec = pl.BlockSpec((1, 1, MEM_HEADS, m, LANES), lambda i, bb: (i, bb, 0, 0, 0))
    return pl.pallas_call(
        _memkv_kernel,
        grid=(depth, b),
        in_specs=[
            pl.BlockSpec((1, m, d), lambda i, bb: (bb, 0, 0)),
            pl.BlockSpec((1, d), lambda i, bb: (0, 0)),
            pl.BlockSpec((1, d, c), lambda i, bb: (i, 0, 0)),
            pl.BlockSpec((1, 1, LANES), lambda i, bb: (i, 0, 0)),
        ],
        out_specs=[ospec, ospec],
        out_shape=[out, out],
        compiler_params=_cparams(("arbitrary", "arbitrary")),
        name="mem_kv",
    )(mem, mem_gain.reshape(1, d), w_kv, k_gain)


def _memattn_kernel(q_ref, k_ref, v_ref, z_ref, o_ref):
    scale = HEAD_DIM ** -0.5
    for h in range(MEM_HEADS):
        s = _dot_nt(q_ref[0, h], k_ref[0, h]) * scale
        e = jnp.exp(s - jnp.max(s, axis=-1, keepdims=True))
        p = e / jnp.sum(e, axis=-1, keepdims=True)
        o = _dot(p.astype(BF16), v_ref[0, h])
        o_ref[:, h * LANES:(h + 1) * LANES] = (o * _silu(z_ref[:, h * LANES:(h + 1) * LANES])).astype(BF16)


def _memattn(mq, mk, mv, proj, b, s, z_unit):
    tq = min(512, s)
    nq = s // tq
    m = mk.shape[2]
    kvspec = pl.BlockSpec((1, MEM_HEADS, m, LANES), lambda bb, i: (bb, 0, 0, 0))
    return pl.pallas_call(
        _memattn_kernel,
        grid=(b, nq),
        in_specs=[
            pl.BlockSpec((1, MEM_HEADS, tq, LANES), lambda bb, i: (bb, 0, i, 0)),
            kvspec, kvspec,
            pl.BlockSpec((tq, MEM_HEADS * LANES), lambda bb, i: (bb * nq + i, z_unit // MEM_HEADS)),
        ],
        out_specs=pl.BlockSpec((tq, MEM_HEADS * LANES), lambda bb, i: (bb * nq + i, 0)),
        out_shape=jax.ShapeDtypeStruct((b * s, MEM_HEADS * LANES), BF16),
        compiler_params=_cparams(("arbitrary", "arbitrary")),
        name="mem_attention",
    )(mq, mk, mv, proj)


def _prep_odd_kernel(p_ref, cos_ref, sin_ref, cos64_ref, sin64_ref, cg_ref, mg_ref,
                     cq_ref, ck_ref, cvt_ref, iq_ref, ik_ref, mq_ref):
    cos, sin = cos_ref[...], sin_ref[...]
    cos64, sin64 = cos64_ref[...], sin64_ref[...]
    rep = DSA_HEADS // DSA_KV_HEADS
    tq = TQ_SPARSE
    n_qb = PREP_ROWS // tq

    def unit(u):
        return p_ref[:, u * LANES:(u + 1) * LANES]

    for h in range(DSA_HEADS):
        q = _rope(_rms_norm(unit(O_CQ + h), cg_ref[0:1, :]), cos, sin, 64).astype(BF16)
        g, r = divmod(h, rep)
        for qb in range(n_qb):
            cq_ref[0, g, qb, r * tq:(r + 1) * tq, :] = q[qb * tq:(qb + 1) * tq, :]
    for h in range(DSA_KV_HEADS):
        ck_ref[0, h] = _rope(_rms_norm(unit(O_CK + h), cg_ref[1:2, :]), cos, sin, 64).astype(BF16)
        cvt_ref[0, h, 0] = _transpose_bf16(unit(O_CV + h))
    lo = lax.broadcasted_iota(jnp.int32, cos.shape, 1) < 64
    for u in range(IDX_HEADS // 2):
        x = _rope(unit(O_IQ + u), cos64, sin64, 32)
        even = jnp.where(lo, x, 0.0).astype(BF16)
        odd = jnp.where(lo, pltpu.roll(x, 64, 1), 0.0).astype(BF16)
        for qb in range(n_qb):
            iq_ref[0, qb, (2 * u) * tq:(2 * u + 1) * tq, :] = even[qb * tq:(qb + 1) * tq, :]
            iq_ref[0, qb, (2 * u + 1) * tq:(2 * u + 2) * tq, :] = odd[qb * tq:(qb + 1) * tq, :]
    ik = _rope(unit(O_IKW), cos64, sin64, 32)
    ik_ref[0] = jnp.where(lo, ik, 0.0).astype(BF16)
    for h in range(MEM_HEADS):
        mq_ref[0, h] = _rms_norm(unit(O_MQ + h), mg_ref[...]).astype(BF16)


def _prep_odd(proj, b, s, cos, sin, cos64, sin64, dsa_gain, mem_gain):
    ts = PREP_ROWS
    nb = s // ts
    c = ODD_PREP_UNITS * LANES
    rep = DSA_HEADS // DSA_KV_HEADS
    row = lambda bb, i: (i, 0)
    const = lambda bb, i: (0, 0)
    seq = lambda n: pl.BlockSpec((1, n, ts, LANES), lambda bb, i: (bb, 0, i, 0))
    return pl.pallas_call(
        _prep_odd_kernel,
        grid=(b, nb),
        in_specs=[
            pl.BlockSpec((ts, c), lambda bb, i: (bb * nb + i, 0)),
            pl.BlockSpec((ts, LANES), row), pl.BlockSpec((ts, LANES), row),
            pl.BlockSpec((ts, LANES), row), pl.BlockSpec((ts, LANES), row),
            pl.BlockSpec((2, LANES), const), pl.BlockSpec((1, LANES), const),
        ],
        out_specs=[
            pl.BlockSpec((1, DSA_KV_HEADS, ts // TQ_SPARSE, rep * TQ_SPARSE, LANES), lambda bb, i: (bb, 0, i, 0, 0)),
            seq(DSA_KV_HEADS),
            pl.BlockSpec((1, DSA_KV_HEADS, 1, LANES, VT_BLOCK), lambda bb, i: (bb, 0, i, 0, 0)),
            pl.BlockSpec((1, ts // TQ_SPARSE, IDX_HEADS * TQ_SPARSE, LANES), lambda bb, i: (bb, i, 0, 0)),
            pl.BlockSpec((1, ts, LANES), lambda bb, i: (bb, i, 0)),
            seq(MEM_HEADS),
        ],
        out_shape=[
            jax.ShapeDtypeStruct((b, DSA_KV_HEADS, s // TQ_SPARSE, rep * TQ_SPARSE, LANES), BF16),
            jax.ShapeDtypeStruct((b, DSA_KV_HEADS, s, LANES), BF16),
            jax.ShapeDtypeStruct((b, DSA_KV_HEADS, s // VT_BLOCK, LANES, VT_BLOCK), BF16),
            jax.ShapeDtypeStruct((b, s // TQ_SPARSE, IDX_HEADS * TQ_SPARSE, LANES), BF16),
            jax.ShapeDtypeStruct((b, s, LANES), BF16),
            jax.ShapeDtypeStruct((b, MEM_HEADS, s, LANES), BF16),
        ],
        compiler_params=_cparams(("arbitrary", "arbitrary")),
        name="prep_odd",
    )(proj, cos, sin, cos64, sin64, dsa_gain, mem_gain)


def _sortable_key(x):
    bits = pltpu.bitcast(x + 0.0, jnp.int32)
    return jnp.where(bits < 0, bits ^ jnp.int32(0x7FFFFFFF), bits)


def _dsa_kernel(q_ref, k_ref, vt_ref, iq_ref, ik_ref, w_ref, ltri_ref, z_ref, o_ref,
                key_ref, m_ref, l_ref, acc_ref, sa_ref, sb_ref, ties_ref, *, tq, tk, top_k, bounded):
    rep = DSA_HEADS // DSA_KV_HEADS
    scale = HEAD_DIM ** -0.5
    q0 = pl.program_id(1) * tq
    t_row = q0 + lax.broadcasted_iota(jnp.int32, (1, tq), 1)
    n_chunks = (q0 + tq + tk - 1) // tk
    int_min = jnp.int32(-2147483648)
    heads_per_dot = 4

    w_t = (w_ref[...] * (IDX_HEADS ** -0.5 * IDX_DIM ** -0.5)).T

    def score_chunk(c):
        k0 = pl.multiple_of(c * tk, tk)
        ik = ik_ref[0, pl.ds(k0, tk), :]
        sc = jnp.zeros((tk, tq), F32)
        for h0 in range(0, IDX_HEADS, heads_per_dot):
            x = _dot_nt(ik, iq_ref[0, 0, h0 * tq:(h0 + heads_per_dot) * tq, :])
            for hh in range(heads_per_dot):
                h = h0 + hh
                sc = sc + jnp.maximum(x[:, hh * tq:(hh + 1) * tq], 0.0) * w_t[IDX_DIM + h:IDX_DIM + h + 1, :]
        causal = (k0 + lax.broadcasted_iota(jnp.int32, (tk, tq), 0)) <= t_row
        key_ref[c] = _sortable_key(jnp.where(causal, sc, MASKED))

    def score_pair(j, carry):
        score_chunk(2 * j)
        score_chunk(2 * j + 1)
        return carry

    def score_one(c, carry):
        score_chunk(c)
        return carry

    lax.fori_loop(0, n_chunks // 2, score_pair, 0)
    lax.fori_loop(n_chunks - n_chunks % 2, n_chunks, score_one, 0)

    def count(pred):
        def one(c):
            return _fold8(pred(key_ref[c]).astype(F32), jnp.sum, short_chains=True)

        def four(j, acc):
            return acc + ((one(4 * j) + one(4 * j + 1)) + (one(4 * j + 2) + one(4 * j + 3)))

        acc = lax.fori_loop(0, n_chunks // 4, four, jnp.zeros((SUBLANES, tq), F32))
        acc = lax.fori_loop(n_chunks - n_chunks % 4, n_chunks, lambda c, a: a + one(c), acc)
        return jnp.sum(acc, axis=0, keepdims=True)

    def search(it, state):
        thr_u, above = state
        cand_u = thr_u | jnp.left_shift(jnp.int32(1), 31 - it)
        cand = cand_u ^ int_min
        cnt = count(lambda kk: kk >= cand)
        take = cnt >= top_k
        return jnp.where(take, cand_u, thr_u), jnp.where(take, above, cnt)

    thr_u, above = lax.fori_loop(0, 32, search, (jnp.zeros((1, tq), jnp.int32), jnp.zeros((1, tq), F32)))
    thr = thr_u ^ int_min
    budget = top_k - above

    _online_init(m_ref, l_ref, acc_ref)
    ties_ref[...] = jnp.zeros(ties_ref.shape, F32)

    def qk(c, g, dst):
        dst[g] = _dot_nt(k_ref[0, g, pl.ds(pl.multiple_of(c * tk, tk), tk), :], q_ref[0, g, 0]) * (scale * LOG2E)

    def mask(c, tail):
        keys = key_ref[c]
        tie = keys == thr
        rank = _dot(ltri_ref[...], tie.astype(BF16)) + ties_ref[...]
        ok = (keys > thr) | (tie & (rank < budget))
        if tail:
            ok = ok & ((c * tk + lax.broadcasted_iota(jnp.int32, (tk, tq), 0)) <= t_row)
        ties_ref[...] = ties_ref[...] + _col_sum(tie.astype(F32))
        return jnp.concatenate([jnp.where(ok, 0.0, NEG_MASK)] * rep, axis=1)

    def soft(c, g, src, bias):
        _online_step(g, src[g] + bias, _pv_blocks(vt_ref, (0, g), c, tk), m_ref, l_ref, acc_ref, bounded)

    _flash_pairs(n_chunks, DSA_KV_HEADS, qk, mask, soft, sa_ref, sb_ref)

    for g in range(DSA_KV_HEADS):
        o_t = _online_result(g, l_ref, acc_ref)
        for r in range(rep):
            h = g * rep + r
            o = o_t[:, r * tq:(r + 1) * tq].T
            o_ref[:, h * LANES:(h + 1) * LANES] = (o * _silu(z_ref[:, h * LANES:(h + 1) * LANES])).astype(BF16)


def _dsa(cq, ck, cvt, iq, ik, ltri, proj, b, s, bounded):
    tq, tk = TQ_SPARSE, TK
    nq = s // tq
    g = DSA_KV_HEADS
    rep = DSA_HEADS // g
    w = rep * tq
    kern = functools.partial(_dsa_kernel, tq=tq, tk=tk, top_k=min(DSA_TOPK_MAX, s // 4), bounded=bounded)
    return pl.pallas_call(
        kern,
        grid=(b, nq),
        in_specs=[
            pl.BlockSpec((1, g, 1, w, LANES), lambda bb, i: (bb, 0, i, 0, 0)),
            pl.BlockSpec((1, g, s, LANES), lambda bb, i: (bb, 0, 0, 0), pipeline_mode=pl.Buffered(1)),
            pl.BlockSpec((1, g, s // VT_BLOCK, LANES, VT_BLOCK), lambda bb, i: (bb, 0, 0, 0, 0),
                         pipeline_mode=pl.Buffered(1)),
            pl.BlockSpec((1, 1, IDX_HEADS * tq, LANES), lambda bb, i: (bb, i, 0, 0)),
            pl.BlockSpec((1, s, LANES), lambda bb, i: (bb, 0, 0), pipeline_mode=pl.Buffered(1)),
            pl.BlockSpec((tq, LANES), lambda bb, i: (bb * nq + i, O_IKW)),
            pl.BlockSpec((tk, tk), lambda bb, i: (0, 0)),
            pl.BlockSpec((tq, DSA_HEADS * LANES), lambda bb, i: (bb * nq + i, O_CZ // DSA_HEADS)),
        ],
        out_specs=pl.BlockSpec((tq, DSA_HEADS * LANES), lambda bb, i: (bb * nq + i, 0)),
        out_shape=jax.ShapeDtypeStruct((b * s, DSA_HEADS * LANES), BF16),
        scratch_shapes=[
            pltpu.VMEM((s // tk, tk, tq), jnp.int32),
            pltpu.VMEM((g, 1, w), F32), pltpu.VMEM((g, SUBLANES, w), F32), pltpu.VMEM((g, LANES, w), F32),
            pltpu.VMEM((g, tk, w), F32), pltpu.VMEM((g, tk, w), F32),
            pltpu.VMEM((1, tq), F32),
        ],
        compiler_params=_cparams(("arbitrary", "arbitrary")),
        name="dsa_attention",
    )(cq, ck, cvt, iq, ik, proj, ltri, proj)


def _outproj_kernel(*refs, n_parts):
    x_ref = refs[0]
    y_refs = refs[1:1 + n_parts]
    w_refs = refs[1 + n_parts:1 + 2 * n_parts]
    o_ref = refs[1 + 2 * n_parts]
    acc = x_ref[...]
    for y_ref, w_ref in zip(y_refs, w_refs):
        acc = acc + _dot(y_ref[...], w_ref[...])
    o_ref[...] = acc


def _outproj(x2, ys, w_out):
    n, d = x2.shape
    tm = min(1024, n)
    tn = 1024
    widths = [y.shape[1] for y in ys]
    starts = np.cumsum([0] + widths[:-1]).tolist()
    ws = [w_out[st:st + wd] for st, wd in zip(starts, widths)]
    kern = functools.partial(_outproj_kernel, n_parts=len(ys))
    return pl.pallas_call(
        kern,
        grid=(d // tn, n // tm),
        in_specs=([pl.BlockSpec((tm, tn), lambda j, i: (i, j))]
                  + [pl.BlockSpec((tm, wd), lambda j, i: (i, 0)) for wd in widths]
                  + [pl.BlockSpec((wd, tn), lambda j, i: (0, j)) for wd in widths]),
        out_specs=pl.BlockSpec((tm, tn), lambda j, i: (i, j)),
        out_shape=jax.ShapeDtypeStruct((n, d), F32),
        compiler_params=_cparams(("arbitrary", "arbitrary")),
        name="out_proj",
    )(x2, *ys, *ws)


def _rope_tables(pos, half, reps):
    inv = ROPE_THETA ** (-jnp.arange(half, dtype=F32) / half)
    ang = pos.astype(F32)[:, None] * inv[None, :]
    cos, sin = jnp.cos(ang), jnp.sin(ang)
    return jnp.tile(jnp.concatenate([cos, cos], -1), (1, reps)), jnp.tile(jnp.concatenate([-sin, sin], -1), (1, reps))


def _split_cols(w, sizes):
    return jnp.split(w, np.cumsum(sizes)[:-1].tolist(), axis=-1)


def _even_weight(w):
    sizes = (1024, 1536, 24, 1024, 512, 512, 512, 512, 512, 512)
    a_q, a_kv, a_g, a_z, b_q, b_k, b_v, b_z, m_q, m_z = _split_cols(w.astype(BF16), sizes)
    d = w.shape[0]
    per_group = 3 * NSA_HEADS // NSA_KV_GROUPS
    gates = [jnp.pad(a_g[:, g * per_group:(g + 1) * per_group], ((0, 0), (0, LANES - per_group)))
             for g in range(NSA_KV_GROUPS)]
    out = jnp.concatenate([a_q, a_kv, b_q, b_k, b_v, m_q, a_z, b_z, m_z] + gates, axis=-1)
    assert out.shape == (d, EVEN_UNITS * LANES)
    return out


def _odd_weight(w):
    sizes = (1536, 512, 512, 1024, 64, 16, 1536, 512, 512)
    c_q, c_k, c_v, i_q, i_k, i_w, c_z, m_q, m_z = _split_cols(w.astype(BF16), sizes)
    d = w.shape[0]
    ikw = jnp.pad(jnp.concatenate([i_k, i_w], -1), ((0, 0), (0, LANES - IDX_DIM - IDX_HEADS)))
    pad = jnp.zeros((d, (O_CZ - ODD_PREP_UNITS) * LANES), BF16)
    out = jnp.concatenate([c_q, c_k, c_v, i_q, m_q, ikw, pad, c_z, m_z], axis=-1)
    assert out.shape == (d, ODD_UNITS * LANES)
    return out


def _overlap_matrix_t(s, n_cmp_pad):
    n_cmp = (s - NSA_CMP_LEN) // NSA_CMP_STRIDE + 1
    n_sel = s // NSA_SEL_LEN
    cmp_start = np.arange(n_cmp) * NSA_CMP_STRIDE
    sel_start = np.arange(n_sel) * NSA_SEL_LEN
    ov = np.clip(np.minimum(cmp_start[:, None] + NSA_CMP_LEN, sel_start[None, :] + NSA_SEL_LEN)
                 - np.maximum(cmp_start[:, None], sel_start[None, :]), 0, None) / NSA_CMP_LEN
    full = np.zeros((n_sel, n_cmp_pad), np.float32)
    full[:, :n_cmp] = ov.T
    return jnp.asarray(full, dtype=BF16)


def _score_bound(dim, gain_q, gain_k, scale):
    return dim * jnp.max(jnp.abs(gain_q)) * jnp.max(jnp.abs(gain_k)) * (scale * LOG2E * 1.02)


def _attend(bound, fn, *operands):
    return lax.cond(bound <= FAST_LOG2_BOUND,
                    lambda *a: fn(*a, bounded=True), lambda *a: fn(*a, bounded=False), *operands)


def kernel(x, mem, norm_gain, mem_norm_gain, mem_w_kv, mem_qk_gain, w_out, even_w_in, nsa_qk_gain, nsa_cmp_pos,
           nsa_cmp_w1, nsa_cmp_w2, diff_qk_gain, diff_lambda, diff_subln_gain, odd_w_in, dsa_qk_gain):
    b, s, d = x.shape
    assert d == D_MODEL and s % TK == 0 and s >= NSA_WINDOW + TQ_SPARSE
    pos = jnp.arange(s)
    cos, sin = _rope_tables(pos, HEAD_DIM // 2, 1)
    cos64, sin64 = _rope_tables(pos, DIFF_QK_DIM // 2, 2)
    n_cmp_pad = s // NSA_CMP_STRIDE
    cmp_last = jnp.arange(n_cmp_pad) * NSA_CMP_STRIDE + NSA_CMP_LEN - 1
    cos_c, sin_c = _rope_tables(cmp_last, HEAD_DIM // 2, 1)
    ovlt = _overlap_matrix_t(s, n_cmp_pad)
    ltri = jnp.asarray(np.tril(np.ones((TK, TK), np.float32), -1), dtype=BF16)
    wbias = _window_bias(TQ_SPARSE)

    mk_all, mv_all = _memkv(mem, mem_norm_gain, mem_w_kv.astype(BF16), mem_qk_gain[:, 1:2, :])
    w_out_b = w_out.astype(BF16)

    x2 = x.reshape(b * s, d)
    for i in range(DEPTH):
        mem_q_gain = mem_qk_gain[i, 0:1, :]
        if i % 2 == 0:
            e = i // 2
            proj = _proj(x2, norm_gain[i], _even_weight(even_w_in[e]), 768)
            dg2 = jnp.tile(diff_qk_gain[e], (1, 2))
            qa, ks, kw, vst, vwt, craw, bq, bk, bvt, mq = _prep_even(
                proj, b, s, cos, sin, cos64, sin64, nsa_qk_gain[e], dg2, mem_q_gain)
            w1 = nsa_cmp_w1[e].reshape(2, NSA_CMP_LEN, HEAD_DIM, NSA_CMP_HIDDEN).astype(BF16)
            kc, vct = _compress(craw, nsa_cmp_pos[e], w1, nsa_cmp_w2[e].astype(BF16),
                                nsa_qk_gain[e, 1:2, :], cos_c, sin_c)
            y_a = _attend(_score_bound(HEAD_DIM, nsa_qk_gain[e, 0], nsa_qk_gain[e, 1:4], HEAD_DIM ** -0.5),
                          functools.partial(_nsa, b=b, s=s), qa, ks, vst, kw, vwt, kc, vct, ovlt, wbias, proj)
            lambda_init = 0.8 - 0.6 * math.exp(-0.3 * i)
            y_b = _attend(_score_bound(DIFF_QK_DIM, diff_qk_gain[e, 0], diff_qk_gain[e, 1], DIFF_QK_DIM ** -0.5),
                          functools.partial(_diff, b=b, s=s, lambda_init=lambda_init),
                          bq, bk, bvt, diff_lambda[e], diff_subln_gain[e].reshape(1, LANES), proj)
            y_m = _memattn(mq, mk_all[i], mv_all[i], proj, b, s, E_MZ)
            ys = [y_a, y_b, y_m]
        else:
            o = i // 2
            proj = _proj(x2, norm_gain[i], _odd_weight(odd_w_in[o]), 512)
            cq, ck, cvt, iq, ik, mq = _prep_odd(proj, b, s, cos, sin, cos64, sin64, dsa_qk_gain[o], mem_q_gain)
            y_c = _attend(_score_bound(HEAD_DIM, dsa_qk_gain[o, 0], dsa_qk_gain[o, 1], HEAD_DIM ** -0.5),
                          functools.partial(_dsa, b=b, s=s), cq, ck, cvt, iq, ik, ltri, proj)
            y_m = _memattn(mq, mk_all[i], mv_all[i], proj, b, s, O_MZ)
            ys = [y_c, y_m]
        x2 = _outproj(x2, ys, w_out_b[i])
    return x2.reshape(b, s, d)
```

```python
import functools
import math

import jax
import jax.numpy as jnp
import numpy as np
from jax import lax
from jax.experimental import pallas as pl
from jax.experimental.pallas import tpu as pltpu

F32 = jnp.float32
BF16 = jnp.bfloat16

D_MODEL = 2048
DEPTH = 4
HEAD_DIM = 128
ROPE_THETA = 10000.0
EPS = 1e-6
MASKED = -1e30
FORCED = 1e9
NSA_HEADS = 8
NSA_KV_GROUPS = 2
NSA_CMP_LEN = 32
NSA_CMP_STRIDE = 16
NSA_CMP_HIDDEN = 256
NSA_SEL_LEN = 64
NSA_SEL_TOPN = 16
NSA_WINDOW = 512
DIFF_HEADS = 4
DIFF_QK_DIM = 64
DSA_HEADS = 12
DSA_KV_HEADS = 4
IDX_HEADS = 16
IDX_DIM = 64
DSA_TOPK_MAX = 256
MEM_HEADS = 4

LANES = 128
SUBLANES = 8
VMEM_LIMIT_BYTES = 56 * 1024 * 1024
LOG2E = 1.4426950408889634

PREP_ROWS = 256
TQ_SPARSE = 128
TQ_DIFF = 256
DIFF_QBLOCKS = 2
NSA_QBLOCKS = 4
TK = 512
VT_BLOCK = 256
NEG_INIT = -1e30
NEG_MASK = -2e30
FAST_LOG2_BOUND = 40.0

EVEN_UNITS = 54
E_AQ, E_AKV, E_BQ, E_BK, E_BV, E_MQ, E_AZ, E_BZ, E_MZ, E_AG = 0, 8, 20, 24, 28, 32, 36, 44, 48, 52
EVEN_PREP_UNITS = E_AZ
ODD_UNITS = 52
O_CQ, O_CK, O_CV, O_IQ, O_MQ, O_IKW, O_CZ, O_MZ = 0, 12, 16, 20, 28, 32, 36, 48
ODD_PREP_UNITS = O_IKW + 1


def _cparams(sem):
    return pltpu.CompilerParams(dimension_semantics=sem, vmem_limit_bytes=VMEM_LIMIT_BYTES)


def _dot(a, b):
    return jnp.dot(a, b, preferred_element_type=F32)


def _dot_nt(a, b):
    return lax.dot_general(a, b, (((1,), (1,)), ((), ())), preferred_element_type=F32)


def _silu(x):
    return x * jax.nn.sigmoid(x)


def _rms_norm(x, gain):
    return x * lax.rsqrt(jnp.mean(x * x, axis=-1, keepdims=True) + EPS) * gain


def _rms_norm_halves(x, gain):
    lo = lax.broadcasted_iota(jnp.int32, x.shape, 1) < 64
    xx = x * x
    s_lo = jnp.sum(jnp.where(lo, xx, 0.0), axis=-1, keepdims=True)
    s_hi = jnp.sum(jnp.where(lo, 0.0, xx), axis=-1, keepdims=True)
    ms = jnp.where(lo, s_lo, s_hi) * (1.0 / 64.0)
    return x * lax.rsqrt(ms + EPS) * gain


def _partner(x, half):
    n = x.shape[-1]
    lane = lax.broadcasted_iota(jnp.int32, x.shape, 1)
    if 2 * half == n:
        return pltpu.roll(x, half, 1)
    a = pltpu.roll(x, half, 1)
    b = pltpu.roll(x, n - half, 1)
    src_a = pltpu.roll(lane, half, 1)
    want = jnp.where((lane & (2 * half - 1)) < half, lane + half, lane - half)
    return jnp.where(src_a == want, a, b)


def _rope(x, cos, sin_signed, half):
    return x * cos + _partner(x, half) * sin_signed


def _transpose_bf16(x):
    n = x.shape[1]
    eye = (lax.broadcasted_iota(jnp.int32, (n, n), 0) == lax.broadcasted_iota(jnp.int32, (n, n), 1)).astype(BF16)
    return _dot_nt(eye, x.astype(BF16)).astype(BF16)


def _fold8(x, op, short_chains=False):
    rows, w = x.shape
    if short_chains:
        x = op(x.reshape(SUBLANES, rows // SUBLANES, w), axis=0)
        rows = rows // SUBLANES
    return op(x.reshape(rows // SUBLANES, SUBLANES, w), axis=0)


def _col_max(x):
    return jnp.max(_fold8(x, jnp.max), axis=0, keepdims=True)


def _col_sum(x):
    return jnp.sum(_fold8(x, jnp.sum), axis=0, keepdims=True)


def _proj_kernel(x_ref, g_ref, w_ref, o_ref, hn_ref):
    @pl.when(pl.program_id(1) == 0)
    def _():
        x = x_ref[...]
        hn_ref[...] = _rms_norm(x, g_ref[...]).astype(BF16)

    o_ref[...] = _dot(hn_ref[...], w_ref[...])


def _proj(x2, gain, w, tn):
    n, d = x2.shape
    c = w.shape[1]
    tm = min(1024, n)
    return pl.pallas_call(
        _proj_kernel,
        grid=(n // tm, c // tn),
        in_specs=[
            pl.BlockSpec((tm, d), lambda i, j: (i, 0)),
            pl.BlockSpec((1, d), lambda i, j: (0, 0)),
            pl.BlockSpec((d, tn), lambda i, j: (0, j)),
        ],
        out_specs=pl.BlockSpec((tm, tn), lambda i, j: (i, j)),
        out_shape=jax.ShapeDtypeStruct((n, c), F32),
        scratch_shapes=[pltpu.VMEM((tm, d), BF16)],
        compiler_params=_cparams(("arbitrary", "arbitrary")),
        name="proj",
    )(x2, gain.reshape(1, d), w)


def _prep_even_kernel(p_ref, cos_ref, sin_ref, cos64_ref, sin64_ref, ng_ref, dg_ref, mg_ref,
                      qa_ref, ks_ref, kw_ref, vst_ref, vwt_ref, craw_ref, bq_ref, bk_ref, bvt_ref, mq_ref):
    cos, sin = cos_ref[...], sin_ref[...]
    cos64, sin64 = cos64_ref[...], sin64_ref[...]
    rep = NSA_HEADS // NSA_KV_GROUPS
    tq = TQ_SPARSE

    def unit(u):
        return p_ref[:, u * LANES:(u + 1) * LANES]

    for h in range(NSA_HEADS):
        q = _rope(_rms_norm(unit(E_AQ + h), ng_ref[0:1, :]), cos, sin, 64).astype(BF16)
        g, r = divmod(h, rep)
        for qb in range(PREP_ROWS // tq):
            qa_ref[0, g, qb, r * tq:(r + 1) * tq, :] = q[qb * tq:(qb + 1) * tq, :]
    for g in range(NSA_KV_GROUPS):
        craw_ref[0, 0, g] = unit(E_AKV + 0 + g)
        craw_ref[0, 1, g] = unit(E_AKV + 2 + g)
        ks_ref[0, g] = _rope(_rms_norm(unit(E_AKV + 4 + g), ng_ref[2:3, :]), cos, sin, 64).astype(BF16)
        vst_ref[0, g, 0] = _transpose_bf16(unit(E_AKV + 6 + g))
        kw_ref[0, g] = _rope(_rms_norm(unit(E_AKV + 8 + g), ng_ref[3:4, :]), cos, sin, 64).astype(BF16)
        vw = unit(E_AKV + 10 + g)
        for kb in range(PREP_ROWS // LANES):
            vwt_ref[0, g, kb] = _transpose_bf16(vw[kb * LANES:(kb + 1) * LANES, :])
    lo = lax.broadcasted_iota(jnp.int32, cos.shape, 1) < 64
    for h in range(DIFF_HEADS):
        q = _rope(_rms_norm_halves(unit(E_BQ + h), dg_ref[0:1, :]), cos64, sin64, 32)
        bq_ref[0, h, 0, 0:PREP_ROWS, :] = jnp.where(lo, q, 0.0).astype(BF16)
        bq_ref[0, h, 0, PREP_ROWS:2 * PREP_ROWS, :] = jnp.where(lo, 0.0, q).astype(BF16)
        bk_ref[0, h] = _rope(_rms_norm_halves(unit(E_BK + h), dg_ref[1:2, :]), cos64, sin64, 32).astype(BF16)
        bvt_ref[0, h, 0] = _transpose_bf16(unit(E_BV + h))
    for h in range(MEM_HEADS):
        mq_ref[0, h] = _rms_norm(unit(E_MQ + h), mg_ref[...]).astype(BF16)


def _prep_even(proj, b, s, cos, sin, cos64, sin64, nsa_gain, diff_gain2, mem_gain):
    ts = PREP_ROWS
    assert ts == TQ_DIFF == VT_BLOCK
    nb = s // ts
    c = EVEN_PREP_UNITS * LANES
    g = NSA_KV_GROUPS
    rep = NSA_HEADS // g
    row = lambda bb, i: (i, 0)
    const = lambda bb, i: (0, 0)
    seq = lambda n: pl.BlockSpec((1, n, ts, LANES), lambda bb, i: (bb, 0, i, 0))
    blk5 = lambda n, k, r, cdim: pl.BlockSpec((1, n, k, r, cdim), lambda bb, i: (bb, 0, i, 0, 0))
    return pl.pallas_call(
        _prep_even_kernel,
        grid=(b, nb),
        in_specs=[
            pl.BlockSpec((ts, c), lambda bb, i: (bb * nb + i, 0)),
            pl.BlockSpec((ts, LANES), row), pl.BlockSpec((ts, LANES), row),
            pl.BlockSpec((ts, LANES), row), pl.BlockSpec((ts, LANES), row),
            pl.BlockSpec((4, LANES), const), pl.BlockSpec((2, LANES), const), pl.BlockSpec((1, LANES), const),
        ],
        out_specs=[
            blk5(g, ts // TQ_SPARSE, rep * TQ_SPARSE, LANES),
            seq(g), seq(g),
            blk5(g, 1, LANES, VT_BLOCK),
            blk5(g, ts // LANES, LANES, LANES),
            pl.BlockSpec((1, 2, g, ts, LANES), lambda bb, i: (bb, 0, 0, i, 0)),
            blk5(DIFF_HEADS, 1, 2 * TQ_DIFF, LANES),
            seq(DIFF_HEADS),
            blk5(DIFF_HEADS, 1, LANES, VT_BLOCK),
            seq(MEM_HEADS),
        ],
        out_shape=[
            jax.ShapeDtypeStruct((b, g, s // TQ_SPARSE, rep * TQ_SPARSE, LANES), BF16),
            jax.ShapeDtypeStruct((b, g, s, LANES), BF16),
            jax.ShapeDtypeStruct((b, g, s, LANES), BF16),
            jax.ShapeDtypeStruct((b, g, s // VT_BLOCK, LANES, VT_BLOCK), BF16),
            jax.ShapeDtypeStruct((b, g, s // LANES, LANES, LANES), BF16),
            jax.ShapeDtypeStruct((b, 2, g, s, LANES), F32),
            jax.ShapeDtypeStruct((b, DIFF_HEADS, s // TQ_DIFF, 2 * TQ_DIFF, LANES), BF16),
            jax.ShapeDtypeStruct((b, DIFF_HEADS, s, LANES), BF16),
            jax.ShapeDtypeStruct((b, DIFF_HEADS, s // VT_BLOCK, LANES, VT_BLOCK), BF16),
            jax.ShapeDtypeStruct((b, MEM_HEADS, s, LANES), BF16),
        ],
        compiler_params=_cparams(("arbitrary", "arbitrary")),
        name="prep_even",
    )(proj, cos, sin, cos64, sin64, nsa_gain, diff_gain2, mem_gain)


def _compress_kernel(x_ref, pe_ref, w1_ref, w2_ref, g_ref, cos_ref, sin_ref, kc_ref, vct_ref, pad_ref, *, s, n_pad):
    for kind in range(2):
        pad_ref[0:s, :] = x_ref[0, kind, 0]
        pad_ref[s:s + NSA_CMP_LEN, :] = jnp.zeros((NSA_CMP_LEN, LANES), F32)
        acc = jnp.zeros((n_pad, NSA_CMP_HIDDEN), F32)
        for l in range(NSA_CMP_LEN):
            rows = pad_ref[pl.ds(l, n_pad, stride=NSA_CMP_STRIDE), :] + pe_ref[kind, l:l + 1, :]
            acc = acc + _dot(rows.astype(BF16), w1_ref[kind, l])
        out = _dot(_silu(acc).astype(BF16), w2_ref[kind])
        if kind == 0:
            kc_ref[0, 0] = _rope(_rms_norm(out, g_ref[...]), cos_ref[...], sin_ref[...], 64).astype(BF16)
        else:
            vct_ref[0, 0] = out.T.astype(BF16)


def _compress(craw, pe, w1, w2, gain, cos_c, sin_c):
    b, _, g, s, _ = craw.shape
    n_pad = s // NSA_CMP_STRIDE
    kern = functools.partial(_compress_kernel, s=s, n_pad=n_pad)
    whole = lambda shape: pl.BlockSpec(shape, lambda bb, gg: (0,) * len(shape))
    return pl.pallas_call(
        kern,
        grid=(b, g),
        in_specs=[
            pl.BlockSpec((1, 2, 1, s, LANES), lambda bb, gg: (bb, 0, gg, 0, 0)),
            whole((2, NSA_CMP_LEN, LANES)),
            whole((2, NSA_CMP_LEN, LANES, NSA_CMP_HIDDEN)),
            whole((2, NSA_CMP_HIDDEN, LANES)),
            whole((1, LANES)), whole((n_pad, LANES)), whole((n_pad, LANES)),
        ],
        out_specs=[
            pl.BlockSpec((1, 1, n_pad, LANES), lambda bb, gg: (bb, gg, 0, 0)),
            pl.BlockSpec((1, 1, LANES, n_pad), lambda bb, gg: (bb, gg, 0, 0)),
        ],
        out_shape=[
            jax.ShapeDtypeStruct((b, g, n_pad, LANES), BF16),
            jax.ShapeDtypeStruct((b, g, LANES, n_pad), BF16),
        ],
        scratch_shapes=[pltpu.VMEM((s + NSA_CMP_LEN, LANES), F32)],
        compiler_params=_cparams(("arbitrary", "arbitrary")),
        name="nsa_compress",
    )(craw, pe, w1, w2, gain, cos_c, sin_c)


def _softmax_parts_t(st, bounded):
    if not bounded:
        st = st - _col_max(st)
    e = jnp.exp2(st)
    return e.astype(BF16), 1.0 / jnp.maximum(_col_sum(e), 1e-30)


def _online_init(m_ref, l_ref, acc_ref):
    m_ref[...] = jnp.full(m_ref.shape, NEG_INIT, F32)
    l_ref[...] = jnp.zeros(l_ref.shape, F32)
    acc_ref[...] = jnp.zeros(acc_ref.shape, F32)


def _online_step(idx, st, pv, m_ref, l_ref, acc_ref, bounded):
    if bounded:
        p = jnp.exp2(st)
        l_ref[idx] = l_ref[idx] + _fold8(p, jnp.sum)
        acc_ref[idx] = acc_ref[idx] + pv(p.astype(BF16))
        return
    m_old = m_ref[idx]
    m_new = jnp.maximum(m_old, _col_max(st))
    alpha = jnp.exp2(m_old - m_new)
    p = jnp.exp2(st - m_new)
    l_ref[idx] = alpha * l_ref[idx] + _fold8(p, jnp.sum)
    acc_ref[idx] = alpha * acc_ref[idx] + pv(p.astype(BF16))
    m_ref[idx] = m_new


def _online_result(idx, l_ref, acc_ref):
    return acc_ref[idx] / jnp.maximum(jnp.sum(l_ref[idx], axis=0, keepdims=True), 1e-30)


def _flash_pairs(n_chunks, n_groups, qk, mask, soft, sa_ref, sb_ref):
    for g in range(n_groups):
        qk(0, g, sa_ref)
    n_pairs = (n_chunks - 1) // 2

    def pair(j, carry):
        a = 2 * j
        bias_a = mask(a, False)
        bias_b = mask(a + 1, False)
        for g in range(n_groups):
            qk(a + 1, g, sb_ref)
            soft(a, g, sa_ref, bias_a)
            qk(a + 2, g, sa_ref)
            soft(a + 1, g, sb_ref, bias_b)
        return carry

    lax.fori_loop(0, n_pairs, pair, 0)
    e = 2 * n_pairs

    @pl.when(e + 1 < n_chunks)
    def _():
        bias_a = mask(e, True)
        bias_b = mask(e + 1, True)
        for g in range(n_groups):
            qk(e + 1, g, sb_ref)
            soft(e, g, sa_ref, bias_a)
            soft(e + 1, g, sb_ref, bias_b)

    @pl.when(e + 1 >= n_chunks)
    def _():
        bias_a = mask(e, True)
        for g in range(n_groups):
            soft(e, g, sa_ref, bias_a)


def _pv_blocks(vt_ref, lead, chunk, tk):
    per = tk // VT_BLOCK

    def pv(p):
        out = _dot(vt_ref[lead + (chunk * per,)], p[0:VT_BLOCK])
        for i in range(1, per):
            out = out + _dot(vt_ref[lead + (chunk * per + i,)], p[i * VT_BLOCK:(i + 1) * VT_BLOCK])
        return out

    return pv


def _top_n_mask_t(scores, n):
    row = lax.broadcasted_iota(jnp.int32, scores.shape, 0).astype(F32)
    height = float(scores.shape[0])
    work = scores
    sel = jnp.zeros(scores.shape, F32)
    for _ in range(n):
        m = jnp.max(work, axis=0, keepdims=True)
        first = jnp.min(jnp.where(work == m, row, height), axis=0, keepdims=True)
        pick = row == first
        sel = jnp.where(pick, 1.0, sel)
        work = jnp.where(pick, -jnp.inf, work)
    return sel


def _nsa_kernel(q_ref, kc_ref, vct_ref, ks_ref, vst_ref, kw_ref, vwt_ref, ovlt_ref, *rest, tq, tk, n_qb, top_n, bounded):
    wbias_refs = rest[:n_qb]
    gl_ref, z_ref, o_ref, m_ref, l_ref, acc_ref, sa_ref, sb_ref, selt_ref, oct_ref, owt_ref = rest[n_qb:]
    rep = NSA_HEADS // NSA_KV_GROUPS
    w = rep * tq
    scale = HEAD_DIM ** -0.5
    step0 = pl.program_id(2) * (n_qb * tq)
    lane_t = lax.broadcasted_iota(jnp.int32, (1, w), 1) & (tq - 1)
    n_cmp_pad = kc_ref.shape[2]
    n_sel = ovlt_ref.shape[0]
    sel_shift = NSA_SEL_LEN.bit_length() - 1
    span = NSA_WINDOW + tq

    for qb in range(n_qb):
        q0 = step0 + qb * tq
        t_row = q0 + lane_t
        t_one = t_row[:, 0:tq]
        q = q_ref[0, 0, qb]

        cmp_last = lax.broadcasted_iota(jnp.int32, (n_cmp_pad, w), 0) * NSA_CMP_STRIDE + (NSA_CMP_LEN - 1)
        st = jnp.where(cmp_last <= t_row, _dot_nt(kc_ref[0, 0], q) * (scale * LOG2E), NEG_MASK)
        p, inv = _softmax_parts_t(st, bounded)
        inv = jnp.where(t_row >= NSA_CMP_LEN - 1, inv, 0.0)
        oct_ref[qb] = _dot(vct_ref[0, 0], p) * inv
        imp_heads = _dot(ovlt_ref[...], p) * inv
        imp = imp_heads[:, 0:tq]
        for r in range(1, rep):
            imp = imp + imp_heads[:, r * tq:(r + 1) * tq]

        j = lax.broadcasted_iota(jnp.int32, (n_sel, tq), 0)
        cur = jnp.right_shift(t_one, sel_shift)
        visible = j <= cur
        forced = (j == 0) | (j >= cur - 1)
        imp = jnp.where(visible, jnp.where(forced, FORCED, imp), MASKED)
        selt_ref[qb] = _top_n_mask_t(imp, top_n)

        start = pl.multiple_of(jnp.maximum(q0 - NSA_WINDOW, 0), tq)
        st = (_dot_nt(kw_ref[0, 0, pl.ds(start, span), :], q) * (scale * LOG2E)
              + jnp.concatenate([wbias_refs[qb][0]] * rep, axis=1))
        p, inv = _softmax_parts_t(st, bounded)
        blk0 = start // LANES
        ow = _dot(vwt_ref[0, 0, blk0], p[0:LANES])
        for i in range(1, span // LANES):
            ow = ow + _dot(vwt_ref[0, 0, blk0 + i], p[i * LANES:(i + 1) * LANES])
        owt_ref[qb] = ow * inv

    _online_init(m_ref, l_ref, acc_ref)
    n_chunks = (step0 + n_qb * tq + tk - 1) // tk
    blocks_per_chunk = tk // NSA_SEL_LEN

    def qk(c, g, dst):
        dst[g] = _dot_nt(ks_ref[0, 0, pl.ds(pl.multiple_of(c * tk, tk), tk), :], q_ref[0, 0, g]) * (scale * LOG2E)

    def mask(c, tail):
        biases = []
        for g in range(n_qb):
            rows = selt_ref[g, pl.ds(pl.multiple_of(c * blocks_per_chunk, blocks_per_chunk), blocks_per_chunk), :]
            ok = jnp.concatenate([jnp.broadcast_to(rows[i:i + 1, :], (NSA_SEL_LEN, tq))
                                  for i in range(blocks_per_chunk)], axis=0) > 0.5
            if tail:
                key = c * tk + lax.broadcasted_iota(jnp.int32, (tk, tq), 0)
                ok = ok & (key <= step0 + g * tq + lane_t[:, 0:tq])
            biases.append(jnp.concatenate([jnp.where(ok, 0.0, NEG_MASK)] * rep, axis=1))
        return biases

    def soft(c, g, src, bias):
        _online_step(g, src[g] + bias[g], _pv_blocks(vst_ref, (0, 0), c, tk), m_ref, l_ref, acc_ref, bounded)

    _flash_pairs(n_chunks, n_qb, qk, mask, soft, sa_ref, sb_ref)

    for qb in range(n_qb):
        rows = slice(qb * tq, (qb + 1) * tq)
        gates_t = jax.nn.sigmoid(gl_ref[rows, :]).T
        os_t = _online_result(qb, l_ref, acc_ref)
        for r in range(rep):
            sl = slice(r * tq, (r + 1) * tq)
            cols = slice(r * LANES, (r + 1) * LANES)
            out_t = (gates_t[3 * r:3 * r + 1, :] * oct_ref[qb, :, sl] + gates_t[3 * r + 1:3 * r + 2, :] * os_t[:, sl]
                     + gates_t[3 * r + 2:3 * r + 3, :] * owt_ref[qb, :, sl])
            o_ref[rows, cols] = (out_t.T * _silu(z_ref[rows, cols])).astype(BF16)


def _window_bias(tq):
    span = NSA_WINDOW + tq
    n_early = NSA_WINDOW // tq
    row = np.arange(span)[:, None]
    lane = np.arange(tq)[None, :]
    tables = []
    for i in range(n_early + 1):
        t = i * tq + lane
        s_pos = (0 if i < n_early else t[0, 0] - NSA_WINDOW) + row
        ok = (s_pos <= t) & (s_pos > t - NSA_WINDOW)
        tables.append(np.where(ok, 0.0, NEG_MASK))
    return jnp.asarray(np.stack(tables), dtype=F32)


def _nsa(qa, ks, vst, kw, vwt, kc, vct, ovlt, wbias, proj, b, s, bounded):
    tq, tk, n_qb = TQ_SPARSE, TK, NSA_QBLOCKS
    assert tk % (n_qb * tq) == 0
    rows = n_qb * tq
    nq = s // rows
    g = NSA_KV_GROUPS
    rep = NSA_HEADS // g
    w = rep * tq
    n_cmp_pad = kc.shape[2]
    n_sel = s // NSA_SEL_LEN
    last_bias = wbias.shape[0] - 1
    kern = functools.partial(_nsa_kernel, tq=tq, tk=tk, n_qb=n_qb, top_n=min(NSA_SEL_TOPN, n_sel), bounded=bounded)
    per_group = lambda shape: pl.BlockSpec((1, 1) + shape, lambda bb, gg, i: (bb, gg) + (0,) * len(shape))
    bias_spec = lambda qb: pl.BlockSpec((1,) + wbias.shape[1:],
                                        lambda bb, gg, i: (jnp.minimum(n_qb * i + qb, last_bias), 0, 0))
    return pl.pallas_call(
        kern,
        grid=(b, g, nq),
        in_specs=[
            pl.BlockSpec((1, 1, n_qb, w, LANES), lambda bb, gg, i: (bb, gg, i, 0, 0)),
            per_group((n_cmp_pad, LANES)), per_group((LANES, n_cmp_pad)),
            per_group((s, LANES)), per_group((s // VT_BLOCK, LANES, VT_BLOCK)),
            per_group((s, LANES)), per_group((s // LANES, LANES, LANES)),
            pl.BlockSpec((n_sel, n_cmp_pad), lambda bb, gg, i: (0, 0)),
        ] + [bias_spec(qb) for qb in range(n_qb)] + [
            pl.BlockSpec((rows, LANES), lambda bb, gg, i: (bb * nq + i, E_AG + gg)),
            pl.BlockSpec((rows, rep * LANES), lambda bb, gg, i: (bb * nq + i, E_AZ // rep + gg)),
        ],
        out_specs=pl.BlockSpec((rows, rep * LANES), lambda bb, gg, i: (bb * nq + i, gg)),
        out_shape=jax.ShapeDtypeStruct((b * s, NSA_HEADS * LANES), BF16),
        scratch_shapes=[
            pltpu.VMEM((n_qb, 1, w), F32), pltpu.VMEM((n_qb, SUBLANES, w), F32), pltpu.VMEM((n_qb, LANES, w), F32),
            pltpu.VMEM((n_qb, tk, w), F32), pltpu.VMEM((n_qb, tk, w), F32),
            pltpu.VMEM((n_qb, n_sel, tq), F32), pltpu.VMEM((n_qb, LANES, w), F32), pltpu.VMEM((n_qb, LANES, w), F32),
        ],
        compiler_params=_cparams(("arbitrary", "arbitrary", "arbitrary")),
        name="nsa_attention",
    )(qa, kc, vct, ks, vst, kw, vwt, ovlt, *([wbias] * n_qb), proj, proj)


def _diff_kernel(q_ref, k_ref, vt_ref, lam_ref, sg_ref, z_ref, o_ref, m_ref, l_ref, acc_ref, sa_ref, sb_ref,
                 *, tq, tk, n_qb, lambda_init, bounded):
    w = 2 * tq
    q0 = pl.program_id(2) * (n_qb * tq)
    lane_t = lax.broadcasted_iota(jnp.int32, (1, w), 1) & (tq - 1)
    _online_init(m_ref, l_ref, acc_ref)
    n_chunks = (q0 + n_qb * tq + tk - 1) // tk

    def qk(c, g, dst):
        dst[g] = (_dot_nt(k_ref[0, 0, pl.ds(pl.multiple_of(c * tk, tk), tk), :], q_ref[0, 0, g])
                  * (DIFF_QK_DIM ** -0.5 * LOG2E))

    def mask(c, tail):
        if not tail:
            return None
        key = c * tk + lax.broadcasted_iota(jnp.int32, (tk, w), 0)
        return [jnp.where(key <= q0 + g * tq + lane_t, 0.0, NEG_MASK) for g in range(n_qb)]

    def soft(c, g, src, bias):
        st = src[g] if bias is None else src[g] + bias[g]
        _online_step(g, st, _pv_blocks(vt_ref, (0, 0), c, tk), m_ref, l_ref, acc_ref, bounded)

    _flash_pairs(n_chunks, n_qb, qk, mask, soft, sa_ref, sb_ref)

    lv = lam_ref[...]
    lam = (jnp.exp(jnp.sum(lv[0:1] * lv[1:2], axis=-1, keepdims=True))
           - jnp.exp(jnp.sum(lv[2:3] * lv[3:4], axis=-1, keepdims=True)) + lambda_init)
    for g in range(n_qb):
        rows = slice(g * tq, (g + 1) * tq)
        o_t = _online_result(g, l_ref, acc_ref)
        o = (o_t[:, 0:tq] - lam * o_t[:, tq:w]).T
        o = _rms_norm(o, sg_ref[...]) * (1.0 - lambda_init)
        o_ref[rows, :] = (o * _silu(z_ref[rows, :])).astype(BF16)


def _diff(bq, bk, bvt, lam_vecs, subln_gain, proj, b, s, lambda_init, bounded):
    tq, tk, n_qb = TQ_DIFF, TK, DIFF_QBLOCKS
    assert n_qb * tq == tk
    rows = n_qb * tq
    nq = s // rows
    w = 2 * tq
    kern = functools.partial(_diff_kernel, tq=tq, tk=tk, n_qb=n_qb, lambda_init=lambda_init, bounded=bounded)
    return pl.pallas_call(
        kern,
        grid=(b, DIFF_HEADS, nq),
        in_specs=[
            pl.BlockSpec((1, 1, n_qb, w, LANES), lambda bb, h, i: (bb, h, i, 0, 0)),
            pl.BlockSpec((1, 1, s, LANES), lambda bb, h, i: (bb, h, 0, 0)),
            pl.BlockSpec((1, 1, s // VT_BLOCK, LANES, VT_BLOCK), lambda bb, h, i: (bb, h, 0, 0, 0)),
            pl.BlockSpec((4, DIFF_QK_DIM), lambda bb, h, i: (0, 0)),
            pl.BlockSpec((1, LANES), lambda bb, h, i: (0, 0)),
            pl.BlockSpec((rows, LANES), lambda bb, h, i: (bb * nq + i, E_BZ + h)),
        ],
        out_specs=pl.BlockSpec((rows, LANES), lambda bb, h, i: (bb * nq + i, h)),
        out_shape=jax.ShapeDtypeStruct((b * s, DIFF_HEADS * LANES), BF16),
        scratch_shapes=[
            pltpu.VMEM((n_qb, 1, w), F32), pltpu.VMEM((n_qb, SUBLANES, w), F32), pltpu.VMEM((n_qb, LANES, w), F32),
            pltpu.VMEM((n_qb, tk, w), F32), pltpu.VMEM((n_qb, tk, w), F32),
        ],
        compiler_params=_cparams(("arbitrary", "arbitrary", "arbitrary")),
        name="diff_attention",
    )(bq, bk, bvt, lam_vecs, subln_gain, proj)


def _memkv_kernel(mem_ref, mg_ref, w_ref, kg_ref, k_ref, v_ref):
    mem_n = _rms_norm(mem_ref[0], mg_ref[...]).astype(BF16)
    kv = _dot(mem_n, w_ref[0])
    for h in range(MEM_HEADS):
        k_ref[0, 0, h] = _rms_norm(kv[:, h * LANES:(h + 1) * LANES], kg_ref[0]).astype(BF16)
        v_ref[0, 0, h] = kv[:, (MEM_HEADS + h) * LANES:(MEM_HEADS + h + 1) * LANES].astype(BF16)


def _memkv(mem, mem_gain, w_kv, k_gain):
    b, m, d = mem.shape
    depth = w_kv.shape[0]
    c = w_kv.shape[2]
    out = jax.ShapeDtypeStruct((depth, b, MEM_HEADS, m, LANES), BF16)
    ospec = pl.BlockSpec((1, 1, MEM_HEADS, m, LANES), lambda i, bb: (i, bb, 0, 0, 0))
    return pl.pallas_call(
        _memkv_kernel,
        grid=(depth, b),
        in_specs=[
            pl.BlockSpec((1, m, d), lambda i, bb: (bb, 0, 0)),
            pl.BlockSpec((1, d), lambda i, bb: (0, 0)),
            pl.BlockSpec((1, d, c), lambda i, bb: (i, 0, 0)),
            pl.BlockSpec((1, 1, LANES), lambda i, bb: (i, 0, 0)),
        ],
        out_specs=[ospec, ospec],
        out_shape=[out, out],
        compiler_params=_cparams(("arbitrary", "arbitrary")),
        name="mem_kv",
    )(mem, mem_gain.reshape(1, d), w_kv, k_gain)


def _memattn_kernel(q_ref, k_ref, v_ref, z_ref, o_ref):
    scale = HEAD_DIM ** -0.5
    for h in range(MEM_HEADS):
        s = _dot_nt(q_ref[0, h], k_ref[0, h]) * scale
        e = jnp.exp(s - jnp.max(s, axis=-1, keepdims=True))
        p = e / jnp.sum(e, axis=-1, keepdims=True)
        o = _dot(p.astype(BF16), v_ref[0, h])
        o_ref[:, h * LANES:(h + 1) * LANES] = (o * _silu(z_ref[:, h * LANES:(h + 1) * LANES])).astype(BF16)


def _memattn(mq, mk, mv, proj, b, s, z_unit):
    tq = min(512, s)
    nq = s // tq
    m = mk.shape[2]
    kvspec = pl.BlockSpec((1, MEM_HEADS, m, LANES), lambda bb, i: (bb, 0, 0, 0))
    return pl.pallas_call(
        _memattn_kernel,
        grid=(b, nq),
        in_specs=[
            pl.BlockSpec((1, MEM_HEADS, tq, LANES), lambda bb, i: (bb, 0, i, 0)),
            kvspec, kvspec,
            pl.BlockSpec((tq, MEM_HEADS * LANES), lambda bb, i: (bb * nq + i, z_unit // MEM_HEADS)),
        ],
        out_specs=pl.BlockSpec((tq, MEM_HEADS * LANES), lambda bb, i: (bb * nq + i, 0)),
        out_shape=jax.ShapeDtypeStruct((b * s, MEM_HEADS * LANES), BF16),
        compiler_params=_cparams(("arbitrary", "arbitrary")),
        name="mem_attention",
    )(mq, mk, mv, proj)


def _prep_odd_kernel(p_ref, cos_ref, sin_ref, cos64_ref, sin64_ref, cg_ref, mg_ref,
                     cq_ref, ck_ref, cvt_ref, iq_ref, ik_ref, mq_ref):
    cos, sin = cos_ref[...], sin_ref[...]
    cos64, sin64 = cos64_ref[...], sin64_ref[...]
    rep = DSA_HEADS // DSA_KV_HEADS
    tq = TQ_SPARSE
    n_qb = PREP_ROWS // tq

    def unit(u):
        return p_ref[:, u * LANES:(u + 1) * LANES]

    for h in range(DSA_HEADS):
        q = _rope(_rms_norm(unit(O_CQ + h), cg_ref[0:1, :]), cos, sin, 64).astype(BF16)
        g, r = divmod(h, rep)
        for qb in range(n_qb):
            cq_ref[0, g, qb, r * tq:(r + 1) * tq, :] = q[qb * tq:(qb + 1) * tq, :]
    for h in range(DSA_KV_HEADS):
        ck_ref[0, h] = _rope(_rms_norm(unit(O_CK + h), cg_ref[1:2, :]), cos, sin, 64).astype(BF16)
        cvt_ref[0, h, 0] = _transpose_bf16(unit(O_CV + h))
    lo = lax.broadcasted_iota(jnp.int32, cos.shape, 1) < 64
    for u in range(IDX_HEADS // 2):
        x = _rope(unit(O_IQ + u), cos64, sin64, 32)
        even = jnp.where(lo, x, 0.0).astype(BF16)
        odd = jnp.where(lo, pltpu.roll(x, 64, 1), 0.0).astype(BF16)
        for qb in range(n_qb):
            iq_ref[0, qb, (2 * u) * tq:(2 * u + 1) * tq, :] = even[qb * tq:(qb + 1) * tq, :]
            iq_ref[0, qb, (2 * u + 1) * tq:(2 * u + 2) * tq, :] = odd[qb * tq:(qb + 1) * tq, :]
    ik = _rope(unit(O_IKW), cos64, sin64, 32)
    ik_ref[0] = jnp.where(lo, ik, 0.0).astype(BF16)
    for h in range(MEM_HEADS):
        mq_ref[0, h] = _rms_norm(unit(O_MQ + h), mg_ref[...]).astype(BF16)


def _prep_odd(proj, b, s, cos, sin, cos64, sin64, dsa_gain, mem_gain):
    ts = PREP_ROWS
    nb = s // ts
    c = ODD_PREP_UNITS * LANES
    rep = DSA_HEADS // DSA_KV_HEADS
    row = lambda bb, i: (i, 0)
    const = lambda bb, i: (0, 0)
    seq = lambda n: pl.BlockSpec((1, n, ts, LANES), lambda bb, i: (bb, 0, i, 0))
    return pl.pallas_call(
        _prep_odd_kernel,
        grid=(b, nb),
        in_specs=[
            pl.BlockSpec((ts, c), lambda bb, i: (bb * nb + i, 0)),
            pl.BlockSpec((ts, LANES), row), pl.BlockSpec((ts, LANES), row),
            pl.BlockSpec((ts, LANES), row), pl.BlockSpec((ts, LANES), row),
            pl.BlockSpec((2, LANES), const), pl.BlockSpec((1, LANES), const),
        ],
        out_specs=[
            pl.BlockSpec((1, DSA_KV_HEADS, ts // TQ_SPARSE, rep * TQ_SPARSE, LANES), lambda bb, i: (bb, 0, i, 0, 0)),
            seq(DSA_KV_HEADS),
            pl.BlockSpec((1, DSA_KV_HEADS, 1, LANES, VT_BLOCK), lambda bb, i: (bb, 0, i, 0, 0)),
            pl.BlockSpec((1, ts // TQ_SPARSE, IDX_HEADS * TQ_SPARSE, LANES), lambda bb, i: (bb, i, 0, 0)),
            pl.BlockSpec((1, ts, LANES), lambda bb, i: (bb, i, 0)),
            seq(MEM_HEADS),
        ],
        out_shape=[
            jax.ShapeDtypeStruct((b, DSA_KV_HEADS, s // TQ_SPARSE, rep * TQ_SPARSE, LANES), BF16),
            jax.ShapeDtypeStruct((b, DSA_KV_HEADS, s, LANES), BF16),
            jax.ShapeDtypeStruct((b, DSA_KV_HEADS, s // VT_BLOCK, LANES, VT_BLOCK), BF16),
            jax.ShapeDtypeStruct((b, s // TQ_SPARSE, IDX_HEADS * TQ_SPARSE, LANES), BF16),
            jax.ShapeDtypeStruct((b, s, LANES), BF16),
            jax.ShapeDtypeStruct((b, MEM_HEADS, s, LANES), BF16),
        ],
        compiler_params=_cparams(("arbitrary", "arbitrary")),
        name="prep_odd",
    )(proj, cos, sin, cos64, sin64, dsa_gain, mem_gain)


def _sortable_key(x):
    bits = pltpu.bitcast(x + 0.0, jnp.int32)
    return jnp.where(bits < 0, bits ^ jnp.int32(0x7FFFFFFF), bits)


def _dsa_kernel(q_ref, k_ref, vt_ref, iq_ref, ik_ref, w_ref, ltri_ref, z_ref, o_ref,
                key_ref, m_ref, l_ref, acc_ref, sa_ref, sb_ref, ties_ref, *, tq, tk, top_k, bounded):
    rep = DSA_HEADS // DSA_KV_HEADS
    scale = HEAD_DIM ** -0.5
    q0 = pl.program_id(1) * tq
    t_row = q0 + lax.broadcasted_iota(jnp.int32, (1, tq), 1)
    n_chunks = (q0 + tq + tk - 1) // tk
    int_min = jnp.int32(-2147483648)
    heads_per_dot = 4

    w_t = (w_ref[...] * (IDX_HEADS ** -0.5 * IDX_DIM ** -0.5)).T

    def score_chunk(c):
        k0 = pl.multiple_of(c * tk, tk)
        ik = ik_ref[0, pl.ds(k0, tk), :]
        sc = jnp.zeros((tk, tq), F32)
        for h0 in range(0, IDX_HEADS, heads_per_dot):
            x = _dot_nt(ik, iq_ref[0, 0, h0 * tq:(h0 + heads_per_dot) * tq, :])
            for hh in range(heads_per_dot):
                h = h0 + hh
                sc = sc + jnp.maximum(x[:, hh * tq:(hh + 1) * tq], 0.0) * w_t[IDX_DIM + h:IDX_DIM + h + 1, :]
        causal = (k0 + lax.broadcasted_iota(jnp.int32, (tk, tq), 0)) <= t_row
        key_ref[c] = _sortable_key(jnp.where(causal, sc, MASKED))

    def score_pair(j, carry):
        score_chunk(2 * j)
        score_chunk(2 * j + 1)
        return carry

    def score_one(c, carry):
        score_chunk(c)
        return carry

    lax.fori_loop(0, n_chunks // 2, score_pair, 0)
    lax.fori_loop(n_chunks - n_chunks % 2, n_chunks, score_one, 0)

    def count(pred):
        def one(c):
            return _fold8(pred(key_ref[c]).astype(F32), jnp.sum, short_chains=True)

        def four(j, acc):
            return acc + ((one(4 * j) + one(4 * j + 1)) + (one(4 * j + 2) + one(4 * j + 3)))

        acc = lax.fori_loop(0, n_chunks // 4, four, jnp.zeros((SUBLANES, tq), F32))
        acc = lax.fori_loop(n_chunks - n_chunks % 4, n_chunks, lambda c, a: a + one(c), acc)
        return jnp.sum(acc, axis=0, keepdims=True)

    def search(it, state):
        thr_u, above, reached = state
        cand_u = thr_u | jnp.left_shift(jnp.int32(1), 31 - it)
        cand = cand_u ^ int_min
        cnt = count(lambda kk: kk >= cand)
        take = cnt >= top_k
        return jnp.where(take, cand_u, thr_u), jnp.where(take, above, cnt), jnp.where(take, cnt, reached)

    all_keys = jnp.full((1, tq), (n_chunks * tk).astype(F32))
    thr_u, above, reached = lax.fori_loop(
        0, 32, search, (jnp.zeros((1, tq), jnp.int32), jnp.zeros((1, tq), F32), all_keys))
    thr = thr_u ^ int_min
    budget = top_k - above
    surplus = jnp.max(reached) > top_k

    _online_init(m_ref, l_ref, acc_ref)
    ties_ref[...] = jnp.zeros(ties_ref.shape, F32)

    def qk(c, g, dst):
        dst[g] = _dot_nt(k_ref[0, g, pl.ds(pl.multiple_of(c * tk, tk), tk), :], q_ref[0, g, 0]) * (scale * LOG2E)

    def mask_ranked(c, tail):
        keys = key_ref[c]
        tie = keys == thr
        rank = _dot(ltri_ref[...], tie.astype(BF16)) + ties_ref[...]
        ok = (keys > thr) | (tie & (rank < budget))
        if tail:
            ok = ok & ((c * tk + lax.broadcasted_iota(jnp.int32, (tk, tq), 0)) <= t_row)
        ties_ref[...] = ties_ref[...] + _col_sum(tie.astype(F32))
        return jnp.concatenate([jnp.where(ok, 0.0, NEG_MASK)] * rep, axis=1)

    def mask_plain(c, tail):
        ok = key_ref[c] >= thr
        if tail:
            ok = ok & ((c * tk + lax.broadcasted_iota(jnp.int32, (tk, tq), 0)) <= t_row)
        return jnp.concatenate([jnp.where(ok, 0.0, NEG_MASK)] * rep, axis=1)

    def soft(c, g, src, bias):
        _online_step(g, src[g] + bias, _pv_blocks(vt_ref, (0, g), c, tk), m_ref, l_ref, acc_ref, bounded)

    @pl.when(surplus)
    def _():
        _flash_pairs(n_chunks, DSA_KV_HEADS, qk, mask_ranked, soft, sa_ref, sb_ref)

    @pl.when(jnp.logical_not(surplus))
    def _():
        _flash_pairs(n_chunks, DSA_KV_HEADS, qk, mask_plain, soft, sa_ref, sb_ref)

    for g in range(DSA_KV_HEADS):
        o_t = _online_result(g, l_ref, acc_ref)
        for r in range(rep):
            h = g * rep + r
            o = o_t[:, r * tq:(r + 1) * tq].T
            o_ref[:, h * LANES:(h + 1) * LANES] = (o * _silu(z_ref[:, h * LANES:(h + 1) * LANES])).astype(BF16)


def _dsa(cq, ck, cvt, iq, ik, ltri, proj, b, s, bounded):
    tq, tk = TQ_SPARSE, TK
    nq = s // tq
    g = DSA_KV_HEADS
    rep = DSA_HEADS // g
    w = rep * tq
    kern = functools.partial(_dsa_kernel, tq=tq, tk=tk, top_k=min(DSA_TOPK_MAX, s // 4), bounded=bounded)
    return pl.pallas_call(
        kern,
        grid=(b, nq),
        in_specs=[
            pl.BlockSpec((1, g, 1, w, LANES), lambda bb, i: (bb, 0, i, 0, 0)),
            pl.BlockSpec((1, g, s, LANES), lambda bb, i: (bb, 0, 0, 0), pipeline_mode=pl.Buffered(1)),
            pl.BlockSpec((1, g, s // VT_BLOCK, LANES, VT_BLOCK), lambda bb, i: (bb, 0, 0, 0, 0),
                         pipeline_mode=pl.Buffered(1)),
            pl.BlockSpec((1, 1, IDX_HEADS * tq, LANES), lambda bb, i: (bb, i, 0, 0)),
            pl.BlockSpec((1, s, LANES), lambda bb, i: (bb, 0, 0), pipeline_mode=pl.Buffered(1)),
            pl.BlockSpec((tq, LANES), lambda bb, i: (bb * nq + i, O_IKW)),
            pl.BlockSpec((tk, tk), lambda bb, i: (0, 0)),
            pl.BlockSpec((tq, DSA_HEADS * LANES), lambda bb, i: (bb * nq + i, O_CZ // DSA_HEADS)),
        ],
        out_specs=pl.BlockSpec((tq, DSA_HEADS * LANES), lambda bb, i: (bb * nq + i, 0)),
        out_shape=jax.ShapeDtypeStruct((b * s, DSA_HEADS * LANES), BF16),
        scratch_shapes=[
            pltpu.VMEM((s // tk, tk, tq), jnp.int32),
            pltpu.VMEM((g, 1, w), F32), pltpu.VMEM((g, SUBLANES, w), F32), pltpu.VMEM((g, LANES, w), F32),
            pltpu.VMEM((g, tk, w), F32), pltpu.VMEM((g, tk, w), F32),
            pltpu.VMEM((1, tq), F32),
        ],
        compiler_params=_cparams(("arbitrary", "arbitrary")),
        name="dsa_attention",
    )(cq, ck, cvt, iq, ik, proj, ltri, proj)


def _outproj_kernel(*refs, n_parts):
    x_ref = refs[0]
    y_refs = refs[1:1 + n_parts]
    w_refs = refs[1 + n_parts:1 + 2 * n_parts]
    o_ref = refs[1 + 2 * n_parts]
    acc = x_ref[...]
    for y_ref, w_ref in zip(y_refs, w_refs):
        acc = acc + _dot(y_ref[...], w_ref[...])
    o_ref[...] = acc


def _outproj(x2, ys, w_out):
    n, d = x2.shape
    tm = min(1024, n)
    tn = 1024
    widths = [y.shape[1] for y in ys]
    starts = np.cumsum([0] + widths[:-1]).tolist()
    ws = [w_out[st:st + wd] for st, wd in zip(starts, widths)]
    kern = functools.partial(_outproj_kernel, n_parts=len(ys))
    return pl.pallas_call(
        kern,
        grid=(d // tn, n // tm),
        in_specs=([pl.BlockSpec((tm, tn), lambda j, i: (i, j))]
                  + [pl.BlockSpec((tm, wd), lambda j, i: (i, 0)) for wd in widths]
                  + [pl.BlockSpec((wd, tn), lambda j, i: (0, j)) for wd in widths]),
        out_specs=pl.BlockSpec((tm, tn), lambda j, i: (i, j)),
        out_shape=jax.ShapeDtypeStruct((n, d), F32),
        compiler_params=_cparams(("arbitrary", "arbitrary")),
        name="out_proj",
    )(x2, *ys, *ws)


def _rope_tables(pos, half, reps):
    inv = ROPE_THETA ** (-jnp.arange(half, dtype=F32) / half)
    ang = pos.astype(F32)[:, None] * inv[None, :]
    cos, sin = jnp.cos(ang), jnp.sin(ang)
    return jnp.tile(jnp.concatenate([cos, cos], -1), (1, reps)), jnp.tile(jnp.concatenate([-sin, sin], -1), (1, reps))


def _split_cols(w, sizes):
    return jnp.split(w, np.cumsum(sizes)[:-1].tolist(), axis=-1)


def _even_weight(w):
    sizes = (1024, 1536, 24, 1024, 512, 512, 512, 512, 512, 512)
    a_q, a_kv, a_g, a_z, b_q, b_k, b_v, b_z, m_q, m_z = _split_cols(w.astype(BF16), sizes)
    d = w.shape[0]
    per_group = 3 * NSA_HEADS // NSA_KV_GROUPS
    gates = [jnp.pad(a_g[:, g * per_group:(g + 1) * per_group], ((0, 0), (0, LANES - per_group)))
             for g in range(NSA_KV_GROUPS)]
    out = jnp.concatenate([a_q, a_kv, b_q, b_k, b_v, m_q, a_z, b_z, m_z] + gates, axis=-1)
    assert out.shape == (d, EVEN_UNITS * LANES)
    return out


def _odd_weight(w):
    sizes = (1536, 512, 512, 1024, 64, 16, 1536, 512, 512)
    c_q, c_k, c_v, i_q, i_k, i_w, c_z, m_q, m_z = _split_cols(w.astype(BF16), sizes)
    d = w.shape[0]
    ikw = jnp.pad(jnp.concatenate([i_k, i_w], -1), ((0, 0), (0, LANES - IDX_DIM - IDX_HEADS)))
    pad = jnp.zeros((d, (O_CZ - ODD_PREP_UNITS) * LANES), BF16)
    out = jnp.concatenate([c_q, c_k, c_v, i_q, m_q, ikw, pad, c_z, m_z], axis=-1)
    assert out.shape == (d, ODD_UNITS * LANES)
    return out


def _overlap_matrix_t(s, n_cmp_pad):
    n_cmp = (s - NSA_CMP_LEN) // NSA_CMP_STRIDE + 1
    n_sel = s // NSA_SEL_LEN
    cmp_start = np.arange(n_cmp) * NSA_CMP_STRIDE
    sel_start = np.arange(n_sel) * NSA_SEL_LEN
    ov = np.clip(np.minimum(cmp_start[:, None] + NSA_CMP_LEN, sel_start[None, :] + NSA_SEL_LEN)
                 - np.maximum(cmp_start[:, None], sel_start[None, :]), 0, None) / NSA_CMP_LEN
    full = np.zeros((n_sel, n_cmp_pad), np.float32)
    full[:, :n_cmp] = ov.T
    return jnp.asarray(full, dtype=BF16)


def _score_bound(dim, gain_q, gain_k, scale):
    return dim * jnp.max(jnp.abs(gain_q)) * jnp.max(jnp.abs(gain_k)) * (scale * LOG2E * 1.02)


def _attend(bound, fn, *operands):
    return lax.cond(bound <= FAST_LOG2_BOUND,
                    lambda *a: fn(*a, bounded=True), lambda *a: fn(*a, bounded=False), *operands)


def kernel(x, mem, norm_gain, mem_norm_gain, mem_w_kv, mem_qk_gain, w_out, even_w_in, nsa_qk_gain, nsa_cmp_pos,
           nsa_cmp_w1, nsa_cmp_w2, diff_qk_gain, diff_lambda, diff_subln_gain, odd_w_in, dsa_qk_gain):
    b, s, d = x.shape
    assert d == D_MODEL and s % TK == 0 and s >= NSA_WINDOW + TQ_SPARSE
    pos = jnp.arange(s)
    cos, sin = _rope_tables(pos, HEAD_DIM // 2, 1)
    cos64, sin64 = _rope_tables(pos, DIFF_QK_DIM // 2, 2)
    n_cmp_pad = s // NSA_CMP_STRIDE
    cmp_last = jnp.arange(n_cmp_pad) * NSA_CMP_STRIDE + NSA_CMP_LEN - 1
    cos_c, sin_c = _rope_tables(cmp_last, HEAD_DIM // 2, 1)
    ovlt = _overlap_matrix_t(s, n_cmp_pad)
    ltri = jnp.asarray(np.tril(np.ones((TK, TK), np.float32), -1), dtype=BF16)
    wbias = _window_bias(TQ_SPARSE)

    mk_all, mv_all = _memkv(mem, mem_norm_gain, mem_w_kv.astype(BF16), mem_qk_gain[:, 1:2, :])
    w_out_b = w_out.astype(BF16)

    x2 = x.reshape(b * s, d)
    for i in range(DEPTH):
        mem_q_gain = mem_qk_gain[i, 0:1, :]
        if i % 2 == 0:
            e = i // 2
            proj = _proj(x2, norm_gain[i], _even_weight(even_w_in[e]), 768)
            dg2 = jnp.tile(diff_qk_gain[e], (1, 2))
            qa, ks, kw, vst, vwt, craw, bq, bk, bvt, mq = _prep_even(
                proj, b, s, cos, sin, cos64, sin64, nsa_qk_gain[e], dg2, mem_q_gain)
            w1 = nsa_cmp_w1[e].reshape(2, NSA_CMP_LEN, HEAD_DIM, NSA_CMP_HIDDEN).astype(BF16)
            kc, vct = _compress(craw, nsa_cmp_pos[e], w1, nsa_cmp_w2[e].astype(BF16),
                                nsa_qk_gain[e, 1:2, :], cos_c, sin_c)
            y_a = _attend(_score_bound(HEAD_DIM, nsa_qk_gain[e, 0], nsa_qk_gain[e, 1:4], HEAD_DIM ** -0.5),
                          functools.partial(_nsa, b=b, s=s), qa, ks, vst, kw, vwt, kc, vct, ovlt, wbias, proj)
            lambda_init = 0.8 - 0.6 * math.exp(-0.3 * i)
            y_b = _attend(_score_bound(DIFF_QK_DIM, diff_qk_gain[e, 0], diff_qk_gain[e, 1], DIFF_QK_DIM ** -0.5),
                          functools.partial(_diff, b=b, s=s, lambda_init=lambda_init),
                          bq, bk, bvt, diff_lambda[e], diff_subln_gain[e].reshape(1, LANES), proj)
            y_m = _memattn(mq, mk_all[i], mv_all[i], proj, b, s, E_MZ)
            ys = [y_a, y_b, y_m]
        else:
            o = i // 2
            proj = _proj(x2, norm_gain[i], _odd_weight(odd_w_in[o]), 512)
            cq, ck, cvt, iq, ik, mq = _prep_odd(proj, b, s, cos, sin, cos64, sin64, dsa_qk_gain[o], mem_q_gain)
            y_c = _attend(_score_bound(HEAD_DIM, dsa_qk_gain[o, 0], dsa_qk_gain[o, 1], HEAD_DIM ** -0.5),
                          functools.partial(_dsa, b=b, s=s), cq, ck, cvt, iq, ik, ltri, proj)
            y_m = _memattn(mq, mk_all[i], mv_all[i], proj, b, s, O_MZ)
            ys = [y_c, y_m]
        x2 = _outproj(x2, ys, w_out_b[i])
    return x2.reshape(b, s, d)
```

```python
import functools
import math

import jax
import jax.numpy as jnp
import numpy as np
from jax import lax
from jax.experimental import pallas as pl
from jax.experimental.pallas import tpu as pltpu

F32 = jnp.float32
BF16 = jnp.bfloat16

D_MODEL = 2048
DEPTH = 4
HEAD_DIM = 128
ROPE_THETA = 10000.0
EPS = 1e-6
MASKED = -1e30
FORCED = 1e9
NSA_HEADS = 8
NSA_KV_GROUPS = 2
NSA_CMP_LEN = 32
NSA_CMP_STRIDE = 16
NSA_CMP_HIDDEN = 256
NSA_SEL_LEN = 64
NSA_SEL_TOPN = 16
NSA_WINDOW = 512
DIFF_HEADS = 4
DIFF_QK_DIM = 64
DSA_HEADS = 12
DSA_KV_HEADS = 4
IDX_HEADS = 16
IDX_DIM = 64
DSA_TOPK_MAX = 256
MEM_HEADS = 4

LANES = 128
SUBLANES = 8
VMEM_LIMIT_BYTES = 56 * 1024 * 1024
LOG2E = 1.4426950408889634

PREP_ROWS = 256
TQ_SPARSE = 128
TQ_DIFF = 256
DIFF_QBLOCKS = 2
NSA_QBLOCKS = 4
TK = 512
VT_BLOCK = 256
NEG_INIT = -1e30
NEG_MASK = -2e30
FAST_LOG2_BOUND = 40.0

EVEN_UNITS = 54
E_AQ, E_AKV, E_BQ, E_BK, E_BV, E_MQ, E_AZ, E_BZ, E_MZ, E_AG = 0, 8, 20, 24, 28, 32, 36, 44, 48, 52
EVEN_PREP_UNITS = E_AZ
ODD_UNITS = 52
O_CQ, O_CK, O_CV, O_IQ, O_MQ, O_IKW, O_CZ, O_MZ = 0, 12, 16, 20, 28, 32, 36, 48
ODD_PREP_UNITS = O_IKW + 1


def _cparams(sem):
    return pltpu.CompilerParams(dimension_semantics=sem, vmem_limit_bytes=VMEM_LIMIT_BYTES)


def _dot(a, b):
    return jnp.dot(a, b, preferred_element_type=F32)


def _dot_nt(a, b):
    return lax.dot_general(a, b, (((1,), (1,)), ((), ())), preferred_element_type=F32)


def _silu(x):
    return x * jax.nn.sigmoid(x)


def _rms_norm(x, gain):
    return x * lax.rsqrt(jnp.mean(x * x, axis=-1, keepdims=True) + EPS) * gain


def _rms_norm_halves(x, gain):
    lo = lax.broadcasted_iota(jnp.int32, x.shape, 1) < 64
    xx = x * x
    s_lo = jnp.sum(jnp.where(lo, xx, 0.0), axis=-1, keepdims=True)
    s_hi = jnp.sum(jnp.where(lo, 0.0, xx), axis=-1, keepdims=True)
    ms = jnp.where(lo, s_lo, s_hi) * (1.0 / 64.0)
    return x * lax.rsqrt(ms + EPS) * gain


def _partner(x, half):
    n = x.shape[-1]
    lane = lax.broadcasted_iota(jnp.int32, x.shape, 1)
    if 2 * half == n:
        return pltpu.roll(x, half, 1)
    a = pltpu.roll(x, half, 1)
    b = pltpu.roll(x, n - half, 1)
    src_a = pltpu.roll(lane, half, 1)
    want = jnp.where((lane & (2 * half - 1)) < half, lane + half, lane - half)
    return jnp.where(src_a == want, a, b)


def _rope(x, cos, sin_signed, half):
    return x * cos + _partner(x, half) * sin_signed


def _transpose_bf16(x):
    n = x.shape[1]
    eye = (lax.broadcasted_iota(jnp.int32, (n, n), 0) == lax.broadcasted_iota(jnp.int32, (n, n), 1)).astype(BF16)
    return _dot_nt(eye, x.astype(BF16)).astype(BF16)


def _fold8(x, op, short_chains=False):
    rows, w = x.shape
    if short_chains:
        x = op(x.reshape(SUBLANES, rows // SUBLANES, w), axis=0)
        rows = rows // SUBLANES
    return op(x.reshape(rows // SUBLANES, SUBLANES, w), axis=0)


def _col_max(x):
    return jnp.max(_fold8(x, jnp.max), axis=0, keepdims=True)


def _col_sum(x):
    return jnp.sum(_fold8(x, jnp.sum), axis=0, keepdims=True)


def _proj_kernel(x_ref, g_ref, w_ref, o_ref, hn_ref):
    @pl.when(pl.program_id(1) == 0)
    def _():
        x = x_ref[...]
        hn_ref[...] = _rms_norm(x, g_ref[...]).astype(BF16)

    o_ref[...] = _dot(hn_ref[...], w_ref[...])


def _proj(x2, gain, w, tn):
    n, d = x2.shape
    c = w.shape[1]
    tm = min(1024, n)
    return pl.pallas_call(
        _proj_kernel,
        grid=(n // tm, c // tn),
        in_specs=[
            pl.BlockSpec((tm, d), lambda i, j: (i, 0)),
            pl.BlockSpec((1, d), lambda i, j: (0, 0)),
            pl.BlockSpec((d, tn), lambda i, j: (0, j)),
        ],
        out_specs=pl.BlockSpec((tm, tn), lambda i, j: (i, j)),
        out_shape=jax.ShapeDtypeStruct((n, c), F32),
        scratch_shapes=[pltpu.VMEM((tm, d), BF16)],
        compiler_params=_cparams(("arbitrary", "arbitrary")),
        name="proj",
    )(x2, gain.reshape(1, d), w)


def _prep_even_kernel(p_ref, cos_ref, sin_ref, cos64_ref, sin64_ref, ng_ref, dg_ref, mg_ref,
                      qa_ref, ks_ref, kw_ref, vst_ref, vwt_ref, craw_ref, bq_ref, bk_ref, bvt_ref, mq_ref):
    cos, sin = cos_ref[...], sin_ref[...]
    cos64, sin64 = cos64_ref[...], sin64_ref[...]
    rep = NSA_HEADS // NSA_KV_GROUPS
    tq = TQ_SPARSE

    def unit(u):
        return p_ref[:, u * LANES:(u + 1) * LANES]

    for h in range(NSA_HEADS):
        q = _rope(_rms_norm(unit(E_AQ + h), ng_ref[0:1, :]), cos, sin, 64).astype(BF16)
        g, r = divmod(h, rep)
        for qb in range(PREP_ROWS // tq):
            qa_ref[0, g, qb, r * tq:(r + 1) * tq, :] = q[qb * tq:(qb + 1) * tq, :]
    for g in range(NSA_KV_GROUPS):
        craw_ref[0, 0, g] = unit(E_AKV + 0 + g)
        craw_ref[0, 1, g] = unit(E_AKV + 2 + g)
        ks_ref[0, g] = _rope(_rms_norm(unit(E_AKV + 4 + g), ng_ref[2:3, :]), cos, sin, 64).astype(BF16)
        vst_ref[0, g, 0] = _transpose_bf16(unit(E_AKV + 6 + g))
        kw_ref[0, g] = _rope(_rms_norm(unit(E_AKV + 8 + g), ng_ref[3:4, :]), cos, sin, 64).astype(BF16)
        vw = unit(E_AKV + 10 + g)
        for kb in range(PREP_ROWS // LANES):
            vwt_ref[0, g, kb] = _transpose_bf16(vw[kb * LANES:(kb + 1) * LANES, :])
    lo = lax.broadcasted_iota(jnp.int32, cos.shape, 1) < 64
    for h in range(DIFF_HEADS):
        q = _rope(_rms_norm_halves(unit(E_BQ + h), dg_ref[0:1, :]), cos64, sin64, 32)
        bq_ref[0, h, 0, 0:PREP_ROWS, :] = jnp.where(lo, q, 0.0).astype(BF16)
        bq_ref[0, h, 0, PREP_ROWS:2 * PREP_ROWS, :] = jnp.where(lo, 0.0, q).astype(BF16)
        bk_ref[0, h] = _rope(_rms_norm_halves(unit(E_BK + h), dg_ref[1:2, :]), cos64, sin64, 32).astype(BF16)
        bvt_ref[0, h, 0] = _transpose_bf16(unit(E_BV + h))
    for h in range(MEM_HEADS):
        mq_ref[0, h] = _rms_norm(unit(E_MQ + h), mg_ref[...]).astype(BF16)


def _prep_even(proj, b, s, cos, sin, cos64, sin64, nsa_gain, diff_gain2, mem_gain):
    ts = PREP_ROWS
    assert ts == TQ_DIFF == VT_BLOCK
    nb = s // ts
    c = EVEN_PREP_UNITS * LANES
    g = NSA_KV_GROUPS
    rep = NSA_HEADS // g
    row = lambda bb, i: (i, 0)
    const = lambda bb, i: (0, 0)
    seq = lambda n: pl.BlockSpec((1, n, ts, LANES), lambda bb, i: (bb, 0, i, 0))
    blk5 = lambda n, k, r, cdim: pl.BlockSpec((1, n, k, r, cdim), lambda bb, i: (bb, 0, i, 0, 0))
    return pl.pallas_call(
        _prep_even_kernel,
        grid=(b, nb),
        in_specs=[
            pl.BlockSpec((ts, c), lambda bb, i: (bb * nb + i, 0)),
            pl.BlockSpec((ts, LANES), row), pl.BlockSpec((ts, LANES), row),
            pl.BlockSpec((ts, LANES), row), pl.BlockSpec((ts, LANES), row),
            pl.BlockSpec((4, LANES), const), pl.BlockSpec((2, LANES), const), pl.BlockSpec((1, LANES), const),
        ],
        out_specs=[
            blk5(g, ts // TQ_SPARSE, rep * TQ_SPARSE, LANES),
            seq(g), seq(g),
            blk5(g, 1, LANES, VT_BLOCK),
            blk5(g, ts // LANES, LANES, LANES),
            pl.BlockSpec((1, 2, g, ts, LANES), lambda bb, i: (bb, 0, 0, i, 0)),
            blk5(DIFF_HEADS, 1, 2 * TQ_DIFF, LANES),
            seq(DIFF_HEADS),
            blk5(DIFF_HEADS, 1, LANES, VT_BLOCK),
            seq(MEM_HEADS),
        ],
        out_shape=[
            jax.ShapeDtypeStruct((b, g, s // TQ_SPARSE, rep * TQ_SPARSE, LANES), BF16),
            jax.ShapeDtypeStruct((b, g, s, LANES), BF16),
            jax.ShapeDtypeStruct((b, g, s, LANES), BF16),
            jax.ShapeDtypeStruct((b, g, s // VT_BLOCK, LANES, VT_BLOCK), BF16),
            jax.ShapeDtypeStruct((b, g, s // LANES, LANES, LANES), BF16),
            jax.ShapeDtypeStruct((b, 2, g, s, LANES), F32),
            jax.ShapeDtypeStruct((b, DIFF_HEADS, s // TQ_DIFF, 2 * TQ_DIFF, LANES), BF16),
            jax.ShapeDtypeStruct((b, DIFF_HEADS, s, LANES), BF16),
            jax.ShapeDtypeStruct((b, DIFF_HEADS, s // VT_BLOCK, LANES, VT_BLOCK), BF16),
            jax.ShapeDtypeStruct((b, MEM_HEADS, s, LANES), BF16),
        ],
        compiler_params=_cparams(("arbitrary", "arbitrary")),
        name="prep_even",
    )(proj, cos, sin, cos64, sin64, nsa_gain, diff_gain2, mem_gain)


def _compress_kernel(x_ref, pe_ref, w1_ref, w2_ref, g_ref, cos_ref, sin_ref, kc_ref, vct_ref, pad_ref, *, s, n_pad):
    for kind in range(2):
        pad_ref[0:s, :] = x_ref[0, kind, 0]
        pad_ref[s:s + NSA_CMP_LEN, :] = jnp.zeros((NSA_CMP_LEN, LANES), F32)
        acc = jnp.zeros((n_pad, NSA_CMP_HIDDEN), F32)
        for l in range(NSA_CMP_LEN):
            rows = pad_ref[pl.ds(l, n_pad, stride=NSA_CMP_STRIDE), :] + pe_ref[kind, l:l + 1, :]
            acc = acc + _dot(rows.astype(BF16), w1_ref[kind, l])
        out = _dot(_silu(acc).astype(BF16), w2_ref[kind])
        if kind == 0:
            kc_ref[0, 0] = _rope(_rms_norm(out, g_ref[...]), cos_ref[...], sin_ref[...], 64).astype(BF16)
        else:
            vct_ref[0, 0] = out.T.astype(BF16)


def _compress(craw, pe, w1, w2, gain, cos_c, sin_c):
    b, _, g, s, _ = craw.shape
    n_pad = s // NSA_CMP_STRIDE
    kern = functools.partial(_compress_kernel, s=s, n_pad=n_pad)
    whole = lambda shape: pl.BlockSpec(shape, lambda bb, gg: (0,) * len(shape))
    return pl.pallas_call(
        kern,
        grid=(b, g),
        in_specs=[
            pl.BlockSpec((1, 2, 1, s, LANES), lambda bb, gg: (bb, 0, gg, 0, 0)),
            whole((2, NSA_CMP_LEN, LANES)),
            whole((2, NSA_CMP_LEN, LANES, NSA_CMP_HIDDEN)),
            whole((2, NSA_CMP_HIDDEN, LANES)),
            whole((1, LANES)), whole((n_pad, LANES)), whole((n_pad, LANES)),
        ],
        out_specs=[
            pl.BlockSpec((1, 1, n_pad, LANES), lambda bb, gg: (bb, gg, 0, 0)),
            pl.BlockSpec((1, 1, LANES, n_pad), lambda bb, gg: (bb, gg, 0, 0)),
        ],
        out_shape=[
            jax.ShapeDtypeStruct((b, g, n_pad, LANES), BF16),
            jax.ShapeDtypeStruct((b, g, LANES, n_pad), BF16),
        ],
        scratch_shapes=[pltpu.VMEM((s + NSA_CMP_LEN, LANES), F32)],
        compiler_params=_cparams(("arbitrary", "arbitrary")),
        name="nsa_compress",
    )(craw, pe, w1, w2, gain, cos_c, sin_c)


def _softmax_parts_t(st, bounded):
    if not bounded:
        st = st - _col_max(st)
    e = jnp.exp2(st)
    return e.astype(BF16), 1.0 / jnp.maximum(_col_sum(e), 1e-30)


def _online_init(m_ref, l_ref, acc_ref):
    m_ref[...] = jnp.full(m_ref.shape, NEG_INIT, F32)
    l_ref[...] = jnp.zeros(l_ref.shape, F32)
    acc_ref[...] = jnp.zeros(acc_ref.shape, F32)


def _online_step(idx, st, pv, m_ref, l_ref, acc_ref, bounded):
    if bounded:
        p = jnp.exp2(st)
        l_ref[idx] = l_ref[idx] + _fold8(p, jnp.sum)
        acc_ref[idx] = acc_ref[idx] + pv(p.astype(BF16))
        return
    m_old = m_ref[idx]
    m_new = jnp.maximum(m_old, _col_max(st))
    alpha = jnp.exp2(m_old - m_new)
    p = jnp.exp2(st - m_new)
    l_ref[idx] = alpha * l_ref[idx] + _fold8(p, jnp.sum)
    acc_ref[idx] = alpha * acc_ref[idx] + pv(p.astype(BF16))
    m_ref[idx] = m_new


def _online_result(idx, l_ref, acc_ref):
    return acc_ref[idx] / jnp.maximum(jnp.sum(l_ref[idx], axis=0, keepdims=True), 1e-30)


def _flash_pairs(n_chunks, n_groups, qk, mask, soft, sa_ref, sb_ref):
    for g in range(n_groups):
        qk(0, g, sa_ref)
    n_pairs = (n_chunks - 1) // 2

    def pair(j, carry):
        a = 2 * j
        bias_a = mask(a, False)
        bias_b = mask(a + 1, False)
        for g in range(n_groups):
            qk(a + 1, g, sb_ref)
            soft(a, g, sa_ref, bias_a)
            qk(a + 2, g, sa_ref)
            soft(a + 1, g, sb_ref, bias_b)
        return carry

    lax.fori_loop(0, n_pairs, pair, 0)
    e = 2 * n_pairs

    @pl.when(e + 1 < n_chunks)
    def _():
        bias_a = mask(e, True)
        bias_b = mask(e + 1, True)
        for g in range(n_groups):
            qk(e + 1, g, sb_ref)
            soft(e, g, sa_ref, bias_a)
            soft(e + 1, g, sb_ref, bias_b)

    @pl.when(e + 1 >= n_chunks)
    def _():
        bias_a = mask(e, True)
        for g in range(n_groups):
            soft(e, g, sa_ref, bias_a)


def _pv_blocks(vt_ref, lead, chunk, tk):
    per = tk // VT_BLOCK

    def pv(p):
        out = _dot(vt_ref[lead + (chunk * per,)], p[0:VT_BLOCK])
        for i in range(1, per):
            out = out + _dot(vt_ref[lead + (chunk * per + i,)], p[i * VT_BLOCK:(i + 1) * VT_BLOCK])
        return out

    return pv


def _top_n_mask_t(scores, n):
    row = lax.broadcasted_iota(jnp.int32, scores.shape, 0).astype(F32)
    height = float(scores.shape[0])
    work = scores
    sel = jnp.zeros(scores.shape, F32)
    for _ in range(n):
        m = jnp.max(work, axis=0, keepdims=True)
        first = jnp.min(jnp.where(work == m, row, height), axis=0, keepdims=True)
        pick = row == first
        sel = jnp.where(pick, 1.0, sel)
        work = jnp.where(pick, -jnp.inf, work)
    return sel


def _nsa_kernel(q_ref, kc_ref, vct_ref, ks_ref, vst_ref, kw_ref, vwt_ref, ovlt_ref, *rest, tq, tk, n_qb, top_n, bounded):
    wbias_refs = rest[:n_qb]
    gl_ref, z_ref, o_ref, m_ref, l_ref, acc_ref, sa_ref, sb_ref, selt_ref, oct_ref, owt_ref = rest[n_qb:]
    rep = NSA_HEADS // NSA_KV_GROUPS
    w = rep * tq
    scale = HEAD_DIM ** -0.5
    step0 = pl.program_id(2) * (n_qb * tq)
    lane_t = lax.broadcasted_iota(jnp.int32, (1, w), 1) & (tq - 1)
    n_cmp_pad = kc_ref.shape[2]
    n_sel = ovlt_ref.shape[0]
    sel_shift = NSA_SEL_LEN.bit_length() - 1
    span = NSA_WINDOW + tq

    for qb in range(n_qb):
        q0 = step0 + qb * tq
        t_row = q0 + lane_t
        t_one = t_row[:, 0:tq]
        q = q_ref[0, 0, qb]

        cmp_last = lax.broadcasted_iota(jnp.int32, (n_cmp_pad, w), 0) * NSA_CMP_STRIDE + (NSA_CMP_LEN - 1)
        st = jnp.where(cmp_last <= t_row, _dot_nt(kc_ref[0, 0], q) * (scale * LOG2E), NEG_MASK)
        p, inv = _softmax_parts_t(st, bounded)
        inv = jnp.where(t_row >= NSA_CMP_LEN - 1, inv, 0.0)
        oct_ref[qb] = _dot(vct_ref[0, 0], p) * inv
        imp_heads = _dot(ovlt_ref[...], p) * inv
        imp = imp_heads[:, 0:tq]
        for r in range(1, rep):
            imp = imp + imp_heads[:, r * tq:(r + 1) * tq]

        j = lax.broadcasted_iota(jnp.int32, (n_sel, tq), 0)
        cur = jnp.right_shift(t_one, sel_shift)
        visible = j <= cur
        forced = (j == 0) | (j >= cur - 1)
        imp = jnp.where(visible, jnp.where(forced, FORCED, imp), MASKED)
        selt_ref[qb] = _top_n_mask_t(imp, top_n)

        start = pl.multiple_of(jnp.maximum(q0 - NSA_WINDOW, 0), tq)
        st = (_dot_nt(kw_ref[0, 0, pl.ds(start, span), :], q) * (scale * LOG2E)
              + jnp.concatenate([wbias_refs[qb][0]] * rep, axis=1))
        p, inv = _softmax_parts_t(st, bounded)
        blk0 = start // LANES
        ow = _dot(vwt_ref[0, 0, blk0], p[0:LANES])
        for i in range(1, span // LANES):
            ow = ow + _dot(vwt_ref[0, 0, blk0 + i], p[i * LANES:(i + 1) * LANES])
        owt_ref[qb] = ow * inv

    _online_init(m_ref, l_ref, acc_ref)
    n_chunks = (step0 + n_qb * tq + tk - 1) // tk
    blocks_per_chunk = tk // NSA_SEL_LEN

    def qk(c, g, dst):
        dst[g] = _dot_nt(ks_ref[0, 0, pl.ds(pl.multiple_of(c * tk, tk), tk), :], q_ref[0, 0, g]) * (scale * LOG2E)

    def mask(c, tail):
        biases = []
        for g in range(n_qb):
            rows = selt_ref[g, pl.ds(pl.multiple_of(c * blocks_per_chunk, blocks_per_chunk), blocks_per_chunk), :]
            ok = jnp.concatenate([jnp.broadcast_to(rows[i:i + 1, :], (NSA_SEL_LEN, tq))
                                  for i in range(blocks_per_chunk)], axis=0) > 0.5
            if tail:
                key = c * tk + lax.broadcasted_iota(jnp.int32, (tk, tq), 0)
                ok = ok & (key <= step0 + g * tq + lane_t[:, 0:tq])
            biases.append(jnp.concatenate([jnp.where(ok, 0.0, NEG_MASK)] * rep, axis=1))
        return biases

    def soft(c, g, src, bias):
        _online_step(g, src[g] + bias[g], _pv_blocks(vst_ref, (0, 0), c, tk), m_ref, l_ref, acc_ref, bounded)

    _flash_pairs(n_chunks, n_qb, qk, mask, soft, sa_ref, sb_ref)

    for qb in range(n_qb):
        rows = slice(qb * tq, (qb + 1) * tq)
        gates_t = jax.nn.sigmoid(gl_ref[rows, :]).T
        os_t = _online_result(qb, l_ref, acc_ref)
        for r in range(rep):
            sl = slice(r * tq, (r + 1) * tq)
            cols = slice(r * LANES, (r + 1) * LANES)
            out_t = (gates_t[3 * r:3 * r + 1, :] * oct_ref[qb, :, sl] + gates_t[3 * r + 1:3 * r + 2, :] * os_t[:, sl]
                     + gates_t[3 * r + 2:3 * r + 3, :] * owt_ref[qb, :, sl])
            o_ref[rows, cols] = (out_t.T * _silu(z_ref[rows, cols])).astype(BF16)


def _window_bias(tq):
    span = NSA_WINDOW + tq
    n_early = NSA_WINDOW // tq
    row = np.arange(span)[:, None]
    lane = np.arange(tq)[None, :]
    tables = []
    for i in range(n_early + 1):
        t = i * tq + lane
        s_pos = (0 if i < n_early else t[0, 0] - NSA_WINDOW) + row
        ok = (s_pos <= t) & (s_pos > t - NSA_WINDOW)
        tables.append(np.where(ok, 0.0, NEG_MASK))
    return jnp.asarray(np.stack(tables), dtype=F32)


def _nsa(qa, ks, vst, kw, vwt, kc, vct, ovlt, wbias, proj, b, s, bounded):
    tq, tk, n_qb = TQ_SPARSE, TK, NSA_QBLOCKS
    assert tk % (n_qb * tq) == 0
    rows = n_qb * tq
    nq = s // rows
    g = NSA_KV_GROUPS
    rep = NSA_HEADS // g
    w = rep * tq
    n_cmp_pad = kc.shape[2]
    n_sel = s // NSA_SEL_LEN
    last_bias = wbias.shape[0] - 1
    kern = functools.partial(_nsa_kernel, tq=tq, tk=tk, n_qb=n_qb, top_n=min(NSA_SEL_TOPN, n_sel), bounded=bounded)
    per_group = lambda shape: pl.BlockSpec((1, 1) + shape, lambda bb, gg, i: (bb, gg) + (0,) * len(shape))
    bias_spec = lambda qb: pl.BlockSpec((1,) + wbias.shape[1:],
                                        lambda bb, gg, i: (jnp.minimum(n_qb * i + qb, last_bias), 0, 0))
    return pl.pallas_call(
        kern,
        grid=(b, g, nq),
        in_specs=[
            pl.BlockSpec((1, 1, n_qb, w, LANES), lambda bb, gg, i: (bb, gg, i, 0, 0)),
            per_group((n_cmp_pad, LANES)), per_group((LANES, n_cmp_pad)),
            per_group((s, LANES)), per_group((s // VT_BLOCK, LANES, VT_BLOCK)),
            per_group((s, LANES)), per_group((s // LANES, LANES, LANES)),
            pl.BlockSpec((n_sel, n_cmp_pad), lambda bb, gg, i: (0, 0)),
        ] + [bias_spec(qb) for qb in range(n_qb)] + [
            pl.BlockSpec((rows, LANES), lambda bb, gg, i: (bb * nq + i, E_AG + gg)),
            pl.BlockSpec((rows, rep * LANES), lambda bb, gg, i: (bb * nq + i, E_AZ // rep + gg)),
        ],
        out_specs=pl.BlockSpec((rows, rep * LANES), lambda bb, gg, i: (bb * nq + i, gg)),
        out_shape=jax.ShapeDtypeStruct((b * s, NSA_HEADS * LANES), BF16),
        scratch_shapes=[
            pltpu.VMEM((n_qb, 1, w), F32), pltpu.VMEM((n_qb, SUBLANES, w), F32), pltpu.VMEM((n_qb, LANES, w), F32),
            pltpu.VMEM((n_qb, tk, w), F32), pltpu.VMEM((n_qb, tk, w), F32),
            pltpu.VMEM((n_qb, n_sel, tq), F32), pltpu.VMEM((n_qb, LANES, w), F32), pltpu.VMEM((n_qb, LANES, w), F32),
        ],
        compiler_params=_cparams(("arbitrary", "arbitrary", "arbitrary")),
        name="nsa_attention",
    )(qa, kc, vct, ks, vst, kw, vwt, ovlt, *([wbias] * n_qb), proj, proj)


def _diff_kernel(q_ref, k_ref, vt_ref, lam_ref, sg_ref, z_ref, o_ref, m_ref, l_ref, acc_ref, sa_ref, sb_ref,
                 *, tq, tk, n_qb, lambda_init, bounded):
    w = 2 * tq
    q0 = pl.program_id(2) * (n_qb * tq)
    lane_t = lax.broadcasted_iota(jnp.int32, (1, w), 1) & (tq - 1)
    _online_init(m_ref, l_ref, acc_ref)
    n_chunks = (q0 + n_qb * tq + tk - 1) // tk

    def qk(c, g, dst):
        dst[g] = (_dot_nt(k_ref[0, 0, pl.ds(pl.multiple_of(c * tk, tk), tk), :], q_ref[0, 0, g])
                  * (DIFF_QK_DIM ** -0.5 * LOG2E))

    def mask(c, tail):
        if not tail:
            return None
        key = c * tk + lax.broadcasted_iota(jnp.int32, (tk, w), 0)
        return [jnp.where(key <= q0 + g * tq + lane_t, 0.0, NEG_MASK) for g in range(n_qb)]

    def soft(c, g, src, bias):
        st = src[g] if bias is None else src[g] + bias[g]
        _online_step(g, st, _pv_blocks(vt_ref, (0, 0), c, tk), m_ref, l_ref, acc_ref, bounded)

    _flash_pairs(n_chunks, n_qb, qk, mask, soft, sa_ref, sb_ref)

    lv = lam_ref[...]
    lam = (jnp.exp(jnp.sum(lv[0:1] * lv[1:2], axis=-1, keepdims=True))
           - jnp.exp(jnp.sum(lv[2:3] * lv[3:4], axis=-1, keepdims=True)) + lambda_init)
    for g in range(n_qb):
        rows = slice(g * tq, (g + 1) * tq)
        o_t = _online_result(g, l_ref, acc_ref)
        o = (o_t[:, 0:tq] - lam * o_t[:, tq:w]).T
        o = _rms_norm(o, sg_ref[...]) * (1.0 - lambda_init)
        o_ref[rows, :] = (o * _silu(z_ref[rows, :])).astype(BF16)


def _diff(bq, bk, bvt, lam_vecs, subln_gain, proj, b, s, lambda_init, bounded):
    tq, tk, n_qb = TQ_DIFF, TK, DIFF_QBLOCKS
    assert n_qb * tq == tk
    rows = n_qb * tq
    nq = s // rows
    w = 2 * tq
    kern = functools.partial(_diff_kernel, tq=tq, tk=tk, n_qb=n_qb, lambda_init=lambda_init, bounded=bounded)
    return pl.pallas_call(
        kern,
        grid=(b, DIFF_HEADS, nq),
        in_specs=[
            pl.BlockSpec((1, 1, n_qb, w, LANES), lambda bb, h, i: (bb, h, i, 0, 0)),
            pl.BlockSpec((1, 1, s, LANES), lambda bb, h, i: (bb, h, 0, 0)),
            pl.BlockSpec((1, 1, s // VT_BLOCK, LANES, VT_BLOCK), lambda bb, h, i: (bb, h, 0, 0, 0)),
            pl.BlockSpec((4, DIFF_QK_DIM), lambda bb, h, i: (0, 0)),
            pl.BlockSpec((1, LANES), lambda bb, h, i: (0, 0)),
            pl.BlockSpec((rows, LANES), lambda bb, h, i: (bb * nq + i, E_BZ + h)),
        ],
        out_specs=pl.BlockSpec((rows, LANES), lambda bb, h, i: (bb * nq + i, h)),
        out_shape=jax.ShapeDtypeStruct((b * s, DIFF_HEADS * LANES), BF16),
        scratch_shapes=[
            pltpu.VMEM((n_qb, 1, w), F32), pltpu.VMEM((n_qb, SUBLANES, w), F32), pltpu.VMEM((n_qb, LANES, w), F32),
            pltpu.VMEM((n_qb, tk, w), F32), pltpu.VMEM((n_qb, tk, w), F32),
        ],
        compiler_params=_cparams(("arbitrary", "arbitrary", "arbitrary")),
        name="diff_attention",
    )(bq, bk, bvt, lam_vecs, subln_gain, proj)


def _memkv_kernel(mem_ref, mg_ref, w_ref, kg_ref, k_ref, v_ref):
    mem_n = _rms_norm(mem_ref[0], mg_ref[...]).astype(BF16)
    kv = _dot(mem_n, w_ref[0])
    for h in range(MEM_HEADS):
        k_ref[0, 0, h] = _rms_norm(kv[:, h * LANES:(h + 1) * LANES], kg_ref[0]).astype(BF16)
        v_ref[0, 0, h] = kv[:, (MEM_HEADS + h) * LANES:(MEM_HEADS + h + 1) * LANES].astype(BF16)


def _memkv(mem, mem_gain, w_kv, k_gain):
    b, m, d = mem.shape
    depth = w_kv.shape[0]
    c = w_kv.shape[2]
    out = jax.ShapeDtypeStruct((depth, b, MEM_HEADS, m, LANES), BF16)
    ospec = pl.BlockSpec((1, 1, MEM_HEADS, m, LANES), lambda i, bb: (i, bb, 0, 0, 0))
    return pl.pallas_call(
        _memkv_kernel,
        grid=(depth, b),
        in_specs=[
            pl.BlockSpec((1, m, d), lambda i, bb: (bb, 0, 0)),
            pl.BlockSpec((1, d), lambda i, bb: (0, 0)),
            pl.BlockSpec((1, d, c), lambda i, bb: (i, 0, 0)),
            pl.BlockSpec((1, 1, LANES), lambda i, bb: (i, 0, 0)),
        ],
        out_specs=[ospec, ospec],
        out_shape=[out, out],
        compiler_params=_cparams(("arbitrary", "arbitrary")),
        name="mem_kv",
    )(mem, mem_gain.reshape(1, d), w_kv, k_gain)


def _memattn_kernel(q_ref, k_ref, v_ref, z_ref, o_ref):
    scale = HEAD_DIM ** -0.5
    for h in range(MEM_HEADS):
        s = _dot_nt(q_ref[0, h], k_ref[0, h]) * scale
        e = jnp.exp(s - jnp.max(s, axis=-1, keepdims=True))
        p = e / jnp.sum(e, axis=-1, keepdims=True)
        o = _dot(p.astype(BF16), v_ref[0, h])
        o_ref[:, h * LANES:(h + 1) * LANES] = (o * _silu(z_ref[:, h * LANES:(h + 1) * LANES])).astype(BF16)


def _memattn(mq, mk, mv, proj, b, s, z_unit):
    tq = min(512, s)
    nq = s // tq
    m = mk.shape[2]
    kvspec = pl.BlockSpec((1, MEM_HEADS, m, LANES), lambda bb, i: (bb, 0, 0, 0))
    return pl.pallas_call(
        _memattn_kernel,
        grid=(b, nq),
        in_specs=[
            pl.BlockSpec((1, MEM_HEADS, tq, LANES), lambda bb, i: (bb, 0, i, 0)),
            kvspec, kvspec,
            pl.BlockSpec((tq, MEM_HEADS * LANES), lambda bb, i: (bb * nq + i, z_unit // MEM_HEADS)),
        ],
        out_specs=pl.BlockSpec((tq, MEM_HEADS * LANES), lambda bb, i: (bb * nq + i, 0)),
        out_shape=jax.ShapeDtypeStruct((b * s, MEM_HEADS * LANES), BF16),
        compiler_params=_cparams(("arbitrary", "arbitrary")),
        name="mem_attention",
    )(mq, mk, mv, proj)


def _prep_odd_kernel(p_ref, cos_ref, sin_ref, cos64_ref, sin64_ref, cg_ref, mg_ref,
                     cq_ref, ck_ref, cvt_ref, iq_ref, ik_ref, mq_ref):
    cos, sin = cos_ref[...], sin_ref[...]
    cos64, sin64 = cos64_ref[...], sin64_ref[...]
    rep = DSA_HEADS // DSA_KV_HEADS
    tq = TQ_SPARSE
    n_qb = PREP_ROWS // tq

    def unit(u):
        return p_ref[:, u * LANES:(u + 1) * LANES]

    for h in range(DSA_HEADS):
        q = _rope(_rms_norm(unit(O_CQ + h), cg_ref[0:1, :]), cos, sin, 64).astype(BF16)
        g, r = divmod(h, rep)
        for qb in range(n_qb):
            cq_ref[0, g, qb, r * tq:(r + 1) * tq, :] = q[qb * tq:(qb + 1) * tq, :]
    for h in range(DSA_KV_HEADS):
        ck_ref[0, h] = _rope(_rms_norm(unit(O_CK + h), cg_ref[1:2, :]), cos, sin, 64).astype(BF16)
        cvt_ref[0, h, 0] = _transpose_bf16(unit(O_CV + h))
    lo = lax.broadcasted_iota(jnp.int32, cos.shape, 1) < 64
    for u in range(IDX_HEADS // 2):
        x = _rope(unit(O_IQ + u), cos64, sin64, 32)
        even = jnp.where(lo, x, 0.0).astype(BF16)
        odd = jnp.where(lo, pltpu.roll(x, 64, 1), 0.0).astype(BF16)
        for qb in range(n_qb):
            iq_ref[0, qb, (2 * u) * tq:(2 * u + 1) * tq, :] = even[qb * tq:(qb + 1) * tq, :]
            iq_ref[0, qb, (2 * u + 1) * tq:(2 * u + 2) * tq, :] = odd[qb * tq:(qb + 1) * tq, :]
    ik = _rope(unit(O_IKW), cos64, sin64, 32)
    ik_ref[0] = jnp.where(lo, ik, 0.0).astype(BF16)
    for h in range(MEM_HEADS):
        mq_ref[0, h] = _rms_norm(unit(O_MQ + h), mg_ref[...]).astype(BF16)


def _prep_odd(proj, b, s, cos, sin, cos64, sin64, dsa_gain, mem_gain):
    ts = PREP_ROWS
    nb = s // ts
    c = ODD_PREP_UNITS * LANES
    rep = DSA_HEADS // DSA_KV_HEADS
    row = lambda bb, i: (i, 0)
    const = lambda bb, i: (0, 0)
    seq = lambda n: pl.BlockSpec((1, n, ts, LANES), lambda bb, i: (bb, 0, i, 0))
    return pl.pallas_call(
        _prep_odd_kernel,
        grid=(b, nb),
        in_specs=[
            pl.BlockSpec((ts, c), lambda bb, i: (bb * nb + i, 0)),
            pl.BlockSpec((ts, LANES), row), pl.BlockSpec((ts, LANES), row),
            pl.BlockSpec((ts, LANES), row), pl.BlockSpec((ts, LANES), row),
            pl.BlockSpec((2, LANES), const), pl.BlockSpec((1, LANES), const),
        ],
        out_specs=[
            pl.BlockSpec((1, DSA_KV_HEADS, ts // TQ_SPARSE, rep * TQ_SPARSE, LANES), lambda bb, i: (bb, 0, i, 0, 0)),
            seq(DSA_KV_HEADS),
            pl.BlockSpec((1, DSA_KV_HEADS, 1, LANES, VT_BLOCK), lambda bb, i: (bb, 0, i, 0, 0)),
            pl.BlockSpec((1, ts // TQ_SPARSE, IDX_HEADS * TQ_SPARSE, LANES), lambda bb, i: (bb, i, 0, 0)),
            pl.BlockSpec((1, ts, LANES), lambda bb, i: (bb, i, 0)),
            seq(MEM_HEADS),
        ],
        out_shape=[
            jax.ShapeDtypeStruct((b, DSA_KV_HEADS, s // TQ_SPARSE, rep * TQ_SPARSE, LANES), BF16),
            jax.ShapeDtypeStruct((b, DSA_KV_HEADS, s, LANES), BF16),
            jax.ShapeDtypeStruct((b, DSA_KV_HEADS, s // VT_BLOCK, LANES, VT_BLOCK), BF16),
            jax.ShapeDtypeStruct((b, s // TQ_SPARSE, IDX_HEADS * TQ_SPARSE, LANES), BF16),
            jax.ShapeDtypeStruct((b, s, LANES), BF16),
            jax.ShapeDtypeStruct((b, MEM_HEADS, s, LANES), BF16),
        ],
        compiler_params=_cparams(("arbitrary", "arbitrary")),
        name="prep_odd",
    )(proj, cos, sin, cos64, sin64, dsa_gain, mem_gain)


def _sortable_key(x):
    bits = pltpu.bitcast(x + 0.0, jnp.int32)
    return jnp.where(bits < 0, bits ^ jnp.int32(0x7FFFFFFF), bits)


def _dsa_kernel(q_ref, k_ref, vt_ref, iq_ref, ik_ref, w_ref, ltri_ref, z_ref, o_ref,
                key_ref, m_ref, l_ref, acc_ref, sa_ref, sb_ref, ties_ref, *, tq, tk, top_k, bounded):
    rep = DSA_HEADS // DSA_KV_HEADS
    scale = HEAD_DIM ** -0.5
    q0 = pl.program_id(1) * tq
    t_row = q0 + lax.broadcasted_iota(jnp.int32, (1, tq), 1)
    n_chunks = (q0 + tq + tk - 1) // tk
    int_min = jnp.int32(-2147483648)
    heads_per_dot = 4

    w_t = (w_ref[...] * (IDX_HEADS ** -0.5 * IDX_DIM ** -0.5)).T

    def score_chunk(c):
        k0 = pl.multiple_of(c * tk, tk)
        ik = ik_ref[0, pl.ds(k0, tk), :]
        sc = jnp.zeros((tk, tq), F32)
        for h0 in range(0, IDX_HEADS, heads_per_dot):
            x = _dot_nt(ik, iq_ref[0, 0, h0 * tq:(h0 + heads_per_dot) * tq, :])
            for hh in range(heads_per_dot):
                h = h0 + hh
                sc = sc + jnp.maximum(x[:, hh * tq:(hh + 1) * tq], 0.0) * w_t[IDX_DIM + h:IDX_DIM + h + 1, :]
        causal = (k0 + lax.broadcasted_iota(jnp.int32, (tk, tq), 0)) <= t_row
        key_ref[c] = _sortable_key(jnp.where(causal, sc, MASKED))

    def score_pair(j, carry):
        score_chunk(2 * j)
        score_chunk(2 * j + 1)
        return carry

    def score_four(j, carry):
        for u in range(4):
            score_chunk(4 * j + u)
        return carry

    def score_one(c, carry):
        score_chunk(c)
        return carry

    lax.fori_loop(0, n_chunks // 4, score_four, 0)
    lax.fori_loop((n_chunks // 4) * 2, n_chunks // 2, score_pair, 0)
    lax.fori_loop(n_chunks - n_chunks % 2, n_chunks, score_one, 0)

    def count(pred):
        def one(c):
            return _fold8(pred(key_ref[c]).astype(F32), jnp.sum, short_chains=True)

        def four(j, acc):
            return acc + ((one(4 * j) + one(4 * j + 1)) + (one(4 * j + 2) + one(4 * j + 3)))

        acc = lax.fori_loop(0, n_chunks // 4, four, jnp.zeros((SUBLANES, tq), F32))
        acc = lax.fori_loop(n_chunks - n_chunks % 4, n_chunks, lambda c, a: a + one(c), acc)
        return jnp.sum(acc, axis=0, keepdims=True)

    def search(it, state):
        thr_u, above, reached = state
        cand_u = thr_u | jnp.left_shift(jnp.int32(1), 31 - it)
        cand = cand_u ^ int_min
        cnt = count(lambda kk: kk >= cand)
        take = cnt >= top_k
        return jnp.where(take, cand_u, thr_u), jnp.where(take, above, cnt), jnp.where(take, cnt, reached)

    all_keys = jnp.full((1, tq), (n_chunks * tk).astype(F32))
    thr_u, above, reached = lax.fori_loop(
        0, 32, search, (jnp.zeros((1, tq), jnp.int32), jnp.zeros((1, tq), F32), all_keys))
    thr = thr_u ^ int_min
    budget = top_k - above
    surplus = jnp.max(reached) > top_k

    _online_init(m_ref, l_ref, acc_ref)
    ties_ref[...] = jnp.zeros(ties_ref.shape, F32)

    def qk(c, g, dst):
        dst[g] = _dot_nt(k_ref[0, g, pl.ds(pl.multiple_of(c * tk, tk), tk), :], q_ref[0, g, 0]) * (scale * LOG2E)

    def mask_ranked(c, tail):
        keys = key_ref[c]
        tie = keys == thr
        rank = _dot(ltri_ref[...], tie.astype(BF16)) + ties_ref[...]
        ok = (keys > thr) | (tie & (rank < budget))
        if tail:
            ok = ok & ((c * tk + lax.broadcasted_iota(jnp.int32, (tk, tq), 0)) <= t_row)
        ties_ref[...] = ties_ref[...] + _col_sum(tie.astype(F32))
        return jnp.concatenate([jnp.where(ok, 0.0, NEG_MASK)] * rep, axis=1)

    def mask_plain(c, tail):
        ok = key_ref[c] >= thr
        if tail:
            ok = ok & ((c * tk + lax.broadcasted_iota(jnp.int32, (tk, tq), 0)) <= t_row)
        return jnp.concatenate([jnp.where(ok, 0.0, NEG_MASK)] * rep, axis=1)

    def soft(c, g, src, bias):
        _online_step(g, src[g] + bias, _pv_blocks(vt_ref, (0, g), c, tk), m_ref, l_ref, acc_ref, bounded)

    @pl.when(surplus)
    def _():
        _flash_pairs(n_chunks, DSA_KV_HEADS, qk, mask_ranked, soft, sa_ref, sb_ref)

    @pl.when(jnp.logical_not(surplus))
    def _():
        _flash_pairs(n_chunks, DSA_KV_HEADS, qk, mask_plain, soft, sa_ref, sb_ref)

    for g in range(DSA_KV_HEADS):
        o_t = _online_result(g, l_ref, acc_ref)
        for r in range(rep):
            h = g * rep + r
            o = o_t[:, r * tq:(r + 1) * tq].T
            o_ref[:, h * LANES:(h + 1) * LANES] = (o * _silu(z_ref[:, h * LANES:(h + 1) * LANES])).astype(BF16)


def _dsa(cq, ck, cvt, iq, ik, ltri, proj, b, s, bounded):
    tq, tk = TQ_SPARSE, TK
    nq = s // tq
    g = DSA_KV_HEADS
    rep = DSA_HEADS // g
    w = rep * tq
    kern = functools.partial(_dsa_kernel, tq=tq, tk=tk, top_k=min(DSA_TOPK_MAX, s // 4), bounded=bounded)
    return pl.pallas_call(
        kern,
        grid=(b, nq),
        in_specs=[
            pl.BlockSpec((1, g, 1, w, LANES), lambda bb, i: (bb, 0, i, 0, 0)),
            pl.BlockSpec((1, g, s, LANES), lambda bb, i: (bb, 0, 0, 0), pipeline_mode=pl.Buffered(1)),
            pl.BlockSpec((1, g, s // VT_BLOCK, LANES, VT_BLOCK), lambda bb, i: (bb, 0, 0, 0, 0),
                         pipeline_mode=pl.Buffered(1)),
            pl.BlockSpec((1, 1, IDX_HEADS * tq, LANES), lambda bb, i: (bb, i, 0, 0)),
            pl.BlockSpec((1, s, LANES), lambda bb, i: (bb, 0, 0), pipeline_mode=pl.Buffered(1)),
            pl.BlockSpec((tq, LANES), lambda bb, i: (bb * nq + i, O_IKW)),
            pl.BlockSpec((tk, tk), lambda bb, i: (0, 0)),
            pl.BlockSpec((tq, DSA_HEADS * LANES), lambda bb, i: (bb * nq + i, O_CZ // DSA_HEADS)),
        ],
        out_specs=pl.BlockSpec((tq, DSA_HEADS * LANES), lambda bb, i: (bb * nq + i, 0)),
        out_shape=jax.ShapeDtypeStruct((b * s, DSA_HEADS * LANES), BF16),
        scratch_shapes=[
            pltpu.VMEM((s // tk, tk, tq), jnp.int32),
            pltpu.VMEM((g, 1, w), F32), pltpu.VMEM((g, SUBLANES, w), F32), pltpu.VMEM((g, LANES, w), F32),
            pltpu.VMEM((g, tk, w), F32), pltpu.VMEM((g, tk, w), F32),
            pltpu.VMEM((1, tq), F32),
        ],
        compiler_params=_cparams(("arbitrary", "arbitrary")),
        name="dsa_attention",
    )(cq, ck, cvt, iq, ik, proj, ltri, proj)


def _outproj_kernel(*refs, n_parts):
    x_ref = refs[0]
    y_refs = refs[1:1 + n_parts]
    w_refs = refs[1 + n_parts:1 + 2 * n_parts]
    o_ref = refs[1 + 2 * n_parts]
    acc = x_ref[...]
    for y_ref, w_ref in zip(y_refs, w_refs):
        acc = acc + _dot(y_ref[...], w_ref[...])
    o_ref[...] = acc


def _outproj(x2, ys, w_out):
    n, d = x2.shape
    tm = min(1024, n)
    tn = 1024
    widths = [y.shape[1] for y in ys]
    starts = np.cumsum([0] + widths[:-1]).tolist()
    ws = [w_out[st:st + wd] for st, wd in zip(starts, widths)]
    kern = functools.partial(_outproj_kernel, n_parts=len(ys))
    return pl.pallas_call(
        kern,
        grid=(d // tn, n // tm),
        in_specs=([pl.BlockSpec((tm, tn), lambda j, i: (i, j))]
                  + [pl.BlockSpec((tm, wd), lambda j, i: (i, 0)) for wd in widths]
                  + [pl.BlockSpec((wd, tn), lambda j, i: (0, j)) for wd in widths]),
        out_specs=pl.BlockSpec((tm, tn), lambda j, i: (i, j)),
        out_shape=jax.ShapeDtypeStruct((n, d), F32),
        compiler_params=_cparams(("arbitrary", "arbitrary")),
        name="out_proj",
    )(x2, *ys, *ws)


def _rope_tables(pos, half, reps):
    inv = ROPE_THETA ** (-jnp.arange(half, dtype=F32) / half)
    ang = pos.astype(F32)[:, None] * inv[None, :]
    cos, sin = jnp.cos(ang), jnp.sin(ang)
    return jnp.tile(jnp.concatenate([cos, cos], -1), (1, reps)), jnp.tile(jnp.concatenate([-sin, sin], -1), (1, reps))


def _split_cols(w, sizes):
    return jnp.split(w, np.cumsum(sizes)[:-1].tolist(), axis=-1)


def _even_weight(w):
    sizes = (1024, 1536, 24, 1024, 512, 512, 512, 512, 512, 512)
    a_q, a_kv, a_g, a_z, b_q, b_k, b_v, b_z, m_q, m_z = _split_cols(w.astype(BF16), sizes)
    d = w.shape[0]
    per_group = 3 * NSA_HEADS // NSA_KV_GROUPS
    gates = [jnp.pad(a_g[:, g * per_group:(g + 1) * per_group], ((0, 0), (0, LANES - per_group)))
             for g in range(NSA_KV_GROUPS)]
    out = jnp.concatenate([a_q, a_kv, b_q, b_k, b_v, m_q, a_z, b_z, m_z] + gates, axis=-1)
    assert out.shape == (d, EVEN_UNITS * LANES)
    return out


def _odd_weight(w):
    sizes = (1536, 512, 512, 1024, 64, 16, 1536, 512, 512)
    c_q, c_k, c_v, i_q, i_k, i_w, c_z, m_q, m_z = _split_cols(w.astype(BF16), sizes)
    d = w.shape[0]
    ikw = jnp.pad(jnp.concatenate([i_k, i_w], -1), ((0, 0), (0, LANES - IDX_DIM - IDX_HEADS)))
    pad = jnp.zeros((d, (O_CZ - ODD_PREP_UNITS) * LANES), BF16)
    out = jnp.concatenate([c_q, c_k, c_v, i_q, m_q, ikw, pad, c_z, m_z], axis=-1)
    assert out.shape == (d, ODD_UNITS * LANES)
    return out


def _overlap_matrix_t(s, n_cmp_pad):
    n_cmp = (s - NSA_CMP_LEN) // NSA_CMP_STRIDE + 1
    n_sel = s // NSA_SEL_LEN
    cmp_start = np.arange(n_cmp) * NSA_CMP_STRIDE
    sel_start = np.arange(n_sel) * NSA_SEL_LEN
    ov = np.clip(np.minimum(cmp_start[:, None] + NSA_CMP_LEN, sel_start[None, :] + NSA_SEL_LEN)
                 - np.maximum(cmp_start[:, None], sel_start[None, :]), 0, None) / NSA_CMP_LEN
    full = np.zeros((n_sel, n_cmp_pad), np.float32)
    full[:, :n_cmp] = ov.T
    return jnp.asarray(full, dtype=BF16)


def _score_bound(dim, gain_q, gain_k, scale):
    return dim * jnp.max(jnp.abs(gain_q)) * jnp.max(jnp.abs(gain_k)) * (scale * LOG2E * 1.02)


def _attend(bound, fn, *operands):
    return lax.cond(bound <= FAST_LOG2_BOUND,
                    lambda *a: fn(*a, bounded=True), lambda *a: fn(*a, bounded=False), *operands)


def kernel(x, mem, norm_gain, mem_norm_gain, mem_w_kv, mem_qk_gain, w_out, even_w_in, nsa_qk_gain, nsa_cmp_pos,
           nsa_cmp_w1, nsa_cmp_w2, diff_qk_gain, diff_lambda, diff_subln_gain, odd_w_in, dsa_qk_gain):
    b, s, d = x.shape
    assert d == D_MODEL and s % TK == 0 and s >= NSA_WINDOW + TQ_SPARSE
    pos = jnp.arange(s)
    cos, sin = _rope_tables(pos, HEAD_DIM // 2, 1)
    cos64, sin64 = _rope_tables(pos, DIFF_QK_DIM // 2, 2)
    n_cmp_pad = s // NSA_CMP_STRIDE
    cmp_last = jnp.arange(n_cmp_pad) * NSA_CMP_STRIDE + NSA_CMP_LEN - 1
    cos_c, sin_c = _rope_tables(cmp_last, HEAD_DIM // 2, 1)
    ovlt = _overlap_matrix_t(s, n_cmp_pad)
    ltri = jnp.asarray(np.tril(np.ones((TK, TK), np.float32), -1), dtype=BF16)
    wbias = _window_bias(TQ_SPARSE)

    mk_all, mv_all = _memkv(mem, mem_norm_gain, mem_w_kv.astype(BF16), mem_qk_gain[:, 1:2, :])
    w_out_b = w_out.astype(BF16)

    x2 = x.reshape(b * s, d)
    for i in range(DEPTH):
        mem_q_gain = mem_qk_gain[i, 0:1, :]
        if i % 2 == 0:
            e = i // 2
            proj = _proj(x2, norm_gain[i], _even_weight(even_w_in[e]), 768)
            dg2 = jnp.tile(diff_qk_gain[e], (1, 2))
            qa, ks, kw, vst, vwt, craw, bq, bk, bvt, mq = _prep_even(
                proj, b, s, cos, sin, cos64, sin64, nsa_qk_gain[e], dg2, mem_q_gain)
            w1 = nsa_cmp_w1[e].reshape(2, NSA_CMP_LEN, HEAD_DIM, NSA_CMP_HIDDEN).astype(BF16)
            kc, vct = _compress(craw, nsa_cmp_pos[e], w1, nsa_cmp_w2[e].astype(BF16),
                                nsa_qk_gain[e, 1:2, :], cos_c, sin_c)
            y_a = _attend(_score_bound(HEAD_DIM, nsa_qk_gain[e, 0], nsa_qk_gain[e, 1:4], HEAD_DIM ** -0.5),
                          functools.partial(_nsa, b=b, s=s), qa, ks, vst, kw, vwt, kc, vct, ovlt, wbias, proj)
            lambda_init = 0.8 - 0.6 * math.exp(-0.3 * i)
            y_b = _attend(_score_bound(DIFF_QK_DIM, diff_qk_gain[e, 0], diff_qk_gain[e, 1], DIFF_QK_DIM ** -0.5),
                          functools.partial(_diff, b=b, s=s, lambda_init=lambda_init),
                          bq, bk, bvt, diff_lambda[e], diff_subln_gain[e].reshape(1, LANES), proj)
            y_m = _memattn(mq, mk_all[i], mv_all[i], proj, b, s, E_MZ)
            ys = [y_a, y_b, y_m]
        else:
            o = i // 2
            proj = _proj(x2, norm_gain[i], _odd_weight(odd_w_in[o]), 512)
            cq, ck, cvt, iq, ik, mq = _prep_odd(proj, b, s, cos, sin, cos64, sin64, dsa_qk_gain[o], mem_q_gain)
            y_c = _attend(_score_bound(HEAD_DIM, dsa_qk_gain[o, 0], dsa_qk_gain[o, 1], HEAD_DIM ** -0.5),
                          functools.partial(_dsa, b=b, s=s), cq, ck, cvt, iq, ik, ltri, proj)
            y_m = _memattn(mq, mk_all[i], mv_all[i], proj, b, s, O_MZ)
            ys = [y_c, y_m]
        x2 = _outproj(x2, ys, w_out_b[i])
    return x2.reshape(b, s, d)
```
